```python
import math
import jax, jax.numpy as jnp
from jax import lax
import numpy as np

D_MODEL = 1024
BATCH = 2
SEQ = 8192
DEPTH = 1
DEC_BATCH = 128
DEC_SEQ = 8
PAST_LEN = 16384
PAGE_SIZE = 128

SSM_WIDTH = D_MODEL // 2
SSM_GROUP = 16
SSM_GROUPS = SSM_WIDTH // SSM_GROUP
SSM_STATE = 64
DT_MIN = 0.001
DT_MAX = 0.1
SWA_HEADS = 8
SWA_KV_HEADS = 2
SWA_Q_PER_KV = SWA_HEADS // SWA_KV_HEADS
SWA_HD = 64
SWA_WIDTH = SWA_HEADS * SWA_HD
SWA_KV_WIDTH = SWA_KV_HEADS * SWA_HD
WINDOW = 128
SWA_BLOCK = WINDOW
ROPE_THETA = 10000.0
N_MEM = 256
MEM_HEADS = 4
MEM_HD = 128
MEM_WIDTH = MEM_HEADS * MEM_HD
N_BRANCH = 3
SPLIT_POINTS = (SSM_WIDTH,
                SSM_WIDTH + SWA_WIDTH,
                SSM_WIDTH + SWA_WIDTH + SWA_KV_WIDTH,
                SSM_WIDTH + SWA_WIDTH + 2 * SWA_KV_WIDTH,
                SSM_WIDTH + SWA_WIDTH + 2 * SWA_KV_WIDTH + MEM_WIDTH)
IN_COLS = SPLIT_POINTS[-1] + N_BRANCH * D_MODEL
N_EXPERT_GROUPS = 4
EXPERTS_PER_GROUP = 8
N_EXPERTS = N_EXPERT_GROUPS * EXPERTS_PER_GROUP
TOP_K = 2
D_FF_EXPERT = 256

EPS = 1e-6
NEG_INF = -1e30
SWA_SCALE = SWA_HD ** -0.5
MEM_SCALE = MEM_HD ** -0.5

kernel_name = 'griffin_s5_swa_memory_hmoe_step'


def _rmsnorm(x, g):
    xf = x.astype(jnp.float32)
    xf = xf * lax.rsqrt(jnp.mean(xf * xf, axis=-1, keepdims=True) + EPS)
    return (xf * g.astype(jnp.float32)).astype(x.dtype)


def _rope(x, pos):
    half = x.shape[-1] // 2
    inv = ROPE_THETA ** (-jnp.arange(half, dtype=jnp.float32) / half)
    ang = pos.astype(jnp.float32)[:, None] * inv[None, :]
    cos = jnp.cos(ang)[:, None, :]
    sin = jnp.sin(ang)[:, None, :]
    xf = x.astype(jnp.float32)
    x1, x2 = xf[..., :half], xf[..., half:]
    return jnp.concatenate([x1 * cos - x2 * sin, x2 * cos + x1 * sin], axis=-1).astype(x.dtype)


def _s5_discretise(a_re, a_im, log_dt, b_re, b_im):
    dt = jnp.exp(log_dt.astype(jnp.float32))[:, None]
    ar = a_re.astype(jnp.float32)
    ai = a_im.astype(jnp.float32)
    mag = jnp.exp(ar * dt)
    abar_re = mag * jnp.cos(ai * dt)
    abar_im = mag * jnp.sin(ai * dt)
    den = ar * ar + ai * ai
    nr = abar_re - 1.0
    ni = abar_im
    coef_re = (nr * ar + ni * ai) / den
    coef_im = (ni * ar - nr * ai) / den
    br = b_re.astype(jnp.float32)
    bi = b_im.astype(jnp.float32)
    bbar_re = coef_re[..., None] * br - coef_im[..., None] * bi
    bbar_im = coef_re[..., None] * bi + coef_im[..., None] * br
    return abar_re, abar_im, bbar_re, bbar_im


def _complex_affine_combine(e1, e2):
    a1r, a1i, b1r, b1i = e1
    a2r, a2i, b2r, b2i = e2
    return (a2r * a1r - a2i * a1i, a2r * a1i + a2i * a1r,
            a2r * b1r - a2i * b1i + b2r, a2r * b1i + a2i * b1r + b2i)


def _s5_mixer(u, s0_re, s0_im, w):
    n, l, _ = u.shape
    uf = u.astype(jnp.float32).reshape(n, l, SSM_GROUPS, SSM_GROUP)
    abr, abi, bbr, bbi = _s5_discretise(w['ssm_a_re'], w['ssm_a_im'], w['ssm_log_dt'], w['ssm_b_re'], w['ssm_b_im'])
    bu_re = jnp.einsum('nlgh,gph->nlgp', uf, bbr)
    bu_im = jnp.einsum('nlgh,gph->nlgp', uf, bbi)
    s0r = s0_re.astype(jnp.float32)
    s0i = s0_im.astype(jnp.float32)
    bu_re = bu_re.at[:, 0].add(abr * s0r - abi * s0i)
    bu_im = bu_im.at[:, 0].add(abr * s0i + abi * s0r)
    a_re_t = jnp.broadcast_to(abr, bu_re.shape)
    a_im_t = jnp.broadcast_to(abi, bu_im.shape)
    _, _, s_re, s_im = lax.associative_scan(_complex_affine_combine, (a_re_t, a_im_t, bu_re, bu_im), axis=1)
    c_re = w['ssm_c_re'].astype(jnp.float32)
    c_im = w['ssm_c_im'].astype(jnp.float32)
    y = jnp.einsum('ghp,nlgp->nlgh', c_re, s_re) - jnp.einsum('ghp,nlgp->nlgh', c_im, s_im)
    y = y + w['ssm_d'].astype(jnp.float32).reshape(SSM_GROUPS, SSM_GROUP) * uf
    y = jax.nn.gelu(y.reshape(n, l, SSM_WIDTH))
    y = y * jax.nn.sigmoid(y @ w['w_glu'].astype(jnp.float32))
    return y.astype(u.dtype), s_re[:, -1].astype(s0_re.dtype), s_im[:, -1].astype(s0_im.dtype)


def _sink_softmax(s, mask, sinks):
    sk = sinks.astype(jnp.float32).reshape(SWA_KV_HEADS, SWA_Q_PER_KV, 1, 1)
    s = jnp.where(mask, s, NEG_INF)
    m = jnp.maximum(jnp.max(s, axis=-1, keepdims=True), sk)
    e = jnp.exp(s - m)
    return e / (jnp.sum(e, axis=-1, keepdims=True) + jnp.exp(sk - m))


def _swa_prompt(q, k, v, sinks):
    n, l = q.shape[:2]
    nb = l // SWA_BLOCK
    qb = q.reshape(n, nb, SWA_BLOCK, SWA_KV_HEADS, SWA_Q_PER_KV, SWA_HD)
    kb = k.reshape(n, nb, SWA_BLOCK, SWA_KV_HEADS, SWA_HD)
    vb = v.reshape(n, nb, SWA_BLOCK, SWA_KV_HEADS, SWA_HD)
    kk = jnp.concatenate([jnp.concatenate([jnp.zeros_like(kb[:, :1]), kb[:, :-1]], axis=1), kb], axis=2)
    vv = jnp.concatenate([jnp.concatenate([jnp.zeros_like(vb[:, :1]), vb[:, :-1]], axis=1), vb], axis=2)
    s = jnp.einsum('nbqkgd,nbskd->nbkgqs', qb, kk, preferred_element_type=jnp.float32) * SWA_SCALE
    i = jnp.arange(SWA_BLOCK)[:, None]
    j = jnp.arange(2 * SWA_BLOCK)[None, :]
    rel = i + SWA_BLOCK - j
    in_win = (rel >= 0) & (rel < WINDOW)
    first = in_win & (j >= SWA_BLOCK)
    mask = jnp.where((jnp.arange(nb) == 0)[:, None, None], first[None], in_win[None])
    p = _sink_softmax(s, mask[None, :, None, None], sinks)
    o = jnp.einsum('nbkgqs,nbskd->nbqkgd', p.astype(vv.dtype), vv)
    return o.reshape(n, l, SWA_WIDTH)


def _swa_sample(q, k_new, v_new, past_k, past_v, sinks):
    n, s_len = q.shape[:2]
    wb = past_k.shape[1]
    kk = jnp.concatenate([past_k, k_new], axis=1)
    vv = jnp.concatenate([past_v, v_new], axis=1)
    qpos = PAST_LEN + jnp.arange(s_len, dtype=jnp.int32)
    kpos = jnp.concatenate([PAST_LEN - wb + jnp.arange(wb, dtype=jnp.int32), qpos])
    rel = qpos[:, None] - kpos[None, :]
    mask = (rel >= 0) & (rel < WINDOW)
    qg = q.reshape(n, s_len, SWA_KV_HEADS, SWA_Q_PER_KV, SWA_HD)
    sc = jnp.einsum('nqkgd,nskd->nkgqs', qg, kk, preferred_element_type=jnp.float32) * SWA_SCALE
    p = _sink_softmax(sc, mask, sinks)
    o = jnp.einsum('nkgqs,nskd->nqkgd', p.astype(vv.dtype), vv).reshape(n, s_len, SWA_WIDTH)
    return o, kk[:, -wb:], vv[:, -wb:]


def _mem_kv(mem, w):
    n, m, _ = mem.shape
    hm = _rmsnorm(mem, w['norm_mem'])
    kv = (hm @ w['w_mem_kv']).reshape(n, m, 2, MEM_HEADS, MEM_HD)
    return _rmsnorm(kv[:, :, 0], w['mem_k_norm']), kv[:, :, 1]


def _mem_attend(q, k, v):
    n, l = q.shape[:2]
    sc = jnp.einsum('nlhd,nmhd->nhlm', q, k, preferred_element_type=jnp.float32) * MEM_SCALE
    p = jax.nn.softmax(sc, axis=-1)
    return jnp.einsum('nhlm,nmhd->nlhd', p.astype(v.dtype), v).reshape(n, l, MEM_WIDTH)


def _hier_moe(h, w):
    shp = h.shape
    t = h.reshape(-1, D_MODEL)
    pg = jax.nn.softmax((t @ w['w_router_group'] + w['b_router_group']).astype(jnp.float32), axis=-1)
    pg_top, g_idx = lax.top_k(pg, 1)
    le = (t @ w['w_router_expert'] + w['b_router_expert']).astype(jnp.float32)
    le = le.reshape(-1, N_EXPERT_GROUPS, EXPERTS_PER_GROUP)
    le_sel = jnp.take_along_axis(le, g_idx[:, :, None], axis=1)[:, 0]
    pe_top, e_idx = lax.top_k(jax.nn.softmax(le_sel, axis=-1), TOP_K)
    wts = pg_top * pe_top / jnp.sum(pe_top, axis=-1, keepdims=True)
    expert = g_idx * EXPERTS_PER_GROUP + e_idx
    combine = jnp.sum(jax.nn.one_hot(expert, N_EXPERTS, dtype=jnp.float32) * wts[..., None], axis=1)
    out = jnp.zeros(t.shape, jnp.float32)
    for e in range(N_EXPERTS):
        gu = t @ w['w_exp_in'][e]
        a = jax.nn.silu(gu[:, :D_FF_EXPERT]) * gu[:, D_FF_EXPERT:]
        out = out + combine[:, e:e + 1] * (a @ w['w_exp_down'][e]).astype(jnp.float32)
    return out.astype(h.dtype).reshape(shp)


def _layer(x, pos, s0_re, s0_im, swa_past_k, swa_past_v, mem_k, mem_v, w):
    n, l, _ = x.shape
    h = _rmsnorm(x, w['norm_mix'])
    proj = h @ w['w_in']
    u, q_swa, k_swa, v_swa, q_mem, gates = jnp.split(proj, SPLIT_POINTS, axis=-1)
    y_ssm, s_re, s_im = _s5_mixer(u, s0_re, s0_im, w)
    q_swa = _rope(_rmsnorm(q_swa.reshape(n, l, SWA_HEADS, SWA_HD), w['swa_q_norm']), pos)
    k_swa = _rope(_rmsnorm(k_swa.reshape(n, l, SWA_KV_HEADS, SWA_HD), w['swa_k_norm']), pos)
    v_swa = v_swa.reshape(n, l, SWA_KV_HEADS, SWA_HD)
    if swa_past_k is None:
        y_swa = _swa_prompt(q_swa, k_swa, v_swa, w['swa_sinks'])
        win = min(WINDOW, l)
        new_k, new_v = k_swa[:, l - win:], v_swa[:, l - win:]
    else:
        y_swa, new_k, new_v = _swa_sample(q_swa, k_swa, v_swa, swa_past_k, swa_past_v, w['swa_sinks'])
    q_mem = _rmsnorm(q_mem.reshape(n, l, MEM_HEADS, MEM_HD), w['mem_q_norm'])
    y_mem = _mem_attend(q_mem, mem_k, mem_v)
    g = jax.nn.sigmoid(gates.astype(jnp.float32)).reshape(n, l, N_BRANCH, D_MODEL).astype(x.dtype)
    merged = (g[:, :, 0] * (y_ssm @ w['w_br_ssm'])
              + g[:, :, 1] * (y_swa @ w['w_br_swa'])
              + g[:, :, 2] * (y_mem @ w['w_br_mem']))
    x = x + merged @ w['w_o']
    x = x + _hier_moe(_rmsnorm(x, w['norm_ffn']), w)
    return x, s_re, s_im, new_k, new_v


def setup_inputs(seed: int = 0) -> dict:
    key = jax.random.key(seed)
    ks = iter(jax.random.split(key, 64))

    def nrm(shape, scale):
        return scale * jax.random.normal(next(ks), shape, jnp.float32)

    win_buf = min(WINDOW, PAST_LEN)
    a_im_base = jnp.pi * jnp.arange(SSM_STATE, dtype=jnp.float32)
    return {
        'x_prompt': nrm((BATCH, SEQ, D_MODEL), 1.0),
        'x_sample': nrm((DEC_BATCH, DEC_SEQ, D_MODEL), 1.0),
        'mem_prompt': nrm((BATCH, N_MEM, D_MODEL), 1.0),
        'state_ssm_re': nrm((DEPTH, DEC_BATCH, SSM_GROUPS, SSM_STATE), 0.5),
        'state_ssm_im': nrm((DEPTH, DEC_BATCH, SSM_GROUPS, SSM_STATE), 0.5),
        'cache_swa_k': nrm((DEPTH, DEC_BATCH, win_buf, SWA_KV_HEADS, SWA_HD), 1.0),
        'cache_swa_v': nrm((DEPTH, DEC_BATCH, win_buf, SWA_KV_HEADS, SWA_HD), 1.0),
        'cache_mem_k': nrm((DEPTH, DEC_BATCH, N_MEM, MEM_HEADS, MEM_HD), 1.0),
        'cache_mem_v': nrm((DEPTH, DEC_BATCH, N_MEM, MEM_HEADS, MEM_HD), 1.0),
        'norm_mix': 1.0 + nrm((DEPTH, D_MODEL), 0.01),
        'w_in': nrm((DEPTH, D_MODEL, IN_COLS), D_MODEL ** -0.5),
        'ssm_a_re': -0.5 + nrm((DEPTH, SSM_GROUPS, SSM_STATE), 0.01),
        'ssm_a_im': a_im_base + nrm((DEPTH, SSM_GROUPS, SSM_STATE), 0.01),
        'ssm_log_dt': jax.random.uniform(next(ks), (DEPTH, SSM_GROUPS), jnp.float32,
                                         minval=math.log(DT_MIN), maxval=math.log(DT_MAX)),
        'ssm_b_re': nrm((DEPTH, SSM_GROUPS, SSM_STATE, SSM_GROUP), (2 * SSM_GROUP) ** -0.5),
        'ssm_b_im': nrm((DEPTH, SSM_GROUPS, SSM_STATE, SSM_GROUP), (2 * SSM_GROUP) ** -0.5),
        'ssm_c_re': nrm((DEPTH, SSM_GROUPS, SSM_GROUP, SSM_STATE), (2 * SSM_STATE) ** -0.5),
        'ssm_c_im': nrm((DEPTH, SSM_GROUPS, SSM_GROUP, SSM_STATE), (2 * SSM_STATE) ** -0.5),
        'ssm_d': nrm((DEPTH, SSM_WIDTH), 1.0),
        'w_glu': nrm((DEPTH, SSM_WIDTH, SSM_WIDTH), SSM_WIDTH ** -0.5),
        'swa_q_norm': 1.0 + nrm((DEPTH, SWA_HD), 0.01),
        'swa_k_norm': 1.0 + nrm((DEPTH, SWA_HD), 0.01),
        'swa_sinks': nrm((DEPTH, SWA_HEADS), 0.5),
        'norm_mem': 1.0 + nrm((DEPTH, D_MODEL), 0.01),
        'w_mem_kv': nrm((DEPTH, D_MODEL, 2 * MEM_WIDTH), D_MODEL ** -0.5),
        'mem_q_norm': 1.0 + nrm((DEPTH, MEM_HD), 0.01),
        'mem_k_norm': 1.0 + nrm((DEPTH, MEM_HD), 0.01),
        'w_br_ssm': nrm((DEPTH, SSM_WIDTH, D_MODEL), SSM_WIDTH ** -0.5),
        'w_br_swa': nrm((DEPTH, SWA_WIDTH, D_MODEL), SWA_WIDTH ** -0.5),
        'w_br_mem': nrm((DEPTH, MEM_WIDTH, D_MODEL), MEM_WIDTH ** -0.5),
        'w_o': nrm((DEPTH, D_MODEL, D_MODEL), D_MODEL ** -0.5),
        'norm_ffn': 1.0 + nrm((DEPTH, D_MODEL), 0.01),
        'w_router_group': nrm((DEPTH, D_MODEL, N_EXPERT_GROUPS), D_MODEL ** -0.5),
        'b_router_group': nrm((DEPTH, N_EXPERT_GROUPS), 0.01),
        'w_router_expert': nrm((DEPTH, D_MODEL, N_EXPERTS), D_MODEL ** -0.5),
        'b_router_expert': nrm((DEPTH, N_EXPERTS), 0.01),
        'w_exp_in': nrm((DEPTH, N_EXPERTS, D_MODEL, 2 * D_FF_EXPERT), D_MODEL ** -0.5),
        'w_exp_down': nrm((DEPTH, N_EXPERTS, D_FF_EXPERT, D_MODEL), D_FF_EXPERT ** -0.5),
    }


def reference(x_prompt, x_sample, mem_prompt, state_ssm_re, state_ssm_im, cache_swa_k, cache_swa_v,
              cache_mem_k, cache_mem_v, norm_mix, w_in, ssm_a_re, ssm_a_im, ssm_log_dt, ssm_b_re, ssm_b_im,
              ssm_c_re, ssm_c_im, ssm_d, w_glu, swa_q_norm, swa_k_norm, swa_sinks, norm_mem, w_mem_kv,
              mem_q_norm, mem_k_norm, w_br_ssm, w_br_swa, w_br_mem, w_o, norm_ffn, w_router_group,
              b_router_group, w_router_expert, b_router_expert, w_exp_in, w_exp_down):
    pos_p = jnp.arange(x_prompt.shape[1], dtype=jnp.int32)
    pos_s = PAST_LEN + jnp.arange(x_sample.shape[1], dtype=jnp.int32)
    xp, xs = x_prompt, x_sample
    p_re, p_im, p_k, p_v, p_mk, p_mv = [], [], [], [], [], []
    s_re, s_im, s_k, s_v = [], [], [], []
    for l in range(DEPTH):
        w = dict(norm_mix=norm_mix[l], w_in=w_in[l], ssm_a_re=ssm_a_re[l], ssm_a_im=ssm_a_im[l],
                 ssm_log_dt=ssm_log_dt[l], ssm_b_re=ssm_b_re[l], ssm_b_im=ssm_b_im[l],
                 ssm_c_re=ssm_c_re[l], ssm_c_im=ssm_c_im[l], ssm_d=ssm_d[l], w_glu=w_glu[l],
                 swa_q_norm=swa_q_norm[l], swa_k_norm=swa_k_norm[l], swa_sinks=swa_sinks[l],
                 norm_mem=norm_mem[l], w_mem_kv=w_mem_kv[l], mem_q_norm=mem_q_norm[l],
                 mem_k_norm=mem_k_norm[l], w_br_ssm=w_br_ssm[l], w_br_swa=w_br_swa[l],
                 w_br_mem=w_br_mem[l], w_o=w_o[l], norm_ffn=norm_ffn[l],
                 w_router_group=w_router_group[l], b_router_group=b_router_group[l],
                 w_router_expert=w_router_expert[l], b_router_expert=b_router_expert[l],
                 w_exp_in=w_exp_in[l], w_exp_down=w_exp_down[l])
        mk, mv = _mem_kv(mem_prompt, w)
        s0 = jnp.zeros((xp.shape[0], SSM_GROUPS, SSM_STATE), xp.dtype)
        xp, r, i, k, v = _layer(xp, pos_p, s0, s0, None, None, mk, mv, w)
        p_re.append(r); p_im.append(i); p_k.append(k); p_v.append(v); p_mk.append(mk); p_mv.append(mv)
        xs, r, i, k, v = _layer(xs, pos_s, state_ssm_re[l], state_ssm_im[l], cache_swa_k[l], cache_swa_v[l],
                                cache_mem_k[l], cache_mem_v[l], w)
        s_re.append(r); s_im.append(i); s_k.append(k); s_v.append(v)
    return (xp, xs,
            jnp.stack(p_re), jnp.stack(p_im), jnp.stack(p_k), jnp.stack(p_v), jnp.stack(p_mk), jnp.stack(p_mv),
            jnp.stack(s_re), jnp.stack(s_im), jnp.stack(s_k), jnp.stack(s_v))
```

```python
import functools
import math

import jax
import jax.numpy as jnp
from jax import lax
from jax.experimental import pallas as pl
from jax.experimental.pallas import tpu as pltpu

F32 = jnp.float32
BF16 = jnp.bfloat16

D_MODEL = 1024
SSM_WIDTH = 512
SSM_GROUP = 16
SSM_GROUPS = 32
SSM_STATE = 64
SSM_CH = SSM_GROUPS * SSM_STATE
SWA_HEADS = 8
SWA_KV_HEADS = 2
SWA_Q_PER_KV = SWA_HEADS // SWA_KV_HEADS
SWA_HD = 64
SWA_WIDTH = SWA_HEADS * SWA_HD
SWA_KV_WIDTH = SWA_KV_HEADS * SWA_HD
WINDOW = 128
PAST_LEN = 16384
ROPE_THETA = 10000.0
N_MEM = 256
MEM_HEADS = 4
MEM_HD = 128
MEM_WIDTH = MEM_HEADS * MEM_HD
N_BRANCH = 3
PROJ_A = SSM_WIDTH + SWA_WIDTH + 2 * SWA_KV_WIDTH + MEM_WIDTH
N_GROUPS_E = 4
EXPERTS_PER_GROUP = 8
N_EXPERTS = 32
D_FF = 256
EPS = 1e-6
NEG_INF = -1e30
SWA_SCALE = SWA_HD ** -0.5
MEM_SCALE = MEM_HD ** -0.5

LANES = 128
SUBLANES = 8
VMEM_LIMIT = 56 * 1024 * 1024

TM_IN = 512
TC_SSM = 512
SCAN_W = 512
TQ_ATT = 512
SEQ_BLK = 8
TM_MRG = 256
TM_MOE = 2048
MOE_SUB = 512


def _cparams(sem):
    return pltpu.CompilerParams(dimension_semantics=sem, vmem_limit_bytes=VMEM_LIMIT)


def _full(shape):
    nd = len(shape)
    return pl.BlockSpec(shape, lambda *_: (0,) * nd)


def _split_bf16(x):
    hi = x.astype(BF16)
    lo = (x - hi.astype(F32)).astype(BF16)
    return hi, lo


def _seg_mean_sq(x, ones_blk, width):
    hi, lo = _split_bf16(x * x)
    s = jnp.dot(hi, ones_blk, preferred_element_type=F32) + jnp.dot(lo, ones_blk, preferred_element_type=F32)
    return s * (1.0 / width)


def _rms_rows(x, gain):
    return x * lax.rsqrt(jnp.mean(x * x, axis=-1, keepdims=True) + EPS) * gain


def _block_ones(n, width):
    i = jnp.arange(n) // width
    return (i[:, None] == i[None, :]).astype(BF16)


def _rope_cols(x, cos, sin_signed, lane_in_head):
    n = x.shape[1]
    reps = n // LANES
    if reps > 1:
        cos = jnp.concatenate([cos] * reps, axis=1)
        sin_signed = jnp.concatenate([sin_signed] * reps, axis=1)
    half = SWA_HD // 2
    partner = jnp.where(lane_in_head < half, pltpu.roll(x, n - half, axis=1), pltpu.roll(x, half, axis=1))
    return x * cos + partner * sin_signed


def _inproj_kernel(x_ref, gmix_ref, w_ref, gq_ref, gk_ref, gm_ref, cos_ref, sin_ref, o64_ref, o128_ref,
                   u_ref, q_ref, k_ref, v_ref, qm_ref):
    x = x_ref[...]
    h = _rms_rows(x, gmix_ref[...]).astype(BF16)
    proj = jnp.dot(h, w_ref[...], preferred_element_type=F32)
    c0 = SSM_WIDTH
    c1 = c0 + SWA_WIDTH
    c2 = c1 + SWA_KV_WIDTH
    c3 = c2 + SWA_KV_WIDTH
    u_ref[...] = proj[:, :c0]
    q = proj[:, c0:c1]
    k = proj[:, c1:c2]
    v_ref[...] = proj[:, c2:c3]
    qm = proj[:, c3:]
    cos = cos_ref[...]
    sin = sin_ref[...]
    o64 = o64_ref[...]
    lane_q = lax.broadcasted_iota(jnp.int32, q.shape, 1) % SWA_HD
    qn = q * lax.rsqrt(_seg_mean_sq(q, o64, SWA_HD) + EPS) * gq_ref[...]
    q_ref[...] = (_rope_cols(qn, cos, sin, lane_q) * SWA_SCALE).astype(BF16)
    lane_k = lax.broadcasted_iota(jnp.int32, k.shape, 1) % SWA_HD
    kn = k * lax.rsqrt(_seg_mean_sq(k, o64[:SWA_KV_WIDTH, :SWA_KV_WIDTH], SWA_HD) + EPS) * gk_ref[...]
    k_ref[...] = _rope_cols(kn, cos, sin, lane_k)
    qmn = qm * lax.rsqrt(_seg_mean_sq(qm, o128_ref[...], MEM_HD) + EPS) * gm_ref[...]
    qm_ref[...] = qmn.astype(BF16)


def _inproj(x2d, cos, sin, pos_blocks, p):
    t = x2d.shape[0]
    tm = TM_IN
    grid = (t // tm,)
    row = lambda i: (i, 0)
    tab = lambda i: (i % pos_blocks, 0)
    out_shape = (
        jax.ShapeDtypeStruct((t, SSM_WIDTH), F32),
        jax.ShapeDtypeStruct((t, SWA_WIDTH), BF16),
        jax.ShapeDtypeStruct((t, SWA_KV_WIDTH), F32),
        jax.ShapeDtypeStruct((t, SWA_KV_WIDTH), F32),
        jax.ShapeDtypeStruct((t, MEM_WIDTH), BF16),
    )
    return pl.pallas_call(
        _inproj_kernel,
        grid=grid,
        in_specs=[
            pl.BlockSpec((tm, D_MODEL), row),
            _full((1, D_MODEL)),
            _full((D_MODEL, PROJ_A)),
            _full((1, SWA_WIDTH)),
            _full((1, SWA_KV_WIDTH)),
            _full((1, MEM_WIDTH)),
            pl.BlockSpec((tm, LANES), tab),
            pl.BlockSpec((tm, LANES), tab),
            _full((SWA_WIDTH, SWA_WIDTH)),
            _full((MEM_WIDTH, MEM_WIDTH)),
        ],
        out_specs=(
            pl.BlockSpec((tm, SSM_WIDTH), row),
            pl.BlockSpec((tm, SWA_WIDTH), row),
            pl.BlockSpec((tm, SWA_KV_WIDTH), row),
            pl.BlockSpec((tm, SWA_KV_WIDTH), row),
            pl.BlockSpec((tm, MEM_WIDTH), row),
        ),
        out_shape=out_shape,
        compiler_params=_cparams(("parallel",)),
        name="inproj",
    )(x2d, p["g_mix"], p["w_a"], p["g_q"], p["g_k"], p["g_qm"], cos, sin, p["ones64"], p["ones128"])


def _ssm_kernel(u_ref, s0r_ref, s0i_ref, bre_ref, bim_ref, cre_ref, cim_ref, d_ref, wglu_ref,
                lvr_ref, lvi_ref, cpr_ref, cpi_ref,
                y_ref, fr_ref, fi_ref, sr_ref, si_ref, car_ref, cai_ref, *, chained):
    tc = u_ref.shape[0]
    half_u = SSM_WIDTH // 2
    half_c = SSM_CH // 2
    u = u_ref[...]
    ub = u.astype(BF16)
    for hh in range(2):
        us = ub[:, hh * half_u:(hh + 1) * half_u]
        cs = slice(hh * half_c, (hh + 1) * half_c)
        sr_ref[:, cs] = jnp.dot(us, bre_ref[hh], preferred_element_type=F32)
        si_ref[:, cs] = jnp.dot(us, bim_ref[hh], preferred_element_type=F32)

    if chained:
        @pl.when(pl.program_id(1) == 0)
        def _():
            car_ref[...] = s0r_ref[0]
            cai_ref[...] = s0i_ref[0]

    n_tiles = tc // SUBLANES
    for sl in range(SSM_CH // SCAN_W):
        cols = slice(sl * SCAN_W, (sl + 1) * SCAN_W)
        lv = [(lvr_ref[j, :, cols], lvi_ref[j, :, cols]) for j in range(3)]
        cpr = cpr_ref[:, cols]
        cpi = cpi_ref[:, cols]

        def tile(i, carry, cols=cols, lv=lv, cpr=cpr, cpi=cpi):
            r0 = pl.multiple_of(i * SUBLANES, SUBLANES)
            xr = sr_ref[pl.ds(r0, SUBLANES), cols]
            xi = si_ref[pl.ds(r0, SUBLANES), cols]
            for j, d in enumerate((1, 2, 4)):
                pr, pi = lv[j]
                shr = pltpu.roll(xr, d, axis=0)
                shi = pltpu.roll(xi, d, axis=0)
                xr, xi = xr + pr * shr - pi * shi, xi + pr * shi + pi * shr
            if chained:
                c_r, c_i = carry
            else:
                c_r = s0r_ref[pl.ds(i, 1), cols]
                c_i = s0i_ref[pl.ds(i, 1), cols]
            cb_r = jnp.broadcast_to(c_r, xr.shape)
            cb_i = jnp.broadcast_to(c_i, xr.shape)
            xr, xi = xr + cpr * cb_r - cpi * cb_i, xi + cpr * cb_i + cpi * cb_r
            sr_ref[pl.ds(r0, SUBLANES), cols] = xr
            si_ref[pl.ds(r0, SUBLANES), cols] = xi
            last_r = xr[SUBLANES - 1:SUBLANES, :]
            last_i = xi[SUBLANES - 1:SUBLANES, :]
            if chained:
                return last_r, last_i
            fr_ref[pl.ds(i, 1), cols] = last_r
            fi_ref[pl.ds(i, 1), cols] = last_i
            return carry

        if chained:
            init = (car_ref[:, cols], cai_ref[:, cols])
        else:
            init = (jnp.zeros((1, SCAN_W), F32), jnp.zeros((1, SCAN_W), F32))
        c_r, c_i = lax.fori_loop(0, n_tiles, tile, init)
        if chained:
            car_ref[:, cols] = c_r
            cai_ref[:, cols] = c_i

    if chained:
        fr_ref[0] = car_ref[...]
        fi_ref[0] = cai_ref[...]

    ys = []
    for hh in range(2):
        cs = slice(hh * half_c, (hh + 1) * half_c)
        ys.append(jnp.dot(sr_ref[:, cs].astype(BF16), cre_ref[hh], preferred_element_type=F32)
                  - jnp.dot(si_ref[:, cs].astype(BF16), cim_ref[hh], preferred_element_type=F32))
    y = jnp.concatenate(ys, axis=1) + d_ref[...] * u
    y = jax.nn.gelu(y)
    gate = jax.nn.sigmoid(jnp.dot(y.astype(BF16), wglu_ref[...], preferred_element_type=F32))
    y_ref[...] = (y * gate).astype(BF16)


def _ssm(u, s0r, s0i, p, *, n_seq, chained):
    t = u.shape[0]
    tc = TC_SSM
    if chained:
        per = t // n_seq // tc
        grid = (n_seq, per)
        row = lambda n, c: (n * per + c, 0)
        st = lambda n, c: (n, 0, 0)
        s0_spec = pl.BlockSpec((1, 1, SSM_CH), st)
        f_spec = pl.BlockSpec((1, 1, SSM_CH), st)
        f_shape = jax.ShapeDtypeStruct((n_seq, 1, SSM_CH), F32)
        sem = ("parallel", "arbitrary")
    else:
        grid = (t // tc,)
        row = lambda c: (c, 0)
        s0_spec = pl.BlockSpec((tc // SUBLANES, SSM_CH), row)
        f_spec = pl.BlockSpec((tc // SUBLANES, SSM_CH), row)
        f_shape = jax.ShapeDtypeStruct((t // SUBLANES, SSM_CH), F32)
        sem = ("parallel",)
    hu = SSM_WIDTH // 2
    hc = SSM_CH // 2
    return pl.pallas_call(
        functools.partial(_ssm_kernel, chained=chained),
        grid=grid,
        in_specs=[
            pl.BlockSpec((tc, SSM_WIDTH), row),
            s0_spec, s0_spec,
            _full((2, hu, hc)), _full((2, hu, hc)),
            _full((2, hc, hu)), _full((2, hc, hu)),
            _full((1, SSM_WIDTH)),
            _full((SSM_WIDTH, SSM_WIDTH)),
            _full((3, SUBLANES, SSM_CH)), _full((3, SUBLANES, SSM_CH)),
            _full((SUBLANES, SSM_CH)), _full((SUBLANES, SSM_CH)),
        ],
        out_specs=(pl.BlockSpec((tc, SSM_WIDTH), row), f_spec, f_spec),
        out_shape=(jax.ShapeDtypeStruct((t, SSM_WIDTH), BF16), f_shape, f_shape),
        scratch_shapes=[
            pltpu.VMEM((tc, SSM_CH), F32), pltpu.VMEM((tc, SSM_CH), F32),
            pltpu.VMEM((1, SSM_CH), F32), pltpu.VMEM((1, SSM_CH), F32),
        ],
        compiler_params=_cparams(sem),
        name="ssm_chained" if chained else "ssm_tiles",
    )(u, s0r, s0i, p["b_re"], p["b_im"], p["c_re"], p["c_im"], p["ssm_d"], p["w_glu"],
      p["lv_re"], p["lv_im"], p["cp_re"], p["cp_im"])


def _ssm_params(a_re, a_im, log_dt, b_re, b_im, c_re, c_im):
    dt = jnp.exp(log_dt)[:, None]
    mag = jnp.exp(a_re * dt)
    abr = mag * jnp.cos(a_im * dt)
    abi = mag * jnp.sin(a_im * dt)
    den = a_re * a_re + a_im * a_im
    nr = abr - 1.0
    ni = abi
    coef_re = (nr * a_re + ni * a_im) / den
    coef_im = (ni * a_re - nr * a_im) / den
    bbr = coef_re[..., None] * b_re - coef_im[..., None] * b_im
    bbi = coef_re[..., None] * b_im + coef_im[..., None] * b_re

    half_g = SSM_GROUPS // 2
    eye = jnp.eye(half_g, dtype=F32)

    def b_blocks(bb):
        bb = bb.reshape(2, half_g, SSM_STATE, SSM_GROUP)
        m = jnp.einsum("zgph,gk->zghkp", bb, eye)
        return m.reshape(2, half_g * SSM_GROUP, half_g * SSM_STATE).astype(BF16)

    def c_blocks(cc):
        cc = cc.reshape(2, half_g, SSM_GROUP, SSM_STATE)
        m = jnp.einsum("zghp,gk->zgpkh", cc, eye)
        return m.reshape(2, half_g * SSM_STATE, half_g * SSM_GROUP).astype(BF16)

    ar = abr.reshape(1, SSM_CH)
    ai = abi.reshape(1, SSM_CH)
    pows = [(ar, ai)]
    for _ in range(SUBLANES - 1):
        pr, pi = pows[-1]
        pows.append((pr * ar - pi * ai, pr * ai + pi * ar))
    rows = jnp.arange(SUBLANES)[:, None]
    lv_re = jnp.stack([jnp.where(rows >= d, pows[d - 1][0], 0.0) for d in (1, 2, 4)])
    lv_im = jnp.stack([jnp.where(rows >= d, pows[d - 1][1], 0.0) for d in (1, 2, 4)])
    cp_re = jnp.concatenate([pw[0] for pw in pows], axis=0)
    cp_im = jnp.concatenate([pw[1] for pw in pows], axis=0)
    return dict(b_re=b_blocks(bbr), b_im=b_blocks(bbi), c_re=c_blocks(c_re), c_im=c_blocks(c_im),
                lv_re=lv_re, lv_im=lv_im, cp_re=cp_re, cp_im=cp_im)


def _memkv_kernel(m_ref, g_ref, w_ref, gk_ref, o128_ref, k_ref, v_ref):
    hm = _rms_rows(m_ref[0], g_ref[...]).astype(BF16)
    kv = jnp.dot(hm, w_ref[...], preferred_element_type=F32)
    k = kv[:, :MEM_WIDTH]
    k_ref[0] = k * lax.rsqrt(_seg_mean_sq(k, o128_ref[...], MEM_HD) + EPS) * gk_ref[...]
    v_ref[0] = kv[:, MEM_WIDTH:]


def _memkv(mem, p):
    n = mem.shape[0]
    blk = lambda i: (i, 0, 0)
    shp = jax.ShapeDtypeStruct((n, N_MEM, MEM_WIDTH), F32)
    return pl.pallas_call(
        _memkv_kernel,
        grid=(n,),
        in_specs=[pl.BlockSpec((1, N_MEM, D_MODEL), blk), _full((1, D_MODEL)),
                  _full((D_MODEL, 2 * MEM_WIDTH)), _full((1, MEM_WIDTH)), _full((MEM_WIDTH, MEM_WIDTH))],
        out_specs=(pl.BlockSpec((1, N_MEM, MEM_WIDTH), blk), pl.BlockSpec((1, N_MEM, MEM_WIDTH), blk)),
        out_shape=(shp, shp),
        compiler_params=_cparams(("parallel",)),
        name="memkv",
    )(mem, p["g_mem"], p["w_mem_kv"], p["g_km"], p["ones128"])


def _dup_heads(x, lane):
    sw = pltpu.roll(x, SWA_HD, axis=1)
    lo = lane < SWA_HD
    return jnp.where(lo, x, sw), jnp.where(lo, sw, x)


def _swa_group(q_blk, kk, vv, g, mask, sink_ref, tq):
    lane = lax.broadcasted_iota(jnp.int32, (tq, LANES), 1)
    rows = []
    sinks = []
    for hl in range(SWA_Q_PER_KV):
        h = g * SWA_Q_PER_KV + hl
        pair = q_blk[:, (h // 2) * LANES:(h // 2 + 1) * LANES]
        keep = (lane < SWA_HD) if h % 2 == 0 else (lane >= SWA_HD)
        rows.append(jnp.where(keep, pair, 0.0))
        sinks.append(jnp.full((tq, 1), sink_ref[h], F32))
    qq = jnp.concatenate(rows, axis=0).astype(BF16)
    sk = jnp.concatenate(sinks, axis=0)
    s = lax.dot_general(qq, kk, (((1,), (1,)), ((), ())), preferred_element_type=F32)
    s = jnp.where(mask, s, NEG_INF)
    m = jnp.maximum(jnp.max(s, axis=-1, keepdims=True), sk)
    e = jnp.exp(s - m)
    pr = e / (jnp.sum(e, axis=-1, keepdims=True) + jnp.exp(sk - m))
    o = jnp.dot(pr.astype(BF16), vv, preferred_element_type=F32)
    lo = lane < SWA_HD
    return [jnp.where(lo, o[(2 * j) * tq:(2 * j + 1) * tq], o[(2 * j + 1) * tq:(2 * j + 2) * tq]) for j in range(2)]


def _mem_heads(qm, k_all, v_all):
    outs = []
    for h in range(MEM_HEADS):
        cs = slice(h * MEM_HD, (h + 1) * MEM_HD)
        s = lax.dot_general(qm[:, cs], k_all[:, cs], (((1,), (1,)), ((), ())), preferred_element_type=F32) * MEM_SCALE
        m = jnp.max(s, axis=-1, keepdims=True)
        e = jnp.exp(s - m)
        pr = e / jnp.sum(e, axis=-1, keepdims=True)
        outs.append(jnp.dot(pr.astype(BF16), v_all[:, cs], preferred_element_type=F32))
    return jnp.concatenate(outs, axis=1)


def _attn_prompt_kernel(sink_ref, q_ref, k_ref, v_ref, kp_ref, vp_ref, qm_ref, mk_ref, mv_ref, ys_ref, ym_ref):
    tq = q_ref.shape[0]
    blk = WINDOW
    rows = SWA_Q_PER_KV * blk
    i = lax.broadcasted_iota(jnp.int32, (rows, 2 * blk), 0) % blk
    j = lax.broadcasted_iota(jnp.int32, (rows, 2 * blk), 1)
    lo = jnp.where(j < blk, i + 1, blk)
    hi = jnp.where(j < blk, blk, blk + i + 1)
    first_lo = jnp.where(pl.program_id(1) == 0, blk, 0)
    lane_k = lax.broadcasted_iota(jnp.int32, (2 * blk, LANES), 1)
    for b in range(tq // blk):
        rs = slice(b * blk, (b + 1) * blk)
        if b == 0:
            k2 = jnp.concatenate([kp_ref[...], k_ref[rs, :]], axis=0)
            v2 = jnp.concatenate([vp_ref[...], v_ref[rs, :]], axis=0)
            mask = (j >= jnp.maximum(lo, first_lo)) & (j < hi)
        else:
            k2 = k_ref[(b - 1) * blk:(b + 1) * blk, :]
            v2 = v_ref[(b - 1) * blk:(b + 1) * blk, :]
            mask = (j >= lo) & (j < hi)
        kks = _dup_heads(k2, lane_k)
        vvs = _dup_heads(v2, lane_k)
        q_blk = q_ref[rs, :].astype(F32)
        pairs = []
        for g in range(SWA_KV_HEADS):
            pairs += _swa_group(q_blk, kks[g].astype(BF16), vvs[g].astype(BF16), g, mask, sink_ref, blk)
        ys_ref[rs, :] = jnp.concatenate(pairs, axis=1).astype(BF16)
    ym_ref[...] = _mem_heads(qm_ref[...], mk_ref[0].astype(BF16), mv_ref[0].astype(BF16)).astype(BF16)


def _attn_prompt(q, k, v, qm, mk, mv, sinks, n_seq):
    t = q.shape[0]
    tq = TQ_ATT
    per = t // n_seq // tq
    sub = tq // WINDOW
    row = lambda n, c: (n * per + c, 0)
    prev = lambda n, c: (jnp.maximum((n * per + c) * sub - 1, 0), 0)
    memb = lambda n, c: (n, 0, 0)
    return pl.pallas_call(
        _attn_prompt_kernel,
        grid=(n_seq, per),
        in_specs=[
            pl.BlockSpec(memory_space=pltpu.SMEM),
            pl.BlockSpec((tq, SWA_WIDTH), row),
            pl.BlockSpec((tq, SWA_KV_WIDTH), row),
            pl.BlockSpec((tq, SWA_KV_WIDTH), row),
            pl.BlockSpec((WINDOW, SWA_KV_WIDTH), prev),
            pl.BlockSpec((WINDOW, SWA_KV_WIDTH), prev),
            pl.BlockSpec((tq, MEM_WIDTH), row),
            pl.BlockSpec((1, N_MEM, MEM_WIDTH), memb),
            pl.BlockSpec((1, N_MEM, MEM_WIDTH), memb),
        ],
        out_specs=(pl.BlockSpec((tq, SWA_WIDTH), row), pl.BlockSpec((tq, MEM_WIDTH), row)),
        out_shape=(jax.ShapeDtypeStruct((t, SWA_WIDTH), BF16), jax.ShapeDtypeStruct((t, MEM_WIDTH), BF16)),
        compiler_params=_cparams(("parallel", "parallel")),
        name="attn_prompt",
    )(sinks, q, k, v, k, v, qm, mk, mv)


def _attn_sample_kernel(sink_ref, q_ref, k_ref, v_ref, pk_ref, pv_ref, qm_ref, mk_ref, mv_ref,
                        ys_ref, ym_ref, nk_ref, nv_ref, *, s_len):
    wb = pk_ref.shape[1]
    n_keys = wb + s_len
    rows = SWA_Q_PER_KV * s_len
    i = lax.broadcasted_iota(jnp.int32, (rows, n_keys), 0) % s_len
    j = lax.broadcasted_iota(jnp.int32, (rows, n_keys), 1)
    rel = i + wb - j
    mask = (rel >= 0) & (rel < WINDOW)
    lane_k = lax.broadcasted_iota(jnp.int32, (n_keys, LANES), 1)
    q_all = q_ref[...].astype(F32)
    qm_all = qm_ref[...].astype(F32)
    ys, ym = [], []
    for s in range(pk_ref.shape[0]):
        rs = slice(s * s_len, (s + 1) * s_len)
        k_all = jnp.concatenate([pk_ref[s], k_ref[rs, :]], axis=0)
        v_all = jnp.concatenate([pv_ref[s], v_ref[rs, :]], axis=0)
        nk_ref[s] = k_all[n_keys - wb:, :]
        nv_ref[s] = v_all[n_keys - wb:, :]
        kks = _dup_heads(k_all, lane_k)
        vvs = _dup_heads(v_all, lane_k)
        pairs = []
        for g in range(SWA_KV_HEADS):
            pairs += _swa_group(q_all[rs, :], kks[g].astype(BF16), vvs[g].astype(BF16), g, mask, sink_ref, s_len)
        ys.append(jnp.concatenate(pairs, axis=1))
        ym.append(_mem_heads(qm_all[rs, :].astype(BF16), mk_ref[s].astype(BF16), mv_ref[s].astype(BF16)))
    ys_ref[...] = jnp.concatenate(ys, axis=0).astype(BF16)
    ym_ref[...] = jnp.concatenate(ym, axis=0).astype(BF16)


def _attn_sample(q, k, v, past_k, past_v, qm, mk, mv, sinks, s_len):
    t = q.shape[0]
    n_seq, wb = past_k.shape[0], past_k.shape[1]
    sb = SEQ_BLK
    rows = sb * s_len
    row = lambda c: (c, 0)
    seq = lambda c: (c, 0, 0)
    cache_shape = jax.ShapeDtypeStruct((n_seq, wb, SWA_KV_WIDTH), F32)
    return pl.pallas_call(
        functools.partial(_attn_sample_kernel, s_len=s_len),
        grid=(n_seq // sb,),
        in_specs=[
            pl.BlockSpec(memory_space=pltpu.SMEM),
            pl.BlockSpec((rows, SWA_WIDTH), row),
            pl.BlockSpec((rows, SWA_KV_WIDTH), row),
            pl.BlockSpec((rows, SWA_KV_WIDTH), row),
            pl.BlockSpec((sb, wb, SWA_KV_WIDTH), seq),
            pl.BlockSpec((sb, wb, SWA_KV_WIDTH), seq),
            pl.BlockSpec((rows, MEM_WIDTH), row),
            pl.BlockSpec((sb, N_MEM, MEM_WIDTH), seq),
            pl.BlockSpec((sb, N_MEM, MEM_WIDTH), seq),
        ],
        out_specs=(pl.BlockSpec((rows, SWA_WIDTH), row), pl.BlockSpec((rows, MEM_WIDTH), row),
                   pl.BlockSpec((sb, wb, SWA_KV_WIDTH), seq), pl.BlockSpec((sb, wb, SWA_KV_WIDTH), seq)),
        out_shape=(jax.ShapeDtypeStruct((t, SWA_WIDTH), BF16), jax.ShapeDtypeStruct((t, MEM_WIDTH), BF16),
                   cache_shape, cache_shape),
        compiler_params=_cparams(("parallel",)),
        name="attn_sample",
    )(sinks, q, k, v, past_k, past_v, qm, mk, mv)


def _first_argmax(x, valid, lane):
    xm = jnp.where(valid, x, -jnp.inf)
    mx = jnp.max(xm, axis=-1, keepdims=True)
    idx = jnp.min(jnp.where(xm == mx, lane, LANES), axis=-1, keepdims=True)
    return mx, lane == idx, idx


def _merge_kernel(x_ref, gmix_ref, wg_ref, ya_ref, yb_ref, yc_ref, wa_ref, wb_ref, wc_ref, wo_ref, gffn_ref,
                  wr_hi_ref, wr_lo_ref, br_ref, x1_ref, hn_ref, comb_ref):
    x = x_ref[...]
    h = _rms_rows(x, gmix_ref[...]).astype(BF16)
    gates = jax.nn.sigmoid(jnp.dot(h, wg_ref[...], preferred_element_type=F32))
    merged = (gates[:, :D_MODEL] * jnp.dot(ya_ref[...], wa_ref[...], preferred_element_type=F32)
              + gates[:, D_MODEL:2 * D_MODEL] * jnp.dot(yb_ref[...], wb_ref[...], preferred_element_type=F32)
              + gates[:, 2 * D_MODEL:] * jnp.dot(yc_ref[...], wc_ref[...], preferred_element_type=F32))
    x1 = x + jnp.dot(merged.astype(BF16), wo_ref[...], preferred_element_type=F32)
    x1_ref[...] = x1
    hn = _rms_rows(x1, gffn_ref[...])
    hn_ref[...] = hn.astype(BF16)

    hi, lo = _split_bf16(hn)
    w_hi = wr_hi_ref[...]
    logits = (jnp.dot(hi, w_hi, preferred_element_type=F32) + jnp.dot(lo, w_hi, preferred_element_type=F32)
              + jnp.dot(hi, wr_lo_ref[...], preferred_element_type=F32)) + br_ref[...]
    lane = lax.broadcasted_iota(jnp.int32, logits.shape, 1)
    is_grp = lane < N_GROUPS_E
    g_max, _, g_idx = _first_argmax(logits, is_grp, lane)
    pg_top = 1.0 / jnp.sum(jnp.where(is_grp, jnp.exp(logits - g_max), 0.0), axis=-1, keepdims=True)
    e_lo = N_GROUPS_E + g_idx * EXPERTS_PER_GROUP
    in_grp = (lane >= e_lo) & (lane < e_lo + EXPERTS_PER_GROUP)
    e_max, first, _ = _first_argmax(logits, in_grp, lane)
    ex = jnp.where(in_grp, jnp.exp(logits - e_max), 0.0)
    pe = ex / jnp.sum(ex, axis=-1, keepdims=True)
    _, second, _ = _first_argmax(logits, in_grp & jnp.logical_not(first), lane)
    p1 = jnp.sum(jnp.where(first, pe, 0.0), axis=-1, keepdims=True)
    p2 = jnp.sum(jnp.where(second, pe, 0.0), axis=-1, keepdims=True)
    wts = jnp.where(first | second, pg_top * pe / (p1 + p2), 0.0)
    comb_ref[...] = wts


def _merge(x2d, ya, yb, yc, p):
    t = x2d.shape[0]
    tm = TM_MRG
    row = lambda i: (i, 0)
    return pl.pallas_call(
        _merge_kernel,
        grid=(t // tm,),
        in_specs=[
            pl.BlockSpec((tm, D_MODEL), row),
            _full((1, D_MODEL)),
            _full((D_MODEL, N_BRANCH * D_MODEL)),
            pl.BlockSpec((tm, SSM_WIDTH), row),
            pl.BlockSpec((tm, SWA_WIDTH), row),
            pl.BlockSpec((tm, MEM_WIDTH), row),
            _full((SSM_WIDTH, D_MODEL)), _full((SWA_WIDTH, D_MODEL)), _full((MEM_WIDTH, D_MODEL)),
            _full((D_MODEL, D_MODEL)),
            _full((1, D_MODEL)),
            _full((D_MODEL, LANES)), _full((D_MODEL, LANES)), _full((1, LANES)),
        ],
        out_specs=(pl.BlockSpec((tm, D_MODEL), row), pl.BlockSpec((tm, D_MODEL), row),
                   pl.BlockSpec((tm, LANES), row)),
        out_shape=(jax.ShapeDtypeStruct((t, D_MODEL), F32), jax.ShapeDtypeStruct((t, D_MODEL), BF16),
                   jax.ShapeDtypeStruct((t, LANES), F32)),
        compiler_params=_cparams(("parallel",)),
        name="merge",
    )(x2d, p["g_mix"], p["w_gates"], ya, yb, yc, p["w_br_ssm"], p["w_br_swa"], p["w_br_mem"], p["w_o"],
      p["g_ffn"], p["w_r_hi"], p["w_r_lo"], p["b_r"])


def _moe_kernel(x1_ref, hn_ref, comb_ref, win_ref, wdn_ref, o_ref):
    e = pl.program_id(1)

    @pl.when(e == 0)
    def _():
        o_ref[...] = x1_ref[...]

    w_in = win_ref[0]
    w_dn = wdn_ref[0]
    n_sub = o_ref.shape[0] // MOE_SUB

    def sub(i, _):
        r0 = pl.multiple_of(i * MOE_SUB, SUBLANES)
        rs = pl.ds(r0, MOE_SUB)
        comb = comb_ref[rs, :]
        lane = lax.broadcasted_iota(jnp.int32, comb.shape, 1)
        c = jnp.sum(jnp.where(lane == e + N_GROUPS_E, comb, 0.0), axis=-1, keepdims=True)
        gu = jnp.dot(hn_ref[rs, :], w_in, preferred_element_type=F32)
        a = jax.nn.silu(gu[:, :D_FF]) * gu[:, D_FF:]
        o_ref[rs, :] += c * jnp.dot(a.astype(BF16), w_dn, preferred_element_type=F32)
        return 0

    lax.fori_loop(0, n_sub, sub, 0)


def _moe(x1, hn, comb, p):
    t = x1.shape[0]
    tm = TM_MOE if t % TM_MOE == 0 else t
    row = lambda i, e: (i, 0)
    wsel = lambda i, e: (e, 0, 0)
    return pl.pallas_call(
        _moe_kernel,
        grid=(t // tm, N_EXPERTS),
        in_specs=[
            pl.BlockSpec((tm, D_MODEL), row),
            pl.BlockSpec((tm, D_MODEL), row),
            pl.BlockSpec((tm, LANES), row),
            pl.BlockSpec((1, D_MODEL, 2 * D_FF), wsel),
            pl.BlockSpec((1, D_FF, D_MODEL), wsel),
        ],
        out_specs=pl.BlockSpec((tm, D_MODEL), row),
        out_shape=jax.ShapeDtypeStruct((t, D_MODEL), F32),
        compiler_params=_cparams(("parallel", "arbitrary")),
        name="moe",
    )(x1, hn, comb, p["w_exp_in"], p["w_exp_down"])


def _rope_tables(pos):
    half = SWA_HD // 2
    inv = ROPE_THETA ** (-jnp.arange(half, dtype=F32) / half)
    ang = pos.astype(F32)[:, None] * inv[None, :]
    cos = jnp.cos(ang)
    sin = jnp.sin(ang)
    cos = jnp.concatenate([cos, cos, cos, cos], axis=1)
    sin = jnp.concatenate([-sin, sin, -sin, sin], axis=1)
    return cos, sin


def kernel(x_prompt, x_sample, mem_prompt, state_ssm_re, state_ssm_im, cache_swa_k, cache_swa_v, cache_mem_k, cache_mem_v, norm_mix, w_in, ssm_a_re, ssm_a_im, ssm_log_dt, ssm_b_re, ssm_b_im, ssm_c_re, ssm_c_im, ssm_d, w_glu, swa_q_norm, swa_k_norm, swa_sinks, norm_mem, w_mem_kv, mem_q_norm, mem_k_norm, w_br_ssm, w_br_swa, w_br_mem, w_o, norm_ffn, w_router_group, b_router_group, w_router_expert, b_router_expert, w_exp_in, w_exp_down):
    depth = w_in.shape[0]
    assert depth == 1
    nb, seq, _ = x_prompt.shape
    db, dseq, _ = x_sample.shape
    assert dseq == SUBLANES
    l = 0

    w_r = jnp.concatenate([w_router_group[l], w_router_expert[l]], axis=1)
    w_r = jnp.pad(w_r, ((0, 0), (0, LANES - w_r.shape[1])))
    w_r_hi = w_r.astype(BF16)
    b_r = jnp.pad(jnp.concatenate([b_router_group[l], b_router_expert[l]]), (0, LANES - N_GROUPS_E - N_EXPERTS))
    p = dict(
        g_mix=norm_mix[l][None], w_a=w_in[l][:, :PROJ_A].astype(BF16), w_gates=w_in[l][:, PROJ_A:].astype(BF16),
        g_q=jnp.tile(swa_q_norm[l], SWA_HEADS)[None], g_k=jnp.tile(swa_k_norm[l], SWA_KV_HEADS)[None],
        g_qm=jnp.tile(mem_q_norm[l], MEM_HEADS)[None], g_km=jnp.tile(mem_k_norm[l], MEM_HEADS)[None],
        ones64=_block_ones(SWA_WIDTH, SWA_HD), ones128=_block_ones(MEM_WIDTH, MEM_HD),
        ssm_d=ssm_d[l][None], w_glu=w_glu[l].astype(BF16),
        g_mem=norm_mem[l][None], w_mem_kv=w_mem_kv[l].astype(BF16),
        w_br_ssm=w_br_ssm[l].astype(BF16), w_br_swa=w_br_swa[l].astype(BF16), w_br_mem=w_br_mem[l].astype(BF16),
        w_o=w_o[l].astype(BF16), g_ffn=norm_ffn[l][None],
        w_r_hi=w_r_hi, w_r_lo=(w_r - w_r_hi.astype(F32)).astype(BF16), b_r=b_r[None],
        w_exp_in=w_exp_in[l].astype(BF16), w_exp_down=w_exp_down[l].astype(BF16),
    )
    p.update(_ssm_params(ssm_a_re[l], ssm_a_im[l], ssm_log_dt[l], ssm_b_re[l], ssm_b_im[l],
                         ssm_c_re[l], ssm_c_im[l]))
    sinks = swa_sinks[l]

    xp = x_prompt.reshape(nb * seq, D_MODEL)
    cos_p, sin_p = _rope_tables(jnp.arange(seq, dtype=jnp.int32))
    u, q, k, v, qm = _inproj(xp, cos_p, sin_p, seq // TM_IN, p)
    zeros_state = jnp.zeros((nb, 1, SSM_CH), F32)
    y_ssm, pr, pi = _ssm(u, zeros_state, zeros_state, p, n_seq=nb, chained=True)
    mk, mv = _memkv(mem_prompt, p)
    y_swa, y_mem = _attn_prompt(q, k, v, qm, mk, mv, sinks, nb)
    x1, hn, comb = _merge(xp, y_ssm, y_swa, y_mem, p)
    yp = _moe(x1, hn, comb, p).reshape(nb, seq, D_MODEL)
    win = min(WINDOW, seq)
    p_k = k.reshape(nb, seq, SWA_KV_HEADS, SWA_HD)[:, seq - win:]
    p_v = v.reshape(nb, seq, SWA_KV_HEADS, SWA_HD)[:, seq - win:]

    xs = x_sample.reshape(db * dseq, D_MODEL)
    cos_s, sin_s = _rope_tables(PAST_LEN + jnp.arange(dseq, dtype=jnp.int32))
    reps = TM_IN // dseq
    us, qs, ks, vs, qms = _inproj(xs, jnp.tile(cos_s, (reps, 1)), jnp.tile(sin_s, (reps, 1)), 1, p)
    ys_ssm, sr, si = _ssm(us, state_ssm_re[l].reshape(db, SSM_CH), state_ssm_im[l].reshape(db, SSM_CH), p,
                          n_seq=db, chained=False)
    wb = cache_swa_k.shape[2]
    ys_swa, ys_mem, s_k, s_v = _attn_sample(
        qs, ks, vs, cache_swa_k[l].reshape(db, wb, SWA_KV_WIDTH), cache_swa_v[l].reshape(db, wb, SWA_KV_WIDTH),
        qms, cache_mem_k[l].reshape(db, N_MEM, MEM_WIDTH), cache_mem_v[l].reshape(db, N_MEM, MEM_WIDTH),
        sinks, dseq)
    x1s, hns, combs = _merge(xs, ys_ssm, ys_swa, ys_mem, p)
    ys = _moe(x1s, hns, combs, p).reshape(db, dseq, D_MODEL)

    g, s = SSM_GROUPS, SSM_STATE
    return (yp, ys,
            pr.reshape(1, nb, g, s), pi.reshape(1, nb, g, s),
            p_k[None], p_v[None],
            mk.reshape(1, nb, N_MEM, MEM_HEADS, MEM_HD), mv.reshape(1, nb, N_MEM, MEM_HEADS, MEM_HD),
            sr.reshape(1, db, g, s), si.reshape(1, db, g, s),
            s_k.reshape(1, db, wb, SWA_KV_HEADS, SWA_HD), s_v.reshape(1, db, wb, SWA_KV_HEADS, SWA_HD))
```

```python
import functools
import math

import jax
import jax.numpy as jnp
from jax import lax
from jax.experimental import pallas as pl
from jax.experimental.pallas import tpu as pltpu

F32 = jnp.float32
BF16 = jnp.bfloat16

D_MODEL = 1024
SSM_WIDTH = 512
SSM_GROUP = 16
SSM_GROUPS = 32
SSM_STATE = 64
SSM_CH = SSM_GROUPS * SSM_STATE
SWA_HEADS = 8
SWA_KV_HEADS = 2
SWA_Q_PER_KV = SWA_HEADS // SWA_KV_HEADS
SWA_HD = 64
SWA_WIDTH = SWA_HEADS * SWA_HD
SWA_KV_WIDTH = SWA_KV_HEADS * SWA_HD
WINDOW = 128
PAST_LEN = 16384
ROPE_THETA = 10000.0
N_MEM = 256
MEM_HEADS = 4
MEM_HD = 128
MEM_WIDTH = MEM_HEADS * MEM_HD
N_BRANCH = 3
PROJ_A = SSM_WIDTH + SWA_WIDTH + 2 * SWA_KV_WIDTH + MEM_WIDTH
N_GROUPS_E = 4
EXPERTS_PER_GROUP = 8
N_EXPERTS = 32
D_FF = 256
EPS = 1e-6
NEG_INF = -1e30
SWA_SCALE = SWA_HD ** -0.5
MEM_SCALE = MEM_HD ** -0.5

LANES = 128
SUBLANES = 8
VMEM_LIMIT = 56 * 1024 * 1024

TM_IN = 512
TC_SSM = 512
SCAN_W = 512
TQ_ATT = 512
SEQ_BLK = 8
TM_MRG = 256
TM_MOE = 2048
MOE_SUB = 512


def _cparams(sem):
    return pltpu.CompilerParams(dimension_semantics=sem, vmem_limit_bytes=VMEM_LIMIT)


def _full(shape):
    nd = len(shape)
    return pl.BlockSpec(shape, lambda *_: (0,) * nd)


def _split_bf16(x):
    hi = x.astype(BF16)
    lo = (x - hi.astype(F32)).astype(BF16)
    return hi, lo


def _seg_mean_sq(x, ones_blk, width):
    hi, lo = _split_bf16(x * x)
    s = jnp.dot(hi, ones_blk, preferred_element_type=F32) + jnp.dot(lo, ones_blk, preferred_element_type=F32)
    return s * (1.0 / width)


def _rms_rows(x, gain):
    return x * lax.rsqrt(jnp.mean(x * x, axis=-1, keepdims=True) + EPS) * gain


def _block_ones(n, width):
    i = jnp.arange(n) // width
    return (i[:, None] == i[None, :]).astype(BF16)


def _rope_cols(x, cos, sin_signed, lane_in_head):
    n = x.shape[1]
    reps = n // LANES
    if reps > 1:
        cos = jnp.concatenate([cos] * reps, axis=1)
        sin_signed = jnp.concatenate([sin_signed] * reps, axis=1)
    half = SWA_HD // 2
    partner = jnp.where(lane_in_head < half, pltpu.roll(x, n - half, axis=1), pltpu.roll(x, half, axis=1))
    return x * cos + partner * sin_signed


def _inproj_kernel(x_ref, gmix_ref, w_ref, gq_ref, gk_ref, gm_ref, cos_ref, sin_ref, o64_ref, o128_ref,
                   u_ref, q_ref, k_ref, v_ref, qm_ref):
    x = x_ref[...]
    h = _rms_rows(x, gmix_ref[...]).astype(BF16)
    proj = jnp.dot(h, w_ref[...], preferred_element_type=F32)
    c0 = SSM_WIDTH
    c1 = c0 + SWA_WIDTH
    c2 = c1 + SWA_KV_WIDTH
    c3 = c2 + SWA_KV_WIDTH
    u_ref[...] = proj[:, :c0]
    q = proj[:, c0:c1]
    k = proj[:, c1:c2]
    v_ref[...] = proj[:, c2:c3]
    qm = proj[:, c3:]
    cos = cos_ref[...]
    sin = sin_ref[...]
    o64 = o64_ref[...]
    lane_q = lax.broadcasted_iota(jnp.int32, q.shape, 1) % SWA_HD
    qn = q * lax.rsqrt(_seg_mean_sq(q, o64, SWA_HD) + EPS) * gq_ref[...]
    q_ref[...] = (_rope_cols(qn, cos, sin, lane_q) * SWA_SCALE).astype(BF16)
    lane_k = lax.broadcasted_iota(jnp.int32, k.shape, 1) % SWA_HD
    kn = k * lax.rsqrt(_seg_mean_sq(k, o64[:SWA_KV_WIDTH, :SWA_KV_WIDTH], SWA_HD) + EPS) * gk_ref[...]
    k_ref[...] = _rope_cols(kn, cos, sin, lane_k)
    qmn = qm * lax.rsqrt(_seg_mean_sq(qm, o128_ref[...], MEM_HD) + EPS) * gm_ref[...]
    qm_ref[...] = qmn.astype(BF16)


def _inproj(x2d, cos, sin, pos_blocks, p):
    t = x2d.shape[0]
    tm = TM_IN
    grid = (t // tm,)
    row = lambda i: (i, 0)
    tab = lambda i: (i % pos_blocks, 0)
    out_shape = (
        jax.ShapeDtypeStruct((t, SSM_WIDTH), F32),
        jax.ShapeDtypeStruct((t, SWA_WIDTH), BF16),
        jax.ShapeDtypeStruct((t, SWA_KV_WIDTH), F32),
        jax.ShapeDtypeStruct((t, SWA_KV_WIDTH), F32),
        jax.ShapeDtypeStruct((t, MEM_WIDTH), BF16),
    )
    return pl.pallas_call(
        _inproj_kernel,
        grid=grid,
        in_specs=[
            pl.BlockSpec((tm, D_MODEL), row),
            _full((1, D_MODEL)),
            _full((D_MODEL, PROJ_A)),
            _full((1, SWA_WIDTH)),
            _full((1, SWA_KV_WIDTH)),
            _full((1, MEM_WIDTH)),
            pl.BlockSpec((tm, LANES), tab),
            pl.BlockSpec((tm, LANES), tab),
            _full((SWA_WIDTH, SWA_WIDTH)),
            _full((MEM_WIDTH, MEM_WIDTH)),
        ],
        out_specs=(
            pl.BlockSpec((tm, SSM_WIDTH), row),
            pl.BlockSpec((tm, SWA_WIDTH), row),
            pl.BlockSpec((tm, SWA_KV_WIDTH), row),
            pl.BlockSpec((tm, SWA_KV_WIDTH), row),
            pl.BlockSpec((tm, MEM_WIDTH), row),
        ),
        out_shape=out_shape,
        compiler_params=_cparams(("parallel",)),
        name="inproj",
    )(x2d, p["g_mix"], p["w_a"], p["g_q"], p["g_k"], p["g_qm"], cos, sin, p["ones64"], p["ones128"])


def _ssm_kernel(u_ref, s0r_ref, s0i_ref, bre_ref, bim_ref, cre_ref, cim_ref, d_ref, wglu_ref,
                lvr_ref, lvi_ref, cpr_ref, cpi_ref,
                y_ref, fr_ref, fi_ref, sr_ref, si_ref, car_ref, cai_ref, *, chained):
    tc = u_ref.shape[0]
    half_u = SSM_WIDTH // 2
    half_c = SSM_CH // 2
    u = u_ref[...]
    ub = u.astype(BF16)
    for hh in range(2):
        us = ub[:, hh * half_u:(hh + 1) * half_u]
        cs = slice(hh * half_c, (hh + 1) * half_c)
        sr_ref[:, cs] = jnp.dot(us, bre_ref[hh], preferred_element_type=F32)
        si_ref[:, cs] = jnp.dot(us, bim_ref[hh], preferred_element_type=F32)

    if chained:
        @pl.when(pl.program_id(1) == 0)
        def _():
            car_ref[...] = s0r_ref[0]
            cai_ref[...] = s0i_ref[0]

    n_tiles = tc // SUBLANES
    for sl in range(SSM_CH // SCAN_W):
        cols = slice(sl * SCAN_W, (sl + 1) * SCAN_W)
        lv = [(lvr_ref[j, :, cols], lvi_ref[j, :, cols]) for j in range(3)]
        cpr = cpr_ref[:, cols]
        cpi = cpi_ref[:, cols]

        def tile(i, carry, cols=cols, lv=lv, cpr=cpr, cpi=cpi):
            r0 = pl.multiple_of(i * SUBLANES, SUBLANES)
            xr = sr_ref[pl.ds(r0, SUBLANES), cols]
            xi = si_ref[pl.ds(r0, SUBLANES), cols]
            for j, d in enumerate((1, 2, 4)):
                pr, pi = lv[j]
                shr = pltpu.roll(xr, d, axis=0)
                shi = pltpu.roll(xi, d, axis=0)
                xr, xi = xr + pr * shr - pi * shi, xi + pr * shi + pi * shr
            if chained:
                c_r, c_i = carry
            else:
                c_r = s0r_ref[pl.ds(i, 1), cols]
                c_i = s0i_ref[pl.ds(i, 1), cols]
            cb_r = jnp.broadcast_to(c_r, xr.shape)
            cb_i = jnp.broadcast_to(c_i, xr.shape)
            xr, xi = xr + cpr * cb_r - cpi * cb_i, xi + cpr * cb_i + cpi * cb_r
            sr_ref[pl.ds(r0, SUBLANES), cols] = xr
            si_ref[pl.ds(r0, SUBLANES), cols] = xi
            last_r = xr[SUBLANES - 1:SUBLANES, :]
            last_i = xi[SUBLANES - 1:SUBLANES, :]
            if chained:
                return last_r, last_i
            fr_ref[pl.ds(i, 1), cols] = last_r
            fi_ref[pl.ds(i, 1), cols] = last_i
            return carry

        if chained:
            init = (car_ref[:, cols], cai_ref[:, cols])
        else:
            init = (jnp.zeros((1, SCAN_W), F32), jnp.zeros((1, SCAN_W), F32))
        c_r, c_i = lax.fori_loop(0, n_tiles, tile, init)
        if chained:
            car_ref[:, cols] = c_r
            cai_ref[:, cols] = c_i

    if chained:
        fr_ref[0] = car_ref[...]
        fi_ref[0] = cai_ref[...]

    ys = []
    for hh in range(2):
        cs = slice(hh * half_c, (hh + 1) * half_c)
        ys.append(jnp.dot(sr_ref[:, cs].astype(BF16), cre_ref[hh], preferred_element_type=F32)
                  - jnp.dot(si_ref[:, cs].astype(BF16), cim_ref[hh], preferred_element_type=F32))
    y = jnp.concatenate(ys, axis=1) + d_ref[...] * u
    y = jax.nn.gelu(y)
    gate = jax.nn.sigmoid(jnp.dot(y.astype(BF16), wglu_ref[...], preferred_element_type=F32))
    y_ref[...] = (y * gate).astype(BF16)


def _ssm(u, s0r, s0i, p, *, n_seq, chained):
    t = u.shape[0]
    tc = TC_SSM
    if chained:
        per = t // n_seq // tc
        grid = (n_seq, per)
        row = lambda n, c: (n * per + c, 0)
        st = lambda n, c: (n, 0, 0)
        s0_spec = pl.BlockSpec((1, 1, SSM_CH), st)
        f_spec = pl.BlockSpec((1, 1, SSM_CH), st)
        f_shape = jax.ShapeDtypeStruct((n_seq, 1, SSM_CH), F32)
        sem = ("parallel", "arbitrary")
    else:
        grid = (t // tc,)
        row = lambda c: (c, 0)
        s0_spec = pl.BlockSpec((tc // SUBLANES, SSM_CH), row)
        f_spec = pl.BlockSpec((tc // SUBLANES, SSM_CH), row)
        f_shape = jax.ShapeDtypeStruct((t // SUBLANES, SSM_CH), F32)
        sem = ("parallel",)
    hu = SSM_WIDTH // 2
    hc = SSM_CH // 2
    return pl.pallas_call(
        functools.partial(_ssm_kernel, chained=chained),
        grid=grid,
        in_specs=[
            pl.BlockSpec((tc, SSM_WIDTH), row),
            s0_spec, s0_spec,
            _full((2, hu, hc)), _full((2, hu, hc)),
            _full((2, hc, hu)), _full((2, hc, hu)),
            _full((1, SSM_WIDTH)),
            _full((SSM_WIDTH, SSM_WIDTH)),
            _full((3, SUBLANES, SSM_CH)), _full((3, SUBLANES, SSM_CH)),
            _full((SUBLANES, SSM_CH)), _full((SUBLANES, SSM_CH)),
        ],
        out_specs=(pl.BlockSpec((tc, SSM_WIDTH), row), f_spec, f_spec),
        out_shape=(jax.ShapeDtypeStruct((t, SSM_WIDTH), BF16), f_shape, f_shape),
        scratch_shapes=[
            pltpu.VMEM((tc, SSM_CH), F32), pltpu.VMEM((tc, SSM_CH), F32),
            pltpu.VMEM((1, SSM_CH), F32), pltpu.VMEM((1, SSM_CH), F32),
        ],
        compiler_params=_cparams(sem),
        name="ssm_chained" if chained else "ssm_tiles",
    )(u, s0r, s0i, p["b_re"], p["b_im"], p["c_re"], p["c_im"], p["ssm_d"], p["w_glu"],
      p["lv_re"], p["lv_im"], p["cp_re"], p["cp_im"])


def _ssm_params(a_re, a_im, log_dt, b_re, b_im, c_re, c_im):
    dt = jnp.exp(log_dt)[:, None]
    mag = jnp.exp(a_re * dt)
    abr = mag * jnp.cos(a_im * dt)
    abi = mag * jnp.sin(a_im * dt)
    den = a_re * a_re + a_im * a_im
    nr = abr - 1.0
    ni = abi
    coef_re = (nr * a_re + ni * a_im) / den
    coef_im = (ni * a_re - nr * a_im) / den
    bbr = coef_re[..., None] * b_re - coef_im[..., None] * b_im
    bbi = coef_re[..., None] * b_im + coef_im[..., None] * b_re

    half_g = SSM_GROUPS // 2
    eye = jnp.eye(half_g, dtype=F32)

    def b_blocks(bb):
        bb = bb.reshape(2, half_g, SSM_STATE, SSM_GROUP)
        m = jnp.einsum("zgph,gk->zghkp", bb, eye)
        return m.reshape(2, half_g * SSM_GROUP, half_g * SSM_STATE).astype(BF16)

    def c_blocks(cc):
        cc = cc.reshape(2, half_g, SSM_GROUP, SSM_STATE)
        m = jnp.einsum("zghp,gk->zgpkh", cc, eye)
        return m.reshape(2, half_g * SSM_STATE, half_g * SSM_GROUP).astype(BF16)

    ar = abr.reshape(1, SSM_CH)
    ai = abi.reshape(1, SSM_CH)
    pows = [(ar, ai)]
    for _ in range(SUBLANES - 1):
        pr, pi = pows[-1]
        pows.append((pr * ar - pi * ai, pr * ai + pi * ar))
    rows = jnp.arange(SUBLANES)[:, None]
    lv_re = jnp.stack([jnp.where(rows >= d, pows[d - 1][0], 0.0) for d in (1, 2, 4)])
    lv_im = jnp.stack([jnp.where(rows >= d, pows[d - 1][1], 0.0) for d in (1, 2, 4)])
    cp_re = jnp.concatenate([pw[0] for pw in pows], axis=0)
    cp_im = jnp.concatenate([pw[1] for pw in pows], axis=0)
    return dict(b_re=b_blocks(bbr), b_im=b_blocks(bbi), c_re=c_blocks(c_re), c_im=c_blocks(c_im),
                lv_re=lv_re, lv_im=lv_im, cp_re=cp_re, cp_im=cp_im)


def _memkv_kernel(m_ref, g_ref, w_ref, gk_ref, o128_ref, k_ref, v_ref):
    hm = _rms_rows(m_ref[0], g_ref[...]).astype(BF16)
    kv = jnp.dot(hm, w_ref[...], preferred_element_type=F32)
    k = kv[:, :MEM_WIDTH]
    k_ref[0] = k * lax.rsqrt(_seg_mean_sq(k, o128_ref[...], MEM_HD) + EPS) * gk_ref[...]
    v_ref[0] = kv[:, MEM_WIDTH:]


def _memkv(mem, p):
    n = mem.shape[0]
    blk = lambda i: (i, 0, 0)
    shp = jax.ShapeDtypeStruct((n, N_MEM, MEM_WIDTH), F32)
    return pl.pallas_call(
        _memkv_kernel,
        grid=(n,),
        in_specs=[pl.BlockSpec((1, N_MEM, D_MODEL), blk), _full((1, D_MODEL)),
                  _full((D_MODEL, 2 * MEM_WIDTH)), _full((1, MEM_WIDTH)), _full((MEM_WIDTH, MEM_WIDTH))],
        out_specs=(pl.BlockSpec((1, N_MEM, MEM_WIDTH), blk), pl.BlockSpec((1, N_MEM, MEM_WIDTH), blk)),
        out_shape=(shp, shp),
        compiler_params=_cparams(("parallel",)),
        name="memkv",
    )(mem, p["g_mem"], p["w_mem_kv"], p["g_km"], p["ones128"])


def _dup_heads(x, lane):
    sw = pltpu.roll(x, SWA_HD, axis=x.ndim - 1)
    lo = lane < SWA_HD
    return jnp.where(lo, x, sw), jnp.where(lo, sw, x)


def _swa_group(q_blk, kk, vv, g, mask, sink_ref):
    tq = q_blk.shape[-2]
    shp = q_blk.shape[:-1]
    lane = lax.broadcasted_iota(jnp.int32, shp + (LANES,), len(shp))
    rows = []
    sinks = []
    for hl in range(SWA_Q_PER_KV):
        h = g * SWA_Q_PER_KV + hl
        pair = q_blk[..., (h // 2) * LANES:(h // 2 + 1) * LANES]
        keep = (lane < SWA_HD) if h % 2 == 0 else (lane >= SWA_HD)
        rows.append(jnp.where(keep, pair, 0.0))
        sinks.append(jnp.full(shp + (1,), sink_ref[h], F32))
    qq = jnp.concatenate(rows, axis=-2).astype(BF16)
    sk = jnp.concatenate(sinks, axis=-2)
    s = jnp.einsum("...qd,...kd->...qk", qq, kk, preferred_element_type=F32)
    s = jnp.where(mask, s, NEG_INF)
    m = jnp.maximum(jnp.max(s, axis=-1, keepdims=True), sk)
    e = jnp.exp(s - m)
    pr = e / (jnp.sum(e, axis=-1, keepdims=True) + jnp.exp(sk - m))
    o = jnp.einsum("...qk,...kd->...qd", pr.astype(BF16), vv, preferred_element_type=F32)
    lo = lane < SWA_HD
    return [jnp.where(lo, o[..., (2 * j) * tq:(2 * j + 1) * tq, :], o[..., (2 * j + 1) * tq:(2 * j + 2) * tq, :])
            for j in range(2)]


def _mem_heads(qm, k_head, v_head):
    outs = []
    for h in range(MEM_HEADS):
        cs = slice(h * MEM_HD, (h + 1) * MEM_HD)
        s = jnp.einsum("...qd,...kd->...qk", qm[..., cs], k_head(h).astype(BF16),
                       preferred_element_type=F32) * MEM_SCALE
        m = jnp.max(s, axis=-1, keepdims=True)
        e = jnp.exp(s - m)
        pr = e / jnp.sum(e, axis=-1, keepdims=True)
        outs.append(jnp.einsum("...qk,...kd->...qd", pr.astype(BF16), v_head(h).astype(BF16),
                               preferred_element_type=F32))
    return jnp.concatenate(outs, axis=-1)


def _attn_prompt_kernel(sink_ref, q_ref, k_ref, v_ref, kp_ref, vp_ref, qm_ref, mk_ref, mv_ref, ys_ref, ym_ref):
    tq = q_ref.shape[0]
    blk = WINDOW
    rows = SWA_Q_PER_KV * blk
    i = lax.broadcasted_iota(jnp.int32, (rows, 2 * blk), 0) % blk
    j = lax.broadcasted_iota(jnp.int32, (rows, 2 * blk), 1)
    lo = jnp.where(j < blk, i + 1, blk)
    hi = jnp.where(j < blk, blk, blk + i + 1)
    first_lo = jnp.where(pl.program_id(1) == 0, blk, 0)
    lane_k = lax.broadcasted_iota(jnp.int32, (2 * blk, LANES), 1)
    for b in range(tq // blk):
        rs = slice(b * blk, (b + 1) * blk)
        if b == 0:
            k2 = jnp.concatenate([kp_ref[...], k_ref[rs, :]], axis=0)
            v2 = jnp.concatenate([vp_ref[...], v_ref[rs, :]], axis=0)
            mask = (j >= jnp.maximum(lo, first_lo)) & (j < hi)
        else:
            k2 = k_ref[(b - 1) * blk:(b + 1) * blk, :]
            v2 = v_ref[(b - 1) * blk:(b + 1) * blk, :]
            mask = (j >= lo) & (j < hi)
        kks = _dup_heads(k2, lane_k)
        vvs = _dup_heads(v2, lane_k)
        q_blk = q_ref[rs, :].astype(F32)
        pairs = []
        for g in range(SWA_KV_HEADS):
            pairs += _swa_group(q_blk, kks[g].astype(BF16), vvs[g].astype(BF16), g, mask, sink_ref)
        ys_ref[rs, :] = jnp.concatenate(pairs, axis=1).astype(BF16)
    ym_ref[...] = _mem_heads(qm_ref[...], lambda h: mk_ref[0, :, h * MEM_HD:(h + 1) * MEM_HD],
                             lambda h: mv_ref[0, :, h * MEM_HD:(h + 1) * MEM_HD]).astype(BF16)


def _attn_prompt(q, k, v, qm, mk, mv, sinks, n_seq):
    t = q.shape[0]
    tq = TQ_ATT
    per = t // n_seq // tq
    sub = tq // WINDOW
    row = lambda n, c: (n * per + c, 0)
    prev = lambda n, c: (jnp.maximum((n * per + c) * sub - 1, 0), 0)
    memb = lambda n, c: (n, 0, 0)
    return pl.pallas_call(
        _attn_prompt_kernel,
        grid=(n_seq, per),
        in_specs=[
            pl.BlockSpec(memory_space=pltpu.SMEM),
            pl.BlockSpec((tq, SWA_WIDTH), row),
            pl.BlockSpec((tq, SWA_KV_WIDTH), row),
            pl.BlockSpec((tq, SWA_KV_WIDTH), row),
            pl.BlockSpec((WINDOW, SWA_KV_WIDTH), prev),
            pl.BlockSpec((WINDOW, SWA_KV_WIDTH), prev),
            pl.BlockSpec((tq, MEM_WIDTH), row),
            pl.BlockSpec((1, N_MEM, MEM_WIDTH), memb),
            pl.BlockSpec((1, N_MEM, MEM_WIDTH), memb),
        ],
        out_specs=(pl.BlockSpec((tq, SWA_WIDTH), row), pl.BlockSpec((tq, MEM_WIDTH), row)),
        out_shape=(jax.ShapeDtypeStruct((t, SWA_WIDTH), BF16), jax.ShapeDtypeStruct((t, MEM_WIDTH), BF16)),
        compiler_params=_cparams(("parallel", "parallel")),
        name="attn_prompt",
    )(sinks, q, k, v, k, v, qm, mk, mv)


def _attn_sample_kernel(sink_ref, q_ref, k_ref, v_ref, pk_ref, pv_ref, qm_ref, mk_ref, mv_ref,
                        ys_ref, ym_ref, nk_ref, nv_ref, *, s_len):
    sb, wb = pk_ref.shape[0], pk_ref.shape[1]
    n_keys = wb + s_len
    rows = SWA_Q_PER_KV * s_len
    i = lax.broadcasted_iota(jnp.int32, (sb, rows, n_keys), 1) % s_len
    j = lax.broadcasted_iota(jnp.int32, (sb, rows, n_keys), 2)
    rel = i + wb - j
    mask = (rel >= 0) & (rel < WINDOW)
    k_all = jnp.concatenate([pk_ref[...], k_ref[...].reshape(sb, s_len, SWA_KV_WIDTH)], axis=1)
    v_all = jnp.concatenate([pv_ref[...], v_ref[...].reshape(sb, s_len, SWA_KV_WIDTH)], axis=1)
    nk_ref[...] = k_all[:, n_keys - wb:, :]
    nv_ref[...] = v_all[:, n_keys - wb:, :]
    lane_k = lax.broadcasted_iota(jnp.int32, k_all.shape, 2)
    kks = _dup_heads(k_all, lane_k)
    vvs = _dup_heads(v_all, lane_k)
    q3 = q_ref[...].astype(F32).reshape(sb, s_len, SWA_WIDTH)
    pairs = []
    for g in range(SWA_KV_HEADS):
        pairs += _swa_group(q3, kks[g].astype(BF16), vvs[g].astype(BF16), g, mask, sink_ref)
    ys_ref[...] = jnp.concatenate(pairs, axis=-1).reshape(sb * s_len, SWA_WIDTH).astype(BF16)
    qm3 = qm_ref[...].astype(F32).reshape(sb, s_len, MEM_WIDTH).astype(BF16)
    head_rows = lambda h: pl.ds(h, N_MEM, stride=MEM_HEADS)
    ym = _mem_heads(qm3, lambda h: mk_ref[:, head_rows(h), :], lambda h: mv_ref[:, head_rows(h), :])
    ym_ref[...] = ym.reshape(sb * s_len, MEM_WIDTH).astype(BF16)


def _attn_sample(q, k, v, past_k, past_v, qm, mk, mv, sinks, s_len):
    t = q.shape[0]
    n_seq, wb = past_k.shape[0], past_k.shape[1]
    sb = SEQ_BLK
    rows = sb * s_len
    row = lambda c: (c, 0)
    seq = lambda c: (c, 0, 0)
    cache_shape = jax.ShapeDtypeStruct((n_seq, wb, SWA_KV_WIDTH), F32)
    return pl.pallas_call(
        functools.partial(_attn_sample_kernel, s_len=s_len),
        grid=(n_seq // sb,),
        in_specs=[
            pl.BlockSpec(memory_space=pltpu.SMEM),
            pl.BlockSpec((rows, SWA_WIDTH), row),
            pl.BlockSpec((rows, SWA_KV_WIDTH), row),
            pl.BlockSpec((rows, SWA_KV_WIDTH), row),
            pl.BlockSpec((sb, wb, SWA_KV_WIDTH), seq),
            pl.BlockSpec((sb, wb, SWA_KV_WIDTH), seq),
            pl.BlockSpec((rows, MEM_WIDTH), row),
            pl.BlockSpec((sb, N_MEM * MEM_HEADS, MEM_HD), seq),
            pl.BlockSpec((sb, N_MEM * MEM_HEADS, MEM_HD), seq),
        ],
        out_specs=(pl.BlockSpec((rows, SWA_WIDTH), row), pl.BlockSpec((rows, MEM_WIDTH), row),
                   pl.BlockSpec((sb, wb, SWA_KV_WIDTH), seq), pl.BlockSpec((sb, wb, SWA_KV_WIDTH), seq)),
        out_shape=(jax.ShapeDtypeStruct((t, SWA_WIDTH), BF16), jax.ShapeDtypeStruct((t, MEM_WIDTH), BF16),
                   cache_shape, cache_shape),
        compiler_params=_cparams(("parallel",)),
        name="attn_sample",
    )(sinks, q, k, v, past_k, past_v, qm, mk, mv)


def _first_argmax(x, valid, lane):
    xm = jnp.where(valid, x, -jnp.inf)
    mx = jnp.max(xm, axis=-1, keepdims=True)
    idx = jnp.min(jnp.where(xm == mx, lane, LANES), axis=-1, keepdims=True)
    return mx, lane == idx, idx


def _merge_kernel(x_ref, gmix_ref, wg_ref, ya_ref, yb_ref, yc_ref, wa_ref, wb_ref, wc_ref, wo_ref, gffn_ref,
                  wr_hi_ref, wr_lo_ref, br_ref, x1_ref, hn_ref, comb_ref):
    x = x_ref[...]
    h = _rms_rows(x, gmix_ref[...]).astype(BF16)
    gates = jax.nn.sigmoid(jnp.dot(h, wg_ref[...], preferred_element_type=F32))
    merged = (gates[:, :D_MODEL] * jnp.dot(ya_ref[...], wa_ref[...], preferred_element_type=F32)
              + gates[:, D_MODEL:2 * D_MODEL] * jnp.dot(yb_ref[...], wb_ref[...], preferred_element_type=F32)
              + gates[:, 2 * D_MODEL:] * jnp.dot(yc_ref[...], wc_ref[...], preferred_element_type=F32))
    x1 = x + jnp.dot(merged.astype(BF16), wo_ref[...], preferred_element_type=F32)
    x1_ref[...] = x1
    hn = _rms_rows(x1, gffn_ref[...])
    hn_ref[...] = hn.astype(BF16)

    hi, lo = _split_bf16(hn)
    w_hi = wr_hi_ref[...]
    logits = (jnp.dot(hi, w_hi, preferred_element_type=F32) + jnp.dot(lo, w_hi, preferred_element_type=F32)
              + jnp.dot(hi, wr_lo_ref[...], preferred_element_type=F32)) + br_ref[...]
    lane = lax.broadcasted_iota(jnp.int32, logits.shape, 1)
    is_grp = lane < N_GROUPS_E
    g_max, _, g_idx = _first_argmax(logits, is_grp, lane)
    pg_top = 1.0 / jnp.sum(jnp.where(is_grp, jnp.exp(logits - g_max), 0.0), axis=-1, keepdims=True)
    e_lo = N_GROUPS_E + g_idx * EXPERTS_PER_GROUP
    in_grp = (lane >= e_lo) & (lane < e_lo + EXPERTS_PER_GROUP)
    e_max, first, _ = _first_argmax(logits, in_grp, lane)
    ex = jnp.where(in_grp, jnp.exp(logits - e_max), 0.0)
    pe = ex / jnp.sum(ex, axis=-1, keepdims=True)
    _, second, _ = _first_argmax(logits, in_grp & jnp.logical_not(first), lane)
    p1 = jnp.sum(jnp.where(first, pe, 0.0), axis=-1, keepdims=True)
    p2 = jnp.sum(jnp.where(second, pe, 0.0), axis=-1, keepdims=True)
    wts = jnp.where(first | second, pg_top * pe / (p1 + p2), 0.0)
    comb_ref[...] = wts


def _merge(x2d, ya, yb, yc, p):
    t = x2d.shape[0]
    tm = TM_MRG
    row = lambda i: (i, 0)
    return pl.pallas_call(
        _merge_kernel,
        grid=(t // tm,),
        in_specs=[
            pl.BlockSpec((tm, D_MODEL), row),
            _full((1, D_MODEL)),
            _full((D_MODEL, N_BRANCH * D_MODEL)),
            pl.BlockSpec((tm, SSM_WIDTH), row),
            pl.BlockSpec((tm, SWA_WIDTH), row),
            pl.BlockSpec((tm, MEM_WIDTH), row),
            _full((SSM_WIDTH, D_MODEL)), _full((SWA_WIDTH, D_MODEL)), _full((MEM_WIDTH, D_MODEL)),
            _full((D_MODEL, D_MODEL)),
            _full((1, D_MODEL)),
            _full((D_MODEL, LANES)), _full((D_MODEL, LANES)), _full((1, LANES)),
        ],
        out_specs=(pl.BlockSpec((tm, D_MODEL), row), pl.BlockSpec((tm, D_MODEL), row),
                   pl.BlockSpec((tm, LANES), row)),
        out_shape=(jax.ShapeDtypeStruct((t, D_MODEL), F32), jax.ShapeDtypeStruct((t, D_MODEL), BF16),
                   jax.ShapeDtypeStruct((t, LANES), F32)),
        compiler_params=_cparams(("parallel",)),
        name="merge",
    )(x2d, p["g_mix"], p["w_gates"], ya, yb, yc, p["w_br_ssm"], p["w_br_swa"], p["w_br_mem"], p["w_o"],
      p["g_ffn"], p["w_r_hi"], p["w_r_lo"], p["b_r"])


def _moe_kernel(x1_ref, hn_ref, comb_ref, win_ref, wdn_ref, o_ref):
    e = pl.program_id(1)

    @pl.when(e == 0)
    def _():
        o_ref[...] = x1_ref[...]

    w_in = win_ref[0]
    w_dn = wdn_ref[0]
    n_sub = o_ref.shape[0] // MOE_SUB

    def sub(i, _):
        r0 = pl.multiple_of(i * MOE_SUB, SUBLANES)
        rs = pl.ds(r0, MOE_SUB)
        comb = comb_ref[rs, :]
        lane = lax.broadcasted_iota(jnp.int32, comb.shape, 1)
        c = jnp.sum(jnp.where(lane == e + N_GROUPS_E, comb, 0.0), axis=-1, keepdims=True)
        gu = jnp.dot(hn_ref[rs, :], w_in, preferred_element_type=F32)
        a = jax.nn.silu(gu[:, :D_FF]) * gu[:, D_FF:]
        o_ref[rs, :] += c * jnp.dot(a.astype(BF16), w_dn, preferred_element_type=F32)
        return 0

    lax.fori_loop(0, n_sub, sub, 0)


def _moe(x1, hn, comb, p):
    t = x1.shape[0]
    tm = TM_MOE if t % TM_MOE == 0 else t
    row = lambda i, e: (i, 0)
    wsel = lambda i, e: (e, 0, 0)
    return pl.pallas_call(
        _moe_kernel,
        grid=(t // tm, N_EXPERTS),
        in_specs=[
            pl.BlockSpec((tm, D_MODEL), row),
            pl.BlockSpec((tm, D_MODEL), row),
            pl.BlockSpec((tm, LANES), row),
            pl.BlockSpec((1, D_MODEL, 2 * D_FF), wsel),
            pl.BlockSpec((1, D_FF, D_MODEL), wsel),
        ],
        out_specs=pl.BlockSpec((tm, D_MODEL), row),
        out_shape=jax.ShapeDtypeStruct((t, D_MODEL), F32),
        compiler_params=_cparams(("parallel", "arbitrary")),
        name="moe",
    )(x1, hn, comb, p["w_exp_in"], p["w_exp_down"])


def _rope_tables(pos):
    half = SWA_HD // 2
    inv = ROPE_THETA ** (-jnp.arange(half, dtype=F32) / half)
    ang = pos.astype(F32)[:, None] * inv[None, :]
    cos = jnp.cos(ang)
    sin = jnp.sin(ang)
    cos = jnp.concatenate([cos, cos, cos, cos], axis=1)
    sin = jnp.concatenate([-sin, sin, -sin, sin], axis=1)
    return cos, sin


def kernel(x_prompt, x_sample, mem_prompt, state_ssm_re, state_ssm_im, cache_swa_k, cache_swa_v, cache_mem_k, cache_mem_v, norm_mix, w_in, ssm_a_re, ssm_a_im, ssm_log_dt, ssm_b_re, ssm_b_im, ssm_c_re, ssm_c_im, ssm_d, w_glu, swa_q_norm, swa_k_norm, swa_sinks, norm_mem, w_mem_kv, mem_q_norm, mem_k_norm, w_br_ssm, w_br_swa, w_br_mem, w_o, norm_ffn, w_router_group, b_router_group, w_router_expert, b_router_expert, w_exp_in, w_exp_down):
    depth = w_in.shape[0]
    assert depth == 1
    nb, seq, _ = x_prompt.shape
    db, dseq, _ = x_sample.shape
    assert dseq == SUBLANES
    l = 0

    w_r = jnp.concatenate([w_router_group[l], w_router_expert[l]], axis=1)
    w_r = jnp.pad(w_r, ((0, 0), (0, LANES - w_r.shape[1])))
    w_r_hi = w_r.astype(BF16)
    b_r = jnp.pad(jnp.concatenate([b_router_group[l], b_router_expert[l]]), (0, LANES - N_GROUPS_E - N_EXPERTS))
    p = dict(
        g_mix=norm_mix[l][None], w_a=w_in[l][:, :PROJ_A].astype(BF16), w_gates=w_in[l][:, PROJ_A:].astype(BF16),
        g_q=jnp.tile(swa_q_norm[l], SWA_HEADS)[None], g_k=jnp.tile(swa_k_norm[l], SWA_KV_HEADS)[None],
        g_qm=jnp.tile(mem_q_norm[l], MEM_HEADS)[None], g_km=jnp.tile(mem_k_norm[l], MEM_HEADS)[None],
        ones64=_block_ones(SWA_WIDTH, SWA_HD), ones128=_block_ones(MEM_WIDTH, MEM_HD),
        ssm_d=ssm_d[l][None], w_glu=w_glu[l].astype(BF16),
        g_mem=norm_mem[l][None], w_mem_kv=w_mem_kv[l].astype(BF16),
        w_br_ssm=w_br_ssm[l].astype(BF16), w_br_swa=w_br_swa[l].astype(BF16), w_br_mem=w_br_mem[l].astype(BF16),
        w_o=w_o[l].astype(BF16), g_ffn=norm_ffn[l][None],
        w_r_hi=w_r_hi, w_r_lo=(w_r - w_r_hi.astype(F32)).astype(BF16), b_r=b_r[None],
        w_exp_in=w_exp_in[l].astype(BF16), w_exp_down=w_exp_down[l].astype(BF16),
    )
    p.update(_ssm_params(ssm_a_re[l], ssm_a_im[l], ssm_log_dt[l], ssm_b_re[l], ssm_b_im[l],
                         ssm_c_re[l], ssm_c_im[l]))
    sinks = swa_sinks[l]

    xp = x_prompt.reshape(nb * seq, D_MODEL)
    cos_p, sin_p = _rope_tables(jnp.arange(seq, dtype=jnp.int32))
    u, q, k, v, qm = _inproj(xp, cos_p, sin_p, seq // TM_IN, p)
    zeros_state = jnp.zeros((nb, 1, SSM_CH), F32)
    y_ssm, pr, pi = _ssm(u, zeros_state, zeros_state, p, n_seq=nb, chained=True)
    mk, mv = _memkv(mem_prompt, p)
    y_swa, y_mem = _attn_prompt(q, k, v, qm, mk, mv, sinks, nb)
    x1, hn, comb = _merge(xp, y_ssm, y_swa, y_mem, p)
    yp = _moe(x1, hn, comb, p).reshape(nb, seq, D_MODEL)
    win = min(WINDOW, seq)
    p_k = k.reshape(nb, seq, SWA_KV_HEADS, SWA_HD)[:, seq - win:]
    p_v = v.reshape(nb, seq, SWA_KV_HEADS, SWA_HD)[:, seq - win:]

    xs = x_sample.reshape(db * dseq, D_MODEL)
    cos_s, sin_s = _rope_tables(PAST_LEN + jnp.arange(dseq, dtype=jnp.int32))
    reps = TM_IN // dseq
    us, qs, ks, vs, qms = _inproj(xs, jnp.tile(cos_s, (reps, 1)), jnp.tile(sin_s, (reps, 1)), 1, p)
    ys_ssm, sr, si = _ssm(us, state_ssm_re[l].reshape(db, SSM_CH), state_ssm_im[l].reshape(db, SSM_CH), p,
                          n_seq=db, chained=False)
    wb = cache_swa_k.shape[2]
    ys_swa, ys_mem, s_k, s_v = _attn_sample(
        qs, ks, vs, cache_swa_k[l].reshape(db, wb, SWA_KV_WIDTH), cache_swa_v[l].reshape(db, wb, SWA_KV_WIDTH),
        qms, cache_mem_k.reshape(db, N_MEM * MEM_HEADS, MEM_HD), cache_mem_v.reshape(db, N_MEM * MEM_HEADS, MEM_HD),
        sinks, dseq)
    x1s, hns, combs = _merge(xs, ys_ssm, ys_swa, ys_mem, p)
    ys = _moe(x1s, hns, combs, p).reshape(db, dseq, D_MODEL)

    g, s = SSM_GROUPS, SSM_STATE
    return (yp, ys,
            pr.reshape(1, nb, g, s), pi.reshape(1, nb, g, s),
            p_k[None], p_v[None],
            mk.reshape(1, nb, N_MEM, MEM_HEADS, MEM_HD), mv.reshape(1, nb, N_MEM, MEM_HEADS, MEM_HD),
            sr.reshape(1, db, g, s), si.reshape(1, db, g, s),
            s_k.reshape(1, db, wb, SWA_KV_HEADS, SWA_HD), s_v.reshape(1, db, wb, SWA_KV_HEADS, SWA_HD))
```

```python
import functools
import math

import jax
import jax.numpy as jnp
from jax import lax
from jax.experimental import pallas as pl
from jax.experimental.pallas import tpu as pltpu

F32 = jnp.float32
BF16 = jnp.bfloat16

D_MODEL = 1024
SSM_WIDTH = 512
SSM_GROUP = 16
SSM_GROUPS = 32
SSM_STATE = 64
SSM_CH = SSM_GROUPS * SSM_STATE
SWA_HEADS = 8
SWA_KV_HEADS = 2
SWA_Q_PER_KV = SWA_HEADS // SWA_KV_HEADS
SWA_HD = 64
SWA_WIDTH = SWA_HEADS * SWA_HD
SWA_KV_WIDTH = SWA_KV_HEADS * SWA_HD
WINDOW = 128
PAST_LEN = 16384
ROPE_THETA = 10000.0
N_MEM = 256
MEM_HEADS = 4
MEM_HD = 128
MEM_WIDTH = MEM_HEADS * MEM_HD
N_BRANCH = 3
PROJ_A = SSM_WIDTH + SWA_WIDTH + 2 * SWA_KV_WIDTH + MEM_WIDTH
N_GROUPS_E = 4
EXPERTS_PER_GROUP = 8
N_EXPERTS = 32
D_FF = 256
EPS = 1e-6
NEG_INF = -1e30
SWA_SCALE = SWA_HD ** -0.5
MEM_SCALE = MEM_HD ** -0.5

LANES = 128
SUBLANES = 8
VMEM_LIMIT = 56 * 1024 * 1024

TM_IN = 512
TC_SSM = 512
SCAN_W = 512
TQ_ATT = 512
SEQ_BLK = 8
TM_MRG = 256
TM_EXP = 256

ROUTE_W = LANES
HX_W = D_MODEL + ROUTE_W
PAIRS_PER_GROUP = EXPERTS_PER_GROUP * (EXPERTS_PER_GROUP - 1) // 2
N_BUCKETS = N_GROUPS_E * PAIRS_PER_GROUP


def _cparams(sem):
    return pltpu.CompilerParams(dimension_semantics=sem, vmem_limit_bytes=VMEM_LIMIT)


def _full(shape):
    nd = len(shape)
    return pl.BlockSpec(shape, lambda *_: (0,) * nd)


def _split_bf16(x):
    hi = x.astype(BF16)
    lo = (x - hi.astype(F32)).astype(BF16)
    return hi, lo


def _seg_mean_sq(x, ones_blk, width):
    hi, lo = _split_bf16(x * x)
    s = jnp.dot(hi, ones_blk, preferred_element_type=F32) + jnp.dot(lo, ones_blk, preferred_element_type=F32)
    return s * (1.0 / width)


def _rms_rows(x, gain):
    return x * lax.rsqrt(jnp.mean(x * x, axis=-1, keepdims=True) + EPS) * gain


def _block_ones(n, width):
    i = jnp.arange(n) // width
    return (i[:, None] == i[None, :]).astype(BF16)


def _rope_cols(x, cos, sin_signed, lane_in_head):
    n = x.shape[1]
    reps = n // LANES
    if reps > 1:
        cos = jnp.concatenate([cos] * reps, axis=1)
        sin_signed = jnp.concatenate([sin_signed] * reps, axis=1)
    half = SWA_HD // 2
    partner = jnp.where(lane_in_head < half, pltpu.roll(x, n - half, axis=1), pltpu.roll(x, half, axis=1))
    return x * cos + partner * sin_signed


def _inproj_kernel(x_ref, gmix_ref, w_ref, gq_ref, gk_ref, gm_ref, cos_ref, sin_ref, o64_ref, o128_ref,
                   u_ref, q_ref, k_ref, v_ref, qm_ref):
    x = x_ref[...]
    h = _rms_rows(x, gmix_ref[...]).astype(BF16)
    proj = jnp.dot(h, w_ref[...], preferred_element_type=F32)
    c0 = SSM_WIDTH
    c1 = c0 + SWA_WIDTH
    c2 = c1 + SWA_KV_WIDTH
    c3 = c2 + SWA_KV_WIDTH
    u_ref[...] = proj[:, :c0]
    q = proj[:, c0:c1]
    k = proj[:, c1:c2]
    v_ref[...] = proj[:, c2:c3]
    qm = proj[:, c3:]
    cos = cos_ref[...]
    sin = sin_ref[...]
    o64 = o64_ref[...]
    lane_q = lax.broadcasted_iota(jnp.int32, q.shape, 1) % SWA_HD
    qn = q * lax.rsqrt(_seg_mean_sq(q, o64, SWA_HD) + EPS) * gq_ref[...]
    q_ref[...] = (_rope_cols(qn, cos, sin, lane_q) * SWA_SCALE).astype(BF16)
    lane_k = lax.broadcasted_iota(jnp.int32, k.shape, 1) % SWA_HD
    kn = k * lax.rsqrt(_seg_mean_sq(k, o64[:SWA_KV_WIDTH, :SWA_KV_WIDTH], SWA_HD) + EPS) * gk_ref[...]
    k_ref[...] = _rope_cols(kn, cos, sin, lane_k)
    qmn = qm * lax.rsqrt(_seg_mean_sq(qm, o128_ref[...], MEM_HD) + EPS) * gm_ref[...]
    qm_ref[...] = qmn.astype(BF16)


def _inproj(x2d, cos, sin, pos_blocks, p):
    t = x2d.shape[0]
    tm = TM_IN
    grid = (t // tm,)
    row = lambda i: (i, 0)
    tab = lambda i: (i % pos_blocks, 0)
    out_shape = (
        jax.ShapeDtypeStruct((t, SSM_WIDTH), F32),
        jax.ShapeDtypeStruct((t, SWA_WIDTH), BF16),
        jax.ShapeDtypeStruct((t, SWA_KV_WIDTH), F32),
        jax.ShapeDtypeStruct((t, SWA_KV_WIDTH), F32),
        jax.ShapeDtypeStruct((t, MEM_WIDTH), BF16),
    )
    return pl.pallas_call(
        _inproj_kernel,
        grid=grid,
        in_specs=[
            pl.BlockSpec((tm, D_MODEL), row),
            _full((1, D_MODEL)),
            _full((D_MODEL, PROJ_A)),
            _full((1, SWA_WIDTH)),
            _full((1, SWA_KV_WIDTH)),
            _full((1, MEM_WIDTH)),
            pl.BlockSpec((tm, LANES), tab),
            pl.BlockSpec((tm, LANES), tab),
            _full((SWA_WIDTH, SWA_WIDTH)),
            _full((MEM_WIDTH, MEM_WIDTH)),
        ],
        out_specs=(
            pl.BlockSpec((tm, SSM_WIDTH), row),
            pl.BlockSpec((tm, SWA_WIDTH), row),
            pl.BlockSpec((tm, SWA_KV_WIDTH), row),
            pl.BlockSpec((tm, SWA_KV_WIDTH), row),
            pl.BlockSpec((tm, MEM_WIDTH), row),
        ),
        out_shape=out_shape,
        compiler_params=_cparams(("parallel",)),
        name="inproj",
    )(x2d, p["g_mix"], p["w_a"], p["g_q"], p["g_k"], p["g_qm"], cos, sin, p["ones64"], p["ones128"])


def _ssm_kernel(u_ref, s0r_ref, s0i_ref, bre_ref, bim_ref, cre_ref, cim_ref, d_ref, wglu_ref,
                lvr_ref, lvi_ref, cpr_ref, cpi_ref,
                y_ref, fr_ref, fi_ref, sr_ref, si_ref, car_ref, cai_ref, *, chained):
    tc = u_ref.shape[0]
    half_u = SSM_WIDTH // 2
    half_c = SSM_CH // 2
    u = u_ref[...]
    ub = u.astype(BF16)
    for hh in range(2):
        us = ub[:, hh * half_u:(hh + 1) * half_u]
        cs = slice(hh * half_c, (hh + 1) * half_c)
        sr_ref[:, cs] = jnp.dot(us, bre_ref[hh], preferred_element_type=F32)
        si_ref[:, cs] = jnp.dot(us, bim_ref[hh], preferred_element_type=F32)

    if chained:
        @pl.when(pl.program_id(1) == 0)
        def _():
            car_ref[...] = s0r_ref[0]
            cai_ref[...] = s0i_ref[0]

    n_tiles = tc // SUBLANES
    for sl in range(SSM_CH // SCAN_W):
        cols = slice(sl * SCAN_W, (sl + 1) * SCAN_W)
        lv = [(lvr_ref[j, :, cols], lvi_ref[j, :, cols]) for j in range(3)]
        cpr = cpr_ref[:, cols]
        cpi = cpi_ref[:, cols]

        def tile(i, carry, cols=cols, lv=lv, cpr=cpr, cpi=cpi):
            r0 = pl.multiple_of(i * SUBLANES, SUBLANES)
            xr = sr_ref[pl.ds(r0, SUBLANES), cols]
            xi = si_ref[pl.ds(r0, SUBLANES), cols]
            for j, d in enumerate((1, 2, 4)):
                pr, pi = lv[j]
                shr = pltpu.roll(xr, d, axis=0)
                shi = pltpu.roll(xi, d, axis=0)
                xr, xi = xr + pr * shr - pi * shi, xi + pr * shi + pi * shr
            if chained:
                c_r, c_i = carry
            else:
                c_r = s0r_ref[pl.ds(i, 1), cols]
                c_i = s0i_ref[pl.ds(i, 1), cols]
            cb_r = jnp.broadcast_to(c_r, xr.shape)
            cb_i = jnp.broadcast_to(c_i, xr.shape)
            xr, xi = xr + cpr * cb_r - cpi * cb_i, xi + cpr * cb_i + cpi * cb_r
            sr_ref[pl.ds(r0, SUBLANES), cols] = xr
            si_ref[pl.ds(r0, SUBLANES), cols] = xi
            last_r = xr[SUBLANES - 1:SUBLANES, :]
            last_i = xi[SUBLANES - 1:SUBLANES, :]
            if chained:
                return last_r, last_i
            fr_ref[pl.ds(i, 1), cols] = last_r
            fi_ref[pl.ds(i, 1), cols] = last_i
            return carry

        if chained:
            init = (car_ref[:, cols], cai_ref[:, cols])
        else:
            init = (jnp.zeros((1, SCAN_W), F32), jnp.zeros((1, SCAN_W), F32))
        c_r, c_i = lax.fori_loop(0, n_tiles, tile, init)
        if chained:
            car_ref[:, cols] = c_r
            cai_ref[:, cols] = c_i

    if chained:
        fr_ref[0] = car_ref[...]
        fi_ref[0] = cai_ref[...]

    ys = []
    for hh in range(2):
        cs = slice(hh * half_c, (hh + 1) * half_c)
        ys.append(jnp.dot(sr_ref[:, cs].astype(BF16), cre_ref[hh], preferred_element_type=F32)
                  - jnp.dot(si_ref[:, cs].astype(BF16), cim_ref[hh], preferred_element_type=F32))
    y = jnp.concatenate(ys, axis=1) + d_ref[...] * u
    y = jax.nn.gelu(y)
    gate = jax.nn.sigmoid(jnp.dot(y.astype(BF16), wglu_ref[...], preferred_element_type=F32))
    y_ref[...] = (y * gate).astype(BF16)


def _ssm(u, s0r, s0i, p, *, n_seq, chained):
    t = u.shape[0]
    tc = TC_SSM
    if chained:
        per = t // n_seq // tc
        grid = (n_seq, per)
        row = lambda n, c: (n * per + c, 0)
        st = lambda n, c: (n, 0, 0)
        s0_spec = pl.BlockSpec((1, 1, SSM_CH), st)
        f_spec = pl.BlockSpec((1, 1, SSM_CH), st)
        f_shape = jax.ShapeDtypeStruct((n_seq, 1, SSM_CH), F32)
        sem = ("parallel", "arbitrary")
    else:
        grid = (t // tc,)
        row = lambda c: (c, 0)
        s0_spec = pl.BlockSpec((tc // SUBLANES, SSM_CH), row)
        f_spec = pl.BlockSpec((tc // SUBLANES, SSM_CH), row)
        f_shape = jax.ShapeDtypeStruct((t // SUBLANES, SSM_CH), F32)
        sem = ("parallel",)
    hu = SSM_WIDTH // 2
    hc = SSM_CH // 2
    return pl.pallas_call(
        functools.partial(_ssm_kernel, chained=chained),
        grid=grid,
        in_specs=[
            pl.BlockSpec((tc, SSM_WIDTH), row),
            s0_spec, s0_spec,
            _full((2, hu, hc)), _full((2, hu, hc)),
            _full((2, hc, hu)), _full((2, hc, hu)),
            _full((1, SSM_WIDTH)),
            _full((SSM_WIDTH, SSM_WIDTH)),
            _full((3, SUBLANES, SSM_CH)), _full((3, SUBLANES, SSM_CH)),
            _full((SUBLANES, SSM_CH)), _full((SUBLANES, SSM_CH)),
        ],
        out_specs=(pl.BlockSpec((tc, SSM_WIDTH), row), f_spec, f_spec),
        out_shape=(jax.ShapeDtypeStruct((t, SSM_WIDTH), BF16), f_shape, f_shape),
        scratch_shapes=[
            pltpu.VMEM((tc, SSM_CH), F32), pltpu.VMEM((tc, SSM_CH), F32),
            pltpu.VMEM((1, SSM_CH), F32), pltpu.VMEM((1, SSM_CH), F32),
        ],
        compiler_params=_cparams(sem),
        name="ssm_chained" if chained else "ssm_tiles",
    )(u, s0r, s0i, p["b_re"], p["b_im"], p["c_re"], p["c_im"], p["ssm_d"], p["w_glu"],
      p["lv_re"], p["lv_im"], p["cp_re"], p["cp_im"])


def _ssm_params(a_re, a_im, log_dt, b_re, b_im, c_re, c_im):
    dt = jnp.exp(log_dt)[:, None]
    mag = jnp.exp(a_re * dt)
    abr = mag * jnp.cos(a_im * dt)
    abi = mag * jnp.sin(a_im * dt)
    den = a_re * a_re + a_im * a_im
    nr = abr - 1.0
    ni = abi
    coef_re = (nr * a_re + ni * a_im) / den
    coef_im = (ni * a_re - nr * a_im) / den
    bbr = coef_re[..., None] * b_re - coef_im[..., None] * b_im
    bbi = coef_re[..., None] * b_im + coef_im[..., None] * b_re

    half_g = SSM_GROUPS // 2
    eye = jnp.eye(half_g, dtype=F32)

    def b_blocks(bb):
        bb = bb.reshape(2, half_g, SSM_STATE, SSM_GROUP)
        m = jnp.einsum("zgph,gk->zghkp", bb, eye)
        return m.reshape(2, half_g * SSM_GROUP, half_g * SSM_STATE).astype(BF16)

    def c_blocks(cc):
        cc = cc.reshape(2, half_g, SSM_GROUP, SSM_STATE)
        m = jnp.einsum("zghp,gk->zgpkh", cc, eye)
        return m.reshape(2, half_g * SSM_STATE, half_g * SSM_GROUP).astype(BF16)

    ar = abr.reshape(1, SSM_CH)
    ai = abi.reshape(1, SSM_CH)
    pows = [(ar, ai)]
    for _ in range(SUBLANES - 1):
        pr, pi = pows[-1]
        pows.append((pr * ar - pi * ai, pr * ai + pi * ar))
    rows = jnp.arange(SUBLANES)[:, None]
    lv_re = jnp.stack([jnp.where(rows >= d, pows[d - 1][0], 0.0) for d in (1, 2, 4)])
    lv_im = jnp.stack([jnp.where(rows >= d, pows[d - 1][1], 0.0) for d in (1, 2, 4)])
    cp_re = jnp.concatenate([pw[0] for pw in pows], axis=0)
    cp_im = jnp.concatenate([pw[1] for pw in pows], axis=0)
    return dict(b_re=b_blocks(bbr), b_im=b_blocks(bbi), c_re=c_blocks(c_re), c_im=c_blocks(c_im),
                lv_re=lv_re, lv_im=lv_im, cp_re=cp_re, cp_im=cp_im)


def _memkv_kernel(m_ref, g_ref, w_ref, gk_ref, o128_ref, k_ref, v_ref):
    hm = _rms_rows(m_ref[0], g_ref[...]).astype(BF16)
    kv = jnp.dot(hm, w_ref[...], preferred_element_type=F32)
    k = kv[:, :MEM_WIDTH]
    k_ref[0] = k * lax.rsqrt(_seg_mean_sq(k, o128_ref[...], MEM_HD) + EPS) * gk_ref[...]
    v_ref[0] = kv[:, MEM_WIDTH:]


def _memkv(mem, p):
    n = mem.shape[0]
    blk = lambda i: (i, 0, 0)
    shp = jax.ShapeDtypeStruct((n, N_MEM, MEM_WIDTH), F32)
    return pl.pallas_call(
        _memkv_kernel,
        grid=(n,),
        in_specs=[pl.BlockSpec((1, N_MEM, D_MODEL), blk), _full((1, D_MODEL)),
                  _full((D_MODEL, 2 * MEM_WIDTH)), _full((1, MEM_WIDTH)), _full((MEM_WIDTH, MEM_WIDTH))],
        out_specs=(pl.BlockSpec((1, N_MEM, MEM_WIDTH), blk), pl.BlockSpec((1, N_MEM, MEM_WIDTH), blk)),
        out_shape=(shp, shp),
        compiler_params=_cparams(("parallel",)),
        name="memkv",
    )(mem, p["g_mem"], p["w_mem_kv"], p["g_km"], p["ones128"])


def _dup_heads(x, lane):
    sw = pltpu.roll(x, SWA_HD, axis=x.ndim - 1)
    lo = lane < SWA_HD
    return jnp.where(lo, x, sw), jnp.where(lo, sw, x)


def _swa_group(q_blk, kk, vv, g, mask, sink_ref):
    tq = q_blk.shape[-2]
    shp = q_blk.shape[:-1]
    lane = lax.broadcasted_iota(jnp.int32, shp + (LANES,), len(shp))
    rows = []
    sinks = []
    for hl in range(SWA_Q_PER_KV):
        h = g * SWA_Q_PER_KV + hl
        pair = q_blk[..., (h // 2) * LANES:(h // 2 + 1) * LANES]
        keep = (lane < SWA_HD) if h % 2 == 0 else (lane >= SWA_HD)
        rows.append(jnp.where(keep, pair, 0.0))
        sinks.append(jnp.full(shp + (1,), sink_ref[h], F32))
    qq = jnp.concatenate(rows, axis=-2).astype(BF16)
    sk = jnp.concatenate(sinks, axis=-2)
    s = jnp.einsum("...qd,...kd->...qk", qq, kk, preferred_element_type=F32)
    s = jnp.where(mask, s, NEG_INF)
    m = jnp.maximum(jnp.max(s, axis=-1, keepdims=True), sk)
    e = jnp.exp(s - m)
    pr = e / (jnp.sum(e, axis=-1, keepdims=True) + jnp.exp(sk - m))
    o = jnp.einsum("...qk,...kd->...qd", pr.astype(BF16), vv, preferred_element_type=F32)
    lo = lane < SWA_HD
    return [jnp.where(lo, o[..., (2 * j) * tq:(2 * j + 1) * tq, :], o[..., (2 * j + 1) * tq:(2 * j + 2) * tq, :])
            for j in range(2)]


def _mem_heads(qm, k_head, v_head):
    outs = []
    for h in range(MEM_HEADS):
        cs = slice(h * MEM_HD, (h + 1) * MEM_HD)
        s = jnp.einsum("...qd,...kd->...qk", qm[..., cs], k_head(h).astype(BF16),
                       preferred_element_type=F32) * MEM_SCALE
        m = jnp.max(s, axis=-1, keepdims=True)
        e = jnp.exp(s - m)
        pr = e / jnp.sum(e, axis=-1, keepdims=True)
        outs.append(jnp.einsum("...qk,...kd->...qd", pr.astype(BF16), v_head(h).astype(BF16),
                               preferred_element_type=F32))
    return jnp.concatenate(outs, axis=-1)


def _attn_prompt_kernel(sink_ref, q_ref, k_ref, v_ref, kp_ref, vp_ref, qm_ref, mk_ref, mv_ref, ys_ref, ym_ref):
    tq = q_ref.shape[0]
    blk = WINDOW
    rows = SWA_Q_PER_KV * blk
    i = lax.broadcasted_iota(jnp.int32, (rows, 2 * blk), 0) % blk
    j = lax.broadcasted_iota(jnp.int32, (rows, 2 * blk), 1)
    lo = jnp.where(j < blk, i + 1, blk)
    hi = jnp.where(j < blk, blk, blk + i + 1)
    first_lo = jnp.where(pl.program_id(1) == 0, blk, 0)
    lane_k = lax.broadcasted_iota(jnp.int32, (2 * blk, LANES), 1)
    for b in range(tq // blk):
        rs = slice(b * blk, (b + 1) * blk)
        if b == 0:
            k2 = jnp.concatenate([kp_ref[...], k_ref[rs, :]], axis=0)
            v2 = jnp.concatenate([vp_ref[...], v_ref[rs, :]], axis=0)
            mask = (j >= jnp.maximum(lo, first_lo)) & (j < hi)
        else:
            k2 = k_ref[(b - 1) * blk:(b + 1) * blk, :]
            v2 = v_ref[(b - 1) * blk:(b + 1) * blk, :]
            mask = (j >= lo) & (j < hi)
        kks = _dup_heads(k2, lane_k)
        vvs = _dup_heads(v2, lane_k)
        q_blk = q_ref[rs, :].astype(F32)
        pairs = []
        for g in range(SWA_KV_HEADS):
            pairs += _swa_group(q_blk, kks[g].astype(BF16), vvs[g].astype(BF16), g, mask, sink_ref)
        ys_ref[rs, :] = jnp.concatenate(pairs, axis=1).astype(BF16)
    ym_ref[...] = _mem_heads(qm_ref[...], lambda h: mk_ref[0, :, h * MEM_HD:(h + 1) * MEM_HD],
                             lambda h: mv_ref[0, :, h * MEM_HD:(h + 1) * MEM_HD]).astype(BF16)


def _attn_prompt(q, k, v, qm, mk, mv, sinks, n_seq):
    t = q.shape[0]
    tq = TQ_ATT
    per = t // n_seq // tq
    sub = tq // WINDOW
    row = lambda n, c: (n * per + c, 0)
    prev = lambda n, c: (jnp.maximum((n * per + c) * sub - 1, 0), 0)
    memb = lambda n, c: (n, 0, 0)
    return pl.pallas_call(
        _attn_prompt_kernel,
        grid=(n_seq, per),
        in_specs=[
            pl.BlockSpec(memory_space=pltpu.SMEM),
            pl.BlockSpec((tq, SWA_WIDTH), row),
            pl.BlockSpec((tq, SWA_KV_WIDTH), row),
            pl.BlockSpec((tq, SWA_KV_WIDTH), row),
            pl.BlockSpec((WINDOW, SWA_KV_WIDTH), prev),
            pl.BlockSpec((WINDOW, SWA_KV_WIDTH), prev),
            pl.BlockSpec((tq, MEM_WIDTH), row),
            pl.BlockSpec((1, N_MEM, MEM_WIDTH), memb),
            pl.BlockSpec((1, N_MEM, MEM_WIDTH), memb),
        ],
        out_specs=(pl.BlockSpec((tq, SWA_WIDTH), row), pl.BlockSpec((tq, MEM_WIDTH), row)),
        out_shape=(jax.ShapeDtypeStruct((t, SWA_WIDTH), BF16), jax.ShapeDtypeStruct((t, MEM_WIDTH), BF16)),
        compiler_params=_cparams(("parallel", "parallel")),
        name="attn_prompt",
    )(sinks, q, k, v, k, v, qm, mk, mv)


def _attn_sample_kernel(sink_ref, q_ref, k_ref, v_ref, pk_ref, pv_ref, qm_ref, mk_ref, mv_ref,
                        ys_ref, ym_ref, nk_ref, nv_ref, *, s_len):
    sb, wb = pk_ref.shape[0], pk_ref.shape[1]
    n_keys = wb + s_len
    rows = SWA_Q_PER_KV * s_len
    i = lax.broadcasted_iota(jnp.int32, (sb, rows, n_keys), 1) % s_len
    j = lax.broadcasted_iota(jnp.int32, (sb, rows, n_keys), 2)
    rel = i + wb - j
    mask = (rel >= 0) & (rel < WINDOW)
    k_all = jnp.concatenate([pk_ref[...], k_ref[...].reshape(sb, s_len, SWA_KV_WIDTH)], axis=1)
    v_all = jnp.concatenate([pv_ref[...], v_ref[...].reshape(sb, s_len, SWA_KV_WIDTH)], axis=1)
    nk_ref[...] = k_all[:, n_keys - wb:, :]
    nv_ref[...] = v_all[:, n_keys - wb:, :]
    lane_k = lax.broadcasted_iota(jnp.int32, k_all.shape, 2)
    kks = _dup_heads(k_all, lane_k)
    vvs = _dup_heads(v_all, lane_k)
    q3 = q_ref[...].astype(F32).reshape(sb, s_len, SWA_WIDTH)
    pairs = []
    for g in range(SWA_KV_HEADS):
        pairs += _swa_group(q3, kks[g].astype(BF16), vvs[g].astype(BF16), g, mask, sink_ref)
    ys_ref[...] = jnp.concatenate(pairs, axis=-1).reshape(sb * s_len, SWA_WIDTH).astype(BF16)
    qm3 = qm_ref[...].astype(F32).reshape(sb, s_len, MEM_WIDTH).astype(BF16)
    head_rows = lambda h: pl.ds(h, N_MEM, stride=MEM_HEADS)
    ym = _mem_heads(qm3, lambda h: mk_ref[:, head_rows(h), :], lambda h: mv_ref[:, head_rows(h), :])
    ym_ref[...] = ym.reshape(sb * s_len, MEM_WIDTH).astype(BF16)


def _attn_sample(q, k, v, past_k, past_v, qm, mk, mv, sinks, s_len):
    t = q.shape[0]
    n_seq, wb = past_k.shape[0], past_k.shape[1]
    sb = SEQ_BLK
    rows = sb * s_len
    row = lambda c: (c, 0)
    seq = lambda c: (c, 0, 0)
    cache_shape = jax.ShapeDtypeStruct((n_seq, wb, SWA_KV_WIDTH), F32)
    return pl.pallas_call(
        functools.partial(_attn_sample_kernel, s_len=s_len),
        grid=(n_seq // sb,),
        in_specs=[
            pl.BlockSpec(memory_space=pltpu.SMEM),
            pl.BlockSpec((rows, SWA_WIDTH), row),
            pl.BlockSpec((rows, SWA_KV_WIDTH), row),
            pl.BlockSpec((rows, SWA_KV_WIDTH), row),
            pl.BlockSpec((sb, wb, SWA_KV_WIDTH), seq),
            pl.BlockSpec((sb, wb, SWA_KV_WIDTH), seq),
            pl.BlockSpec((rows, MEM_WIDTH), row),
            pl.BlockSpec((sb, N_MEM * MEM_HEADS, MEM_HD), seq),
            pl.BlockSpec((sb, N_MEM * MEM_HEADS, MEM_HD), seq),
        ],
        out_specs=(pl.BlockSpec((rows, SWA_WIDTH), row), pl.BlockSpec((rows, MEM_WIDTH), row),
                   pl.BlockSpec((sb, wb, SWA_KV_WIDTH), seq), pl.BlockSpec((sb, wb, SWA_KV_WIDTH), seq)),
        out_shape=(jax.ShapeDtypeStruct((t, SWA_WIDTH), BF16), jax.ShapeDtypeStruct((t, MEM_WIDTH), BF16),
                   cache_shape, cache_shape),
        compiler_params=_cparams(("parallel",)),
        name="attn_sample",
    )(sinks, q, k, v, past_k, past_v, qm, mk, mv)


def _first_argmax(x, valid, lane):
    xm = jnp.where(valid, x, -jnp.inf)
    mx = jnp.max(xm, axis=-1, keepdims=True)
    idx = jnp.min(jnp.where(xm == mx, lane, LANES), axis=-1, keepdims=True)
    return mx, lane == idx, idx


def _merge_kernel(xp_ref, yap_ref, ybp_ref, ycp_ref, xs_ref, yas_ref, ybs_ref, ycs_ref, *rest, n_blk_p):
    @pl.when(pl.program_id(0) < n_blk_p)
    def _():
        _merge_rows(xp_ref, yap_ref, ybp_ref, ycp_ref, *rest)

    @pl.when(pl.program_id(0) >= n_blk_p)
    def _():
        _merge_rows(xs_ref, yas_ref, ybs_ref, ycs_ref, *rest)


def _merge_rows(x_ref, ya_ref, yb_ref, yc_ref, gmix_ref, wg_ref, wa_ref, wb_ref, wc_ref, wo_ref, gffn_ref,
                wr_hi_ref, wr_lo_ref, br_ref, x1_ref, hx_ref, cnt_ref, carry_ref):
    x = x_ref[...]
    h = _rms_rows(x, gmix_ref[...]).astype(BF16)
    gates = jax.nn.sigmoid(jnp.dot(h, wg_ref[...], preferred_element_type=F32))
    merged = (gates[:, :D_MODEL] * jnp.dot(ya_ref[...], wa_ref[...], preferred_element_type=F32)
              + gates[:, D_MODEL:2 * D_MODEL] * jnp.dot(yb_ref[...], wb_ref[...], preferred_element_type=F32)
              + gates[:, 2 * D_MODEL:] * jnp.dot(yc_ref[...], wc_ref[...], preferred_element_type=F32))
    x1 = x + jnp.dot(merged.astype(BF16), wo_ref[...], preferred_element_type=F32)
    x1_ref[...] = x1
    hn = _rms_rows(x1, gffn_ref[...])
    hx_ref[:, :D_MODEL] = hn

    hi, lo = _split_bf16(hn)
    w_hi = wr_hi_ref[...]
    logits = (jnp.dot(hi, w_hi, preferred_element_type=F32) + jnp.dot(lo, w_hi, preferred_element_type=F32)
              + jnp.dot(hi, wr_lo_ref[...], preferred_element_type=F32)) + br_ref[...]
    lane = lax.broadcasted_iota(jnp.int32, logits.shape, 1)
    is_grp = lane < N_GROUPS_E
    g_max, _, g_idx = _first_argmax(logits, is_grp, lane)
    pg_top = 1.0 / jnp.sum(jnp.where(is_grp, jnp.exp(logits - g_max), 0.0), axis=-1, keepdims=True)
    e_lo = N_GROUPS_E + g_idx * EXPERTS_PER_GROUP
    in_grp = (lane >= e_lo) & (lane < e_lo + EXPERTS_PER_GROUP)
    e_max, first, i1 = _first_argmax(logits, in_grp, lane)
    ex = jnp.where(in_grp, jnp.exp(logits - e_max), 0.0)
    pe = ex / jnp.sum(ex, axis=-1, keepdims=True)
    _, second, i2 = _first_argmax(logits, in_grp & jnp.logical_not(first), lane)
    p1 = jnp.sum(jnp.where(first, pe, 0.0), axis=-1, keepdims=True)
    p2 = jnp.sum(jnp.where(second, pe, 0.0), axis=-1, keepdims=True)
    w1 = pg_top * p1 / (p1 + p2)
    w2 = pg_top * p2 / (p1 + p2)

    a1 = i1 - e_lo
    a2 = i2 - e_lo
    e_a = jnp.minimum(a1, a2)
    e_b = jnp.maximum(a1, a2)
    pair = jnp.right_shift(e_a * (2 * EXPERTS_PER_GROUP - 1 - e_a), 1) + (e_b - e_a - 1)
    bucket = g_idx * PAIRS_PER_GROUP + pair
    w_a = jnp.where(a1 < a2, w1, w2)
    w_b = jnp.where(a1 < a2, w2, w1)

    @pl.when(pl.program_id(0) == 0)
    def _():
        carry_ref[...] = jnp.zeros_like(carry_ref)

    tm = x.shape[0]
    onehot = lane == bucket
    tri = (lax.broadcasted_iota(jnp.int32, (tm, tm), 1) <= lax.broadcasted_iota(jnp.int32, (tm, tm), 0))
    csum = jnp.dot(jnp.where(tri, 1.0, 0.0).astype(BF16), jnp.where(onehot, 1.0, 0.0).astype(BF16),
                   preferred_element_type=F32)
    carry = carry_ref[...]
    rank = jnp.sum(jnp.where(onehot, csum + carry, 0.0), axis=-1, keepdims=True) - 1.0
    carry = carry + csum[tm - 1:tm, :]
    carry_ref[...] = carry
    cnt_ref[...] = carry
    route = jnp.where(lane == 0, bucket.astype(F32),
                      jnp.where(lane == 1, w_a, jnp.where(lane == 2, w_b, jnp.where(lane == 3, rank, 0.0))))
    hx_ref[:, D_MODEL:] = route


def _merge(prompt_rows, sample_rows, p):
    tm = TM_MRG
    t_p, t_s = prompt_rows[0].shape[0], sample_rows[0].shape[0]
    nbp = t_p // tm
    t_all = t_p + t_s
    first = lambda i: (jnp.minimum(i, nbp - 1), 0)
    second = lambda i: (jnp.maximum(i - nbp, 0), 0)
    row = lambda i: (i, 0)
    widths = (D_MODEL, SSM_WIDTH, SWA_WIDTH, MEM_WIDTH)
    in_specs = ([pl.BlockSpec((tm, w), first) for w in widths] + [pl.BlockSpec((tm, w), second) for w in widths] + [
        _full((1, D_MODEL)),
        _full((D_MODEL, N_BRANCH * D_MODEL)),
        _full((SSM_WIDTH, D_MODEL)), _full((SWA_WIDTH, D_MODEL)), _full((MEM_WIDTH, D_MODEL)),
        _full((D_MODEL, D_MODEL)),
        _full((1, D_MODEL)),
        _full((D_MODEL, LANES)), _full((D_MODEL, LANES)), _full((1, LANES)),
    ])
    return pl.pallas_call(
        functools.partial(_merge_kernel, n_blk_p=nbp),
        grid=(t_all // tm,),
        in_specs=in_specs,
        out_specs=(pl.BlockSpec((tm, D_MODEL), row), pl.BlockSpec((tm, HX_W), row), _full((1, LANES))),
        out_shape=(jax.ShapeDtypeStruct((t_all, D_MODEL), F32), jax.ShapeDtypeStruct((t_all, HX_W), F32),
                   jax.ShapeDtypeStruct((1, LANES), F32)),
        scratch_shapes=[pltpu.VMEM((1, LANES), F32)],
        compiler_params=_cparams(("arbitrary",)),
        name="merge",
    )(*prompt_rows, *sample_rows, p["g_mix"], p["w_gates"], p["w_br_ssm"], p["w_br_swa"], p["w_br_mem"], p["w_o"],
      p["g_ffn"], p["w_r_hi"], p["w_r_lo"], p["b_r"])


def _pos_kernel(route_ref, off_ref, coff_ref, pos_ref):
    r = route_ref[...]
    lane = lax.broadcasted_iota(jnp.int32, r.shape, 1)
    mine = lane == r[:, 0:1].astype(jnp.int32)
    off = jnp.sum(jnp.where(mine, off_ref[...], 0.0), axis=-1, keepdims=True)
    coff = jnp.sum(jnp.where(mine, coff_ref[...], 0.0), axis=-1, keepdims=True)
    rank = r[:, 3:4]
    pos_ref[...] = jnp.where(lane == 0, off + rank, coff + rank).astype(jnp.int32)


def _sorted_pos(hx, off, coff):
    t = hx.shape[0]
    tm = TM_IN
    return pl.pallas_call(
        _pos_kernel,
        grid=(t // tm,),
        in_specs=[pl.BlockSpec((tm, ROUTE_W), lambda i: (i, D_MODEL // ROUTE_W)), _full((1, LANES)),
                  _full((1, LANES))],
        out_specs=pl.BlockSpec((tm, LANES), lambda i: (i, 0)),
        out_shape=jax.ShapeDtypeStruct((t, LANES), jnp.int32),
        compiler_params=_cparams(("parallel",)),
        name="sorted_pos",
    )(hx, off, coff)


def _inv_kernel(pos_ref, idx_ref):
    def body(t, _):
        idx_ref[pos_ref[t]] = t
        return 0

    lax.fori_loop(0, pos_ref.shape[0], body, 0, unroll=8)


def _invert(pos):
    return pl.pallas_call(
        _inv_kernel,
        in_specs=[pl.BlockSpec(memory_space=pltpu.SMEM)],
        out_specs=pl.BlockSpec(memory_space=pltpu.SMEM),
        out_shape=jax.ShapeDtypeStruct(pos.shape, jnp.int32),
        name="invert_perm",
    )(pos)


def _bucket_kernel(idx_ref, ta_ref, tb_ref, nv_ref, cb_ref, nr_ref, hx_hbm, wia_ref, wda_ref, wib_ref, wdb_ref,
                   ys_ref, buf, sem):
    j = pl.program_id(0)
    n_real = nr_ref[0]

    def issue(tile, slot):
        base = cb_ref[tile]
        n_valid = nv_ref[tile]

        def body(r, _):
            src = idx_ref[base + jnp.where(r < n_valid, r, 0)]
            pltpu.make_async_copy(hx_hbm.at[pl.ds(src, 1), :], buf.at[slot, pl.ds(r, 1), :], sem.at[slot]).start()
            return 0

        lax.fori_loop(0, TM_EXP, body, 0, unroll=8)

    @pl.when(j == 0)
    def _():
        issue(0, 0)

    @pl.when(j + 1 < n_real)
    def _():
        issue(j + 1, (j + 1) % 2)

    @pl.when(j < n_real)
    def _():
        slot = j % 2
        pltpu.make_async_copy(hx_hbm.at[pl.ds(0, TM_EXP), :], buf.at[slot], sem.at[slot]).wait()
        xb = buf[slot]
        x = xb[:, :D_MODEL].astype(BF16)

        def ffn(wi_ref, wd_ref):
            gu = jnp.dot(x, wi_ref[0], preferred_element_type=F32)
            a = jax.nn.silu(gu[:, :D_FF]) * gu[:, D_FF:]
            return jnp.dot(a.astype(BF16), wd_ref[0], preferred_element_type=F32)

        ys_ref[...] = (xb[:, D_MODEL + 1:D_MODEL + 2] * ffn(wia_ref, wda_ref)
                       + xb[:, D_MODEL + 2:D_MODEL + 3] * ffn(wib_ref, wdb_ref))


def _bucket_ffn(hx, idx, tile_a, tile_b, tile_nv, tile_cb, n_real, p):
    n_tiles = tile_a.shape[0]
    wa = lambda j, idx, ta, tb, nv, cb, nr: (ta[j], 0, 0)
    wb = lambda j, idx, ta, tb, nv, cb, nr: (tb[j], 0, 0)
    out = lambda j, idx, ta, tb, nv, cb, nr: (jnp.minimum(j, nr[0] - 1), 0)
    return pl.pallas_call(
        _bucket_kernel,
        grid_spec=pltpu.PrefetchScalarGridSpec(
            num_scalar_prefetch=6,
            grid=(n_tiles,),
            in_specs=[
                pl.BlockSpec(memory_space=pl.ANY),
                pl.BlockSpec((1, D_MODEL, 2 * D_FF), wa), pl.BlockSpec((1, D_FF, D_MODEL), wa),
                pl.BlockSpec((1, D_MODEL, 2 * D_FF), wb), pl.BlockSpec((1, D_FF, D_MODEL), wb),
            ],
            out_specs=pl.BlockSpec((TM_EXP, D_MODEL), out),
            scratch_shapes=[pltpu.VMEM((2, TM_EXP, HX_W), F32), pltpu.SemaphoreType.DMA((2,))],
        ),
        out_shape=jax.ShapeDtypeStruct((n_tiles * TM_EXP, D_MODEL), F32),
        compiler_params=_cparams(("arbitrary",)),
        name="bucket_ffn",
    )(idx, tile_a, tile_b, tile_nv, tile_cb, n_real, hx,
      p["w_exp_in"], p["w_exp_down"], p["w_exp_in"], p["w_exp_down"])


def _back_kernel(pos_ref, x1_ref, ys_hbm, o_ref, buf, sem, *, t0):
    i = pl.program_id(0)
    tm = o_ref.shape[0]

    def issue(tile, slot):
        base = t0 + tile * tm

        def body(r, _):
            src = pos_ref[base + r]
            pltpu.make_async_copy(ys_hbm.at[pl.ds(src, 1), :], buf.at[slot, pl.ds(r, 1), :], sem.at[slot]).start()
            return 0

        lax.fori_loop(0, tm, body, 0, unroll=8)

    @pl.when(i == 0)
    def _():
        issue(0, 0)

    @pl.when(i + 1 < pl.num_programs(0))
    def _():
        issue(i + 1, (i + 1) % 2)

    slot = i % 2
    pltpu.make_async_copy(ys_hbm.at[pl.ds(0, tm), :], buf.at[slot], sem.at[slot]).wait()
    o_ref[...] = x1_ref[...] + buf[slot]


def _unsort_add(x1, ys, pos, t0, t):
    tm = TM_MRG
    row = lambda i, pos: (i, 0)
    return pl.pallas_call(
        functools.partial(_back_kernel, t0=t0),
        grid_spec=pltpu.PrefetchScalarGridSpec(
            num_scalar_prefetch=1,
            grid=(t // tm,),
            in_specs=[pl.BlockSpec((tm, D_MODEL), lambda i, pos: (i + t0 // tm, 0)),
                      pl.BlockSpec(memory_space=pl.ANY)],
            out_specs=pl.BlockSpec((tm, D_MODEL), row),
            scratch_shapes=[pltpu.VMEM((2, tm, D_MODEL), F32), pltpu.SemaphoreType.DMA((2,))],
        ),
        out_shape=jax.ShapeDtypeStruct((t, D_MODEL), F32),
        compiler_params=_cparams(("arbitrary",)),
        name="unsort_add",
    )(pos, x1, ys)


def _bucket_experts():
    lo, hi = [], []
    for g in range(N_GROUPS_E):
        for a in range(EXPERTS_PER_GROUP):
            for b in range(a + 1, EXPERTS_PER_GROUP):
                lo.append(g * EXPERTS_PER_GROUP + a)
                hi.append(g * EXPERTS_PER_GROUP + b)
    return jnp.asarray(lo, jnp.int32), jnp.asarray(hi, jnp.int32)


def _tile_tables(counts, n_tiles):
    cnt = counts[0, :N_BUCKETS].astype(jnp.int32)
    nt = (cnt + TM_EXP - 1) // TM_EXP
    tend = jnp.cumsum(nt)
    tstart = tend - nt
    cstart = jnp.cumsum(cnt) - cnt
    pad = lambda v: jnp.zeros((1, LANES), F32).at[0, :N_BUCKETS].set(v.astype(F32))
    j = jnp.arange(n_tiles, dtype=jnp.int32)
    b = jnp.minimum(jnp.sum((tend[None, :] <= j[:, None]).astype(jnp.int32), axis=1), N_BUCKETS - 1)
    in_bucket = (j - tstart[b]) * TM_EXP
    nv = jnp.clip(cnt[b] - in_bucket, 0, TM_EXP)
    e_lo, e_hi = _bucket_experts()
    return pad(tstart * TM_EXP), pad(cstart), e_lo[b], e_hi[b], nv, cstart[b] + in_bucket, tend[-1:]


def _rope_tables(pos):
    half = SWA_HD // 2
    inv = ROPE_THETA ** (-jnp.arange(half, dtype=F32) / half)
    ang = pos.astype(F32)[:, None] * inv[None, :]
    cos = jnp.cos(ang)
    sin = jnp.sin(ang)
    cos = jnp.concatenate([cos, cos, cos, cos], axis=1)
    sin = jnp.concatenate([-sin, sin, -sin, sin], axis=1)
    return cos, sin


def kernel(x_prompt, x_sample, mem_prompt, state_ssm_re, state_ssm_im, cache_swa_k, cache_swa_v, cache_mem_k, cache_mem_v, norm_mix, w_in, ssm_a_re, ssm_a_im, ssm_log_dt, ssm_b_re, ssm_b_im, ssm_c_re, ssm_c_im, ssm_d, w_glu, swa_q_norm, swa_k_norm, swa_sinks, norm_mem, w_mem_kv, mem_q_norm, mem_k_norm, w_br_ssm, w_br_swa, w_br_mem, w_o, norm_ffn, w_router_group, b_router_group, w_router_expert, b_router_expert, w_exp_in, w_exp_down):
    depth = w_in.shape[0]
    assert depth == 1
    nb, seq, _ = x_prompt.shape
    db, dseq, _ = x_sample.shape
    assert dseq == SUBLANES
    l = 0

    w_r = jnp.concatenate([w_router_group[l], w_router_expert[l]], axis=1)
    w_r = jnp.pad(w_r, ((0, 0), (0, LANES - w_r.shape[1])))
    w_r_hi = w_r.astype(BF16)
    b_r = jnp.pad(jnp.concatenate([b_router_group[l], b_router_expert[l]]), (0, LANES - N_GROUPS_E - N_EXPERTS))
    p = dict(
        g_mix=norm_mix[l][None], w_a=w_in[l][:, :PROJ_A].astype(BF16), w_gates=w_in[l][:, PROJ_A:].astype(BF16),
        g_q=jnp.tile(swa_q_norm[l], SWA_HEADS)[None], g_k=jnp.tile(swa_k_norm[l], SWA_KV_HEADS)[None],
        g_qm=jnp.tile(mem_q_norm[l], MEM_HEADS)[None], g_km=jnp.tile(mem_k_norm[l], MEM_HEADS)[None],
        ones64=_block_ones(SWA_WIDTH, SWA_HD), ones128=_block_ones(MEM_WIDTH, MEM_HD),
        ssm_d=ssm_d[l][None], w_glu=w_glu[l].astype(BF16),
        g_mem=norm_mem[l][None], w_mem_kv=w_mem_kv[l].astype(BF16),
        w_br_ssm=w_br_ssm[l].astype(BF16), w_br_swa=w_br_swa[l].astype(BF16), w_br_mem=w_br_mem[l].astype(BF16),
        w_o=w_o[l].astype(BF16), g_ffn=norm_ffn[l][None],
        w_r_hi=w_r_hi, w_r_lo=(w_r - w_r_hi.astype(F32)).astype(BF16), b_r=b_r[None],
        w_exp_in=w_exp_in[l].astype(BF16), w_exp_down=w_exp_down[l].astype(BF16),
    )
    p.update(_ssm_params(ssm_a_re[l], ssm_a_im[l], ssm_log_dt[l], ssm_b_re[l], ssm_b_im[l],
                         ssm_c_re[l], ssm_c_im[l]))
    sinks = swa_sinks[l]

    xp = x_prompt.reshape(nb * seq, D_MODEL)
    cos_p, sin_p = _rope_tables(jnp.arange(seq, dtype=jnp.int32))
    u, q, k, v, qm = _inproj(xp, cos_p, sin_p, seq // TM_IN, p)
    zeros_state = jnp.zeros((nb, 1, SSM_CH), F32)
    y_ssm, pr, pi = _ssm(u, zeros_state, zeros_state, p, n_seq=nb, chained=True)
    mk, mv = _memkv(mem_prompt, p)
    y_swa, y_mem = _attn_prompt(q, k, v, qm, mk, mv, sinks, nb)
    win = min(WINDOW, seq)
    p_k = k.reshape(nb, seq, SWA_KV_HEADS, SWA_HD)[:, seq - win:]
    p_v = v.reshape(nb, seq, SWA_KV_HEADS, SWA_HD)[:, seq - win:]

    xs = x_sample.reshape(db * dseq, D_MODEL)
    cos_s, sin_s = _rope_tables(PAST_LEN + jnp.arange(dseq, dtype=jnp.int32))
    reps = TM_IN // dseq
    us, qs, ks, vs, qms = _inproj(xs, jnp.tile(cos_s, (reps, 1)), jnp.tile(sin_s, (reps, 1)), 1, p)
    ys_ssm, sr, si = _ssm(us, state_ssm_re[l].reshape(db, SSM_CH), state_ssm_im[l].reshape(db, SSM_CH), p,
                          n_seq=db, chained=False)
    wb = cache_swa_k.shape[2]
    ys_swa, ys_mem, s_k, s_v = _attn_sample(
        qs, ks, vs, cache_swa_k[l].reshape(db, wb, SWA_KV_WIDTH), cache_swa_v[l].reshape(db, wb, SWA_KV_WIDTH),
        qms, cache_mem_k.reshape(db, N_MEM * MEM_HEADS, MEM_HD), cache_mem_v.reshape(db, N_MEM * MEM_HEADS, MEM_HD),
        sinks, dseq)
    t_p, t_s = nb * seq, db * dseq
    t_all = t_p + t_s
    x1, hx, counts = _merge((xp, y_ssm, y_swa, y_mem), (xs, ys_ssm, ys_swa, ys_mem), p)

    n_tiles = t_all // TM_EXP + N_BUCKETS
    off, coff, tile_a, tile_b, tile_nv, tile_cb, n_real = _tile_tables(counts, n_tiles)
    pos2 = _sorted_pos(hx, off, coff)
    pos = pos2[:, 0]
    idx = _invert(pos2[:, 1])
    y_sorted = _bucket_ffn(hx, idx, tile_a, tile_b, tile_nv, tile_cb, n_real, p)
    yp = _unsort_add(x1, y_sorted, pos, 0, t_p).reshape(nb, seq, D_MODEL)
    ys = _unsort_add(x1, y_sorted, pos, t_p, t_s).reshape(db, dseq, D_MODEL)

    g, s = SSM_GROUPS, SSM_STATE
    return (yp, ys,
            pr.reshape(1, nb, g, s), pi.reshape(1, nb, g, s),
            p_k[None], p_v[None],
            mk.reshape(1, nb, N_MEM, MEM_HEADS, MEM_HD), mv.reshape(1, nb, N_MEM, MEM_HEADS, MEM_HD),
            sr.reshape(1, db, g, s), si.reshape(1, db, g, s),
            s_k.reshape(1, db, wb, SWA_KV_HEADS, SWA_HD), s_v.reshape(1, db, wb, SWA_KV_HEADS, SWA_HD))
```

```python
import functools
import math

import jax
import jax.numpy as jnp
from jax import lax
from jax.experimental import pallas as pl
from jax.experimental.pallas import tpu as pltpu

F32 = jnp.float32
BF16 = jnp.bfloat16

D_MODEL = 1024
SSM_WIDTH = 512
SSM_GROUP = 16
SSM_GROUPS = 32
SSM_STATE = 64
SSM_CH = SSM_GROUPS * SSM_STATE
SWA_HEADS = 8
SWA_KV_HEADS = 2
SWA_Q_PER_KV = SWA_HEADS // SWA_KV_HEADS
SWA_HD = 64
SWA_WIDTH = SWA_HEADS * SWA_HD
SWA_KV_WIDTH = SWA_KV_HEADS * SWA_HD
WINDOW = 128
PAST_LEN = 16384
ROPE_THETA = 10000.0
N_MEM = 256
MEM_HEADS = 4
MEM_HD = 128
MEM_WIDTH = MEM_HEADS * MEM_HD
N_BRANCH = 3
PROJ_A = SSM_WIDTH + SWA_WIDTH + 2 * SWA_KV_WIDTH + MEM_WIDTH
N_GROUPS_E = 4
EXPERTS_PER_GROUP = 8
N_EXPERTS = 32
D_FF = 256
EPS = 1e-6
NEG_INF = -1e30
SWA_SCALE = SWA_HD ** -0.5
MEM_SCALE = MEM_HD ** -0.5

LANES = 128
SUBLANES = 8
VMEM_LIMIT = 56 * 1024 * 1024

TM_IN = 512
TC_SSM = 512
SCAN_W = 512
TQ_ATT = 512
SEQ_BLK = 8
TM_MRG = 256
TM_EXP = 256

ROUTE_W = LANES
ROW_TILES = D_MODEL // LANES
HX_ROWS = 2 * ROW_TILES
PAIRS_PER_GROUP = EXPERTS_PER_GROUP * (EXPERTS_PER_GROUP - 1) // 2
N_BUCKETS = N_GROUPS_E * PAIRS_PER_GROUP


def _cparams(sem):
    return pltpu.CompilerParams(dimension_semantics=sem, vmem_limit_bytes=VMEM_LIMIT)


def _full(shape):
    nd = len(shape)
    return pl.BlockSpec(shape, lambda *_: (0,) * nd)


def _split_bf16(x):
    hi = x.astype(BF16)
    lo = (x - hi.astype(F32)).astype(BF16)
    return hi, lo


def _seg_mean_sq(x, ones_blk, width):
    hi, lo = _split_bf16(x * x)
    s = jnp.dot(hi, ones_blk, preferred_element_type=F32) + jnp.dot(lo, ones_blk, preferred_element_type=F32)
    return s * (1.0 / width)


def _rms_rows(x, gain):
    return x * lax.rsqrt(jnp.mean(x * x, axis=-1, keepdims=True) + EPS) * gain


def _block_ones(n, width):
    i = jnp.arange(n) // width
    return (i[:, None] == i[None, :]).astype(BF16)


def _rope_cols(x, cos, sin_signed, lane_in_head):
    n = x.shape[1]
    reps = n // LANES
    if reps > 1:
        cos = jnp.concatenate([cos] * reps, axis=1)
        sin_signed = jnp.concatenate([sin_signed] * reps, axis=1)
    half = SWA_HD // 2
    partner = jnp.where(lane_in_head < half, pltpu.roll(x, n - half, axis=1), pltpu.roll(x, half, axis=1))
    return x * cos + partner * sin_signed


def _inproj_kernel(x_ref, gmix_ref, w_ref, gq_ref, gk_ref, gm_ref, cos_ref, sin_ref, o64_ref, o128_ref,
                   u_ref, q_ref, k_ref, v_ref, qm_ref):
    x = x_ref[...]
    h = _rms_rows(x, gmix_ref[...]).astype(BF16)
    proj = jnp.dot(h, w_ref[...], preferred_element_type=F32)
    c0 = SSM_WIDTH
    c1 = c0 + SWA_WIDTH
    c2 = c1 + SWA_KV_WIDTH
    c3 = c2 + SWA_KV_WIDTH
    u_ref[...] = proj[:, :c0]
    q = proj[:, c0:c1]
    k = proj[:, c1:c2]
    v_ref[...] = proj[:, c2:c3]
    qm = proj[:, c3:]
    cos = cos_ref[...]
    sin = sin_ref[...]
    o64 = o64_ref[...]
    lane_q = lax.broadcasted_iota(jnp.int32, q.shape, 1) % SWA_HD
    qn = q * lax.rsqrt(_seg_mean_sq(q, o64, SWA_HD) + EPS) * gq_ref[...]
    q_ref[...] = (_rope_cols(qn, cos, sin, lane_q) * SWA_SCALE).astype(BF16)
    lane_k = lax.broadcasted_iota(jnp.int32, k.shape, 1) % SWA_HD
    kn = k * lax.rsqrt(_seg_mean_sq(k, o64[:SWA_KV_WIDTH, :SWA_KV_WIDTH], SWA_HD) + EPS) * gk_ref[...]
    k_ref[...] = _rope_cols(kn, cos, sin, lane_k)
    qmn = qm * lax.rsqrt(_seg_mean_sq(qm, o128_ref[...], MEM_HD) + EPS) * gm_ref[...]
    qm_ref[...] = qmn.astype(BF16)


def _inproj(x2d, cos, sin, pos_blocks, p):
    t = x2d.shape[0]
    tm = TM_IN
    grid = (t // tm,)
    row = lambda i: (i, 0)
    tab = lambda i: (i % pos_blocks, 0)
    out_shape = (
        jax.ShapeDtypeStruct((t, SSM_WIDTH), F32),
        jax.ShapeDtypeStruct((t, SWA_WIDTH), BF16),
        jax.ShapeDtypeStruct((t, SWA_KV_WIDTH), F32),
        jax.ShapeDtypeStruct((t, SWA_KV_WIDTH), F32),
        jax.ShapeDtypeStruct((t, MEM_WIDTH), BF16),
    )
    return pl.pallas_call(
        _inproj_kernel,
        grid=grid,
        in_specs=[
            pl.BlockSpec((tm, D_MODEL), row),
            _full((1, D_MODEL)),
            _full((D_MODEL, PROJ_A)),
            _full((1, SWA_WIDTH)),
            _full((1, SWA_KV_WIDTH)),
            _full((1, MEM_WIDTH)),
            pl.BlockSpec((tm, LANES), tab),
            pl.BlockSpec((tm, LANES), tab),
            _full((SWA_WIDTH, SWA_WIDTH)),
            _full((MEM_WIDTH, MEM_WIDTH)),
        ],
        out_specs=(
            pl.BlockSpec((tm, SSM_WIDTH), row),
            pl.BlockSpec((tm, SWA_WIDTH), row),
            pl.BlockSpec((tm, SWA_KV_WIDTH), row),
            pl.BlockSpec((tm, SWA_KV_WIDTH), row),
            pl.BlockSpec((tm, MEM_WIDTH), row),
        ),
        out_shape=out_shape,
        compiler_params=_cparams(("parallel",)),
        name="inproj",
    )(x2d, p["g_mix"], p["w_a"], p["g_q"], p["g_k"], p["g_qm"], cos, sin, p["ones64"], p["ones128"])


def _ssm_kernel(u_ref, s0r_ref, s0i_ref, bre_ref, bim_ref, cre_ref, cim_ref, d_ref, wglu_ref,
                lvr_ref, lvi_ref, cpr_ref, cpi_ref,
                y_ref, fr_ref, fi_ref, sr_ref, si_ref, car_ref, cai_ref, *, chained):
    tc = u_ref.shape[0]
    half_u = SSM_WIDTH // 2
    half_c = SSM_CH // 2
    u = u_ref[...]
    ub = u.astype(BF16)
    for hh in range(2):
        us = ub[:, hh * half_u:(hh + 1) * half_u]
        cs = slice(hh * half_c, (hh + 1) * half_c)
        sr_ref[:, cs] = jnp.dot(us, bre_ref[hh], preferred_element_type=F32)
        si_ref[:, cs] = jnp.dot(us, bim_ref[hh], preferred_element_type=F32)

    if chained:
        @pl.when(pl.program_id(1) == 0)
        def _():
            car_ref[...] = s0r_ref[0]
            cai_ref[...] = s0i_ref[0]

    n_tiles = tc // SUBLANES
    for sl in range(SSM_CH // SCAN_W):
        cols = slice(sl * SCAN_W, (sl + 1) * SCAN_W)
        lv = [(lvr_ref[j, :, cols], lvi_ref[j, :, cols]) for j in range(3)]
        cpr = cpr_ref[:, cols]
        cpi = cpi_ref[:, cols]

        def tile(i, carry, cols=cols, lv=lv, cpr=cpr, cpi=cpi):
            r0 = pl.multiple_of(i * SUBLANES, SUBLANES)
            xr = sr_ref[pl.ds(r0, SUBLANES), cols]
            xi = si_ref[pl.ds(r0, SUBLANES), cols]
            for j, d in enumerate((1, 2, 4)):
                pr, pi = lv[j]
                shr = pltpu.roll(xr, d, axis=0)
                shi = pltpu.roll(xi, d, axis=0)
                xr, xi = xr + pr * shr - pi * shi, xi + pr * shi + pi * shr
            if chained:
                c_r, c_i = carry
            else:
                c_r = s0r_ref[pl.ds(i, 1), cols]
                c_i = s0i_ref[pl.ds(i, 1), cols]
            cb_r = jnp.broadcast_to(c_r, xr.shape)
            cb_i = jnp.broadcast_to(c_i, xr.shape)
            xr, xi = xr + cpr * cb_r - cpi * cb_i, xi + cpr * cb_i + cpi * cb_r
            sr_ref[pl.ds(r0, SUBLANES), cols] = xr
            si_ref[pl.ds(r0, SUBLANES), cols] = xi
            last_r = xr[SUBLANES - 1:SUBLANES, :]
            last_i = xi[SUBLANES - 1:SUBLANES, :]
            if chained:
                return last_r, last_i
            fr_ref[pl.ds(i, 1), cols] = last_r
            fi_ref[pl.ds(i, 1), cols] = last_i
            return carry

        if chained:
            init = (car_ref[:, cols], cai_ref[:, cols])
        else:
            init = (jnp.zeros((1, SCAN_W), F32), jnp.zeros((1, SCAN_W), F32))
        c_r, c_i = lax.fori_loop(0, n_tiles, tile, init)
        if chained:
            car_ref[:, cols] = c_r
            cai_ref[:, cols] = c_i

    if chained:
        fr_ref[0] = car_ref[...]
        fi_ref[0] = cai_ref[...]

    ys = []
    for hh in range(2):
        cs = slice(hh * half_c, (hh + 1) * half_c)
        ys.append(jnp.dot(sr_ref[:, cs].astype(BF16), cre_ref[hh], preferred_element_type=F32)
                  - jnp.dot(si_ref[:, cs].astype(BF16), cim_ref[hh], preferred_element_type=F32))
    y = jnp.concatenate(ys, axis=1) + d_ref[...] * u
    y = jax.nn.gelu(y)
    gate = jax.nn.sigmoid(jnp.dot(y.astype(BF16), wglu_ref[...], preferred_element_type=F32))
    y_ref[...] = (y * gate).astype(BF16)


def _ssm(u, s0r, s0i, p, *, n_seq, chained):
    t = u.shape[0]
    tc = TC_SSM
    if chained:
        per = t // n_seq // tc
        grid = (n_seq, per)
        row = lambda n, c: (n * per + c, 0)
        st = lambda n, c: (n, 0, 0)
        s0_spec = pl.BlockSpec((1, 1, SSM_CH), st)
        f_spec = pl.BlockSpec((1, 1, SSM_CH), st)
        f_shape = jax.ShapeDtypeStruct((n_seq, 1, SSM_CH), F32)
        sem = ("parallel", "arbitrary")
    else:
        grid = (t // tc,)
        row = lambda c: (c, 0)
        s0_spec = pl.BlockSpec((tc // SUBLANES, SSM_CH), row)
        f_spec = pl.BlockSpec((tc // SUBLANES, SSM_CH), row)
        f_shape = jax.ShapeDtypeStruct((t // SUBLANES, SSM_CH), F32)
        sem = ("parallel",)
    hu = SSM_WIDTH // 2
    hc = SSM_CH // 2
    return pl.pallas_call(
        functools.partial(_ssm_kernel, chained=chained),
        grid=grid,
        in_specs=[
            pl.BlockSpec((tc, SSM_WIDTH), row),
            s0_spec, s0_spec,
            _full((2, hu, hc)), _full((2, hu, hc)),
            _full((2, hc, hu)), _full((2, hc, hu)),
            _full((1, SSM_WIDTH)),
            _full((SSM_WIDTH, SSM_WIDTH)),
            _full((3, SUBLANES, SSM_CH)), _full((3, SUBLANES, SSM_CH)),
            _full((SUBLANES, SSM_CH)), _full((SUBLANES, SSM_CH)),
        ],
        out_specs=(pl.BlockSpec((tc, SSM_WIDTH), row), f_spec, f_spec),
        out_shape=(jax.ShapeDtypeStruct((t, SSM_WIDTH), BF16), f_shape, f_shape),
        scratch_shapes=[
            pltpu.VMEM((tc, SSM_CH), F32), pltpu.VMEM((tc, SSM_CH), F32),
            pltpu.VMEM((1, SSM_CH), F32), pltpu.VMEM((1, SSM_CH), F32),
        ],
        compiler_params=_cparams(sem),
        name="ssm_chained" if chained else "ssm_tiles",
    )(u, s0r, s0i, p["b_re"], p["b_im"], p["c_re"], p["c_im"], p["ssm_d"], p["w_glu"],
      p["lv_re"], p["lv_im"], p["cp_re"], p["cp_im"])


def _ssm_params(a_re, a_im, log_dt, b_re, b_im, c_re, c_im):
    dt = jnp.exp(log_dt)[:, None]
    mag = jnp.exp(a_re * dt)
    abr = mag * jnp.cos(a_im * dt)
    abi = mag * jnp.sin(a_im * dt)
    den = a_re * a_re + a_im * a_im
    nr = abr - 1.0
    ni = abi
    coef_re = (nr * a_re + ni * a_im) / den
    coef_im = (ni * a_re - nr * a_im) / den
    bbr = coef_re[..., None] * b_re - coef_im[..., None] * b_im
    bbi = coef_re[..., None] * b_im + coef_im[..., None] * b_re

    half_g = SSM_GROUPS // 2
    eye = jnp.eye(half_g, dtype=F32)

    def b_blocks(bb):
        bb = bb.reshape(2, half_g, SSM_STATE, SSM_GROUP)
        m = jnp.einsum("zgph,gk->zghkp", bb, eye)
        return m.reshape(2, half_g * SSM_GROUP, half_g * SSM_STATE).astype(BF16)

    def c_blocks(cc):
        cc = cc.reshape(2, half_g, SSM_GROUP, SSM_STATE)
        m = jnp.einsum("zghp,gk->zgpkh", cc, eye)
        return m.reshape(2, half_g * SSM_STATE, half_g * SSM_GROUP).astype(BF16)

    ar = abr.reshape(1, SSM_CH)
    ai = abi.reshape(1, SSM_CH)
    pows = [(ar, ai)]
    for _ in range(SUBLANES - 1):
        pr, pi = pows[-1]
        pows.append((pr * ar - pi * ai, pr * ai + pi * ar))
    rows = jnp.arange(SUBLANES)[:, None]
    lv_re = jnp.stack([jnp.where(rows >= d, pows[d - 1][0], 0.0) for d in (1, 2, 4)])
    lv_im = jnp.stack([jnp.where(rows >= d, pows[d - 1][1], 0.0) for d in (1, 2, 4)])
    cp_re = jnp.concatenate([pw[0] for pw in pows], axis=0)
    cp_im = jnp.concatenate([pw[1] for pw in pows], axis=0)
    return dict(b_re=b_blocks(bbr), b_im=b_blocks(bbi), c_re=c_blocks(c_re), c_im=c_blocks(c_im),
                lv_re=lv_re, lv_im=lv_im, cp_re=cp_re, cp_im=cp_im)


def _memkv_kernel(m_ref, g_ref, w_ref, gk_ref, o128_ref, k_ref, v_ref):
    hm = _rms_rows(m_ref[0], g_ref[...]).astype(BF16)
    kv = jnp.dot(hm, w_ref[...], preferred_element_type=F32)
    k = kv[:, :MEM_WIDTH]
    k_ref[0] = k * lax.rsqrt(_seg_mean_sq(k, o128_ref[...], MEM_HD) + EPS) * gk_ref[...]
    v_ref[0] = kv[:, MEM_WIDTH:]


def _memkv(mem, p):
    n = mem.shape[0]
    blk = lambda i: (i, 0, 0)
    shp = jax.ShapeDtypeStruct((n, N_MEM, MEM_WIDTH), F32)
    return pl.pallas_call(
        _memkv_kernel,
        grid=(n,),
        in_specs=[pl.BlockSpec((1, N_MEM, D_MODEL), blk), _full((1, D_MODEL)),
                  _full((D_MODEL, 2 * MEM_WIDTH)), _full((1, MEM_WIDTH)), _full((MEM_WIDTH, MEM_WIDTH))],
        out_specs=(pl.BlockSpec((1, N_MEM, MEM_WIDTH), blk), pl.BlockSpec((1, N_MEM, MEM_WIDTH), blk)),
        out_shape=(shp, shp),
        compiler_params=_cparams(("parallel",)),
        name="memkv",
    )(mem, p["g_mem"], p["w_mem_kv"], p["g_km"], p["ones128"])


def _dup_heads(x, lane):
    sw = pltpu.roll(x, SWA_HD, axis=x.ndim - 1)
    lo = lane < SWA_HD
    return jnp.where(lo, x, sw), jnp.where(lo, sw, x)


def _swa_group(q_blk, kk, vv, g, mask, sink_ref):
    tq = q_blk.shape[-2]
    shp = q_blk.shape[:-1]
    lane = lax.broadcasted_iota(jnp.int32, shp + (LANES,), len(shp))
    rows = []
    sinks = []
    for hl in range(SWA_Q_PER_KV):
        h = g * SWA_Q_PER_KV + hl
        pair = q_blk[..., (h // 2) * LANES:(h // 2 + 1) * LANES]
        keep = (lane < SWA_HD) if h % 2 == 0 else (lane >= SWA_HD)
        rows.append(jnp.where(keep, pair, 0.0))
        sinks.append(jnp.full(shp + (1,), sink_ref[h], F32))
    qq = jnp.concatenate(rows, axis=-2).astype(BF16)
    sk = jnp.concatenate(sinks, axis=-2)
    s = jnp.einsum("...qd,...kd->...qk", qq, kk, preferred_element_type=F32)
    s = jnp.where(mask, s, NEG_INF)
    m = jnp.maximum(jnp.max(s, axis=-1, keepdims=True), sk)
    e = jnp.exp(s - m)
    pr = e / (jnp.sum(e, axis=-1, keepdims=True) + jnp.exp(sk - m))
    o = jnp.einsum("...qk,...kd->...qd", pr.astype(BF16), vv, preferred_element_type=F32)
    lo = lane < SWA_HD
    return [jnp.where(lo, o[..., (2 * j) * tq:(2 * j + 1) * tq, :], o[..., (2 * j + 1) * tq:(2 * j + 2) * tq, :])
            for j in range(2)]


def _mem_heads(qm, k_head, v_head):
    outs = []
    for h in range(MEM_HEADS):
        cs = slice(h * MEM_HD, (h + 1) * MEM_HD)
        s = jnp.einsum("...qd,...kd->...qk", qm[..., cs], k_head(h).astype(BF16),
                       preferred_element_type=F32) * MEM_SCALE
        m = jnp.max(s, axis=-1, keepdims=True)
        e = jnp.exp(s - m)
        pr = e / jnp.sum(e, axis=-1, keepdims=True)
        outs.append(jnp.einsum("...qk,...kd->...qd", pr.astype(BF16), v_head(h).astype(BF16),
                               preferred_element_type=F32))
    return jnp.concatenate(outs, axis=-1)


def _attn_prompt_kernel(sink_ref, q_ref, k_ref, v_ref, kp_ref, vp_ref, qm_ref, mk_ref, mv_ref, ys_ref, ym_ref):
    tq = q_ref.shape[0]
    blk = WINDOW
    rows = SWA_Q_PER_KV * blk
    i = lax.broadcasted_iota(jnp.int32, (rows, 2 * blk), 0) % blk
    j = lax.broadcasted_iota(jnp.int32, (rows, 2 * blk), 1)
    lo = jnp.where(j < blk, i + 1, blk)
    hi = jnp.where(j < blk, blk, blk + i + 1)
    first_lo = jnp.where(pl.program_id(1) == 0, blk, 0)
    lane_k = lax.broadcasted_iota(jnp.int32, (2 * blk, LANES), 1)
    for b in range(tq // blk):
        rs = slice(b * blk, (b + 1) * blk)
        if b == 0:
            k2 = jnp.concatenate([kp_ref[...], k_ref[rs, :]], axis=0)
            v2 = jnp.concatenate([vp_ref[...], v_ref[rs, :]], axis=0)
            mask = (j >= jnp.maximum(lo, first_lo)) & (j < hi)
        else:
            k2 = k_ref[(b - 1) * blk:(b + 1) * blk, :]
            v2 = v_ref[(b - 1) * blk:(b + 1) * blk, :]
            mask = (j >= lo) & (j < hi)
        kks = _dup_heads(k2, lane_k)
        vvs = _dup_heads(v2, lane_k)
        q_blk = q_ref[rs, :].astype(F32)
        pairs = []
        for g in range(SWA_KV_HEADS):
            pairs += _swa_group(q_blk, kks[g].astype(BF16), vvs[g].astype(BF16), g, mask, sink_ref)
        ys_ref[rs, :] = jnp.concatenate(pairs, axis=1).astype(BF16)
    ym_ref[...] = _mem_heads(qm_ref[...], lambda h: mk_ref[0, :, h * MEM_HD:(h + 1) * MEM_HD],
                             lambda h: mv_ref[0, :, h * MEM_HD:(h + 1) * MEM_HD]).astype(BF16)


def _attn_prompt(q, k, v, qm, mk, mv, sinks, n_seq):
    t = q.shape[0]
    tq = TQ_ATT
    per = t // n_seq // tq
    sub = tq // WINDOW
    row = lambda n, c: (n * per + c, 0)
    prev = lambda n, c: (jnp.maximum((n * per + c) * sub - 1, 0), 0)
    memb = lambda n, c: (n, 0, 0)
    return pl.pallas_call(
        _attn_prompt_kernel,
        grid=(n_seq, per),
        in_specs=[
            pl.BlockSpec(memory_space=pltpu.SMEM),
            pl.BlockSpec((tq, SWA_WIDTH), row),
            pl.BlockSpec((tq, SWA_KV_WIDTH), row),
            pl.BlockSpec((tq, SWA_KV_WIDTH), row),
            pl.BlockSpec((WINDOW, SWA_KV_WIDTH), prev),
            pl.BlockSpec((WINDOW, SWA_KV_WIDTH), prev),
            pl.BlockSpec((tq, MEM_WIDTH), row),
            pl.BlockSpec((1, N_MEM, MEM_WIDTH), memb),
            pl.BlockSpec((1, N_MEM, MEM_WIDTH), memb),
        ],
        out_specs=(pl.BlockSpec((tq, SWA_WIDTH), row), pl.BlockSpec((tq, MEM_WIDTH), row)),
        out_shape=(jax.ShapeDtypeStruct((t, SWA_WIDTH), BF16), jax.ShapeDtypeStruct((t, MEM_WIDTH), BF16)),
        compiler_params=_cparams(("parallel", "parallel")),
        name="attn_prompt",
    )(sinks, q, k, v, k, v, qm, mk, mv)


def _attn_sample_kernel(sink_ref, q_ref, k_ref, v_ref, pk_ref, pv_ref, qm_ref, mk_ref, mv_ref,
                        ys_ref, ym_ref, nk_ref, nv_ref, *, s_len):
    sb, wb = pk_ref.shape[0], pk_ref.shape[1]
    n_keys = wb + s_len
    rows = SWA_Q_PER_KV * s_len
    i = lax.broadcasted_iota(jnp.int32, (sb, rows, n_keys), 1) % s_len
    j = lax.broadcasted_iota(jnp.int32, (sb, rows, n_keys), 2)
    rel = i + wb - j
    mask = (rel >= 0) & (rel < WINDOW)
    k_all = jnp.concatenate([pk_ref[...], k_ref[...].reshape(sb, s_len, SWA_KV_WIDTH)], axis=1)
    v_all = jnp.concatenate([pv_ref[...], v_ref[...].reshape(sb, s_len, SWA_KV_WIDTH)], axis=1)
    nk_ref[...] = k_all[:, n_keys - wb:, :]
    nv_ref[...] = v_all[:, n_keys - wb:, :]
    lane_k = lax.broadcasted_iota(jnp.int32, k_all.shape, 2)
    kks = _dup_heads(k_all, lane_k)
    vvs = _dup_heads(v_all, lane_k)
    q3 = q_ref[...].astype(F32).reshape(sb, s_len, SWA_WIDTH)
    pairs = []
    for g in range(SWA_KV_HEADS):
        pairs += _swa_group(q3, kks[g].astype(BF16), vvs[g].astype(BF16), g, mask, sink_ref)
    ys_ref[...] = jnp.concatenate(pairs, axis=-1).reshape(sb * s_len, SWA_WIDTH).astype(BF16)
    qm3 = qm_ref[...].astype(F32).reshape(sb, s_len, MEM_WIDTH).astype(BF16)
    head_rows = lambda h: pl.ds(h, N_MEM, stride=MEM_HEADS)
    ym = _mem_heads(qm3, lambda h: mk_ref[:, head_rows(h), :], lambda h: mv_ref[:, head_rows(h), :])
    ym_ref[...] = ym.reshape(sb * s_len, MEM_WIDTH).astype(BF16)


def _attn_sample(q, k, v, past_k, past_v, qm, mk, mv, sinks, s_len):
    t = q.shape[0]
    n_seq, wb = past_k.shape[0], past_k.shape[1]
    sb = SEQ_BLK
    rows = sb * s_len
    row = lambda c: (c, 0)
    seq = lambda c: (c, 0, 0)
    cache_shape = jax.ShapeDtypeStruct((n_seq, wb, SWA_KV_WIDTH), F32)
    return pl.pallas_call(
        functools.partial(_attn_sample_kernel, s_len=s_len),
        grid=(n_seq // sb,),
        in_specs=[
            pl.BlockSpec(memory_space=pltpu.SMEM),
            pl.BlockSpec((rows, SWA_WIDTH), row),
            pl.BlockSpec((rows, SWA_KV_WIDTH), row),
            pl.BlockSpec((rows, SWA_KV_WIDTH), row),
            pl.BlockSpec((sb, wb, SWA_KV_WIDTH), seq),
            pl.BlockSpec((sb, wb, SWA_KV_WIDTH), seq),
            pl.BlockSpec((rows, MEM_WIDTH), row),
            pl.BlockSpec((sb, N_MEM * MEM_HEADS, MEM_HD), seq),
            pl.BlockSpec((sb, N_MEM * MEM_HEADS, MEM_HD), seq),
        ],
        out_specs=(pl.BlockSpec((rows, SWA_WIDTH), row), pl.BlockSpec((rows, MEM_WIDTH), row),
                   pl.BlockSpec((sb, wb, SWA_KV_WIDTH), seq), pl.BlockSpec((sb, wb, SWA_KV_WIDTH), seq)),
        out_shape=(jax.ShapeDtypeStruct((t, SWA_WIDTH), BF16), jax.ShapeDtypeStruct((t, MEM_WIDTH), BF16),
                   cache_shape, cache_shape),
        compiler_params=_cparams(("parallel",)),
        name="attn_sample",
    )(sinks, q, k, v, past_k, past_v, qm, mk, mv)


def _first_argmax(x, valid, lane):
    xm = jnp.where(valid, x, -jnp.inf)
    mx = jnp.max(xm, axis=-1, keepdims=True)
    idx = jnp.min(jnp.where(xm == mx, lane, LANES), axis=-1, keepdims=True)
    return mx, lane == idx, idx


def _merge_kernel(xp_ref, yap_ref, ybp_ref, ycp_ref, xs_ref, yas_ref, ybs_ref, ycs_ref, *rest, n_blk_p):
    @pl.when(pl.program_id(0) < n_blk_p)
    def _():
        _merge_rows(xp_ref, yap_ref, ybp_ref, ycp_ref, *rest)

    @pl.when(pl.program_id(0) >= n_blk_p)
    def _():
        _merge_rows(xs_ref, yas_ref, ybs_ref, ycs_ref, *rest)


def _merge_rows(x_ref, ya_ref, yb_ref, yc_ref, gmix_ref, wg_ref, wa_ref, wb_ref, wc_ref, wo_ref, gffn_ref,
                wr_hi_ref, wr_lo_ref, br_ref, x1_ref, hx_ref, route_ref, cnt_ref, carry_ref):
    x = x_ref[...]
    h = _rms_rows(x, gmix_ref[...]).astype(BF16)
    gates = jax.nn.sigmoid(jnp.dot(h, wg_ref[...], preferred_element_type=F32))
    merged = (gates[:, :D_MODEL] * jnp.dot(ya_ref[...], wa_ref[...], preferred_element_type=F32)
              + gates[:, D_MODEL:2 * D_MODEL] * jnp.dot(yb_ref[...], wb_ref[...], preferred_element_type=F32)
              + gates[:, 2 * D_MODEL:] * jnp.dot(yc_ref[...], wc_ref[...], preferred_element_type=F32))
    x1 = x + jnp.dot(merged.astype(BF16), wo_ref[...], preferred_element_type=F32)
    x1_ref[...] = x1
    hn = _rms_rows(x1, gffn_ref[...])
    tm = x.shape[0]
    slab_row = lambda c: pl.ds(c, tm, stride=HX_ROWS)
    for c in range(ROW_TILES):
        hx_ref[slab_row(c), :] = hn[:, c * LANES:(c + 1) * LANES]

    hi, lo = _split_bf16(hn)
    w_hi = wr_hi_ref[...]
    logits = (jnp.dot(hi, w_hi, preferred_element_type=F32) + jnp.dot(lo, w_hi, preferred_element_type=F32)
              + jnp.dot(hi, wr_lo_ref[...], preferred_element_type=F32)) + br_ref[...]
    lane = lax.broadcasted_iota(jnp.int32, logits.shape, 1)
    is_grp = lane < N_GROUPS_E
    g_max, _, g_idx = _first_argmax(logits, is_grp, lane)
    pg_top = 1.0 / jnp.sum(jnp.where(is_grp, jnp.exp(logits - g_max), 0.0), axis=-1, keepdims=True)
    e_lo = N_GROUPS_E + g_idx * EXPERTS_PER_GROUP
    in_grp = (lane >= e_lo) & (lane < e_lo + EXPERTS_PER_GROUP)
    e_max, first, i1 = _first_argmax(logits, in_grp, lane)
    ex = jnp.where(in_grp, jnp.exp(logits - e_max), 0.0)
    pe = ex / jnp.sum(ex, axis=-1, keepdims=True)
    _, second, i2 = _first_argmax(logits, in_grp & jnp.logical_not(first), lane)
    p1 = jnp.sum(jnp.where(first, pe, 0.0), axis=-1, keepdims=True)
    p2 = jnp.sum(jnp.where(second, pe, 0.0), axis=-1, keepdims=True)
    w1 = pg_top * p1 / (p1 + p2)
    w2 = pg_top * p2 / (p1 + p2)

    a1 = i1 - e_lo
    a2 = i2 - e_lo
    e_a = jnp.minimum(a1, a2)
    e_b = jnp.maximum(a1, a2)
    pair = jnp.right_shift(e_a * (2 * EXPERTS_PER_GROUP - 1 - e_a), 1) + (e_b - e_a - 1)
    bucket = g_idx * PAIRS_PER_GROUP + pair
    w_a = jnp.where(a1 < a2, w1, w2)
    w_b = jnp.where(a1 < a2, w2, w1)

    @pl.when(pl.program_id(0) == 0)
    def _():
        carry_ref[...] = jnp.zeros_like(carry_ref)

    onehot = lane == bucket
    tri = (lax.broadcasted_iota(jnp.int32, (tm, tm), 1) <= lax.broadcasted_iota(jnp.int32, (tm, tm), 0))
    csum = jnp.dot(jnp.where(tri, 1.0, 0.0).astype(BF16), jnp.where(onehot, 1.0, 0.0).astype(BF16),
                   preferred_element_type=F32)
    carry = carry_ref[...]
    rank = jnp.sum(jnp.where(onehot, csum + carry, 0.0), axis=-1, keepdims=True) - 1.0
    carry = carry + csum[tm - 1:tm, :]
    carry_ref[...] = carry
    cnt_ref[...] = carry
    route = jnp.where(lane == 0, bucket.astype(F32),
                      jnp.where(lane == 1, w_a, jnp.where(lane == 2, w_b, jnp.where(lane == 3, rank, 0.0))))
    route_ref[...] = route
    hx_ref[slab_row(ROW_TILES), :] = route
    for c in range(ROW_TILES + 1, HX_ROWS):
        hx_ref[slab_row(c), :] = jnp.zeros_like(route)


def _merge(prompt_rows, sample_rows, p):
    tm = TM_MRG
    t_p, t_s = prompt_rows[0].shape[0], sample_rows[0].shape[0]
    nbp = t_p // tm
    t_all = t_p + t_s
    first = lambda i: (jnp.minimum(i, nbp - 1), 0)
    second = lambda i: (jnp.maximum(i - nbp, 0), 0)
    row = lambda i: (i, 0)
    widths = (D_MODEL, SSM_WIDTH, SWA_WIDTH, MEM_WIDTH)
    in_specs = ([pl.BlockSpec((tm, w), first) for w in widths] + [pl.BlockSpec((tm, w), second) for w in widths] + [
        _full((1, D_MODEL)),
        _full((D_MODEL, N_BRANCH * D_MODEL)),
        _full((SSM_WIDTH, D_MODEL)), _full((SWA_WIDTH, D_MODEL)), _full((MEM_WIDTH, D_MODEL)),
        _full((D_MODEL, D_MODEL)),
        _full((1, D_MODEL)),
        _full((D_MODEL, LANES)), _full((D_MODEL, LANES)), _full((1, LANES)),
    ])
    return pl.pallas_call(
        functools.partial(_merge_kernel, n_blk_p=nbp),
        grid=(t_all // tm,),
        in_specs=in_specs,
        out_specs=(pl.BlockSpec((tm, D_MODEL), row), pl.BlockSpec((tm * HX_ROWS, LANES), row),
                   pl.BlockSpec((tm, ROUTE_W), row), _full((1, LANES))),
        out_shape=(jax.ShapeDtypeStruct((t_all, D_MODEL), F32), jax.ShapeDtypeStruct((t_all * HX_ROWS, LANES), F32),
                   jax.ShapeDtypeStruct((t_all, ROUTE_W), F32), jax.ShapeDtypeStruct((1, LANES), F32)),
        scratch_shapes=[pltpu.VMEM((1, LANES), F32)],
        compiler_params=_cparams(("arbitrary",)),
        name="merge",
    )(*prompt_rows, *sample_rows, p["g_mix"], p["w_gates"], p["w_br_ssm"], p["w_br_swa"], p["w_br_mem"], p["w_o"],
      p["g_ffn"], p["w_r_hi"], p["w_r_lo"], p["b_r"])


def _pos_kernel(route_ref, off_ref, coff_ref, pos_ref):
    r = route_ref[...]
    lane = lax.broadcasted_iota(jnp.int32, r.shape, 1)
    mine = lane == r[:, 0:1].astype(jnp.int32)
    off = jnp.sum(jnp.where(mine, off_ref[...], 0.0), axis=-1, keepdims=True)
    coff = jnp.sum(jnp.where(mine, coff_ref[...], 0.0), axis=-1, keepdims=True)
    rank = r[:, 3:4]
    pos_ref[...] = jnp.where(lane == 0, off + rank, coff + rank).astype(jnp.int32)


def _sorted_pos(route, off, coff):
    t = route.shape[0]
    tm = TM_IN
    return pl.pallas_call(
        _pos_kernel,
        grid=(t // tm,),
        in_specs=[pl.BlockSpec((tm, ROUTE_W), lambda i: (i, 0)), _full((1, LANES)), _full((1, LANES))],
        out_specs=pl.BlockSpec((tm, LANES), lambda i: (i, 0)),
        out_shape=jax.ShapeDtypeStruct((t, LANES), jnp.int32),
        compiler_params=_cparams(("parallel",)),
        name="sorted_pos",
    )(route, off, coff)


def _inv_kernel(pos_ref, idx_ref):
    def body(t, _):
        idx_ref[pos_ref[t]] = t
        return 0

    lax.fori_loop(0, pos_ref.shape[0], body, 0, unroll=8)


def _invert(pos):
    return pl.pallas_call(
        _inv_kernel,
        in_specs=[pl.BlockSpec(memory_space=pltpu.SMEM)],
        out_specs=pl.BlockSpec(memory_space=pltpu.SMEM),
        out_shape=jax.ShapeDtypeStruct(pos.shape, jnp.int32),
        name="invert_perm",
    )(pos)


def _bucket_kernel(idx_ref, ta_ref, tb_ref, nv_ref, cb_ref, nr_ref, hx_hbm, wia_ref, wda_ref, wib_ref, wdb_ref,
                   ys_ref, buf, sem):
    j = pl.program_id(0)
    n_real = nr_ref[0]

    def issue_row(tile_base, n_valid, slot, r, dst_row):
        src = idx_ref[tile_base + jnp.where(r < n_valid, r, 0)]
        pltpu.make_async_copy(hx_hbm.at[pl.ds(pl.multiple_of(src * HX_ROWS, HX_ROWS), HX_ROWS), :],
                              buf.at[slot, pl.ds(dst_row, HX_ROWS), :], sem.at[slot]).start()

    def wait_tile(slot):
        pltpu.make_async_copy(hx_hbm.at[pl.ds(0, TM_EXP * HX_ROWS), :], buf.at[slot], sem.at[slot]).wait()

    @pl.when(j == 0)
    def _():
        base, n_valid = cb_ref[0], nv_ref[0]

        def body(r, _):
            issue_row(base, n_valid, 0, r, pl.multiple_of(r * HX_ROWS, HX_ROWS))
            return 0

        lax.fori_loop(0, TM_EXP, body, 0, unroll=8)

    @pl.when(j < n_real)
    def _():
        slot = j % 2
        wait_tile(slot)
        nxt = jnp.minimum(j + 1, n_real - 1)
        base, n_valid = cb_ref[nxt], nv_ref[nxt]
        for r in range(TM_EXP):
            issue_row(base, n_valid, 1 - slot, r, r * HX_ROWS)

        slab_row = lambda c: buf[slot, pl.ds(c, TM_EXP, stride=HX_ROWS), :]
        x = jnp.concatenate([slab_row(c) for c in range(ROW_TILES)], axis=1).astype(BF16)
        route = slab_row(ROW_TILES)

        def ffn(wi_ref, wd_ref):
            gu = jnp.dot(x, wi_ref[0], preferred_element_type=F32)
            a = jax.nn.silu(gu[:, :D_FF]) * gu[:, D_FF:]
            return jnp.dot(a.astype(BF16), wd_ref[0], preferred_element_type=F32)

        y = route[:, 1:2] * ffn(wia_ref, wda_ref) + route[:, 2:3] * ffn(wib_ref, wdb_ref)
        for c in range(ROW_TILES):
            ys_ref[pl.ds(c, TM_EXP, stride=ROW_TILES), :] = y[:, c * LANES:(c + 1) * LANES]

        @pl.when(j == n_real - 1)
        def _():
            wait_tile(1 - slot)


def _bucket_ffn(hx, idx, tile_a, tile_b, tile_nv, tile_cb, n_real, p):
    n_tiles = tile_a.shape[0]
    wa = lambda j, idx, ta, tb, nv, cb, nr: (ta[j], 0, 0)
    wb = lambda j, idx, ta, tb, nv, cb, nr: (tb[j], 0, 0)
    out = lambda j, idx, ta, tb, nv, cb, nr: (jnp.minimum(j, nr[0] - 1), 0)
    return pl.pallas_call(
        _bucket_kernel,
        grid_spec=pltpu.PrefetchScalarGridSpec(
            num_scalar_prefetch=6,
            grid=(n_tiles,),
            in_specs=[
                pl.BlockSpec(memory_space=pl.ANY),
                pl.BlockSpec((1, D_MODEL, 2 * D_FF), wa), pl.BlockSpec((1, D_FF, D_MODEL), wa),
                pl.BlockSpec((1, D_MODEL, 2 * D_FF), wb), pl.BlockSpec((1, D_FF, D_MODEL), wb),
            ],
            out_specs=pl.BlockSpec((TM_EXP * ROW_TILES, LANES), out),
            scratch_shapes=[pltpu.VMEM((2, TM_EXP * HX_ROWS, LANES), F32), pltpu.SemaphoreType.DMA((2,))],
        ),
        out_shape=jax.ShapeDtypeStruct((n_tiles * TM_EXP * ROW_TILES, LANES), F32),
        compiler_params=_cparams(("arbitrary",)),
        name="bucket_ffn",
    )(idx, tile_a, tile_b, tile_nv, tile_cb, n_real, hx,
      p["w_exp_in"], p["w_exp_down"], p["w_exp_in"], p["w_exp_down"])


def _back_kernel(pos_ref, x1_ref, ys_hbm, o_ref, buf, sem, *, t0):
    i = pl.program_id(0)
    tm = o_ref.shape[0]

    def issue(tile, slot):
        base = t0 + tile * tm

        def body(r, _):
            src = pl.multiple_of(pos_ref[base + r] * ROW_TILES, ROW_TILES)
            pltpu.make_async_copy(ys_hbm.at[pl.ds(src, ROW_TILES), :],
                                  buf.at[slot, pl.ds(pl.multiple_of(r * ROW_TILES, ROW_TILES), ROW_TILES), :],
                                  sem.at[slot]).start()
            return 0

        lax.fori_loop(0, tm, body, 0, unroll=8)

    @pl.when(i == 0)
    def _():
        issue(0, 0)

    @pl.when(i + 1 < pl.num_programs(0))
    def _():
        issue(i + 1, (i + 1) % 2)

    slot = i % 2
    pltpu.make_async_copy(ys_hbm.at[pl.ds(0, tm * ROW_TILES), :], buf.at[slot], sem.at[slot]).wait()
    y = jnp.concatenate([buf[slot, pl.ds(c, tm, stride=ROW_TILES), :] for c in range(ROW_TILES)], axis=1)
    o_ref[...] = x1_ref[...] + y


def _unsort_add(x1, ys, pos, t0, t):
    tm = TM_MRG
    row = lambda i, pos: (i, 0)
    return pl.pallas_call(
        functools.partial(_back_kernel, t0=t0),
        grid_spec=pltpu.PrefetchScalarGridSpec(
            num_scalar_prefetch=1,
            grid=(t // tm,),
            in_specs=[pl.BlockSpec((tm, D_MODEL), lambda i, pos: (i + t0 // tm, 0)),
                      pl.BlockSpec(memory_space=pl.ANY)],
            out_specs=pl.BlockSpec((tm, D_MODEL), row),
            scratch_shapes=[pltpu.VMEM((2, tm * ROW_TILES, LANES), F32), pltpu.SemaphoreType.DMA((2,))],
        ),
        out_shape=jax.ShapeDtypeStruct((t, D_MODEL), F32),
        compiler_params=_cparams(("arbitrary",)),
        name="unsort_add",
    )(pos, x1, ys)


def _bucket_experts():
    lo, hi = [], []
    for g in range(N_GROUPS_E):
        for a in range(EXPERTS_PER_GROUP):
            for b in range(a + 1, EXPERTS_PER_GROUP):
                lo.append(g * EXPERTS_PER_GROUP + a)
                hi.append(g * EXPERTS_PER_GROUP + b)
    return jnp.asarray(lo, jnp.int32), jnp.asarray(hi, jnp.int32)


def _tile_tables(counts, n_tiles):
    cnt = counts[0, :N_BUCKETS].astype(jnp.int32)
    nt = (cnt + TM_EXP - 1) // TM_EXP
    tend = jnp.cumsum(nt)
    tstart = tend - nt
    cstart = jnp.cumsum(cnt) - cnt
    pad = lambda v: jnp.zeros((1, LANES), F32).at[0, :N_BUCKETS].set(v.astype(F32))
    j = jnp.arange(n_tiles, dtype=jnp.int32)
    b = jnp.minimum(jnp.sum((tend[None, :] <= j[:, None]).astype(jnp.int32), axis=1), N_BUCKETS - 1)
    in_bucket = (j - tstart[b]) * TM_EXP
    nv = jnp.clip(cnt[b] - in_bucket, 0, TM_EXP)
    e_lo, e_hi = _bucket_experts()
    return pad(tstart * TM_EXP), pad(cstart), e_lo[b], e_hi[b], nv, cstart[b] + in_bucket, tend[-1:]


def _rope_tables(pos):
    half = SWA_HD // 2
    inv = ROPE_THETA ** (-jnp.arange(half, dtype=F32) / half)
    ang = pos.astype(F32)[:, None] * inv[None, :]
    cos = jnp.cos(ang)
    sin = jnp.sin(ang)
    cos = jnp.concatenate([cos, cos, cos, cos], axis=1)
    sin = jnp.concatenate([-sin, sin, -sin, sin], axis=1)
    return cos, sin


def kernel(x_prompt, x_sample, mem_prompt, state_ssm_re, state_ssm_im, cache_swa_k, cache_swa_v, cache_mem_k, cache_mem_v, norm_mix, w_in, ssm_a_re, ssm_a_im, ssm_log_dt, ssm_b_re, ssm_b_im, ssm_c_re, ssm_c_im, ssm_d, w_glu, swa_q_norm, swa_k_norm, swa_sinks, norm_mem, w_mem_kv, mem_q_norm, mem_k_norm, w_br_ssm, w_br_swa, w_br_mem, w_o, norm_ffn, w_router_group, b_router_group, w_router_expert, b_router_expert, w_exp_in, w_exp_down):
    depth = w_in.shape[0]
    assert depth == 1
    nb, seq, _ = x_prompt.shape
    db, dseq, _ = x_sample.shape
    assert dseq == SUBLANES
    l = 0

    w_r = jnp.concatenate([w_router_group[l], w_router_expert[l]], axis=1)
    w_r = jnp.pad(w_r, ((0, 0), (0, LANES - w_r.shape[1])))
    w_r_hi = w_r.astype(BF16)
    b_r = jnp.pad(jnp.concatenate([b_router_group[l], b_router_expert[l]]), (0, LANES - N_GROUPS_E - N_EXPERTS))
    p = dict(
        g_mix=norm_mix[l][None], w_a=w_in[l][:, :PROJ_A].astype(BF16), w_gates=w_in[l][:, PROJ_A:].astype(BF16),
        g_q=jnp.tile(swa_q_norm[l], SWA_HEADS)[None], g_k=jnp.tile(swa_k_norm[l], SWA_KV_HEADS)[None],
        g_qm=jnp.tile(mem_q_norm[l], MEM_HEADS)[None], g_km=jnp.tile(mem_k_norm[l], MEM_HEADS)[None],
        ones64=_block_ones(SWA_WIDTH, SWA_HD), ones128=_block_ones(MEM_WIDTH, MEM_HD),
        ssm_d=ssm_d[l][None], w_glu=w_glu[l].astype(BF16),
        g_mem=norm_mem[l][None], w_mem_kv=w_mem_kv[l].astype(BF16),
        w_br_ssm=w_br_ssm[l].astype(BF16), w_br_swa=w_br_swa[l].astype(BF16), w_br_mem=w_br_mem[l].astype(BF16),
        w_o=w_o[l].astype(BF16), g_ffn=norm_ffn[l][None],
        w_r_hi=w_r_hi, w_r_lo=(w_r - w_r_hi.astype(F32)).astype(BF16), b_r=b_r[None],
        w_exp_in=w_exp_in[l].astype(BF16), w_exp_down=w_exp_down[l].astype(BF16),
    )
    p.update(_ssm_params(ssm_a_re[l], ssm_a_im[l], ssm_log_dt[l], ssm_b_re[l], ssm_b_im[l],
                         ssm_c_re[l], ssm_c_im[l]))
    sinks = swa_sinks[l]

    xp = x_prompt.reshape(nb * seq, D_MODEL)
    cos_p, sin_p = _rope_tables(jnp.arange(seq, dtype=jnp.int32))
    u, q, k, v, qm = _inproj(xp, cos_p, sin_p, seq // TM_IN, p)
    zeros_state = jnp.zeros((nb, 1, SSM_CH), F32)
    y_ssm, pr, pi = _ssm(u, zeros_state, zeros_state, p, n_seq=nb, chained=True)
    mk, mv = _memkv(mem_prompt, p)
    y_swa, y_mem = _attn_prompt(q, k, v, qm, mk, mv, sinks, nb)
    win = min(WINDOW, seq)
    p_k = k.reshape(nb, seq, SWA_KV_HEADS, SWA_HD)[:, seq - win:]
    p_v = v.reshape(nb, seq, SWA_KV_HEADS, SWA_HD)[:, seq - win:]

    xs = x_sample.reshape(db * dseq, D_MODEL)
    cos_s, sin_s = _rope_tables(PAST_LEN + jnp.arange(dseq, dtype=jnp.int32))
    reps = TM_IN // dseq
    us, qs, ks, vs, qms = _inproj(xs, jnp.tile(cos_s, (reps, 1)), jnp.tile(sin_s, (reps, 1)), 1, p)
    ys_ssm, sr, si = _ssm(us, state_ssm_re[l].reshape(db, SSM_CH), state_ssm_im[l].reshape(db, SSM_CH), p,
                          n_seq=db, chained=False)
    wb = cache_swa_k.shape[2]
    ys_swa, ys_mem, s_k, s_v = _attn_sample(
        qs, ks, vs, cache_swa_k[l].reshape(db, wb, SWA_KV_WIDTH), cache_swa_v[l].reshape(db, wb, SWA_KV_WIDTH),
        qms, cache_mem_k.reshape(db, N_MEM * MEM_HEADS, MEM_HD), cache_mem_v.reshape(db, N_MEM * MEM_HEADS, MEM_HD),
        sinks, dseq)
    t_p, t_s = nb * seq, db * dseq
    t_all = t_p + t_s
    x1, hx, route, counts = _merge((xp, y_ssm, y_swa, y_mem), (xs, ys_ssm, ys_swa, ys_mem), p)

    n_tiles = t_all // TM_EXP + N_BUCKETS
    off, coff, tile_a, tile_b, tile_nv, tile_cb, n_real = _tile_tables(counts, n_tiles)
    pos2 = _sorted_pos(route, off, coff)
    pos = pos2[:, 0]
    idx = _invert(pos2[:, 1])
    y_sorted = _bucket_ffn(hx, idx, tile_a, tile_b, tile_nv, tile_cb, n_real, p)
    yp = _unsort_add(x1, y_sorted, pos, 0, t_p).reshape(nb, seq, D_MODEL)
    ys = _unsort_add(x1, y_sorted, pos, t_p, t_s).reshape(db, dseq, D_MODEL)

    g, s = SSM_GROUPS, SSM_STATE
    return (yp, ys,
            pr.reshape(1, nb, g, s), pi.reshape(1, nb, g, s),
            p_k[None], p_v[None],
            mk.reshape(1, nb, N_MEM, MEM_HEADS, MEM_HD), mv.reshape(1, nb, N_MEM, MEM_HEADS, MEM_HD),
            sr.reshape(1, db, g, s), si.reshape(1, db, g, s),
            s_k.reshape(1, db, wb, SWA_KV_HEADS, SWA_HD), s_v.reshape(1, db, wb, SWA_KV_HEADS, SWA_HD))
```

```python
import functools
import math

import jax
import jax.numpy as jnp
from jax import lax
from jax.experimental import pallas as pl
from jax.experimental.pallas import tpu as pltpu

F32 = jnp.float32
BF16 = jnp.bfloat16

D_MODEL = 1024
SSM_WIDTH = 512
SSM_GROUP = 16
SSM_GROUPS = 32
SSM_STATE = 64
SSM_CH = SSM_GROUPS * SSM_STATE
SWA_HEADS = 8
SWA_KV_HEADS = 2
SWA_Q_PER_KV = SWA_HEADS // SWA_KV_HEADS
SWA_HD = 64
SWA_WIDTH = SWA_HEADS * SWA_HD
SWA_KV_WIDTH = SWA_KV_HEADS * SWA_HD
WINDOW = 128
PAST_LEN = 16384
ROPE_THETA = 10000.0
N_MEM = 256
MEM_HEADS = 4
MEM_HD = 128
MEM_WIDTH = MEM_HEADS * MEM_HD
N_BRANCH = 3
PROJ_A = SSM_WIDTH + SWA_WIDTH + 2 * SWA_KV_WIDTH + MEM_WIDTH
N_GROUPS_E = 4
EXPERTS_PER_GROUP = 8
N_EXPERTS = 32
D_FF = 256
EPS = 1e-6
NEG_INF = -1e30
SWA_SCALE = SWA_HD ** -0.5
MEM_SCALE = MEM_HD ** -0.5

LANES = 128
SUBLANES = 8
VMEM_LIMIT = 56 * 1024 * 1024

TM_IN = 512
TC_SSM = 512
SCAN_W = 512
TQ_ATT = 512
SEQ_BLK = 8
TM_MRG = 256
TM_EXP = 256

ROUTE_W = LANES
ROW_TILES = D_MODEL // LANES
HX_ROWS = 2 * ROW_TILES
PAIRS_PER_GROUP = EXPERTS_PER_GROUP * (EXPERTS_PER_GROUP - 1) // 2
N_BUCKETS = N_GROUPS_E * PAIRS_PER_GROUP


def _cparams(sem):
    return pltpu.CompilerParams(dimension_semantics=sem, vmem_limit_bytes=VMEM_LIMIT)


def _full(shape):
    nd = len(shape)
    return pl.BlockSpec(shape, lambda *_: (0,) * nd)


def _split_bf16(x):
    hi = x.astype(BF16)
    lo = (x - hi.astype(F32)).astype(BF16)
    return hi, lo


def _seg_mean_sq(x, ones_blk, width):
    hi, lo = _split_bf16(x * x)
    s = jnp.dot(hi, ones_blk, preferred_element_type=F32) + jnp.dot(lo, ones_blk, preferred_element_type=F32)
    return s * (1.0 / width)


def _rms_rows(x, gain):
    return x * lax.rsqrt(jnp.mean(x * x, axis=-1, keepdims=True) + EPS) * gain


def _block_ones(n, width):
    i = jnp.arange(n) // width
    return (i[:, None] == i[None, :]).astype(BF16)


def _rope_cols(x, cos, sin_signed, lane_in_head):
    n = x.shape[1]
    reps = n // LANES
    if reps > 1:
        cos = jnp.concatenate([cos] * reps, axis=1)
        sin_signed = jnp.concatenate([sin_signed] * reps, axis=1)
    half = SWA_HD // 2
    partner = jnp.where(lane_in_head < half, pltpu.roll(x, n - half, axis=1), pltpu.roll(x, half, axis=1))
    return x * cos + partner * sin_signed


def _inproj_kernel(x_ref, gmix_ref, w_ref, gq_ref, gk_ref, gm_ref, cos_ref, sin_ref, o64_ref, o128_ref,
                   u_ref, q_ref, k_ref, v_ref, qm_ref):
    x = x_ref[...]
    h = _rms_rows(x, gmix_ref[...]).astype(BF16)
    proj = jnp.dot(h, w_ref[...], preferred_element_type=F32)
    c0 = SSM_WIDTH
    c1 = c0 + SWA_WIDTH
    c2 = c1 + SWA_KV_WIDTH
    c3 = c2 + SWA_KV_WIDTH
    u_ref[...] = proj[:, :c0]
    q = proj[:, c0:c1]
    k = proj[:, c1:c2]
    v_ref[...] = proj[:, c2:c3]
    qm = proj[:, c3:]
    cos = cos_ref[...]
    sin = sin_ref[...]
    o64 = o64_ref[...]
    lane_q = lax.broadcasted_iota(jnp.int32, q.shape, 1) % SWA_HD
    qn = q * lax.rsqrt(_seg_mean_sq(q, o64, SWA_HD) + EPS) * gq_ref[...]
    q_ref[...] = (_rope_cols(qn, cos, sin, lane_q) * SWA_SCALE).astype(BF16)
    lane_k = lax.broadcasted_iota(jnp.int32, k.shape, 1) % SWA_HD
    kn = k * lax.rsqrt(_seg_mean_sq(k, o64[:SWA_KV_WIDTH, :SWA_KV_WIDTH], SWA_HD) + EPS) * gk_ref[...]
    k_ref[...] = _rope_cols(kn, cos, sin, lane_k)
    qmn = qm * lax.rsqrt(_seg_mean_sq(qm, o128_ref[...], MEM_HD) + EPS) * gm_ref[...]
    qm_ref[...] = qmn.astype(BF16)


def _inproj(x2d, cos, sin, pos_blocks, p):
    t = x2d.shape[0]
    tm = TM_IN
    grid = (t // tm,)
    row = lambda i: (i, 0)
    tab = lambda i: (i % pos_blocks, 0)
    out_shape = (
        jax.ShapeDtypeStruct((t, SSM_WIDTH), F32),
        jax.ShapeDtypeStruct((t, SWA_WIDTH), BF16),
        jax.ShapeDtypeStruct((t, SWA_KV_WIDTH), F32),
        jax.ShapeDtypeStruct((t, SWA_KV_WIDTH), F32),
        jax.ShapeDtypeStruct((t, MEM_WIDTH), BF16),
    )
    return pl.pallas_call(
        _inproj_kernel,
        grid=grid,
        in_specs=[
            pl.BlockSpec((tm, D_MODEL), row),
            _full((1, D_MODEL)),
            _full((D_MODEL, PROJ_A)),
            _full((1, SWA_WIDTH)),
            _full((1, SWA_KV_WIDTH)),
            _full((1, MEM_WIDTH)),
            pl.BlockSpec((tm, LANES), tab),
            pl.BlockSpec((tm, LANES), tab),
            _full((SWA_WIDTH, SWA_WIDTH)),
            _full((MEM_WIDTH, MEM_WIDTH)),
        ],
        out_specs=(
            pl.BlockSpec((tm, SSM_WIDTH), row),
            pl.BlockSpec((tm, SWA_WIDTH), row),
            pl.BlockSpec((tm, SWA_KV_WIDTH), row),
            pl.BlockSpec((tm, SWA_KV_WIDTH), row),
            pl.BlockSpec((tm, MEM_WIDTH), row),
        ),
        out_shape=out_shape,
        compiler_params=_cparams(("parallel",)),
        name="inproj",
    )(x2d, p["g_mix"], p["w_a"], p["g_q"], p["g_k"], p["g_qm"], cos, sin, p["ones64"], p["ones128"])


def _ssm_kernel(u_ref, s0r_ref, s0i_ref, bre_ref, bim_ref, cre_ref, cim_ref, d_ref, wglu_ref,
                lvr_ref, lvi_ref, cpr_ref, cpi_ref,
                y_ref, fr_ref, fi_ref, sr_ref, si_ref, car_ref, cai_ref, *, chained):
    tc = u_ref.shape[0]
    half_u = SSM_WIDTH // 2
    half_c = SSM_CH // 2
    u = u_ref[...]
    ub = u.astype(BF16)
    for hh in range(2):
        us = ub[:, hh * half_u:(hh + 1) * half_u]
        cs = slice(hh * half_c, (hh + 1) * half_c)
        sr_ref[:, cs] = jnp.dot(us, bre_ref[hh], preferred_element_type=F32)
        si_ref[:, cs] = jnp.dot(us, bim_ref[hh], preferred_element_type=F32)

    if chained:
        @pl.when(pl.program_id(1) == 0)
        def _():
            car_ref[...] = s0r_ref[0]
            cai_ref[...] = s0i_ref[0]

    n_tiles = tc // SUBLANES
    for sl in range(SSM_CH // SCAN_W):
        cols = slice(sl * SCAN_W, (sl + 1) * SCAN_W)
        lv = [(lvr_ref[j, :, cols], lvi_ref[j, :, cols]) for j in range(3)]
        cpr = cpr_ref[:, cols]
        cpi = cpi_ref[:, cols]

        def tile(i, carry, cols=cols, lv=lv, cpr=cpr, cpi=cpi):
            r0 = pl.multiple_of(i * SUBLANES, SUBLANES)
            xr = sr_ref[pl.ds(r0, SUBLANES), cols]
            xi = si_ref[pl.ds(r0, SUBLANES), cols]
            for j, d in enumerate((1, 2, 4)):
                pr, pi = lv[j]
                shr = pltpu.roll(xr, d, axis=0)
                shi = pltpu.roll(xi, d, axis=0)
                xr, xi = xr + pr * shr - pi * shi, xi + pr * shi + pi * shr
            if chained:
                c_r, c_i = carry
            else:
                c_r = s0r_ref[pl.ds(i, 1), cols]
                c_i = s0i_ref[pl.ds(i, 1), cols]
            cb_r = jnp.broadcast_to(c_r, xr.shape)
            cb_i = jnp.broadcast_to(c_i, xr.shape)
            xr, xi = xr + cpr * cb_r - cpi * cb_i, xi + cpr * cb_i + cpi * cb_r
            sr_ref[pl.ds(r0, SUBLANES), cols] = xr
            si_ref[pl.ds(r0, SUBLANES), cols] = xi
            last_r = xr[SUBLANES - 1:SUBLANES, :]
            last_i = xi[SUBLANES - 1:SUBLANES, :]
            if chained:
                return last_r, last_i
            fr_ref[pl.ds(i, 1), cols] = last_r
            fi_ref[pl.ds(i, 1), cols] = last_i
            return carry

        if chained:
            init = (car_ref[:, cols], cai_ref[:, cols])
        else:
            init = (jnp.zeros((1, SCAN_W), F32), jnp.zeros((1, SCAN_W), F32))
        c_r, c_i = lax.fori_loop(0, n_tiles, tile, init)
        if chained:
            car_ref[:, cols] = c_r
            cai_ref[:, cols] = c_i

    if chained:
        fr_ref[0] = car_ref[...]
        fi_ref[0] = cai_ref[...]

    ys = []
    for hh in range(2):
        cs = slice(hh * half_c, (hh + 1) * half_c)
        ys.append(jnp.dot(sr_ref[:, cs].astype(BF16), cre_ref[hh], preferred_element_type=F32)
                  - jnp.dot(si_ref[:, cs].astype(BF16), cim_ref[hh], preferred_element_type=F32))
    y = jnp.concatenate(ys, axis=1) + d_ref[...] * u
    y = jax.nn.gelu(y)
    gate = jax.nn.sigmoid(jnp.dot(y.astype(BF16), wglu_ref[...], preferred_element_type=F32))
    y_ref[...] = (y * gate).astype(BF16)


def _ssm(u, s0r, s0i, p, *, n_seq, chained):
    t = u.shape[0]
    tc = TC_SSM
    if chained:
        per = t // n_seq // tc
        grid = (n_seq, per)
        row = lambda n, c: (n * per + c, 0)
        st = lambda n, c: (n, 0, 0)
        s0_spec = pl.BlockSpec((1, 1, SSM_CH), st)
        f_spec = pl.BlockSpec((1, 1, SSM_CH), st)
        f_shape = jax.ShapeDtypeStruct((n_seq, 1, SSM_CH), F32)
        sem = ("parallel", "arbitrary")
    else:
        grid = (t // tc,)
        row = lambda c: (c, 0)
        s0_spec = pl.BlockSpec((tc // SUBLANES, SSM_CH), row)
        f_spec = pl.BlockSpec((tc // SUBLANES, SSM_CH), row)
        f_shape = jax.ShapeDtypeStruct((t // SUBLANES, SSM_CH), F32)
        sem = ("parallel",)
    hu = SSM_WIDTH // 2
    hc = SSM_CH // 2
    return pl.pallas_call(
        functools.partial(_ssm_kernel, chained=chained),
        grid=grid,
        in_specs=[
            pl.BlockSpec((tc, SSM_WIDTH), row),
            s0_spec, s0_spec,
            _full((2, hu, hc)), _full((2, hu, hc)),
            _full((2, hc, hu)), _full((2, hc, hu)),
            _full((1, SSM_WIDTH)),
            _full((SSM_WIDTH, SSM_WIDTH)),
            _full((3, SUBLANES, SSM_CH)), _full((3, SUBLANES, SSM_CH)),
            _full((SUBLANES, SSM_CH)), _full((SUBLANES, SSM_CH)),
        ],
        out_specs=(pl.BlockSpec((tc, SSM_WIDTH), row), f_spec, f_spec),
        out_shape=(jax.ShapeDtypeStruct((t, SSM_WIDTH), BF16), f_shape, f_shape),
        scratch_shapes=[
            pltpu.VMEM((tc, SSM_CH), F32), pltpu.VMEM((tc, SSM_CH), F32),
            pltpu.VMEM((1, SSM_CH), F32), pltpu.VMEM((1, SSM_CH), F32),
        ],
        compiler_params=_cparams(sem),
        name="ssm_chained" if chained else "ssm_tiles",
    )(u, s0r, s0i, p["b_re"], p["b_im"], p["c_re"], p["c_im"], p["ssm_d"], p["w_glu"],
      p["lv_re"], p["lv_im"], p["cp_re"], p["cp_im"])


def _ssm_params(a_re, a_im, log_dt, b_re, b_im, c_re, c_im):
    dt = jnp.exp(log_dt)[:, None]
    mag = jnp.exp(a_re * dt)
    abr = mag * jnp.cos(a_im * dt)
    abi = mag * jnp.sin(a_im * dt)
    den = a_re * a_re + a_im * a_im
    nr = abr - 1.0
    ni = abi
    coef_re = (nr * a_re + ni * a_im) / den
    coef_im = (ni * a_re - nr * a_im) / den
    bbr = coef_re[..., None] * b_re - coef_im[..., None] * b_im
    bbi = coef_re[..., None] * b_im + coef_im[..., None] * b_re

    half_g = SSM_GROUPS // 2
    eye = jnp.eye(half_g, dtype=F32)

    def b_blocks(bb):
        bb = bb.reshape(2, half_g, SSM_STATE, SSM_GROUP)
        m = jnp.einsum("zgph,gk->zghkp", bb, eye)
        return m.reshape(2, half_g * SSM_GROUP, half_g * SSM_STATE).astype(BF16)

    def c_blocks(cc):
        cc = cc.reshape(2, half_g, SSM_GROUP, SSM_STATE)
        m = jnp.einsum("zghp,gk->zgpkh", cc, eye)
        return m.reshape(2, half_g * SSM_STATE, half_g * SSM_GROUP).astype(BF16)

    ar = abr.reshape(1, SSM_CH)
    ai = abi.reshape(1, SSM_CH)
    pows = [(ar, ai)]
    for _ in range(SUBLANES - 1):
        pr, pi = pows[-1]
        pows.append((pr * ar - pi * ai, pr * ai + pi * ar))
    rows = jnp.arange(SUBLANES)[:, None]
    lv_re = jnp.stack([jnp.where(rows >= d, pows[d - 1][0], 0.0) for d in (1, 2, 4)])
    lv_im = jnp.stack([jnp.where(rows >= d, pows[d - 1][1], 0.0) for d in (1, 2, 4)])
    cp_re = jnp.concatenate([pw[0] for pw in pows], axis=0)
    cp_im = jnp.concatenate([pw[1] for pw in pows], axis=0)
    return dict(b_re=b_blocks(bbr), b_im=b_blocks(bbi), c_re=c_blocks(c_re), c_im=c_blocks(c_im),
                lv_re=lv_re, lv_im=lv_im, cp_re=cp_re, cp_im=cp_im)


def _memkv_kernel(m_ref, g_ref, w_ref, gk_ref, o128_ref, k_ref, v_ref):
    hm = _rms_rows(m_ref[0], g_ref[...]).astype(BF16)
    kv = jnp.dot(hm, w_ref[...], preferred_element_type=F32)
    k = kv[:, :MEM_WIDTH]
    k_ref[0] = k * lax.rsqrt(_seg_mean_sq(k, o128_ref[...], MEM_HD) + EPS) * gk_ref[...]
    v_ref[0] = kv[:, MEM_WIDTH:]


def _memkv(mem, p):
    n = mem.shape[0]
    blk = lambda i: (i, 0, 0)
    shp = jax.ShapeDtypeStruct((n, N_MEM, MEM_WIDTH), F32)
    return pl.pallas_call(
        _memkv_kernel,
        grid=(n,),
        in_specs=[pl.BlockSpec((1, N_MEM, D_MODEL), blk), _full((1, D_MODEL)),
                  _full((D_MODEL, 2 * MEM_WIDTH)), _full((1, MEM_WIDTH)), _full((MEM_WIDTH, MEM_WIDTH))],
        out_specs=(pl.BlockSpec((1, N_MEM, MEM_WIDTH), blk), pl.BlockSpec((1, N_MEM, MEM_WIDTH), blk)),
        out_shape=(shp, shp),
        compiler_params=_cparams(("parallel",)),
        name="memkv",
    )(mem, p["g_mem"], p["w_mem_kv"], p["g_km"], p["ones128"])


def _dup_heads(x, lane):
    sw = pltpu.roll(x, SWA_HD, axis=x.ndim - 1)
    lo = lane < SWA_HD
    return jnp.where(lo, x, sw), jnp.where(lo, sw, x)


def _swa_group(q_blk, kk, vv, g, mask, sink_ref):
    tq = q_blk.shape[-2]
    shp = q_blk.shape[:-1]
    lane = lax.broadcasted_iota(jnp.int32, shp + (LANES,), len(shp))
    rows = []
    sinks = []
    for hl in range(SWA_Q_PER_KV):
        h = g * SWA_Q_PER_KV + hl
        pair = q_blk[..., (h // 2) * LANES:(h // 2 + 1) * LANES]
        keep = (lane < SWA_HD) if h % 2 == 0 else (lane >= SWA_HD)
        rows.append(jnp.where(keep, pair, 0.0))
        sinks.append(jnp.full(shp + (1,), sink_ref[h], F32))
    qq = jnp.concatenate(rows, axis=-2).astype(BF16)
    sk = jnp.concatenate(sinks, axis=-2)
    s = jnp.einsum("...qd,...kd->...qk", qq, kk, preferred_element_type=F32)
    s = jnp.where(mask, s, NEG_INF)
    m = jnp.maximum(jnp.max(s, axis=-1, keepdims=True), sk)
    e = jnp.exp(s - m)
    pr = e / (jnp.sum(e, axis=-1, keepdims=True) + jnp.exp(sk - m))
    o = jnp.einsum("...qk,...kd->...qd", pr.astype(BF16), vv, preferred_element_type=F32)
    lo = lane < SWA_HD
    return [jnp.where(lo, o[..., (2 * j) * tq:(2 * j + 1) * tq, :], o[..., (2 * j + 1) * tq:(2 * j + 2) * tq, :])
            for j in range(2)]


def _mem_heads(qm, k_head, v_head):
    outs = []
    for h in range(MEM_HEADS):
        cs = slice(h * MEM_HD, (h + 1) * MEM_HD)
        s = jnp.einsum("...qd,...kd->...qk", qm[..., cs], k_head(h).astype(BF16),
                       preferred_element_type=F32) * MEM_SCALE
        m = jnp.max(s, axis=-1, keepdims=True)
        e = jnp.exp(s - m)
        pr = e / jnp.sum(e, axis=-1, keepdims=True)
        outs.append(jnp.einsum("...qk,...kd->...qd", pr.astype(BF16), v_head(h).astype(BF16),
                               preferred_element_type=F32))
    return jnp.concatenate(outs, axis=-1)


def _attn_prompt_kernel(sink_ref, q_ref, k_ref, v_ref, kp_ref, vp_ref, qm_ref, mk_ref, mv_ref, ys_ref, ym_ref):
    tq = q_ref.shape[0]
    blk = WINDOW
    rows = SWA_Q_PER_KV * blk
    i = lax.broadcasted_iota(jnp.int32, (rows, 2 * blk), 0) % blk
    j = lax.broadcasted_iota(jnp.int32, (rows, 2 * blk), 1)
    lo = jnp.where(j < blk, i + 1, blk)
    hi = jnp.where(j < blk, blk, blk + i + 1)
    first_lo = jnp.where(pl.program_id(1) == 0, blk, 0)
    lane_k = lax.broadcasted_iota(jnp.int32, (2 * blk, LANES), 1)
    for b in range(tq // blk):
        rs = slice(b * blk, (b + 1) * blk)
        if b == 0:
            k2 = jnp.concatenate([kp_ref[...], k_ref[rs, :]], axis=0)
            v2 = jnp.concatenate([vp_ref[...], v_ref[rs, :]], axis=0)
            mask = (j >= jnp.maximum(lo, first_lo)) & (j < hi)
        else:
            k2 = k_ref[(b - 1) * blk:(b + 1) * blk, :]
            v2 = v_ref[(b - 1) * blk:(b + 1) * blk, :]
            mask = (j >= lo) & (j < hi)
        kks = _dup_heads(k2, lane_k)
        vvs = _dup_heads(v2, lane_k)
        q_blk = q_ref[rs, :].astype(F32)
        pairs = []
        for g in range(SWA_KV_HEADS):
            pairs += _swa_group(q_blk, kks[g].astype(BF16), vvs[g].astype(BF16), g, mask, sink_ref)
        ys_ref[rs, :] = jnp.concatenate(pairs, axis=1).astype(BF16)
    ym_ref[...] = _mem_heads(qm_ref[...], lambda h: mk_ref[0, :, h * MEM_HD:(h + 1) * MEM_HD],
                             lambda h: mv_ref[0, :, h * MEM_HD:(h + 1) * MEM_HD]).astype(BF16)


def _attn_prompt(q, k, v, qm, mk, mv, sinks, n_seq):
    t = q.shape[0]
    tq = TQ_ATT
    per = t // n_seq // tq
    sub = tq // WINDOW
    row = lambda n, c: (n * per + c, 0)
    prev = lambda n, c: (jnp.maximum((n * per + c) * sub - 1, 0), 0)
    memb = lambda n, c: (n, 0, 0)
    return pl.pallas_call(
        _attn_prompt_kernel,
        grid=(n_seq, per),
        in_specs=[
            pl.BlockSpec(memory_space=pltpu.SMEM),
            pl.BlockSpec((tq, SWA_WIDTH), row),
            pl.BlockSpec((tq, SWA_KV_WIDTH), row),
            pl.BlockSpec((tq, SWA_KV_WIDTH), row),
            pl.BlockSpec((WINDOW, SWA_KV_WIDTH), prev),
            pl.BlockSpec((WINDOW, SWA_KV_WIDTH), prev),
            pl.BlockSpec((tq, MEM_WIDTH), row),
            pl.BlockSpec((1, N_MEM, MEM_WIDTH), memb),
            pl.BlockSpec((1, N_MEM, MEM_WIDTH), memb),
        ],
        out_specs=(pl.BlockSpec((tq, SWA_WIDTH), row), pl.BlockSpec((tq, MEM_WIDTH), row)),
        out_shape=(jax.ShapeDtypeStruct((t, SWA_WIDTH), BF16), jax.ShapeDtypeStruct((t, MEM_WIDTH), BF16)),
        compiler_params=_cparams(("parallel", "parallel")),
        name="attn_prompt",
    )(sinks, q, k, v, k, v, qm, mk, mv)


def _attn_sample_kernel(sink_ref, q_ref, k_ref, v_ref, pk_ref, pv_ref, qm_ref, mk_ref, mv_ref,
                        ys_ref, ym_ref, nk_ref, nv_ref, *, s_len):
    sb, wb = pk_ref.shape[0], pk_ref.shape[1]
    n_keys = wb + s_len
    rows = SWA_Q_PER_KV * s_len
    i = lax.broadcasted_iota(jnp.int32, (sb, rows, n_keys), 1) % s_len
    j = lax.broadcasted_iota(jnp.int32, (sb, rows, n_keys), 2)
    rel = i + wb - j
    mask = (rel >= 0) & (rel < WINDOW)
    k_all = jnp.concatenate([pk_ref[...], k_ref[...].reshape(sb, s_len, SWA_KV_WIDTH)], axis=1)
    v_all = jnp.concatenate([pv_ref[...], v_ref[...].reshape(sb, s_len, SWA_KV_WIDTH)], axis=1)
    nk_ref[...] = k_all[:, n_keys - wb:, :]
    nv_ref[...] = v_all[:, n_keys - wb:, :]
    lane_k = lax.broadcasted_iota(jnp.int32, k_all.shape, 2)
    kks = _dup_heads(k_all, lane_k)
    vvs = _dup_heads(v_all, lane_k)
    q3 = q_ref[...].astype(F32).reshape(sb, s_len, SWA_WIDTH)
    pairs = []
    for g in range(SWA_KV_HEADS):
        pairs += _swa_group(q3, kks[g].astype(BF16), vvs[g].astype(BF16), g, mask, sink_ref)
    ys_ref[...] = jnp.concatenate(pairs, axis=-1).reshape(sb * s_len, SWA_WIDTH).astype(BF16)
    qm3 = qm_ref[...].astype(F32).reshape(sb, s_len, MEM_WIDTH).astype(BF16)
    head_rows = lambda h: pl.ds(h, N_MEM, stride=MEM_HEADS)
    ym = _mem_heads(qm3, lambda h: mk_ref[:, head_rows(h), :], lambda h: mv_ref[:, head_rows(h), :])
    ym_ref[...] = ym.reshape(sb * s_len, MEM_WIDTH).astype(BF16)


def _attn_sample(q, k, v, past_k, past_v, qm, mk, mv, sinks, s_len):
    t = q.shape[0]
    n_seq, wb = past_k.shape[0], past_k.shape[1]
    sb = SEQ_BLK
    rows = sb * s_len
    row = lambda c: (c, 0)
    seq = lambda c: (c, 0, 0)
    cache_shape = jax.ShapeDtypeStruct((n_seq, wb, SWA_KV_WIDTH), F32)
    return pl.pallas_call(
        functools.partial(_attn_sample_kernel, s_len=s_len),
        grid=(n_seq // sb,),
        in_specs=[
            pl.BlockSpec(memory_space=pltpu.SMEM),
            pl.BlockSpec((rows, SWA_WIDTH), row),
            pl.BlockSpec((rows, SWA_KV_WIDTH), row),
            pl.BlockSpec((rows, SWA_KV_WIDTH), row),
            pl.BlockSpec((sb, wb, SWA_KV_WIDTH), seq),
            pl.BlockSpec((sb, wb, SWA_KV_WIDTH), seq),
            pl.BlockSpec((rows, MEM_WIDTH), row),
            pl.BlockSpec((sb, N_MEM * MEM_HEADS, MEM_HD), seq),
            pl.BlockSpec((sb, N_MEM * MEM_HEADS, MEM_HD), seq),
        ],
        out_specs=(pl.BlockSpec((rows, SWA_WIDTH), row), pl.BlockSpec((rows, MEM_WIDTH), row),
                   pl.BlockSpec((sb, wb, SWA_KV_WIDTH), seq), pl.BlockSpec((sb, wb, SWA_KV_WIDTH), seq)),
        out_shape=(jax.ShapeDtypeStruct((t, SWA_WIDTH), BF16), jax.ShapeDtypeStruct((t, MEM_WIDTH), BF16),
                   cache_shape, cache_shape),
        compiler_params=_cparams(("parallel",)),
        name="attn_sample",
    )(sinks, q, k, v, past_k, past_v, qm, mk, mv)


def _first_argmax(x, valid, lane):
    xm = jnp.where(valid, x, -jnp.inf)
    mx = jnp.max(xm, axis=-1, keepdims=True)
    idx = jnp.min(jnp.where(xm == mx, lane, LANES), axis=-1, keepdims=True)
    return mx, lane == idx, idx


def _merge_kernel(xp_ref, yap_ref, ybp_ref, ycp_ref, xs_ref, yas_ref, ybs_ref, ycs_ref, *rest, n_blk_p):
    @pl.when(pl.program_id(0) < n_blk_p)
    def _():
        _merge_rows(xp_ref, yap_ref, ybp_ref, ycp_ref, *rest)

    @pl.when(pl.program_id(0) >= n_blk_p)
    def _():
        _merge_rows(xs_ref, yas_ref, ybs_ref, ycs_ref, *rest)


def _merge_rows(x_ref, ya_ref, yb_ref, yc_ref, gmix_ref, wg_ref, wa_ref, wb_ref, wc_ref, wo_ref, gffn_ref,
                wr_hi_ref, wr_lo_ref, br_ref, x1_ref, hx_ref, route_ref, cnt_ref, carry_ref):
    x = x_ref[...]
    h = _rms_rows(x, gmix_ref[...]).astype(BF16)
    gates = jax.nn.sigmoid(jnp.dot(h, wg_ref[...], preferred_element_type=F32))
    merged = (gates[:, :D_MODEL] * jnp.dot(ya_ref[...], wa_ref[...], preferred_element_type=F32)
              + gates[:, D_MODEL:2 * D_MODEL] * jnp.dot(yb_ref[...], wb_ref[...], preferred_element_type=F32)
              + gates[:, 2 * D_MODEL:] * jnp.dot(yc_ref[...], wc_ref[...], preferred_element_type=F32))
    x1 = x + jnp.dot(merged.astype(BF16), wo_ref[...], preferred_element_type=F32)
    x1_ref[...] = x1
    hn = _rms_rows(x1, gffn_ref[...])
    tm = x.shape[0]
    slab_row = lambda c: pl.ds(c, tm, stride=HX_ROWS)
    for c in range(ROW_TILES):
        hx_ref[slab_row(c), :] = hn[:, c * LANES:(c + 1) * LANES]

    hi, lo = _split_bf16(hn)
    w_hi = wr_hi_ref[...]
    logits = (jnp.dot(hi, w_hi, preferred_element_type=F32) + jnp.dot(lo, w_hi, preferred_element_type=F32)
              + jnp.dot(hi, wr_lo_ref[...], preferred_element_type=F32)) + br_ref[...]
    lane = lax.broadcasted_iota(jnp.int32, logits.shape, 1)
    is_grp = lane < N_GROUPS_E
    g_max, _, g_idx = _first_argmax(logits, is_grp, lane)
    pg_top = 1.0 / jnp.sum(jnp.where(is_grp, jnp.exp(logits - g_max), 0.0), axis=-1, keepdims=True)
    e_lo = N_GROUPS_E + g_idx * EXPERTS_PER_GROUP
    in_grp = (lane >= e_lo) & (lane < e_lo + EXPERTS_PER_GROUP)
    e_max, first, i1 = _first_argmax(logits, in_grp, lane)
    ex = jnp.where(in_grp, jnp.exp(logits - e_max), 0.0)
    pe = ex / jnp.sum(ex, axis=-1, keepdims=True)
    _, second, i2 = _first_argmax(logits, in_grp & jnp.logical_not(first), lane)
    p1 = jnp.sum(jnp.where(first, pe, 0.0), axis=-1, keepdims=True)
    p2 = jnp.sum(jnp.where(second, pe, 0.0), axis=-1, keepdims=True)
    w1 = pg_top * p1 / (p1 + p2)
    w2 = pg_top * p2 / (p1 + p2)

    a1 = i1 - e_lo
    a2 = i2 - e_lo
    e_a = jnp.minimum(a1, a2)
    e_b = jnp.maximum(a1, a2)
    pair = jnp.right_shift(e_a * (2 * EXPERTS_PER_GROUP - 1 - e_a), 1) + (e_b - e_a - 1)
    bucket = g_idx * PAIRS_PER_GROUP + pair
    w_a = jnp.where(a1 < a2, w1, w2)
    w_b = jnp.where(a1 < a2, w2, w1)

    @pl.when(pl.program_id(0) == 0)
    def _():
        carry_ref[...] = jnp.zeros_like(carry_ref)

    onehot = lane == bucket
    tri = (lax.broadcasted_iota(jnp.int32, (tm, tm), 1) <= lax.broadcasted_iota(jnp.int32, (tm, tm), 0))
    csum = jnp.dot(jnp.where(tri, 1.0, 0.0).astype(BF16), jnp.where(onehot, 1.0, 0.0).astype(BF16),
                   preferred_element_type=F32)
    carry = carry_ref[...]
    rank = jnp.sum(jnp.where(onehot, csum + carry, 0.0), axis=-1, keepdims=True) - 1.0
    carry = carry + csum[tm - 1:tm, :]
    carry_ref[...] = carry
    cnt_ref[...] = carry
    route = jnp.where(lane == 0, bucket.astype(F32),
                      jnp.where(lane == 1, w_a, jnp.where(lane == 2, w_b, jnp.where(lane == 3, rank, 0.0))))
    route_ref[...] = route
    hx_ref[slab_row(ROW_TILES), :] = route
    for c in range(ROW_TILES + 1, HX_ROWS):
        hx_ref[slab_row(c), :] = jnp.zeros_like(route)


def _merge(prompt_rows, sample_rows, p):
    tm = TM_MRG
    t_p, t_s = prompt_rows[0].shape[0], sample_rows[0].shape[0]
    nbp = t_p // tm
    t_all = t_p + t_s
    first = lambda i: (jnp.minimum(i, nbp - 1), 0)
    second = lambda i: (jnp.maximum(i - nbp, 0), 0)
    row = lambda i: (i, 0)
    widths = (D_MODEL, SSM_WIDTH, SWA_WIDTH, MEM_WIDTH)
    in_specs = ([pl.BlockSpec((tm, w), first) for w in widths] + [pl.BlockSpec((tm, w), second) for w in widths] + [
        _full((1, D_MODEL)),
        _full((D_MODEL, N_BRANCH * D_MODEL)),
        _full((SSM_WIDTH, D_MODEL)), _full((SWA_WIDTH, D_MODEL)), _full((MEM_WIDTH, D_MODEL)),
        _full((D_MODEL, D_MODEL)),
        _full((1, D_MODEL)),
        _full((D_MODEL, LANES)), _full((D_MODEL, LANES)), _full((1, LANES)),
    ])
    return pl.pallas_call(
        functools.partial(_merge_kernel, n_blk_p=nbp),
        grid=(t_all // tm,),
        in_specs=in_specs,
        out_specs=(pl.BlockSpec((tm, D_MODEL), row), pl.BlockSpec((tm * HX_ROWS, LANES), row),
                   pl.BlockSpec((tm, ROUTE_W), row), _full((1, LANES))),
        out_shape=(jax.ShapeDtypeStruct((t_all, D_MODEL), F32),
                   jax.ShapeDtypeStruct((t_all * HX_ROWS, LANES), F32),
                   jax.ShapeDtypeStruct((t_all, ROUTE_W), F32), jax.ShapeDtypeStruct((1, LANES), F32)),
        scratch_shapes=[pltpu.VMEM((1, LANES), F32)],
        compiler_params=_cparams(("arbitrary",)),
        name="merge",
    )(*prompt_rows, *sample_rows, p["g_mix"], p["w_gates"], p["w_br_ssm"], p["w_br_swa"], p["w_br_mem"], p["w_o"],
      p["g_ffn"], p["w_r_hi"], p["w_r_lo"], p["b_r"])


def _rows_to_lanes(col):
    out = []
    for k in range(col.shape[0] // LANES):
        blk = jnp.broadcast_to(col[k * LANES:(k + 1) * LANES], (LANES, LANES))
        out.append(blk.T[0:1, :])
    return jnp.concatenate(out, axis=0)


def _pos_kernel(route_ref, off_ref, coff_ref, pos_ref, cpos_ref):
    r = route_ref[...]
    lane = lax.broadcasted_iota(jnp.int32, r.shape, 1)
    mine = lane == r[:, 0:1].astype(jnp.int32)
    off = jnp.sum(jnp.where(mine, off_ref[...], 0.0), axis=-1, keepdims=True)
    coff = jnp.sum(jnp.where(mine, coff_ref[...], 0.0), axis=-1, keepdims=True)
    rank = r[:, 3:4]
    pos_ref[...] = _rows_to_lanes(off + rank).astype(jnp.int32)
    cpos_ref[...] = _rows_to_lanes(coff + rank).astype(jnp.int32)


def _sorted_pos(route, off, coff):
    t = route.shape[0]
    tm = SUBLANES * LANES
    shp = jax.ShapeDtypeStruct((t // LANES, LANES), jnp.int32)
    pos, cpos = pl.pallas_call(
        _pos_kernel,
        grid=(t // tm,),
        in_specs=[pl.BlockSpec((tm, ROUTE_W), lambda i: (i, 0)), _full((1, LANES)), _full((1, LANES))],
        out_specs=(pl.BlockSpec((SUBLANES, LANES), lambda i: (i, 0)), pl.BlockSpec((SUBLANES, LANES), lambda i: (i, 0))),
        out_shape=(shp, shp),
        compiler_params=_cparams(("parallel",)),
        name="sorted_pos",
    )(route, off, coff)
    return pos.reshape(t), cpos.reshape(t)


def _inv_kernel(pos_ref, idx_ref):
    def body(t, _):
        idx_ref[pos_ref[t]] = t
        return 0

    lax.fori_loop(0, pos_ref.shape[0], body, 0, unroll=8)


def _invert(pos):
    return pl.pallas_call(
        _inv_kernel,
        in_specs=[pl.BlockSpec(memory_space=pltpu.SMEM)],
        out_specs=pl.BlockSpec(memory_space=pltpu.SMEM),
        out_shape=jax.ShapeDtypeStruct(pos.shape, jnp.int32),
        name="invert_perm",
    )(pos)


def _bucket_kernel(idx_ref, tg_ref, ta_ref, tb_ref, cb_ref, nr_ref, hx_hbm, wi_ref, wd_ref,
                   ys_ref, buf, sem):
    j = pl.program_id(0)
    n_real = nr_ref[0]

    last = idx_ref.shape[0] - 1

    def issue_row(tile_base, slot, r, dst_row, prio):
        src = idx_ref[jnp.minimum(tile_base + r, last)]
        pltpu.make_async_copy(hx_hbm.at[pl.ds(pl.multiple_of(src * HX_ROWS, HX_ROWS), HX_ROWS), :],
                              buf.at[slot, pl.ds(dst_row, HX_ROWS), :], sem.at[slot]).start(priority=prio)

    def wait_tile(slot):
        pltpu.make_async_copy(hx_hbm.at[pl.ds(0, TM_EXP * HX_ROWS), :], buf.at[slot], sem.at[slot]).wait()

    @pl.when(j == 0)
    def _():
        base = cb_ref[0]

        def body(r8, _):
            for k in range(SUBLANES):
                r = r8 * SUBLANES + k
                issue_row(base, 0, r, pl.multiple_of(r * HX_ROWS, HX_ROWS), k % 2)
            return 0

        lax.fori_loop(0, TM_EXP // SUBLANES, body, 0)

    @pl.when(j < n_real)
    def _():
        slot = j % 2
        wait_tile(slot)
        nxt = jnp.minimum(j + 1, n_real - 1)
        base = cb_ref[nxt]
        for r in range(TM_EXP):
            issue_row(base, 1 - slot, r, r * HX_ROWS, r % 2)

        slab_row = lambda c: buf[slot, pl.ds(c, TM_EXP, stride=HX_ROWS), :]
        x = jnp.concatenate([slab_row(c) for c in range(ROW_TILES)], axis=1).astype(BF16)
        route = slab_row(ROW_TILES)

        def ffn(e):
            gu = jnp.dot(x, wi_ref[0, e], preferred_element_type=F32)
            a = jax.nn.silu(gu[:, :D_FF]) * gu[:, D_FF:]
            return jnp.dot(a.astype(BF16), wd_ref[0, e], preferred_element_type=F32)

        y = route[:, 1:2] * ffn(ta_ref[j]) + route[:, 2:3] * ffn(tb_ref[j])
        for c in range(ROW_TILES):
            ys_ref[pl.ds(c, TM_EXP, stride=ROW_TILES), :] = y[:, c * LANES:(c + 1) * LANES]

        @pl.when(j == n_real - 1)
        def _():
            wait_tile(1 - slot)


def _bucket_ffn(hx, idx, tile_g, tile_a, tile_b, tile_cb, n_real, p):
    n_tiles = tile_a.shape[0]
    grp = lambda j, idx, tg, ta, tb, cb, nr: (tg[j], 0, 0, 0)
    out = lambda j, idx, tg, ta, tb, cb, nr: (jnp.minimum(j, nr[0] - 1), 0)
    epg = EXPERTS_PER_GROUP
    return pl.pallas_call(
        _bucket_kernel,
        grid_spec=pltpu.PrefetchScalarGridSpec(
            num_scalar_prefetch=6,
            grid=(n_tiles,),
            in_specs=[
                pl.BlockSpec(memory_space=pl.ANY),
                pl.BlockSpec((1, epg, D_MODEL, 2 * D_FF), grp), pl.BlockSpec((1, epg, D_FF, D_MODEL), grp),
            ],
            out_specs=pl.BlockSpec((TM_EXP * ROW_TILES, LANES), out),
            scratch_shapes=[pltpu.VMEM((2, TM_EXP * HX_ROWS, LANES), F32), pltpu.SemaphoreType.DMA((2,))],
        ),
        out_shape=jax.ShapeDtypeStruct((n_tiles * TM_EXP * ROW_TILES, LANES), F32),
        compiler_params=_cparams(("arbitrary",)),
        name="bucket_ffn",
    )(idx, tile_g, tile_a, tile_b, tile_cb, n_real, hx,
      p["w_exp_in"].reshape(N_GROUPS_E, epg, D_MODEL, 2 * D_FF),
      p["w_exp_down"].reshape(N_GROUPS_E, epg, D_FF, D_MODEL))


def _back_kernel(pos_ref, x1_ref, ys_hbm, o_ref, buf, sem, *, t0):
    i = pl.program_id(0)
    tm = o_ref.shape[0]

    def issue(tile, slot):
        base = t0 + tile * tm

        def body(r8, _):
            for k in range(SUBLANES):
                r = r8 * SUBLANES + k
                src = pl.multiple_of(pos_ref[base + r] * ROW_TILES, ROW_TILES)
                pltpu.make_async_copy(ys_hbm.at[pl.ds(src, ROW_TILES), :],
                                      buf.at[slot, pl.ds(pl.multiple_of(r * ROW_TILES, ROW_TILES), ROW_TILES), :],
                                      sem.at[slot]).start(priority=k % 2)
            return 0

        lax.fori_loop(0, tm // SUBLANES, body, 0)

    @pl.when(i == 0)
    def _():
        issue(0, 0)

    @pl.when(i + 1 < pl.num_programs(0))
    def _():
        issue(i + 1, (i + 1) % 2)

    slot = i % 2
    pltpu.make_async_copy(ys_hbm.at[pl.ds(0, tm * ROW_TILES), :], buf.at[slot], sem.at[slot]).wait()
    y = jnp.concatenate([buf[slot, pl.ds(c, tm, stride=ROW_TILES), :] for c in range(ROW_TILES)], axis=1)
    o_ref[...] = x1_ref[...] + y


def _unsort_add(x1, ys, pos, t0, t):
    tm = TM_MRG
    row = lambda i, pos: (i, 0)
    return pl.pallas_call(
        functools.partial(_back_kernel, t0=t0),
        grid_spec=pltpu.PrefetchScalarGridSpec(
            num_scalar_prefetch=1,
            grid=(t // tm,),
            in_specs=[pl.BlockSpec((tm, D_MODEL), lambda i, pos: (i + t0 // tm, 0)),
                      pl.BlockSpec(memory_space=pl.ANY)],
            out_specs=pl.BlockSpec((tm, D_MODEL), row),
            scratch_shapes=[pltpu.VMEM((2, tm * ROW_TILES, LANES), F32), pltpu.SemaphoreType.DMA((2,))],
        ),
        out_shape=jax.ShapeDtypeStruct((t, D_MODEL), F32),
        compiler_params=_cparams(("arbitrary",)),
        name="unsort_add",
    )(pos, x1, ys)


def _bucket_experts():
    lo, hi = [], []
    for g in range(N_GROUPS_E):
        for a in range(EXPERTS_PER_GROUP):
            for b in range(a + 1, EXPERTS_PER_GROUP):
                lo.append(g * EXPERTS_PER_GROUP + a)
                hi.append(g * EXPERTS_PER_GROUP + b)
    return jnp.asarray(lo, jnp.int32), jnp.asarray(hi, jnp.int32)


def _tile_tables(counts, n_tiles):
    cnt = counts[0, :N_BUCKETS].astype(jnp.int32)
    nt = (cnt + TM_EXP - 1) // TM_EXP
    tend = jnp.cumsum(nt)
    tstart = tend - nt
    cstart = jnp.cumsum(cnt) - cnt
    pad = lambda v: jnp.zeros((1, LANES), F32).at[0, :N_BUCKETS].set(v.astype(F32))
    j = jnp.arange(n_tiles, dtype=jnp.int32)
    b = jnp.minimum(jnp.sum((tend[None, :] <= j[:, None]).astype(jnp.int32), axis=1), N_BUCKETS - 1)
    in_bucket = (j - tstart[b]) * TM_EXP
    e_lo, e_hi = _bucket_experts()
    epg = EXPERTS_PER_GROUP
    return (pad(tstart * TM_EXP), pad(cstart), b // PAIRS_PER_GROUP, e_lo[b] % epg, e_hi[b] % epg,
            cstart[b] + in_bucket, tend[-1:])


def _rope_tables(pos):
    half = SWA_HD // 2
    inv = ROPE_THETA ** (-jnp.arange(half, dtype=F32) / half)
    ang = pos.astype(F32)[:, None] * inv[None, :]
    cos = jnp.cos(ang)
    sin = jnp.sin(ang)
    cos = jnp.concatenate([cos, cos, cos, cos], axis=1)
    sin = jnp.concatenate([-sin, sin, -sin, sin], axis=1)
    return cos, sin


def kernel(x_prompt, x_sample, mem_prompt, state_ssm_re, state_ssm_im, cache_swa_k, cache_swa_v, cache_mem_k, cache_mem_v, norm_mix, w_in, ssm_a_re, ssm_a_im, ssm_log_dt, ssm_b_re, ssm_b_im, ssm_c_re, ssm_c_im, ssm_d, w_glu, swa_q_norm, swa_k_norm, swa_sinks, norm_mem, w_mem_kv, mem_q_norm, mem_k_norm, w_br_ssm, w_br_swa, w_br_mem, w_o, norm_ffn, w_router_group, b_router_group, w_router_expert, b_router_expert, w_exp_in, w_exp_down):
    depth = w_in.shape[0]
    assert depth == 1
    nb, seq, _ = x_prompt.shape
    db, dseq, _ = x_sample.shape
    assert dseq == SUBLANES
    l = 0

    w_r = jnp.concatenate([w_router_group[l], w_router_expert[l]], axis=1)
    w_r = jnp.pad(w_r, ((0, 0), (0, LANES - w_r.shape[1])))
    w_r_hi = w_r.astype(BF16)
    b_r = jnp.pad(jnp.concatenate([b_router_group[l], b_router_expert[l]]), (0, LANES - N_GROUPS_E - N_EXPERTS))
    p = dict(
        g_mix=norm_mix[l][None], w_a=w_in[l][:, :PROJ_A].astype(BF16), w_gates=w_in[l][:, PROJ_A:].astype(BF16),
        g_q=jnp.tile(swa_q_norm[l], SWA_HEADS)[None], g_k=jnp.tile(swa_k_norm[l], SWA_KV_HEADS)[None],
        g_qm=jnp.tile(mem_q_norm[l], MEM_HEADS)[None], g_km=jnp.tile(mem_k_norm[l], MEM_HEADS)[None],
        ones64=_block_ones(SWA_WIDTH, SWA_HD), ones128=_block_ones(MEM_WIDTH, MEM_HD),
        ssm_d=ssm_d[l][None], w_glu=w_glu[l].astype(BF16),
        g_mem=norm_mem[l][None], w_mem_kv=w_mem_kv[l].astype(BF16),
        w_br_ssm=w_br_ssm[l].astype(BF16), w_br_swa=w_br_swa[l].astype(BF16), w_br_mem=w_br_mem[l].astype(BF16),
        w_o=w_o[l].astype(BF16), g_ffn=norm_ffn[l][None],
        w_r_hi=w_r_hi, w_r_lo=(w_r - w_r_hi.astype(F32)).astype(BF16), b_r=b_r[None],
        w_exp_in=w_exp_in[l].astype(BF16), w_exp_down=w_exp_down[l].astype(BF16),
    )
    p.update(_ssm_params(ssm_a_re[l], ssm_a_im[l], ssm_log_dt[l], ssm_b_re[l], ssm_b_im[l],
                         ssm_c_re[l], ssm_c_im[l]))
    sinks = swa_sinks[l]

    xp = x_prompt.reshape(nb * seq, D_MODEL)
    cos_p, sin_p = _rope_tables(jnp.arange(seq, dtype=jnp.int32))
    u, q, k, v, qm = _inproj(xp, cos_p, sin_p, seq // TM_IN, p)
    zeros_state = jnp.zeros((nb, 1, SSM_CH), F32)
    y_ssm, pr, pi = _ssm(u, zeros_state, zeros_state, p, n_seq=nb, chained=True)
    mk, mv = _memkv(mem_prompt, p)
    y_swa, y_mem = _attn_prompt(q, k, v, qm, mk, mv, sinks, nb)
    win = min(WINDOW, seq)
    p_k = k.reshape(nb, seq, SWA_KV_HEADS, SWA_HD)[:, seq - win:]
    p_v = v.reshape(nb, seq, SWA_KV_HEADS, SWA_HD)[:, seq - win:]

    xs = x_sample.reshape(db * dseq, D_MODEL)
    cos_s, sin_s = _rope_tables(PAST_LEN + jnp.arange(dseq, dtype=jnp.int32))
    reps = TM_IN // dseq
    us, qs, ks, vs, qms = _inproj(xs, jnp.tile(cos_s, (reps, 1)), jnp.tile(sin_s, (reps, 1)), 1, p)
    ys_ssm, sr, si = _ssm(us, state_ssm_re[l].reshape(db, SSM_CH), state_ssm_im[l].reshape(db, SSM_CH), p,
                          n_seq=db, chained=False)
    wb = cache_swa_k.shape[2]
    ys_swa, ys_mem, s_k, s_v = _attn_sample(
        qs, ks, vs, cache_swa_k[l].reshape(db, wb, SWA_KV_WIDTH), cache_swa_v[l].reshape(db, wb, SWA_KV_WIDTH),
        qms, cache_mem_k.reshape(db, N_MEM * MEM_HEADS, MEM_HD), cache_mem_v.reshape(db, N_MEM * MEM_HEADS, MEM_HD),
        sinks, dseq)
    t_p, t_s = nb * seq, db * dseq
    t_all = t_p + t_s
    x1, hx, route, counts = _merge((xp, y_ssm, y_swa, y_mem), (xs, ys_ssm, ys_swa, ys_mem), p)

    n_tiles = t_all // TM_EXP + N_BUCKETS
    off, coff, tile_g, tile_a, tile_b, tile_cb, n_real = _tile_tables(counts, n_tiles)
    pos, cpos = _sorted_pos(route, off, coff)
    idx = _invert(cpos)
    y_sorted = _bucket_ffn(hx, idx, tile_g, tile_a, tile_b, tile_cb, n_real, p)
    yp = _unsort_add(x1, y_sorted, pos, 0, t_p).reshape(nb, seq, D_MODEL)
    ys = _unsort_add(x1, y_sorted, pos, t_p, t_s).reshape(db, dseq, D_MODEL)

    g, s = SSM_GROUPS, SSM_STATE
    return (yp, ys,
            pr.reshape(1, nb, g, s), pi.reshape(1, nb, g, s),
            p_k[None], p_v[None],
            mk.reshape(1, nb, N_MEM, MEM_HEADS, MEM_HD), mv.reshape(1, nb, N_MEM, MEM_HEADS, MEM_HD),
            sr.reshape(1, db, g, s), si.reshape(1, db, g, s),
            s_k.reshape(1, db, wb, SWA_KV_HEADS, SWA_HD), s_v.reshape(1, db, wb, SWA_KV_HEADS, SWA_HD))
```

```python
import functools
import math

import jax
import jax.numpy as jnp
from jax import lax
from jax.experimental import pallas as pl
from jax.experimental.pallas import tpu as pltpu

F32 = jnp.float32
BF16 = jnp.bfloat16

D_MODEL = 1024
SSM_WIDTH = 512
SSM_GROUP = 16
SSM_GROUPS = 32
SSM_STATE = 64
SSM_CH = SSM_GROUPS * SSM_STATE
LB = 4
OCT = 8
N_OCT = SSM_GROUPS // OCT
OCT_CH = OCT * SSM_STATE
SWA_HEADS = 8
SWA_KV_HEADS = 2
SWA_Q_PER_KV = SWA_HEADS // SWA_KV_HEADS
SWA_HD = 64
SWA_WIDTH = SWA_HEADS * SWA_HD
SWA_KV_WIDTH = SWA_KV_HEADS * SWA_HD
WINDOW = 128
PAST_LEN = 16384
ROPE_THETA = 10000.0
N_MEM = 256
MEM_HEADS = 4
MEM_HD = 128
MEM_WIDTH = MEM_HEADS * MEM_HD
N_BRANCH = 3
PROJ_A = SSM_WIDTH + SWA_WIDTH + 2 * SWA_KV_WIDTH + MEM_WIDTH
N_GROUPS_E = 4
EXPERTS_PER_GROUP = 8
N_EXPERTS = 32
D_FF = 256
EPS = 1e-6
NEG_INF = -1e30
SWA_SCALE = SWA_HD ** -0.5
MEM_SCALE = MEM_HD ** -0.5

LANES = 128
SUBLANES = 8
VMEM_LIMIT = 56 * 1024 * 1024

TM_IN = 512
TB_SSM = 512
SCAN_W = 512
TQ_ATT = 512
SEQ_BLK = 8
TM_MRG = 512
TM_BACK = 256
TM_EXP = 256

ROUTE_W = LANES
ROW_TILES = D_MODEL // LANES
HX_ROWS = 2 * ROW_TILES
PAIRS_PER_GROUP = EXPERTS_PER_GROUP * (EXPERTS_PER_GROUP - 1) // 2
N_BUCKETS = N_GROUPS_E * PAIRS_PER_GROUP


def _cparams(sem):
    return pltpu.CompilerParams(dimension_semantics=sem, vmem_limit_bytes=VMEM_LIMIT)


def _full(shape):
    nd = len(shape)
    return pl.BlockSpec(shape, lambda *_: (0,) * nd)


def _resident(shape):
    nd = len(shape)
    return pl.BlockSpec(shape, lambda *_: (0,) * nd, pipeline_mode=pl.Buffered(1))


def _split_bf16(x):
    hi = x.astype(BF16)
    lo = (x - hi.astype(F32)).astype(BF16)
    return hi, lo


def _seg_mean_sq(x, ones_blk, width):
    hi, lo = _split_bf16(x * x)
    s = jnp.dot(hi, ones_blk, preferred_element_type=F32) + jnp.dot(lo, ones_blk, preferred_element_type=F32)
    return s * (1.0 / width)


def _rms_rows(x, gain):
    return x * lax.rsqrt(jnp.mean(x * x, axis=-1, keepdims=True) + EPS) * gain


def _block_ones(n, width):
    i = jnp.arange(n) // width
    return (i[:, None] == i[None, :]).astype(BF16)


def _rope_cols(x, cos, sin_signed, lane_in_head):
    n = x.shape[1]
    reps = n // LANES
    if reps > 1:
        cos = jnp.concatenate([cos] * reps, axis=1)
        sin_signed = jnp.concatenate([sin_signed] * reps, axis=1)
    half = SWA_HD // 2
    partner = jnp.where(lane_in_head < half, pltpu.roll(x, n - half, axis=1), pltpu.roll(x, half, axis=1))
    return x * cos + partner * sin_signed


def _inproj_kernel(x_ref, gmix_ref, w_ref, gq_ref, gk_ref, gm_ref, cos_ref, sin_ref, o64_ref, o128_ref,
                   u_ref, q_ref, k_ref, v_ref, qm_ref, u_scr):
    x = x_ref[...]
    h = _rms_rows(x, gmix_ref[...]).astype(BF16)
    proj = jnp.dot(h, w_ref[...], preferred_element_type=F32)
    c0 = SSM_WIDTH
    c1 = c0 + SWA_WIDTH
    c2 = c1 + SWA_KV_WIDTH
    c3 = c2 + SWA_KV_WIDTH
    n_blk = u_ref.shape[0]
    for c in range(c0 // LANES):
        u_scr[c] = proj[:, c * LANES:(c + 1) * LANES]
        for t in range(LB):
            u_ref[:, t * c0 + c * LANES:t * c0 + (c + 1) * LANES] = u_scr[c, pl.ds(t, n_blk, stride=LB), :]
    q = proj[:, c0:c1]
    k = proj[:, c1:c2]
    v_ref[...] = proj[:, c2:c3]
    qm = proj[:, c3:]
    cos = cos_ref[...]
    sin = sin_ref[...]
    o64 = o64_ref[...]
    lane_q = lax.broadcasted_iota(jnp.int32, q.shape, 1) % SWA_HD
    qn = q * lax.rsqrt(_seg_mean_sq(q, o64, SWA_HD) + EPS) * gq_ref[...]
    q_ref[...] = (_rope_cols(qn, cos, sin, lane_q) * SWA_SCALE).astype(BF16)
    lane_k = lax.broadcasted_iota(jnp.int32, k.shape, 1) % SWA_HD
    kn = k * lax.rsqrt(_seg_mean_sq(k, o64[:SWA_KV_WIDTH, :SWA_KV_WIDTH], SWA_HD) + EPS) * gk_ref[...]
    k_ref[...] = _rope_cols(kn, cos, sin, lane_k)
    qmn = qm * lax.rsqrt(_seg_mean_sq(qm, o128_ref[...], MEM_HD) + EPS) * gm_ref[...]
    qm_ref[...] = qmn.astype(BF16)


def _inproj(x2d, cos, sin, pos_blocks, p):
    t = x2d.shape[0]
    tm = TM_IN
    grid = (t // tm,)
    row = lambda i: (i, 0)
    tab = lambda i: (i % pos_blocks, 0)
    out_shape = (
        jax.ShapeDtypeStruct((t // LB, LB * SSM_WIDTH), F32),
        jax.ShapeDtypeStruct((t, SWA_WIDTH), BF16),
        jax.ShapeDtypeStruct((t, SWA_KV_WIDTH), F32),
        jax.ShapeDtypeStruct((t, SWA_KV_WIDTH), F32),
        jax.ShapeDtypeStruct((t, MEM_WIDTH), BF16),
    )
    return pl.pallas_call(
        _inproj_kernel,
        grid=grid,
        in_specs=[
            pl.BlockSpec((tm, D_MODEL), row),
            _full((1, D_MODEL)),
            _resident((D_MODEL, PROJ_A)),
            _full((1, SWA_WIDTH)),
            _full((1, SWA_KV_WIDTH)),
            _full((1, MEM_WIDTH)),
            pl.BlockSpec((tm, LANES), tab),
            pl.BlockSpec((tm, LANES), tab),
            _full((SWA_WIDTH, SWA_WIDTH)),
            _full((MEM_WIDTH, MEM_WIDTH)),
        ],
        out_specs=(
            pl.BlockSpec((tm // LB, LB * SSM_WIDTH), row),
            pl.BlockSpec((tm, SWA_WIDTH), row),
            pl.BlockSpec((tm, SWA_KV_WIDTH), row),
            pl.BlockSpec((tm, SWA_KV_WIDTH), row),
            pl.BlockSpec((tm, MEM_WIDTH), row),
        ),
        out_shape=out_shape,
        scratch_shapes=[pltpu.VMEM((SSM_WIDTH // LANES, tm, LANES), F32)],
        compiler_params=_cparams(("parallel",)),
        name="inproj",
    )(x2d, p["g_mix"], p["w_a"], p["g_q"], p["g_k"], p["g_qm"], cos, sin, p["ones64"], p["ones128"])


def _cmul(x, y):
    return x[0] * y[0] - x[1] * y[1], x[0] * y[1] + x[1] * y[0]


def _ssm_kernel(u_ref, s0r_ref, s0i_ref, wx_ref, wt_ref, wc_ref, d_ref, wglu_ref, abr_ref, abi_ref,
                lvr_ref, lvi_ref, cpr_ref, cpi_ref,
                y_ref, fr_ref, fi_ref, sr_ref, si_ref, car_ref, cai_ref, o_scr, *, chained):
    tb = u_ref.shape[0]
    u = u_ref[...]
    ub = u.astype(BF16)
    lhs = [jnp.concatenate([ub[:, t * SSM_WIDTH + c * LANES:t * SSM_WIDTH + (c + 1) * LANES] for t in range(LB)],
                           axis=1) for c in range(N_OCT)]
    for c in range(N_OCT):
        x = jnp.dot(lhs[c], wx_ref[c], preferred_element_type=F32)
        cs = slice(c * OCT_CH, (c + 1) * OCT_CH)
        sr_ref[:, cs] = x[:, :OCT_CH]
        si_ref[:, cs] = x[:, OCT_CH:]

    if chained:
        @pl.when(pl.program_id(1) == 0)
        def _():
            car_ref[...] = s0r_ref[0]
            cai_ref[...] = s0i_ref[0]

        first_row = lax.broadcasted_iota(jnp.int32, (SUBLANES, SCAN_W), 0) == 0
        for sl in range(SSM_CH // SCAN_W):
            cols = slice(sl * SCAN_W, (sl + 1) * SCAN_W)
            lv = [(lvr_ref[j, :, cols], lvi_ref[j, :, cols]) for j in range(3)]
            cpr = cpr_ref[:, cols]
            cpi = cpi_ref[:, cols]

            def tile(i, carry, cols=cols, lv=lv, cpr=cpr, cpi=cpi):
                r0 = pl.multiple_of(i * SUBLANES, SUBLANES)
                xr = sr_ref[pl.ds(r0, SUBLANES), cols]
                xi = si_ref[pl.ds(r0, SUBLANES), cols]
                for j, d in enumerate((1, 2, 4)):
                    pr, pi = lv[j]
                    shr = pltpu.roll(xr, d, axis=0)
                    shi = pltpu.roll(xi, d, axis=0)
                    xr, xi = xr + pr * shr - pi * shi, xi + pr * shi + pi * shr
                cb_r = jnp.broadcast_to(carry[0], xr.shape)
                cb_i = jnp.broadcast_to(carry[1], xr.shape)
                xr, xi = xr + cpr * cb_r - cpi * cb_i, xi + cpr * cb_i + cpi * cb_r
                sr_ref[pl.ds(r0, SUBLANES), cols] = jnp.where(first_row, cb_r, pltpu.roll(xr, 1, axis=0))
                si_ref[pl.ds(r0, SUBLANES), cols] = jnp.where(first_row, cb_i, pltpu.roll(xi, 1, axis=0))
                return xr[SUBLANES - 1:SUBLANES, :], xi[SUBLANES - 1:SUBLANES, :]

            c_r, c_i = lax.fori_loop(0, tb // SUBLANES, tile, (car_ref[:, cols], cai_ref[:, cols]))
            car_ref[:, cols] = c_r
            cai_ref[:, cols] = c_i
        fr_ref[0] = car_ref[...]
        fi_ref[0] = cai_ref[...]
    else:
        odd = lax.broadcasted_iota(jnp.int32, (tb, SSM_CH), 0) % 2 == 1
        ab = (abr_ref[...], abi_ref[...])
        s0 = (s0r_ref[...], s0i_ref[...])
        x = (sr_ref[...], si_ref[...])
        e_first = _cmul(ab, s0)
        e_first = (e_first[0] + x[0], e_first[1] + x[1])
        prev = (pltpu.roll(e_first[0], 1, axis=0), pltpu.roll(e_first[1], 1, axis=0))
        e_second = _cmul(ab, prev)
        fr_ref[...] = jnp.where(odd, e_second[0] + x[0], e_first[0])
        fi_ref[...] = jnp.where(odd, e_second[1] + x[1], e_first[1])
        sr_ref[...] = jnp.where(odd, prev[0], s0[0])
        si_ref[...] = jnp.where(odd, prev[1], s0[1])

    ys = [[None] * N_OCT for _ in range(LB)]
    for c in range(N_OCT):
        cs = slice(c * OCT_CH, (c + 1) * OCT_CH)
        s_in = jnp.concatenate([sr_ref[:, cs], si_ref[:, cs]], axis=1).astype(BF16)
        yc = (jnp.dot(lhs[c], wt_ref[c], preferred_element_type=F32)
              + jnp.dot(s_in, wc_ref[c], preferred_element_type=F32))
        for t in range(LB):
            ys[t][c] = yc[:, t * LANES:(t + 1) * LANES]
    y = jnp.concatenate([jnp.concatenate(ys[t], axis=1) for t in range(LB)], axis=0)
    us = jnp.concatenate([u[:, t * SSM_WIDTH:(t + 1) * SSM_WIDTH] for t in range(LB)], axis=0)
    y = jax.nn.gelu(y + d_ref[...] * us)
    gate = jax.nn.sigmoid(jnp.dot(y.astype(BF16), wglu_ref[...], preferred_element_type=F32))
    out = y * gate
    for c in range(SSM_WIDTH // LANES):
        for t in range(LB):
            o_scr[c, pl.ds(t, tb, stride=LB), :] = out[t * tb:(t + 1) * tb, c * LANES:(c + 1) * LANES]
        y_ref[:, c * LANES:(c + 1) * LANES] = o_scr[c]


def _ssm(u_blk, s0r, s0i, p, *, n_seq, chained):
    rows = u_blk.shape[0]
    if chained:
        tb = TB_SSM
        per = rows // n_seq // tb
        grid = (n_seq, per)
        row = lambda n, c: (n * per + c, 0)
        st = lambda n, c: (n, 0, 0)
        s0_spec = pl.BlockSpec((1, 1, SSM_CH), st)
        f_spec = pl.BlockSpec((1, 1, SSM_CH), st)
        f_shape = jax.ShapeDtypeStruct((n_seq, 1, SSM_CH), F32)
        sem = ("parallel", "arbitrary")
    else:
        tb = min(TB_SSM, rows)
        grid = (rows // tb,)
        row = lambda c: (c, 0)
        s0_spec = pl.BlockSpec((tb, SSM_CH), row)
        f_spec = pl.BlockSpec((tb, SSM_CH), row)
        f_shape = jax.ShapeDtypeStruct((rows, SSM_CH), F32)
        sem = ("parallel",)
    blk_w = LB * SSM_WIDTH
    return pl.pallas_call(
        functools.partial(_ssm_kernel, chained=chained),
        grid=grid,
        in_specs=[
            pl.BlockSpec((tb, blk_w), row),
            s0_spec, s0_spec,
            _resident((N_OCT, LB * LANES, 2 * OCT_CH)), _resident((N_OCT, LB * LANES, LB * LANES)),
            _resident((N_OCT, 2 * OCT_CH, LB * LANES)),
            _full((1, SSM_WIDTH)),
            _resident((SSM_WIDTH, SSM_WIDTH)),
            _full((1, SSM_CH)), _full((1, SSM_CH)),
            _full((3, SUBLANES, SSM_CH)), _full((3, SUBLANES, SSM_CH)),
            _full((SUBLANES, SSM_CH)), _full((SUBLANES, SSM_CH)),
        ],
        out_specs=(pl.BlockSpec((tb * LB, SSM_WIDTH), row), f_spec, f_spec),
        out_shape=(jax.ShapeDtypeStruct((rows * LB, SSM_WIDTH), F32), f_shape, f_shape),
        scratch_shapes=[
            pltpu.VMEM((tb, SSM_CH), F32), pltpu.VMEM((tb, SSM_CH), F32),
            pltpu.VMEM((1, SSM_CH), F32), pltpu.VMEM((1, SSM_CH), F32),
            pltpu.VMEM((SSM_WIDTH // LANES, tb * LB, LANES), F32),
        ],
        compiler_params=_cparams(sem),
        name="ssm_chained" if chained else "ssm_pairs",
    )(u_blk, s0r, s0i, p["w_x"], p["w_t"], p["w_c"], p["ssm_d"], p["w_glu"], p["ab_re"], p["ab_im"],
      p["lv_re"], p["lv_im"], p["cp_re"], p["cp_im"])


def _ssm_params(a_re, a_im, log_dt, b_re, b_im, c_re, c_im):
    dt = jnp.exp(log_dt)[:, None]
    mag = jnp.exp(a_re * dt)
    abr = mag * jnp.cos(a_im * dt)
    abi = mag * jnp.sin(a_im * dt)
    den = a_re * a_re + a_im * a_im
    nr = abr - 1.0
    ni = abi
    coef_re = (nr * a_re + ni * a_im) / den
    coef_im = (ni * a_re - nr * a_im) / den
    bb = (coef_re[..., None] * b_re - coef_im[..., None] * b_im,
          coef_re[..., None] * b_im + coef_im[..., None] * b_re)
    cc = (c_re, c_im)

    apow = [(jnp.ones_like(abr), jnp.zeros_like(abi))]
    for _ in range(LB):
        apow.append(_cmul(apow[-1], (abr, abi)))
    eye = jnp.eye(OCT, dtype=F32)
    hp = lax.Precision.HIGHEST
    oct5 = lambda v: v.reshape((N_OCT, OCT) + v.shape[1:])

    kd = []
    for d in range(LB):
        m = _cmul(cc, (apow[d][0][:, None, :], apow[d][1][:, None, :]))
        kd.append(jnp.einsum("ghp,gpk->ghk", m[0], bb[0], precision=hp)
                  - jnp.einsum("ghp,gpk->ghk", m[1], bb[1], precision=hp))
    kd = oct5(jnp.stack(kd, axis=1))
    lag = (jnp.arange(LB)[None, :, None] - jnp.arange(LB)[:, None, None] == jnp.arange(LB)[None, None, :])
    w_t = jnp.einsum("cgdhk,abd,gj->cagkbjh", kd, lag.astype(F32), eye, precision=hp)
    w_t = w_t.reshape(N_OCT, LB * LANES, LB * LANES)

    mx = []
    for t in range(LB):
        q = _cmul((apow[LB - 1 - t][0][:, :, None], apow[LB - 1 - t][1][:, :, None]), bb)
        mx.append(jnp.stack(q, axis=-1))
    mx = oct5(jnp.stack(mx, axis=1))
    w_x = jnp.einsum("cgapkz,gj->cagkzjp", mx, eye, precision=hp).reshape(N_OCT, LB * LANES, 2 * OCT_CH)

    mc = []
    for t in range(LB):
        m = _cmul(cc, (apow[t + 1][0][:, None, :], apow[t + 1][1][:, None, :]))
        mc.append(jnp.stack([m[0], -m[1]], axis=-1))
    mc = oct5(jnp.stack(mc, axis=1))
    w_c = jnp.einsum("cgbhpz,gj->czgpbjh", mc, eye, precision=hp).reshape(N_OCT, 2 * OCT_CH, LB * LANES)

    ab = (apow[LB][0].reshape(1, SSM_CH), apow[LB][1].reshape(1, SSM_CH))
    pows = [ab]
    for _ in range(SUBLANES - 1):
        pows.append(_cmul(pows[-1], ab))
    rows = jnp.arange(SUBLANES)[:, None]
    lv_re = jnp.stack([jnp.where(rows >= d, pows[d - 1][0], 0.0) for d in (1, 2, 4)])
    lv_im = jnp.stack([jnp.where(rows >= d, pows[d - 1][1], 0.0) for d in (1, 2, 4)])
    cp_re = jnp.concatenate([pw[0] for pw in pows], axis=0)
    cp_im = jnp.concatenate([pw[1] for pw in pows], axis=0)
    return dict(w_t=w_t.astype(BF16), w_x=w_x.astype(BF16), w_c=w_c.astype(BF16), ab_re=ab[0], ab_im=ab[1],
                lv_re=lv_re, lv_im=lv_im, cp_re=cp_re, cp_im=cp_im)


def _memkv_kernel(m_ref, g_ref, w_ref, gk_ref, o128_ref, k_ref, v_ref):
    hm = _rms_rows(m_ref[0], g_ref[...]).astype(BF16)
    kv = jnp.dot(hm, w_ref[...], preferred_element_type=F32)
    k = kv[:, :MEM_WIDTH]
    k_ref[0] = k * lax.rsqrt(_seg_mean_sq(k, o128_ref[...], MEM_HD) + EPS) * gk_ref[...]
    v_ref[0] = kv[:, MEM_WIDTH:]


def _memkv(mem, p):
    n = mem.shape[0]
    blk = lambda i: (i, 0, 0)
    shp = jax.ShapeDtypeStruct((n, N_MEM, MEM_WIDTH), F32)
    return pl.pallas_call(
        _memkv_kernel,
        grid=(n,),
        in_specs=[pl.BlockSpec((1, N_MEM, D_MODEL), blk), _full((1, D_MODEL)),
                  _full((D_MODEL, 2 * MEM_WIDTH)), _full((1, MEM_WIDTH)), _full((MEM_WIDTH, MEM_WIDTH))],
        out_specs=(pl.BlockSpec((1, N_MEM, MEM_WIDTH), blk), pl.BlockSpec((1, N_MEM, MEM_WIDTH), blk)),
        out_shape=(shp, shp),
        compiler_params=_cparams(("parallel",)),
        name="memkv",
    )(mem, p["g_mem"], p["w_mem_kv"], p["g_km"], p["ones128"])


def _dup_heads(x, lane):
    sw = pltpu.roll(x, SWA_HD, axis=x.ndim - 1)
    lo = lane < SWA_HD
    return jnp.where(lo, x, sw), jnp.where(lo, sw, x)


def _swa_group(q_blk, kk, vv, g, mask, sink_ref):
    tq = q_blk.shape[-2]
    shp = q_blk.shape[:-1]
    lane = lax.broadcasted_iota(jnp.int32, shp + (LANES,), len(shp))
    rows = []
    sinks = []
    for hl in range(SWA_Q_PER_KV):
        h = g * SWA_Q_PER_KV + hl
        pair = q_blk[..., (h // 2) * LANES:(h // 2 + 1) * LANES]
        keep = (lane < SWA_HD) if h % 2 == 0 else (lane >= SWA_HD)
        rows.append(jnp.where(keep, pair, 0.0))
        sinks.append(jnp.full(shp + (1,), sink_ref[h], F32))
    qq = jnp.concatenate(rows, axis=-2).astype(BF16)
    sk = jnp.concatenate(sinks, axis=-2)
    s = jnp.einsum("...qd,...kd->...qk", qq, kk, preferred_element_type=F32)
    s = jnp.where(mask, s, NEG_INF)
    m = jnp.maximum(jnp.max(s, axis=-1, keepdims=True), sk)
    e = jnp.exp(s - m)
    pr = e / (jnp.sum(e, axis=-1, keepdims=True) + jnp.exp(sk - m))
    o = jnp.einsum("...qk,...kd->...qd", pr.astype(BF16), vv, preferred_element_type=F32)
    lo = lane < SWA_HD
    return [jnp.where(lo, o[..., (2 * j) * tq:(2 * j + 1) * tq, :], o[..., (2 * j + 1) * tq:(2 * j + 2) * tq, :])
            for j in range(2)]


def _mem_heads(qm, k_head, v_head):
    outs = []
    for h in range(MEM_HEADS):
        cs = slice(h * MEM_HD, (h + 1) * MEM_HD)
        s = jnp.einsum("...qd,...kd->...qk", qm[..., cs], k_head(h).astype(BF16),
                       preferred_element_type=F32) * MEM_SCALE
        m = jnp.max(s, axis=-1, keepdims=True)
        e = jnp.exp(s - m)
        pr = e / jnp.sum(e, axis=-1, keepdims=True)
        outs.append(jnp.einsum("...qk,...kd->...qd", pr.astype(BF16), v_head(h).astype(BF16),
                               preferred_element_type=F32))
    return jnp.concatenate(outs, axis=-1)


def _attn_prompt_kernel(sink_ref, q_ref, k_ref, v_ref, kp_ref, vp_ref, qm_ref, mk_ref, mv_ref, ys_ref, ym_ref):
    tq = q_ref.shape[0]
    blk = WINDOW
    rows = SWA_Q_PER_KV * blk
    i = lax.broadcasted_iota(jnp.int32, (rows, 2 * blk), 0) % blk
    j = lax.broadcasted_iota(jnp.int32, (rows, 2 * blk), 1)
    lo = jnp.where(j < blk, i + 1, blk)
    hi = jnp.where(j < blk, blk, blk + i + 1)
    first_lo = jnp.where(pl.program_id(1) == 0, blk, 0)
    lane_k = lax.broadcasted_iota(jnp.int32, (2 * blk, LANES), 1)
    for b in range(tq // blk):
        rs = slice(b * blk, (b + 1) * blk)
        if b == 0:
            k2 = jnp.concatenate([kp_ref[...], k_ref[rs, :]], axis=0)
            v2 = jnp.concatenate([vp_ref[...], v_ref[rs, :]], axis=0)
            mask = (j >= jnp.maximum(lo, first_lo)) & (j < hi)
        else:
            k2 = k_ref[(b - 1) * blk:(b + 1) * blk, :]
            v2 = v_ref[(b - 1) * blk:(b + 1) * blk, :]
            mask = (j >= lo) & (j < hi)
        kks = _dup_heads(k2, lane_k)
        vvs = _dup_heads(v2, lane_k)
        q_blk = q_ref[rs, :].astype(F32)
        pairs = []
        for g in range(SWA_KV_HEADS):
            pairs += _swa_group(q_blk, kks[g].astype(BF16), vvs[g].astype(BF16), g, mask, sink_ref)
        ys_ref[rs, :] = jnp.concatenate(pairs, axis=1).astype(BF16)
    ym_ref[...] = _mem_heads(qm_ref[...], lambda h: mk_ref[0, :, h * MEM_HD:(h + 1) * MEM_HD],
                             lambda h: mv_ref[0, :, h * MEM_HD:(h + 1) * MEM_HD]).astype(BF16)


def _attn_prompt(q, k, v, qm, mk, mv, sinks, n_seq):
    t = q.shape[0]
    tq = TQ_ATT
    per = t // n_seq // tq
    sub = tq // WINDOW
    row = lambda n, c: (n * per + c, 0)
    prev = lambda n, c: (jnp.maximum((n * per + c) * sub - 1, 0), 0)
    memb = lambda n, c: (n, 0, 0)
    return pl.pallas_call(
        _attn_prompt_kernel,
        grid=(n_seq, per),
        in_specs=[
            pl.BlockSpec(memory_space=pltpu.SMEM),
            pl.BlockSpec((tq, SWA_WIDTH), row),
            pl.BlockSpec((tq, SWA_KV_WIDTH), row),
            pl.BlockSpec((tq, SWA_KV_WIDTH), row),
            pl.BlockSpec((WINDOW, SWA_KV_WIDTH), prev),
            pl.BlockSpec((WINDOW, SWA_KV_WIDTH), prev),
            pl.BlockSpec((tq, MEM_WIDTH), row),
            pl.BlockSpec((1, N_MEM, MEM_WIDTH), memb),
            pl.BlockSpec((1, N_MEM, MEM_WIDTH), memb),
        ],
        out_specs=(pl.BlockSpec((tq, SWA_WIDTH), row), pl.BlockSpec((tq, MEM_WIDTH), row)),
        out_shape=(jax.ShapeDtypeStruct((t, SWA_WIDTH), BF16), jax.ShapeDtypeStruct((t, MEM_WIDTH), BF16)),
        compiler_params=_cparams(("parallel", "parallel")),
        name="attn_prompt",
    )(sinks, q, k, v, k, v, qm, mk, mv)


def _attn_sample_kernel(sink_ref, q_ref, k_ref, v_ref, pk_ref, pv_ref, qm_ref, mk_ref, mv_ref,
                        ys_ref, ym_ref, nk_ref, nv_ref, *, s_len):
    sb, wb = pk_ref.shape[0], pk_ref.shape[1]
    n_keys = wb + s_len
    rows = SWA_Q_PER_KV * s_len
    i = lax.broadcasted_iota(jnp.int32, (sb, rows, n_keys), 1) % s_len
    j = lax.broadcasted_iota(jnp.int32, (sb, rows, n_keys), 2)
    rel = i + wb - j
    mask = (rel >= 0) & (rel < WINDOW)
    k_all = jnp.concatenate([pk_ref[...], k_ref[...].reshape(sb, s_len, SWA_KV_WIDTH)], axis=1)
    v_all = jnp.concatenate([pv_ref[...], v_ref[...].reshape(sb, s_len, SWA_KV_WIDTH)], axis=1)
    nk_ref[...] = k_all[:, n_keys - wb:, :]
    nv_ref[...] = v_all[:, n_keys - wb:, :]
    lane_k = lax.broadcasted_iota(jnp.int32, k_all.shape, 2)
    kks = _dup_heads(k_all, lane_k)
    vvs = _dup_heads(v_all, lane_k)
    q3 = q_ref[...].astype(F32).reshape(sb, s_len, SWA_WIDTH)
    pairs = []
    for g in range(SWA_KV_HEADS):
        pairs += _swa_group(q3, kks[g].astype(BF16), vvs[g].astype(BF16), g, mask, sink_ref)
    ys_ref[...] = jnp.concatenate(pairs, axis=-1).reshape(sb * s_len, SWA_WIDTH).astype(BF16)
    qm3 = qm_ref[...].astype(F32).reshape(sb, s_len, MEM_WIDTH).astype(BF16)
    head_rows = lambda h: pl.ds(h, N_MEM, stride=MEM_HEADS)
    ym = _mem_heads(qm3, lambda h: mk_ref[:, head_rows(h), :], lambda h: mv_ref[:, head_rows(h), :])
    ym_ref[...] = ym.reshape(sb * s_len, MEM_WIDTH).astype(BF16)


def _attn_sample(q, k, v, past_k, past_v, qm, mk, mv, sinks, s_len):
    t = q.shape[0]
    n_seq, wb = past_k.shape[0], past_k.shape[1]
    sb = SEQ_BLK
    rows = sb * s_len
    row = lambda c: (c, 0)
    seq = lambda c: (c, 0, 0)
    cache_shape = jax.ShapeDtypeStruct((n_seq, wb, SWA_KV_WIDTH), F32)
    return pl.pallas_call(
        functools.partial(_attn_sample_kernel, s_len=s_len),
        grid=(n_seq // sb,),
        in_specs=[
            pl.BlockSpec(memory_space=pltpu.SMEM),
            pl.BlockSpec((rows, SWA_WIDTH), row),
            pl.BlockSpec((rows, SWA_KV_WIDTH), row),
            pl.BlockSpec((rows, SWA_KV_WIDTH), row),
            pl.BlockSpec((sb, wb, SWA_KV_WIDTH), seq),
            pl.BlockSpec((sb, wb, SWA_KV_WIDTH), seq),
            pl.BlockSpec((rows, MEM_WIDTH), row),
            pl.BlockSpec((sb, N_MEM * MEM_HEADS, MEM_HD), seq),
            pl.BlockSpec((sb, N_MEM * MEM_HEADS, MEM_HD), seq),
        ],
        out_specs=(pl.BlockSpec((rows, SWA_WIDTH), row), pl.BlockSpec((rows, MEM_WIDTH), row),
                   pl.BlockSpec((sb, wb, SWA_KV_WIDTH), seq), pl.BlockSpec((sb, wb, SWA_KV_WIDTH), seq)),
        out_shape=(jax.ShapeDtypeStruct((t, SWA_WIDTH), BF16), jax.ShapeDtypeStruct((t, MEM_WIDTH), BF16),
                   cache_shape, cache_shape),
        compiler_params=_cparams(("parallel",)),
        name="attn_sample",
    )(sinks, q, k, v, past_k, past_v, qm, mk, mv)


def _first_argmax(x, valid, lane):
    xm = jnp.where(valid, x, -jnp.inf)
    mx = jnp.max(xm, axis=-1, keepdims=True)
    idx = jnp.min(jnp.where(xm == mx, lane, LANES), axis=-1, keepdims=True)
    return mx, lane == idx, idx


def _merge_kernel(xp_ref, yap_ref, ybp_ref, ycp_ref, xs_ref, yas_ref, ybs_ref, ycs_ref, *rest, n_blk_p):
    @pl.when(pl.program_id(0) < n_blk_p)
    def _():
        _merge_rows(xp_ref, yap_ref, ybp_ref, ycp_ref, *rest)

    @pl.when(pl.program_id(0) >= n_blk_p)
    def _():
        _merge_rows(xs_ref, yas_ref, ybs_ref, ycs_ref, *rest)


def _merge_rows(x_ref, ya_ref, yb_ref, yc_ref, gmix_ref, wg_ref, wa_ref, wb_ref, wc_ref, wo_ref, gffn_ref,
                wr_hi_ref, wr_lo_ref, br_ref, x1_ref, hx_ref, route_ref, cnt_ref, carry_ref):
    x = x_ref[...]
    h = _rms_rows(x, gmix_ref[...]).astype(BF16)
    def branch(k, y_ref, w_ref):
        gate = jax.nn.sigmoid(jnp.dot(h, wg_ref[:, k * D_MODEL:(k + 1) * D_MODEL], preferred_element_type=F32))
        return gate * jnp.dot(y_ref[...].astype(BF16), w_ref[...], preferred_element_type=F32)

    merged = branch(0, ya_ref, wa_ref) + branch(1, yb_ref, wb_ref) + branch(2, yc_ref, wc_ref)
    x1 = x + jnp.dot(merged.astype(BF16), wo_ref[...], preferred_element_type=F32)
    x1_ref[...] = x1
    hn = _rms_rows(x1, gffn_ref[...])
    tm = x.shape[0]
    slab_row = lambda c: pl.ds(c, tm, stride=HX_ROWS)
    for c in range(ROW_TILES):
        hx_ref[slab_row(c), :] = hn[:, c * LANES:(c + 1) * LANES]

    hi, lo = _split_bf16(hn)
    w_hi = wr_hi_ref[...]
    logits = (jnp.dot(hi, w_hi, preferred_element_type=F32) + jnp.dot(lo, w_hi, preferred_element_type=F32)
              + jnp.dot(hi, wr_lo_ref[...], preferred_element_type=F32)) + br_ref[...]
    lane = lax.broadcasted_iota(jnp.int32, logits.shape, 1)
    is_grp = lane < N_GROUPS_E
    g_max, _, g_idx = _first_argmax(logits, is_grp, lane)
    pg_top = 1.0 / jnp.sum(jnp.where(is_grp, jnp.exp(logits - g_max), 0.0), axis=-1, keepdims=True)
    e_lo = N_GROUPS_E + g_idx * EXPERTS_PER_GROUP
    in_grp = (lane >= e_lo) & (lane < e_lo + EXPERTS_PER_GROUP)
    e_max, first, i1 = _first_argmax(logits, in_grp, lane)
    ex = jnp.where(in_grp, jnp.exp(logits - e_max), 0.0)
    pe = ex / jnp.sum(ex, axis=-1, keepdims=True)
    _, second, i2 = _first_argmax(logits, in_grp & jnp.logical_not(first), lane)
    p1 = jnp.sum(jnp.where(first, pe, 0.0), axis=-1, keepdims=True)
    p2 = jnp.sum(jnp.where(second, pe, 0.0), axis=-1, keepdims=True)
    w1 = pg_top * p1 / (p1 + p2)
    w2 = pg_top * p2 / (p1 + p2)

    a1 = i1 - e_lo
    a2 = i2 - e_lo
    e_a = jnp.minimum(a1, a2)
    e_b = jnp.maximum(a1, a2)
    pair = jnp.right_shift(e_a * (2 * EXPERTS_PER_GROUP - 1 - e_a), 1) + (e_b - e_a - 1)
    bucket = g_idx * PAIRS_PER_GROUP + pair
    w_a = jnp.where(a1 < a2, w1, w2)
    w_b = jnp.where(a1 < a2, w2, w1)

    @pl.when(pl.program_id(0) == 0)
    def _():
        carry_ref[...] = jnp.zeros_like(carry_ref)

    onehot = lane == bucket
    tri = (lax.broadcasted_iota(jnp.int32, (tm, tm), 1) <= lax.broadcasted_iota(jnp.int32, (tm, tm), 0))
    csum = jnp.dot(jnp.where(tri, 1.0, 0.0).astype(BF16), jnp.where(onehot, 1.0, 0.0).astype(BF16),
                   preferred_element_type=F32)
    carry = carry_ref[...]
    rank = jnp.sum(jnp.where(onehot, csum + carry, 0.0), axis=-1, keepdims=True) - 1.0
    carry = carry + csum[tm - 1:tm, :]
    carry_ref[...] = carry
    cnt_ref[...] = carry
    route = jnp.where(lane == 0, bucket.astype(F32),
                      jnp.where(lane == 1, w_a, jnp.where(lane == 2, w_b, jnp.where(lane == 3, rank, 0.0))))
    route_ref[...] = route
    hx_ref[slab_row(ROW_TILES), :] = route
    for c in range(ROW_TILES + 1, HX_ROWS):
        hx_ref[slab_row(c), :] = jnp.zeros_like(route)


def _merge(prompt_rows, sample_rows, p):
    tm = TM_MRG
    t_p, t_s = prompt_rows[0].shape[0], sample_rows[0].shape[0]
    nbp = t_p // tm
    t_all = t_p + t_s
    first = lambda i: (jnp.minimum(i, nbp - 1), 0)
    second = lambda i: (jnp.maximum(i - nbp, 0), 0)
    row = lambda i: (i, 0)
    widths = (D_MODEL, SSM_WIDTH, SWA_WIDTH, MEM_WIDTH)
    in_specs = ([pl.BlockSpec((tm, w), first) for w in widths] + [pl.BlockSpec((tm, w), second) for w in widths] + [
        _resident((1, D_MODEL)),
        _resident((D_MODEL, N_BRANCH * D_MODEL)),
        _resident((SSM_WIDTH, D_MODEL)), _resident((SWA_WIDTH, D_MODEL)), _resident((MEM_WIDTH, D_MODEL)),
        _resident((D_MODEL, D_MODEL)),
        _resident((1, D_MODEL)),
        _resident((D_MODEL, LANES)), _resident((D_MODEL, LANES)), _resident((1, LANES)),
    ])
    return pl.pallas_call(
        functools.partial(_merge_kernel, n_blk_p=nbp),
        grid=(t_all // tm,),
        in_specs=in_specs,
        out_specs=(pl.BlockSpec((tm, D_MODEL), row), pl.BlockSpec((tm * HX_ROWS, LANES), row),
                   pl.BlockSpec((tm, ROUTE_W), row), _full((1, LANES))),
        out_shape=(jax.ShapeDtypeStruct((t_all, D_MODEL), F32),
                   jax.ShapeDtypeStruct((t_all * HX_ROWS, LANES), F32),
                   jax.ShapeDtypeStruct((t_all, ROUTE_W), F32), jax.ShapeDtypeStruct((1, LANES), F32)),
        scratch_shapes=[pltpu.VMEM((1, LANES), F32)],
        compiler_params=_cparams(("arbitrary",)),
        name="merge",
    )(*prompt_rows, *sample_rows, p["g_mix"], p["w_gates"], p["w_br_ssm"], p["w_br_swa"], p["w_br_mem"], p["w_o"],
      p["g_ffn"], p["w_r_hi"], p["w_r_lo"], p["b_r"])


def _rows_to_lanes(col):
    out = []
    for k in range(col.shape[0] // LANES):
        blk = jnp.broadcast_to(col[k * LANES:(k + 1) * LANES], (LANES, LANES))
        out.append(blk.T[0:1, :])
    return jnp.concatenate(out, axis=0)


def _pos_kernel(route_ref, off_ref, coff_ref, pos_ref, cpos_ref):
    r = route_ref[...]
    lane = lax.broadcasted_iota(jnp.int32, r.shape, 1)
    mine = lane == r[:, 0:1].astype(jnp.int32)
    off = jnp.sum(jnp.where(mine, off_ref[...], 0.0), axis=-1, keepdims=True)
    coff = jnp.sum(jnp.where(mine, coff_ref[...], 0.0), axis=-1, keepdims=True)
    rank = r[:, 3:4]
    pos_ref[...] = _rows_to_lanes(off + rank).astype(jnp.int32)
    cpos_ref[...] = _rows_to_lanes(coff + rank).astype(jnp.int32)


def _sorted_pos(route, off, coff):
    t = route.shape[0]
    tm = SUBLANES * LANES
    shp = jax.ShapeDtypeStruct((t // LANES, LANES), jnp.int32)
    pos, cpos = pl.pallas_call(
        _pos_kernel,
        grid=(t // tm,),
        in_specs=[pl.BlockSpec((tm, ROUTE_W), lambda i: (i, 0)), _full((1, LANES)), _full((1, LANES))],
        out_specs=(pl.BlockSpec((SUBLANES, LANES), lambda i: (i, 0)), pl.BlockSpec((SUBLANES, LANES), lambda i: (i, 0))),
        out_shape=(shp, shp),
        compiler_params=_cparams(("parallel",)),
        name="sorted_pos",
    )(route, off, coff)
    return pos.reshape(t), cpos.reshape(t)


def _inv_kernel(pos_ref, idx_ref):
    def body(t, _):
        idx_ref[pos_ref[t]] = t
        return 0

    lax.fori_loop(0, pos_ref.shape[0], body, 0, unroll=8)


def _invert(pos):
    return pl.pallas_call(
        _inv_kernel,
        in_specs=[pl.BlockSpec(memory_space=pltpu.SMEM)],
        out_specs=pl.BlockSpec(memory_space=pltpu.SMEM),
        out_shape=jax.ShapeDtypeStruct(pos.shape, jnp.int32),
        name="invert_perm",
    )(pos)


def _bucket_kernel(idx_ref, tg_ref, ta_ref, tb_ref, cb_ref, nr_ref, hx_hbm, wi_ref, wd_ref,
                   ys_ref, buf, sem):
    j = pl.program_id(0)
    n_real = nr_ref[0]

    last = idx_ref.shape[0] - 1

    def issue_row(tile_base, slot, r, dst_row, prio):
        src = idx_ref[jnp.minimum(tile_base + r, last)]
        pltpu.make_async_copy(hx_hbm.at[pl.ds(pl.multiple_of(src * HX_ROWS, HX_ROWS), HX_ROWS), :],
                              buf.at[slot, pl.ds(dst_row, HX_ROWS), :], sem.at[slot]).start(priority=prio)

    def wait_tile(slot):
        pltpu.make_async_copy(hx_hbm.at[pl.ds(0, TM_EXP * HX_ROWS), :], buf.at[slot], sem.at[slot]).wait()

    @pl.when(j == 0)
    def _():
        base = cb_ref[0]

        def body(r8, _):
            for k in range(SUBLANES):
                r = r8 * SUBLANES + k
                issue_row(base, 0, r, pl.multiple_of(r * HX_ROWS, HX_ROWS), k % 2)
            return 0

        lax.fori_loop(0, TM_EXP // SUBLANES, body, 0)

    @pl.when(j < n_real)
    def _():
        slot = j % 2
        wait_tile(slot)
        nxt = jnp.minimum(j + 1, n_real - 1)
        base = cb_ref[nxt]
        for r in range(TM_EXP):
            issue_row(base, 1 - slot, r, r * HX_ROWS, r % 2)

        slab_row = lambda c: buf[slot, pl.ds(c, TM_EXP, stride=HX_ROWS), :]
        x = jnp.concatenate([slab_row(c) for c in range(ROW_TILES)], axis=1).astype(BF16)
        route = slab_row(ROW_TILES)

        def ffn(e):
            gu = jnp.dot(x, wi_ref[0, e], preferred_element_type=F32)
            a = jax.nn.silu(gu[:, :D_FF]) * gu[:, D_FF:]
            return jnp.dot(a.astype(BF16), wd_ref[0, e], preferred_element_type=F32)

        y = route[:, 1:2] * ffn(ta_ref[j]) + route[:, 2:3] * ffn(tb_ref[j])
        for c in range(ROW_TILES):
            ys_ref[pl.ds(c, TM_EXP, stride=ROW_TILES), :] = y[:, c * LANES:(c + 1) * LANES]

        @pl.when(j == n_real - 1)
        def _():
            wait_tile(1 - slot)


def _bucket_ffn(hx, idx, tile_g, tile_a, tile_b, tile_cb, n_real, p):
    n_tiles = tile_a.shape[0]
    grp = lambda j, idx, tg, ta, tb, cb, nr: (tg[j], 0, 0, 0)
    out = lambda j, idx, tg, ta, tb, cb, nr: (jnp.minimum(j, nr[0] - 1), 0)
    epg = EXPERTS_PER_GROUP
    return pl.pallas_call(
        _bucket_kernel,
        grid_spec=pltpu.PrefetchScalarGridSpec(
            num_scalar_prefetch=6,
            grid=(n_tiles,),
            in_specs=[
                pl.BlockSpec(memory_space=pl.ANY),
                pl.BlockSpec((1, epg, D_MODEL, 2 * D_FF), grp), pl.BlockSpec((1, epg, D_FF, D_MODEL), grp),
            ],
            out_specs=pl.BlockSpec((TM_EXP * ROW_TILES, LANES), out),
            scratch_shapes=[pltpu.VMEM((2, TM_EXP * HX_ROWS, LANES), F32), pltpu.SemaphoreType.DMA((2,))],
        ),
        out_shape=jax.ShapeDtypeStruct((n_tiles * TM_EXP * ROW_TILES, LANES), F32),
        compiler_params=_cparams(("arbitrary",)),
        name="bucket_ffn",
    )(idx, tile_g, tile_a, tile_b, tile_cb, n_real, hx,
      p["w_exp_in"].reshape(N_GROUPS_E, epg, D_MODEL, 2 * D_FF),
      p["w_exp_down"].reshape(N_GROUPS_E, epg, D_FF, D_MODEL))


def _back_kernel(pos_ref, x1_ref, ys_hbm, o_ref, buf, sem, *, t0):
    i = pl.program_id(0)
    tm = o_ref.shape[0]

    def issue(tile, slot):
        base = t0 + tile * tm

        def body(r8, _):
            for k in range(SUBLANES):
                r = r8 * SUBLANES + k
                src = pl.multiple_of(pos_ref[base + r] * ROW_TILES, ROW_TILES)
                pltpu.make_async_copy(ys_hbm.at[pl.ds(src, ROW_TILES), :],
                                      buf.at[slot, pl.ds(pl.multiple_of(r * ROW_TILES, ROW_TILES), ROW_TILES), :],
                                      sem.at[slot]).start(priority=k % 2)
            return 0

        lax.fori_loop(0, tm // SUBLANES, body, 0)

    @pl.when(i == 0)
    def _():
        issue(0, 0)

    @pl.when(i + 1 < pl.num_programs(0))
    def _():
        issue(i + 1, (i + 1) % 2)

    slot = i % 2
    pltpu.make_async_copy(ys_hbm.at[pl.ds(0, tm * ROW_TILES), :], buf.at[slot], sem.at[slot]).wait()
    y = jnp.concatenate([buf[slot, pl.ds(c, tm, stride=ROW_TILES), :] for c in range(ROW_TILES)], axis=1)
    o_ref[...] = x1_ref[...] + y


def _unsort_add(x1, ys, pos, t0, t):
    tm = TM_BACK
    row = lambda i, pos: (i, 0)
    return pl.pallas_call(
        functools.partial(_back_kernel, t0=t0),
        grid_spec=pltpu.PrefetchScalarGridSpec(
            num_scalar_prefetch=1,
            grid=(t // tm,),
            in_specs=[pl.BlockSpec((tm, D_MODEL), lambda i, pos: (i + t0 // tm, 0)),
                      pl.BlockSpec(memory_space=pl.ANY)],
            out_specs=pl.BlockSpec((tm, D_MODEL), row),
            scratch_shapes=[pltpu.VMEM((2, tm * ROW_TILES, LANES), F32), pltpu.SemaphoreType.DMA((2,))],
        ),
        out_shape=jax.ShapeDtypeStruct((t, D_MODEL), F32),
        compiler_params=_cparams(("arbitrary",)),
        name="unsort_add",
    )(pos, x1, ys)


def _bucket_experts():
    lo, hi = [], []
    for g in range(N_GROUPS_E):
        for a in range(EXPERTS_PER_GROUP):
            for b in range(a + 1, EXPERTS_PER_GROUP):
                lo.append(g * EXPERTS_PER_GROUP + a)
                hi.append(g * EXPERTS_PER_GROUP + b)
    return jnp.asarray(lo, jnp.int32), jnp.asarray(hi, jnp.int32)


def _tile_tables(counts, n_tiles):
    cnt = counts[0, :N_BUCKETS].astype(jnp.int32)
    nt = (cnt + TM_EXP - 1) // TM_EXP
    tend = jnp.cumsum(nt)
    tstart = tend - nt
    cstart = jnp.cumsum(cnt) - cnt
    pad = lambda v: jnp.zeros((1, LANES), F32).at[0, :N_BUCKETS].set(v.astype(F32))
    j = jnp.arange(n_tiles, dtype=jnp.int32)
    b = jnp.minimum(jnp.sum((tend[None, :] <= j[:, None]).astype(jnp.int32), axis=1), N_BUCKETS - 1)
    in_bucket = (j - tstart[b]) * TM_EXP
    e_lo, e_hi = _bucket_experts()
    epg = EXPERTS_PER_GROUP
    return (pad(tstart * TM_EXP), pad(cstart), b // PAIRS_PER_GROUP, e_lo[b] % epg, e_hi[b] % epg,
            cstart[b] + in_bucket, tend[-1:])


def _rope_tables(pos):
    half = SWA_HD // 2
    inv = ROPE_THETA ** (-jnp.arange(half, dtype=F32) / half)
    ang = pos.astype(F32)[:, None] * inv[None, :]
    cos = jnp.cos(ang)
    sin = jnp.sin(ang)
    cos = jnp.concatenate([cos, cos, cos, cos], axis=1)
    sin = jnp.concatenate([-sin, sin, -sin, sin], axis=1)
    return cos, sin


def kernel(x_prompt, x_sample, mem_prompt, state_ssm_re, state_ssm_im, cache_swa_k, cache_swa_v, cache_mem_k, cache_mem_v, norm_mix, w_in, ssm_a_re, ssm_a_im, ssm_log_dt, ssm_b_re, ssm_b_im, ssm_c_re, ssm_c_im, ssm_d, w_glu, swa_q_norm, swa_k_norm, swa_sinks, norm_mem, w_mem_kv, mem_q_norm, mem_k_norm, w_br_ssm, w_br_swa, w_br_mem, w_o, norm_ffn, w_router_group, b_router_group, w_router_expert, b_router_expert, w_exp_in, w_exp_down):
    depth = w_in.shape[0]
    assert depth == 1
    nb, seq, _ = x_prompt.shape
    db, dseq, _ = x_sample.shape
    assert dseq == 2 * LB
    l = 0

    w_r = jnp.concatenate([w_router_group[l], w_router_expert[l]], axis=1)
    w_r = jnp.pad(w_r, ((0, 0), (0, LANES - w_r.shape[1])))
    w_r_hi = w_r.astype(BF16)
    b_r = jnp.pad(jnp.concatenate([b_router_group[l], b_router_expert[l]]), (0, LANES - N_GROUPS_E - N_EXPERTS))
    p = dict(
        g_mix=norm_mix[l][None], w_a=w_in[l][:, :PROJ_A].astype(BF16), w_gates=w_in[l][:, PROJ_A:].astype(BF16),
        g_q=jnp.tile(swa_q_norm[l], SWA_HEADS)[None], g_k=jnp.tile(swa_k_norm[l], SWA_KV_HEADS)[None],
        g_qm=jnp.tile(mem_q_norm[l], MEM_HEADS)[None], g_km=jnp.tile(mem_k_norm[l], MEM_HEADS)[None],
        ones64=_block_ones(SWA_WIDTH, SWA_HD), ones128=_block_ones(MEM_WIDTH, MEM_HD),
        ssm_d=ssm_d[l][None], w_glu=w_glu[l].astype(BF16),
        g_mem=norm_mem[l][None], w_mem_kv=w_mem_kv[l].astype(BF16),
        w_br_ssm=w_br_ssm[l].astype(BF16), w_br_swa=w_br_swa[l].astype(BF16), w_br_mem=w_br_mem[l].astype(BF16),
        w_o=w_o[l].astype(BF16), g_ffn=norm_ffn[l][None],
        w_r_hi=w_r_hi, w_r_lo=(w_r - w_r_hi.astype(F32)).astype(BF16), b_r=b_r[None],
        w_exp_in=w_exp_in[l].astype(BF16), w_exp_down=w_exp_down[l].astype(BF16),
    )
    p.update(_ssm_params(ssm_a_re[l], ssm_a_im[l], ssm_log_dt[l], ssm_b_re[l], ssm_b_im[l],
                         ssm_c_re[l], ssm_c_im[l]))
    sinks = swa_sinks[l]

    xp = x_prompt.reshape(nb * seq, D_MODEL)
    cos_p, sin_p = _rope_tables(jnp.arange(seq, dtype=jnp.int32))
    u, q, k, v, qm = _inproj(xp, cos_p, sin_p, seq // TM_IN, p)
    zeros_state = jnp.zeros((nb, 1, SSM_CH), F32)
    y_ssm, pr, pi = _ssm(u, zeros_state, zeros_state, p, n_seq=nb, chained=True)
    mk, mv = _memkv(mem_prompt, p)
    y_swa, y_mem = _attn_prompt(q, k, v, qm, mk, mv, sinks, nb)
    win = min(WINDOW, seq)
    p_k = k.reshape(nb, seq, SWA_KV_HEADS, SWA_HD)[:, seq - win:]
    p_v = v.reshape(nb, seq, SWA_KV_HEADS, SWA_HD)[:, seq - win:]

    xs = x_sample.reshape(db * dseq, D_MODEL)
    cos_s, sin_s = _rope_tables(PAST_LEN + jnp.arange(dseq, dtype=jnp.int32))
    reps = TM_IN // dseq
    us, qs, ks, vs, qms = _inproj(xs, jnp.tile(cos_s, (reps, 1)), jnp.tile(sin_s, (reps, 1)), 1, p)
    two_rows = lambda st: jnp.repeat(st.reshape(db, SSM_CH), 2, axis=0)
    ys_ssm, sr, si = _ssm(us, two_rows(state_ssm_re), two_rows(state_ssm_im), p, n_seq=db, chained=False)
    sr, si = sr[1::2], si[1::2]
    wb = cache_swa_k.shape[2]
    ys_swa, ys_mem, s_k, s_v = _attn_sample(
        qs, ks, vs, cache_swa_k[l].reshape(db, wb, SWA_KV_WIDTH), cache_swa_v[l].reshape(db, wb, SWA_KV_WIDTH),
        qms, cache_mem_k.reshape(db, N_MEM * MEM_HEADS, MEM_HD), cache_mem_v.reshape(db, N_MEM * MEM_HEADS, MEM_HD),
        sinks, dseq)
    t_p, t_s = nb * seq, db * dseq
    t_all = t_p + t_s
    x1, hx, route, counts = _merge((xp, y_ssm, y_swa, y_mem), (xs, ys_ssm, ys_swa, ys_mem), p)

    n_tiles = t_all // TM_EXP + N_BUCKETS
    off, coff, tile_g, tile_a, tile_b, tile_cb, n_real = _tile_tables(counts, n_tiles)
    pos, cpos = _sorted_pos(route, off, coff)
    idx = _invert(cpos)
    y_sorted = _bucket_ffn(hx, idx, tile_g, tile_a, tile_b, tile_cb, n_real, p)
    yp = _unsort_add(x1, y_sorted, pos, 0, t_p).reshape(nb, seq, D_MODEL)
    ys = _unsort_add(x1, y_sorted, pos, t_p, t_s).reshape(db, dseq, D_MODEL)

    g, s = SSM_GROUPS, SSM_STATE
    return (yp, ys,
            pr.reshape(1, nb, g, s), pi.reshape(1, nb, g, s),
            p_k[None], p_v[None],
            mk.reshape(1, nb, N_MEM, MEM_HEADS, MEM_HD), mv.reshape(1, nb, N_MEM, MEM_HEADS, MEM_HD),
            sr.reshape(1, db, g, s), si.reshape(1, db, g, s),
            s_k.reshape(1, db, wb, SWA_KV_HEADS, SWA_HD), s_v.reshape(1, db, wb, SWA_KV_HEADS, SWA_HD))
```

```python
import functools
import math

import jax
import jax.numpy as jnp
from jax import lax
from jax.experimental import pallas as pl
from jax.experimental.pallas import tpu as pltpu

F32 = jnp.float32
BF16 = jnp.bfloat16

D_MODEL = 1024
SSM_WIDTH = 512
SSM_GROUP = 16
SSM_GROUPS = 32
SSM_STATE = 64
SSM_CH = SSM_GROUPS * SSM_STATE
LB = 4
OCT = 8
N_OCT = SSM_GROUPS // OCT
OCT_CH = OCT * SSM_STATE
SWA_HEADS = 8
SWA_KV_HEADS = 2
SWA_Q_PER_KV = SWA_HEADS // SWA_KV_HEADS
SWA_HD = 64
SWA_WIDTH = SWA_HEADS * SWA_HD
SWA_KV_WIDTH = SWA_KV_HEADS * SWA_HD
WINDOW = 128
PAST_LEN = 16384
ROPE_THETA = 10000.0
N_MEM = 256
MEM_HEADS = 4
MEM_HD = 128
MEM_WIDTH = MEM_HEADS * MEM_HD
N_BRANCH = 3
PROJ_A = SSM_WIDTH + SWA_WIDTH + 2 * SWA_KV_WIDTH + MEM_WIDTH
N_GROUPS_E = 4
EXPERTS_PER_GROUP = 8
N_EXPERTS = 32
D_FF = 256
EPS = 1e-6
NEG_INF = -1e30
SWA_SCALE = SWA_HD ** -0.5
MEM_SCALE = MEM_HD ** -0.5

LANES = 128
SUBLANES = 8
VMEM_LIMIT = 56 * 1024 * 1024

TM_IN = 512
TB_SSM = 512
SCAN_W = 512
TQ_ATT = 512
SEQ_BLK = 8
TM_MRG = 512
TM_BACK = 512
TM_EXP = 192

ROUTE_W = LANES
ROW_TILES = D_MODEL // LANES
HX_ROWS = 2 * ROW_TILES
PAIRS_PER_GROUP = EXPERTS_PER_GROUP * (EXPERTS_PER_GROUP - 1) // 2
N_BUCKETS = N_GROUPS_E * PAIRS_PER_GROUP


def _cparams(sem):
    return pltpu.CompilerParams(dimension_semantics=sem, vmem_limit_bytes=VMEM_LIMIT)


def _full(shape):
    nd = len(shape)
    return pl.BlockSpec(shape, lambda *_: (0,) * nd)


def _resident(shape):
    nd = len(shape)
    return pl.BlockSpec(shape, lambda *_: (0,) * nd, pipeline_mode=pl.Buffered(1))


def _split_bf16(x):
    hi = x.astype(BF16)
    lo = (x - hi.astype(F32)).astype(BF16)
    return hi, lo


def _seg_mean_sq(x, ones_blk, width):
    hi, lo = _split_bf16(x * x)
    s = jnp.dot(hi, ones_blk, preferred_element_type=F32) + jnp.dot(lo, ones_blk, preferred_element_type=F32)
    return s * (1.0 / width)


def _rms_rows(x, gain):
    return x * lax.rsqrt(jnp.mean(x * x, axis=-1, keepdims=True) + EPS) * gain


def _block_ones(n, width):
    i = jnp.arange(n) // width
    return (i[:, None] == i[None, :]).astype(BF16)


def _rope_cols(x, cos, sin_signed, lane_in_head):
    n = x.shape[1]
    reps = n // LANES
    if reps > 1:
        cos = jnp.concatenate([cos] * reps, axis=1)
        sin_signed = jnp.concatenate([sin_signed] * reps, axis=1)
    half = SWA_HD // 2
    partner = jnp.where(lane_in_head < half, pltpu.roll(x, n - half, axis=1), pltpu.roll(x, half, axis=1))
    return x * cos + partner * sin_signed


def _inproj_kernel(x_ref, gmix_ref, w_ref, gq_ref, gk_ref, gm_ref, cos_ref, sin_ref, o64_ref, o128_ref,
                   u_ref, q_ref, k_ref, v_ref, qm_ref, u_scr):
    x = x_ref[...]
    h = _rms_rows(x, gmix_ref[...]).astype(BF16)
    proj = jnp.dot(h, w_ref[...], preferred_element_type=F32)
    c0 = SSM_WIDTH
    c1 = c0 + SWA_WIDTH
    c2 = c1 + SWA_KV_WIDTH
    c3 = c2 + SWA_KV_WIDTH
    n_blk = u_ref.shape[0]
    for c in range(c0 // LANES):
        u_scr[c] = proj[:, c * LANES:(c + 1) * LANES]
        for t in range(LB):
            u_ref[:, t * c0 + c * LANES:t * c0 + (c + 1) * LANES] = u_scr[c, pl.ds(t, n_blk, stride=LB), :]
    q = proj[:, c0:c1]
    k = proj[:, c1:c2]
    v_ref[...] = proj[:, c2:c3]
    qm = proj[:, c3:]
    cos = cos_ref[...]
    sin = sin_ref[...]
    o64 = o64_ref[...]
    lane_q = lax.broadcasted_iota(jnp.int32, q.shape, 1) % SWA_HD
    qn = q * lax.rsqrt(_seg_mean_sq(q, o64, SWA_HD) + EPS) * gq_ref[...]
    q_ref[...] = (_rope_cols(qn, cos, sin, lane_q) * SWA_SCALE).astype(BF16)
    lane_k = lax.broadcasted_iota(jnp.int32, k.shape, 1) % SWA_HD
    kn = k * lax.rsqrt(_seg_mean_sq(k, o64[:SWA_KV_WIDTH, :SWA_KV_WIDTH], SWA_HD) + EPS) * gk_ref[...]
    k_ref[...] = _rope_cols(kn, cos, sin, lane_k)
    qmn = qm * lax.rsqrt(_seg_mean_sq(qm, o128_ref[...], MEM_HD) + EPS) * gm_ref[...]
    qm_ref[...] = qmn.astype(BF16)


def _inproj(x2d, cos, sin, pos_blocks, p):
    t = x2d.shape[0]
    tm = TM_IN
    grid = (t // tm,)
    row = lambda i: (i, 0)
    tab = lambda i: (i % pos_blocks, 0)
    out_shape = (
        jax.ShapeDtypeStruct((t // LB, LB * SSM_WIDTH), F32),
        jax.ShapeDtypeStruct((t, SWA_WIDTH), BF16),
        jax.ShapeDtypeStruct((t, SWA_KV_WIDTH), F32),
        jax.ShapeDtypeStruct((t, SWA_KV_WIDTH), F32),
        jax.ShapeDtypeStruct((t, MEM_WIDTH), BF16),
    )
    return pl.pallas_call(
        _inproj_kernel,
        grid=grid,
        in_specs=[
            pl.BlockSpec((tm, D_MODEL), row),
            _full((1, D_MODEL)),
            _resident((D_MODEL, PROJ_A)),
            _full((1, SWA_WIDTH)),
            _full((1, SWA_KV_WIDTH)),
            _full((1, MEM_WIDTH)),
            pl.BlockSpec((tm, LANES), tab),
            pl.BlockSpec((tm, LANES), tab),
            _full((SWA_WIDTH, SWA_WIDTH)),
            _full((MEM_WIDTH, MEM_WIDTH)),
        ],
        out_specs=(
            pl.BlockSpec((tm // LB, LB * SSM_WIDTH), row),
            pl.BlockSpec((tm, SWA_WIDTH), row),
            pl.BlockSpec((tm, SWA_KV_WIDTH), row),
            pl.BlockSpec((tm, SWA_KV_WIDTH), row),
            pl.BlockSpec((tm, MEM_WIDTH), row),
        ),
        out_shape=out_shape,
        scratch_shapes=[pltpu.VMEM((SSM_WIDTH // LANES, tm, LANES), F32)],
        compiler_params=_cparams(("parallel",)),
        name="inproj",
    )(x2d, p["g_mix"], p["w_a"], p["g_q"], p["g_k"], p["g_qm"], cos, sin, p["ones64"], p["ones128"])


def _cmul(x, y):
    return x[0] * y[0] - x[1] * y[1], x[0] * y[1] + x[1] * y[0]


def _ssm_kernel(u_ref, s0r_ref, s0i_ref, wx_ref, wt_ref, wc_ref, d_ref, wglu_ref, abr_ref, abi_ref,
                lvr_ref, lvi_ref, cpr_ref, cpi_ref,
                y_ref, fr_ref, fi_ref, sr_ref, si_ref, car_ref, cai_ref, o_scr, *, chained):
    tb = u_ref.shape[0]
    u = u_ref[...]
    ub = u.astype(BF16)
    lhs = [jnp.concatenate([ub[:, t * SSM_WIDTH + c * LANES:t * SSM_WIDTH + (c + 1) * LANES] for t in range(LB)],
                           axis=1) for c in range(N_OCT)]
    for c in range(N_OCT):
        x = jnp.dot(lhs[c], wx_ref[c], preferred_element_type=F32)
        cs = slice(c * OCT_CH, (c + 1) * OCT_CH)
        sr_ref[:, cs] = x[:, :OCT_CH]
        si_ref[:, cs] = x[:, OCT_CH:]

    if chained:
        @pl.when(pl.program_id(1) == 0)
        def _():
            car_ref[...] = s0r_ref[0]
            cai_ref[...] = s0i_ref[0]

        first_row = lax.broadcasted_iota(jnp.int32, (SUBLANES, SCAN_W), 0) == 0
        for sl in range(SSM_CH // SCAN_W):
            cols = slice(sl * SCAN_W, (sl + 1) * SCAN_W)
            lv = [(lvr_ref[j, :, cols], lvi_ref[j, :, cols]) for j in range(3)]
            cpr = cpr_ref[:, cols]
            cpi = cpi_ref[:, cols]

            def tile(i, carry, cols=cols, lv=lv, cpr=cpr, cpi=cpi):
                r0 = pl.multiple_of(i * SUBLANES, SUBLANES)
                xr = sr_ref[pl.ds(r0, SUBLANES), cols]
                xi = si_ref[pl.ds(r0, SUBLANES), cols]
                for j, d in enumerate((1, 2, 4)):
                    pr, pi = lv[j]
                    shr = pltpu.roll(xr, d, axis=0)
                    shi = pltpu.roll(xi, d, axis=0)
                    xr, xi = xr + pr * shr - pi * shi, xi + pr * shi + pi * shr
                cb_r = jnp.broadcast_to(carry[0], xr.shape)
                cb_i = jnp.broadcast_to(carry[1], xr.shape)
                xr, xi = xr + cpr * cb_r - cpi * cb_i, xi + cpr * cb_i + cpi * cb_r
                sr_ref[pl.ds(r0, SUBLANES), cols] = jnp.where(first_row, cb_r, pltpu.roll(xr, 1, axis=0))
                si_ref[pl.ds(r0, SUBLANES), cols] = jnp.where(first_row, cb_i, pltpu.roll(xi, 1, axis=0))
                return xr[SUBLANES - 1:SUBLANES, :], xi[SUBLANES - 1:SUBLANES, :]

            c_r, c_i = lax.fori_loop(0, tb // SUBLANES, tile, (car_ref[:, cols], cai_ref[:, cols]))
            car_ref[:, cols] = c_r
            cai_ref[:, cols] = c_i
        fr_ref[0] = car_ref[...]
        fi_ref[0] = cai_ref[...]
    else:
        odd = lax.broadcasted_iota(jnp.int32, (tb, SSM_CH), 0) % 2 == 1
        ab = (abr_ref[...], abi_ref[...])
        s0 = (s0r_ref[...], s0i_ref[...])
        x = (sr_ref[...], si_ref[...])
        e_first = _cmul(ab, s0)
        e_first = (e_first[0] + x[0], e_first[1] + x[1])
        prev = (pltpu.roll(e_first[0], 1, axis=0), pltpu.roll(e_first[1], 1, axis=0))
        e_second = _cmul(ab, prev)
        fr_ref[...] = jnp.where(odd, e_second[0] + x[0], e_first[0])
        fi_ref[...] = jnp.where(odd, e_second[1] + x[1], e_first[1])
        sr_ref[...] = jnp.where(odd, prev[0], s0[0])
        si_ref[...] = jnp.where(odd, prev[1], s0[1])

    ys = [[None] * N_OCT for _ in range(LB)]
    for c in range(N_OCT):
        cs = slice(c * OCT_CH, (c + 1) * OCT_CH)
        s_in = jnp.concatenate([sr_ref[:, cs], si_ref[:, cs]], axis=1).astype(BF16)
        yc = (jnp.dot(lhs[c], wt_ref[c], preferred_element_type=F32)
              + jnp.dot(s_in, wc_ref[c], preferred_element_type=F32))
        for t in range(LB):
            ys[t][c] = yc[:, t * LANES:(t + 1) * LANES]
    y = jnp.concatenate([jnp.concatenate(ys[t], axis=1) for t in range(LB)], axis=0)
    us = jnp.concatenate([u[:, t * SSM_WIDTH:(t + 1) * SSM_WIDTH] for t in range(LB)], axis=0)
    y = jax.nn.gelu(y + d_ref[...] * us)
    gate = jax.nn.sigmoid(jnp.dot(y.astype(BF16), wglu_ref[...], preferred_element_type=F32))
    out = y * gate
    for c in range(SSM_WIDTH // LANES):
        for t in range(LB):
            o_scr[c, pl.ds(t, tb, stride=LB), :] = out[t * tb:(t + 1) * tb, c * LANES:(c + 1) * LANES]
        y_ref[:, c * LANES:(c + 1) * LANES] = o_scr[c]


def _ssm(u_blk, s0r, s0i, p, *, n_seq, chained):
    rows = u_blk.shape[0]
    if chained:
        tb = TB_SSM
        per = rows // n_seq // tb
        grid = (n_seq, per)
        row = lambda n, c: (n * per + c, 0)
        st = lambda n, c: (n, 0, 0)
        s0_spec = pl.BlockSpec((1, 1, SSM_CH), st)
        f_spec = pl.BlockSpec((1, 1, SSM_CH), st)
        f_shape = jax.ShapeDtypeStruct((n_seq, 1, SSM_CH), F32)
        sem = ("parallel", "arbitrary")
    else:
        tb = min(TB_SSM, rows)
        grid = (rows // tb,)
        row = lambda c: (c, 0)
        s0_spec = pl.BlockSpec((tb, SSM_CH), row)
        f_spec = pl.BlockSpec((tb, SSM_CH), row)
        f_shape = jax.ShapeDtypeStruct((rows, SSM_CH), F32)
        sem = ("parallel",)
    blk_w = LB * SSM_WIDTH
    return pl.pallas_call(
        functools.partial(_ssm_kernel, chained=chained),
        grid=grid,
        in_specs=[
            pl.BlockSpec((tb, blk_w), row),
            s0_spec, s0_spec,
            _resident((N_OCT, LB * LANES, 2 * OCT_CH)), _resident((N_OCT, LB * LANES, LB * LANES)),
            _resident((N_OCT, 2 * OCT_CH, LB * LANES)),
            _full((1, SSM_WIDTH)),
            _resident((SSM_WIDTH, SSM_WIDTH)),
            _full((1, SSM_CH)), _full((1, SSM_CH)),
            _full((3, SUBLANES, SSM_CH)), _full((3, SUBLANES, SSM_CH)),
            _full((SUBLANES, SSM_CH)), _full((SUBLANES, SSM_CH)),
        ],
        out_specs=(pl.BlockSpec((tb * LB, SSM_WIDTH), row), f_spec, f_spec),
        out_shape=(jax.ShapeDtypeStruct((rows * LB, SSM_WIDTH), F32), f_shape, f_shape),
        scratch_shapes=[
            pltpu.VMEM((tb, SSM_CH), F32), pltpu.VMEM((tb, SSM_CH), F32),
            pltpu.VMEM((1, SSM_CH), F32), pltpu.VMEM((1, SSM_CH), F32),
            pltpu.VMEM((SSM_WIDTH // LANES, tb * LB, LANES), F32),
        ],
        compiler_params=_cparams(sem),
        name="ssm_chained" if chained else "ssm_pairs",
    )(u_blk, s0r, s0i, p["w_x"], p["w_t"], p["w_c"], p["ssm_d"], p["w_glu"], p["ab_re"], p["ab_im"],
      p["lv_re"], p["lv_im"], p["cp_re"], p["cp_im"])


def _ssm_params(a_re, a_im, log_dt, b_re, b_im, c_re, c_im):
    dt = jnp.exp(log_dt)[:, None]
    mag = jnp.exp(a_re * dt)
    abr = mag * jnp.cos(a_im * dt)
    abi = mag * jnp.sin(a_im * dt)
    den = a_re * a_re + a_im * a_im
    nr = abr - 1.0
    ni = abi
    coef_re = (nr * a_re + ni * a_im) / den
    coef_im = (ni * a_re - nr * a_im) / den
    bb = (coef_re[..., None] * b_re - coef_im[..., None] * b_im,
          coef_re[..., None] * b_im + coef_im[..., None] * b_re)
    cc = (c_re, c_im)

    apow = [(jnp.ones_like(abr), jnp.zeros_like(abi))]
    for _ in range(LB):
        apow.append(_cmul(apow[-1], (abr, abi)))
    hp = lax.Precision.HIGHEST

    def group_diagonal(vals, col_group_width):
        n_col = vals.shape[-1]
        col_group = (jnp.arange(n_col) // col_group_width) % OCT
        keep = (col_group[None, :] == jnp.arange(OCT)[:, None]).astype(F32)
        out = vals[:, :, None, :, :] * keep[None, None, :, None, :]
        return out.reshape(N_OCT, LB * OCT * SSM_GROUP, n_col).astype(BF16)

    def octets_last(v):
        lead = v.shape[1:-3]
        v = v.reshape((LB,) + lead + (N_OCT, OCT, SSM_GROUP, v.shape[-1]))
        nl = len(lead)
        perm = (1 + nl, 0, 3 + nl) + tuple(range(1, 1 + nl)) + (2 + nl, 4 + nl)
        return v.transpose(perm).reshape(N_OCT, LB, SSM_GROUP, -1)

    bb_t = (bb[0].transpose(0, 2, 1), bb[1].transpose(0, 2, 1))
    mx = [jnp.stack(_cmul((apow[LB - 1 - t][0][:, None, :], apow[LB - 1 - t][1][:, None, :]), bb_t))
          for t in range(LB)]
    w_x = group_diagonal(octets_last(jnp.stack(mx)), SSM_STATE)

    mc = []
    for t in range(LB):
        m = _cmul(cc, (apow[t + 1][0][:, None, :], apow[t + 1][1][:, None, :]))
        mc.append(jnp.stack([m[0], -m[1]]))
    w_c = jnp.swapaxes(group_diagonal(octets_last(jnp.stack(mc)), SSM_STATE), 1, 2)

    kd = []
    for d in range(LB):
        m = _cmul(cc, (apow[d][0][:, None, :], apow[d][1][:, None, :]))
        kd.append(jnp.einsum("ghp,gpk->gkh", m[0], bb[0], precision=hp)
                  - jnp.einsum("ghp,gpk->gkh", m[1], bb[1], precision=hp))
    zero = jnp.zeros_like(kd[0])
    lagged = jnp.stack([jnp.stack([kd[t - t0] if t >= t0 else zero for t in range(LB)]) for t0 in range(LB)])
    w_t = group_diagonal(octets_last(lagged), SSM_GROUP)

    ab = (apow[LB][0].reshape(1, SSM_CH), apow[LB][1].reshape(1, SSM_CH))
    pows = [ab]
    for _ in range(SUBLANES - 1):
        pows.append(_cmul(pows[-1], ab))
    rows = jnp.arange(SUBLANES)[:, None]
    lv_re = jnp.stack([jnp.where(rows >= d, pows[d - 1][0], 0.0) for d in (1, 2, 4)])
    lv_im = jnp.stack([jnp.where(rows >= d, pows[d - 1][1], 0.0) for d in (1, 2, 4)])
    cp_re = jnp.concatenate([pw[0] for pw in pows], axis=0)
    cp_im = jnp.concatenate([pw[1] for pw in pows], axis=0)
    return dict(w_t=w_t.astype(BF16), w_x=w_x.astype(BF16), w_c=w_c.astype(BF16), ab_re=ab[0], ab_im=ab[1],
                lv_re=lv_re, lv_im=lv_im, cp_re=cp_re, cp_im=cp_im)


def _memkv_kernel(m_ref, g_ref, w_ref, gk_ref, o128_ref, k_ref, v_ref):
    hm = _rms_rows(m_ref[0], g_ref[...]).astype(BF16)
    kv = jnp.dot(hm, w_ref[...], preferred_element_type=F32)
    k = kv[:, :MEM_WIDTH]
    k_ref[0] = k * lax.rsqrt(_seg_mean_sq(k, o128_ref[...], MEM_HD) + EPS) * gk_ref[...]
    v_ref[0] = kv[:, MEM_WIDTH:]


def _memkv(mem, p):
    n = mem.shape[0]
    blk = lambda i: (i, 0, 0)
    shp = jax.ShapeDtypeStruct((n, N_MEM, MEM_WIDTH), F32)
    return pl.pallas_call(
        _memkv_kernel,
        grid=(n,),
        in_specs=[pl.BlockSpec((1, N_MEM, D_MODEL), blk), _full((1, D_MODEL)),
                  _full((D_MODEL, 2 * MEM_WIDTH)), _full((1, MEM_WIDTH)), _full((MEM_WIDTH, MEM_WIDTH))],
        out_specs=(pl.BlockSpec((1, N_MEM, MEM_WIDTH), blk), pl.BlockSpec((1, N_MEM, MEM_WIDTH), blk)),
        out_shape=(shp, shp),
        compiler_params=_cparams(("parallel",)),
        name="memkv",
    )(mem, p["g_mem"], p["w_mem_kv"], p["g_km"], p["ones128"])


def _dup_heads(x, lane):
    sw = pltpu.roll(x, SWA_HD, axis=x.ndim - 1)
    lo = lane < SWA_HD
    return jnp.where(lo, x, sw), jnp.where(lo, sw, x)


def _swa_group(q_blk, kk, vv, g, mask, sink_ref):
    tq = q_blk.shape[-2]
    shp = q_blk.shape[:-1]
    lane = lax.broadcasted_iota(jnp.int32, shp + (LANES,), len(shp))
    rows = []
    sinks = []
    for hl in range(SWA_Q_PER_KV):
        h = g * SWA_Q_PER_KV + hl
        pair = q_blk[..., (h // 2) * LANES:(h // 2 + 1) * LANES]
        keep = (lane < SWA_HD) if h % 2 == 0 else (lane >= SWA_HD)
        rows.append(jnp.where(keep, pair, 0.0))
        sinks.append(jnp.full(shp + (1,), sink_ref[h], F32))
    qq = jnp.concatenate(rows, axis=-2).astype(BF16)
    sk = jnp.concatenate(sinks, axis=-2)
    s = jnp.einsum("...qd,...kd->...qk", qq, kk, preferred_element_type=F32)
    s = jnp.where(mask, s, NEG_INF)
    m = jnp.maximum(jnp.max(s, axis=-1, keepdims=True), sk)
    e = jnp.exp(s - m)
    pr = e / (jnp.sum(e, axis=-1, keepdims=True) + jnp.exp(sk - m))
    o = jnp.einsum("...qk,...kd->...qd", pr.astype(BF16), vv, preferred_element_type=F32)
    lo = lane < SWA_HD
    return [jnp.where(lo, o[..., (2 * j) * tq:(2 * j + 1) * tq, :], o[..., (2 * j + 1) * tq:(2 * j + 2) * tq, :])
            for j in range(2)]


def _mem_heads(qm, k_head, v_head):
    outs = []
    for h in range(MEM_HEADS):
        cs = slice(h * MEM_HD, (h + 1) * MEM_HD)
        s = jnp.einsum("...qd,...kd->...qk", qm[..., cs], k_head(h).astype(BF16),
                       preferred_element_type=F32) * MEM_SCALE
        m = jnp.max(s, axis=-1, keepdims=True)
        e = jnp.exp(s - m)
        pr = e / jnp.sum(e, axis=-1, keepdims=True)
        outs.append(jnp.einsum("...qk,...kd->...qd", pr.astype(BF16), v_head(h).astype(BF16),
                               preferred_element_type=F32))
    return jnp.concatenate(outs, axis=-1)


def _attn_prompt_kernel(sink_ref, q_ref, k_ref, v_ref, kp_ref, vp_ref, qm_ref, mk_ref, mv_ref, ys_ref, ym_ref):
    tq = q_ref.shape[0]
    blk = WINDOW
    rows = SWA_Q_PER_KV * blk
    i = lax.broadcasted_iota(jnp.int32, (rows, 2 * blk), 0) % blk
    j = lax.broadcasted_iota(jnp.int32, (rows, 2 * blk), 1)
    lo = jnp.where(j < blk, i + 1, blk)
    hi = jnp.where(j < blk, blk, blk + i + 1)
    first_lo = jnp.where(pl.program_id(1) == 0, blk, 0)
    lane_k = lax.broadcasted_iota(jnp.int32, (2 * blk, LANES), 1)
    for b in range(tq // blk):
        rs = slice(b * blk, (b + 1) * blk)
        if b == 0:
            k2 = jnp.concatenate([kp_ref[...], k_ref[rs, :]], axis=0)
            v2 = jnp.concatenate([vp_ref[...], v_ref[rs, :]], axis=0)
            mask = (j >= jnp.maximum(lo, first_lo)) & (j < hi)
        else:
            k2 = k_ref[(b - 1) * blk:(b + 1) * blk, :]
            v2 = v_ref[(b - 1) * blk:(b + 1) * blk, :]
            mask = (j >= lo) & (j < hi)
        kks = _dup_heads(k2, lane_k)
        vvs = _dup_heads(v2, lane_k)
        q_blk = q_ref[rs, :].astype(F32)
        pairs = []
        for g in range(SWA_KV_HEADS):
            pairs += _swa_group(q_blk, kks[g].astype(BF16), vvs[g].astype(BF16), g, mask, sink_ref)
        ys_ref[rs, :] = jnp.concatenate(pairs, axis=1).astype(BF16)
    ym_ref[...] = _mem_heads(qm_ref[...], lambda h: mk_ref[0, :, h * MEM_HD:(h + 1) * MEM_HD],
                             lambda h: mv_ref[0, :, h * MEM_HD:(h + 1) * MEM_HD]).astype(BF16)


def _attn_prompt(q, k, v, qm, mk, mv, sinks, n_seq):
    t = q.shape[0]
    tq = TQ_ATT
    per = t // n_seq // tq
    sub = tq // WINDOW
    row = lambda n, c: (n * per + c, 0)
    prev = lambda n, c: (jnp.maximum((n * per + c) * sub - 1, 0), 0)
    memb = lambda n, c: (n, 0, 0)
    return pl.pallas_call(
        _attn_prompt_kernel,
        grid=(n_seq, per),
        in_specs=[
            pl.BlockSpec(memory_space=pltpu.SMEM),
            pl.BlockSpec((tq, SWA_WIDTH), row),
            pl.BlockSpec((tq, SWA_KV_WIDTH), row),
            pl.BlockSpec((tq, SWA_KV_WIDTH), row),
            pl.BlockSpec((WINDOW, SWA_KV_WIDTH), prev),
            pl.BlockSpec((WINDOW, SWA_KV_WIDTH), prev),
            pl.BlockSpec((tq, MEM_WIDTH), row),
            pl.BlockSpec((1, N_MEM, MEM_WIDTH), memb),
            pl.BlockSpec((1, N_MEM, MEM_WIDTH), memb),
        ],
        out_specs=(pl.BlockSpec((tq, SWA_WIDTH), row), pl.BlockSpec((tq, MEM_WIDTH), row)),
        out_shape=(jax.ShapeDtypeStruct((t, SWA_WIDTH), BF16), jax.ShapeDtypeStruct((t, MEM_WIDTH), BF16)),
        compiler_params=_cparams(("parallel", "parallel")),
        name="attn_prompt",
    )(sinks, q, k, v, k, v, qm, mk, mv)


def _attn_sample_kernel(sink_ref, q_ref, k_ref, v_ref, pk_ref, pv_ref, qm_ref, mk_ref, mv_ref,
                        ys_ref, ym_ref, nk_ref, nv_ref, *, s_len):
    sb, wb = pk_ref.shape[0], pk_ref.shape[1]
    n_keys = wb + s_len
    rows = SWA_Q_PER_KV * s_len
    i = lax.broadcasted_iota(jnp.int32, (sb, rows, n_keys), 1) % s_len
    j = lax.broadcasted_iota(jnp.int32, (sb, rows, n_keys), 2)
    rel = i + wb - j
    mask = (rel >= 0) & (rel < WINDOW)
    k_all = jnp.concatenate([pk_ref[...], k_ref[...].reshape(sb, s_len, SWA_KV_WIDTH)], axis=1)
    v_all = jnp.concatenate([pv_ref[...], v_ref[...].reshape(sb, s_len, SWA_KV_WIDTH)], axis=1)
    nk_ref[...] = k_all[:, n_keys - wb:, :]
    nv_ref[...] = v_all[:, n_keys - wb:, :]
    lane_k = lax.broadcasted_iota(jnp.int32, k_all.shape, 2)
    kks = _dup_heads(k_all, lane_k)
    vvs = _dup_heads(v_all, lane_k)
    q3 = q_ref[...].astype(F32).reshape(sb, s_len, SWA_WIDTH)
    pairs = []
    for g in range(SWA_KV_HEADS):
        pairs += _swa_group(q3, kks[g].astype(BF16), vvs[g].astype(BF16), g, mask, sink_ref)
    ys_ref[...] = jnp.concatenate(pairs, axis=-1).reshape(sb * s_len, SWA_WIDTH).astype(BF16)
    qm3 = qm_ref[...].astype(F32).reshape(sb, s_len, MEM_WIDTH).astype(BF16)
    head_rows = lambda h: pl.ds(h, N_MEM, stride=MEM_HEADS)
    ym = _mem_heads(qm3, lambda h: mk_ref[:, head_rows(h), :], lambda h: mv_ref[:, head_rows(h), :])
    ym_ref[...] = ym.reshape(sb * s_len, MEM_WIDTH).astype(BF16)


def _attn_sample(q, k, v, past_k, past_v, qm, mk, mv, sinks, s_len):
    t = q.shape[0]
    n_seq, wb = past_k.shape[0], past_k.shape[1]
    sb = SEQ_BLK
    rows = sb * s_len
    row = lambda c: (c, 0)
    seq = lambda c: (c, 0, 0)
    cache_shape = jax.ShapeDtypeStruct((n_seq, wb, SWA_KV_WIDTH), F32)
    return pl.pallas_call(
        functools.partial(_attn_sample_kernel, s_len=s_len),
        grid=(n_seq // sb,),
        in_specs=[
            pl.BlockSpec(memory_space=pltpu.SMEM),
            pl.BlockSpec((rows, SWA_WIDTH), row),
            pl.BlockSpec((rows, SWA_KV_WIDTH), row),
            pl.BlockSpec((rows, SWA_KV_WIDTH), row),
            pl.BlockSpec((sb, wb, SWA_KV_WIDTH), seq),
            pl.BlockSpec((sb, wb, SWA_KV_WIDTH), seq),
            pl.BlockSpec((rows, MEM_WIDTH), row),
            pl.BlockSpec((sb, N_MEM * MEM_HEADS, MEM_HD), seq),
            pl.BlockSpec((sb, N_MEM * MEM_HEADS, MEM_HD), seq),
        ],
        out_specs=(pl.BlockSpec((rows, SWA_WIDTH), row), pl.BlockSpec((rows, MEM_WIDTH), row),
                   pl.BlockSpec((sb, wb, SWA_KV_WIDTH), seq), pl.BlockSpec((sb, wb, SWA_KV_WIDTH), seq)),
        out_shape=(jax.ShapeDtypeStruct((t, SWA_WIDTH), BF16), jax.ShapeDtypeStruct((t, MEM_WIDTH), BF16),
                   cache_shape, cache_shape),
        compiler_params=_cparams(("parallel",)),
        name="attn_sample",
    )(sinks, q, k, v, past_k, past_v, qm, mk, mv)


def _first_argmax(x, valid, lane):
    xm = jnp.where(valid, x, -jnp.inf)
    mx = jnp.max(xm, axis=-1, keepdims=True)
    idx = jnp.min(jnp.where(xm == mx, lane, LANES), axis=-1, keepdims=True)
    return mx, lane == idx, idx


def _merge_kernel(xp_ref, yap_ref, ybp_ref, ycp_ref, xs_ref, yas_ref, ybs_ref, ycs_ref, *rest, n_blk_p):
    @pl.when(pl.program_id(0) < n_blk_p)
    def _():
        _merge_rows(xp_ref, yap_ref, ybp_ref, ycp_ref, *rest)

    @pl.when(pl.program_id(0) >= n_blk_p)
    def _():
        _merge_rows(xs_ref, yas_ref, ybs_ref, ycs_ref, *rest)


def _merge_rows(x_ref, ya_ref, yb_ref, yc_ref, gmix_ref, wg_ref, wa_ref, wb_ref, wc_ref, wo_ref, gffn_ref,
                wr_hi_ref, wr_lo_ref, br_ref, x1_ref, hx_ref, route_ref, cnt_ref, carry_ref):
    x = x_ref[...]
    h = _rms_rows(x, gmix_ref[...]).astype(BF16)
    def branch(k, y_ref, w_ref):
        gate = jax.nn.sigmoid(jnp.dot(h, wg_ref[:, k * D_MODEL:(k + 1) * D_MODEL], preferred_element_type=F32))
        return gate * jnp.dot(y_ref[...].astype(BF16), w_ref[...], preferred_element_type=F32)

    merged = branch(0, ya_ref, wa_ref) + branch(1, yb_ref, wb_ref) + branch(2, yc_ref, wc_ref)
    x1 = x + jnp.dot(merged.astype(BF16), wo_ref[...], preferred_element_type=F32)
    x1_ref[...] = x1
    hn = _rms_rows(x1, gffn_ref[...])
    tm = x.shape[0]
    slab_row = lambda c: pl.ds(c, tm, stride=HX_ROWS)
    for c in range(ROW_TILES):
        hx_ref[slab_row(c), :] = hn[:, c * LANES:(c + 1) * LANES]

    hi, lo = _split_bf16(hn)
    w_hi = wr_hi_ref[...]
    logits = (jnp.dot(hi, w_hi, preferred_element_type=F32) + jnp.dot(lo, w_hi, preferred_element_type=F32)
              + jnp.dot(hi, wr_lo_ref[...], preferred_element_type=F32)) + br_ref[...]
    lane = lax.broadcasted_iota(jnp.int32, logits.shape, 1)
    is_grp = lane < N_GROUPS_E
    g_max, _, g_idx = _first_argmax(logits, is_grp, lane)
    pg_top = 1.0 / jnp.sum(jnp.where(is_grp, jnp.exp(logits - g_max), 0.0), axis=-1, keepdims=True)
    e_lo = N_GROUPS_E + g_idx * EXPERTS_PER_GROUP
    in_grp = (lane >= e_lo) & (lane < e_lo + EXPERTS_PER_GROUP)
    e_max, first, i1 = _first_argmax(logits, in_grp, lane)
    ex = jnp.where(in_grp, jnp.exp(logits - e_max), 0.0)
    pe = ex / jnp.sum(ex, axis=-1, keepdims=True)
    _, second, i2 = _first_argmax(logits, in_grp & jnp.logical_not(first), lane)
    p1 = jnp.sum(jnp.where(first, pe, 0.0), axis=-1, keepdims=True)
    p2 = jnp.sum(jnp.where(second, pe, 0.0), axis=-1, keepdims=True)
    w1 = pg_top * p1 / (p1 + p2)
    w2 = pg_top * p2 / (p1 + p2)

    a1 = i1 - e_lo
    a2 = i2 - e_lo
    e_a = jnp.minimum(a1, a2)
    e_b = jnp.maximum(a1, a2)
    pair = jnp.right_shift(e_a * (2 * EXPERTS_PER_GROUP - 1 - e_a), 1) + (e_b - e_a - 1)
    bucket = g_idx * PAIRS_PER_GROUP + pair
    w_a = jnp.where(a1 < a2, w1, w2)
    w_b = jnp.where(a1 < a2, w2, w1)

    @pl.when(pl.program_id(0) == 0)
    def _():
        carry_ref[...] = jnp.zeros_like(carry_ref)

    onehot = lane == bucket
    tri = (lax.broadcasted_iota(jnp.int32, (tm, tm), 1) <= lax.broadcasted_iota(jnp.int32, (tm, tm), 0))
    csum = jnp.dot(jnp.where(tri, 1.0, 0.0).astype(BF16), jnp.where(onehot, 1.0, 0.0).astype(BF16),
                   preferred_element_type=F32)
    carry = carry_ref[...]
    rank = jnp.sum(jnp.where(onehot, csum + carry, 0.0), axis=-1, keepdims=True) - 1.0
    carry = carry + csum[tm - 1:tm, :]
    carry_ref[...] = carry
    cnt_ref[...] = carry
    route = jnp.where(lane == 0, bucket.astype(F32),
                      jnp.where(lane == 1, w_a, jnp.where(lane == 2, w_b, jnp.where(lane == 3, rank, 0.0))))
    route_ref[...] = route
    hx_ref[slab_row(ROW_TILES), :] = route
    for c in range(ROW_TILES + 1, HX_ROWS):
        hx_ref[slab_row(c), :] = jnp.zeros_like(route)


def _merge(prompt_rows, sample_rows, p):
    tm = TM_MRG
    t_p, t_s = prompt_rows[0].shape[0], sample_rows[0].shape[0]
    nbp = t_p // tm
    t_all = t_p + t_s
    first = lambda i: (jnp.minimum(i, nbp - 1), 0)
    second = lambda i: (jnp.maximum(i - nbp, 0), 0)
    row = lambda i: (i, 0)
    widths = (D_MODEL, SSM_WIDTH, SWA_WIDTH, MEM_WIDTH)
    in_specs = ([pl.BlockSpec((tm, w), first) for w in widths] + [pl.BlockSpec((tm, w), second) for w in widths] + [
        _resident((1, D_MODEL)),
        _resident((D_MODEL, N_BRANCH * D_MODEL)),
        _resident((SSM_WIDTH, D_MODEL)), _resident((SWA_WIDTH, D_MODEL)), _resident((MEM_WIDTH, D_MODEL)),
        _resident((D_MODEL, D_MODEL)),
        _resident((1, D_MODEL)),
        _resident((D_MODEL, LANES)), _resident((D_MODEL, LANES)), _resident((1, LANES)),
    ])
    return pl.pallas_call(
        functools.partial(_merge_kernel, n_blk_p=nbp),
        grid=(t_all // tm,),
        in_specs=in_specs,
        out_specs=(pl.BlockSpec((tm, D_MODEL), row), pl.BlockSpec((tm * HX_ROWS, LANES), row),
                   pl.BlockSpec((tm, ROUTE_W), row), _full((1, LANES))),
        out_shape=(jax.ShapeDtypeStruct((t_all, D_MODEL), F32),
                   jax.ShapeDtypeStruct((t_all * HX_ROWS, LANES), F32),
                   jax.ShapeDtypeStruct((t_all, ROUTE_W), F32), jax.ShapeDtypeStruct((1, LANES), F32)),
        scratch_shapes=[pltpu.VMEM((1, LANES), F32)],
        compiler_params=_cparams(("arbitrary",)),
        name="merge",
    )(*prompt_rows, *sample_rows, p["g_mix"], p["w_gates"], p["w_br_ssm"], p["w_br_swa"], p["w_br_mem"], p["w_o"],
      p["g_ffn"], p["w_r_hi"], p["w_r_lo"], p["b_r"])


def _rows_to_lanes(col):
    out = []
    for k in range(col.shape[0] // LANES):
        blk = jnp.broadcast_to(col[k * LANES:(k + 1) * LANES], (LANES, LANES))
        out.append(blk.T[0:1, :])
    return jnp.concatenate(out, axis=0)


def _pos_kernel(route_ref, off_ref, coff_ref, pos_ref, cpos_ref):
    r = route_ref[...]
    lane = lax.broadcasted_iota(jnp.int32, r.shape, 1)
    mine = lane == r[:, 0:1].astype(jnp.int32)
    off = jnp.sum(jnp.where(mine, off_ref[...], 0.0), axis=-1, keepdims=True)
    coff = jnp.sum(jnp.where(mine, coff_ref[...], 0.0), axis=-1, keepdims=True)
    rank = r[:, 3:4]
    pos_ref[...] = _rows_to_lanes(off + rank).astype(jnp.int32)
    cpos_ref[...] = _rows_to_lanes(coff + rank).astype(jnp.int32)


def _sorted_pos(route, off, coff):
    t = route.shape[0]
    tm = SUBLANES * LANES
    shp = jax.ShapeDtypeStruct((t // LANES, LANES), jnp.int32)
    pos, cpos = pl.pallas_call(
        _pos_kernel,
        grid=(t // tm,),
        in_specs=[pl.BlockSpec((tm, ROUTE_W), lambda i: (i, 0)), _full((1, LANES)), _full((1, LANES))],
        out_specs=(pl.BlockSpec((SUBLANES, LANES), lambda i: (i, 0)), pl.BlockSpec((SUBLANES, LANES), lambda i: (i, 0))),
        out_shape=(shp, shp),
        compiler_params=_cparams(("parallel",)),
        name="sorted_pos",
    )(route, off, coff)
    return pos.reshape(t), cpos.reshape(t)


def _inv_kernel(pos_ref, idx_ref):
    def body(t, _):
        idx_ref[pos_ref[t]] = t
        return 0

    lax.fori_loop(0, pos_ref.shape[0], body, 0, unroll=8)


def _invert(pos):
    return pl.pallas_call(
        _inv_kernel,
        in_specs=[pl.BlockSpec(memory_space=pltpu.SMEM)],
        out_specs=pl.BlockSpec(memory_space=pltpu.SMEM),
        out_shape=jax.ShapeDtypeStruct(pos.shape, jnp.int32),
        name="invert_perm",
    )(pos)


def _bucket_kernel(idx_ref, tg_ref, ta_ref, tb_ref, cb_ref, nr_ref, hx_hbm, wi_ref, wd_ref,
                   ys_ref, buf, sem):
    j = pl.program_id(0)
    n_real = nr_ref[0]

    last = idx_ref.shape[0] - 1

    def issue_row(tile_base, slot, r, dst_row, prio):
        src = idx_ref[jnp.minimum(tile_base + r, last)]
        pltpu.make_async_copy(hx_hbm.at[pl.ds(pl.multiple_of(src * HX_ROWS, HX_ROWS), HX_ROWS), :],
                              buf.at[slot, pl.ds(dst_row, HX_ROWS), :], sem.at[slot]).start(priority=prio)

    def wait_tile(slot):
        pltpu.make_async_copy(hx_hbm.at[pl.ds(0, TM_EXP * HX_ROWS), :], buf.at[slot], sem.at[slot]).wait()

    @pl.when(j == 0)
    def _():
        base = cb_ref[0]

        def body(r8, _):
            for k in range(SUBLANES):
                r = r8 * SUBLANES + k
                issue_row(base, 0, r, pl.multiple_of(r * HX_ROWS, HX_ROWS), k % 2)
            return 0

        lax.fori_loop(0, TM_EXP // SUBLANES, body, 0)

    @pl.when(j < n_real)
    def _():
        slot = j % 2
        wait_tile(slot)
        nxt = jnp.minimum(j + 1, n_real - 1)
        base = cb_ref[nxt]
        for r in range(TM_EXP):
            issue_row(base, 1 - slot, r, r * HX_ROWS, r % 2)

        slab_row = lambda c: buf[slot, pl.ds(c, TM_EXP, stride=HX_ROWS), :]
        x = jnp.concatenate([slab_row(c) for c in range(ROW_TILES)], axis=1).astype(BF16)
        route = slab_row(ROW_TILES)

        def ffn(e):
            gu = jnp.dot(x, wi_ref[0, e], preferred_element_type=F32)
            a = jax.nn.silu(gu[:, :D_FF]) * gu[:, D_FF:]
            return jnp.dot(a.astype(BF16), wd_ref[0, e], preferred_element_type=F32)

        y = route[:, 1:2] * ffn(ta_ref[j]) + route[:, 2:3] * ffn(tb_ref[j])
        for c in range(ROW_TILES):
            ys_ref[pl.ds(c, TM_EXP, stride=ROW_TILES), :] = y[:, c * LANES:(c + 1) * LANES]

        @pl.when(j == n_real - 1)
        def _():
            wait_tile(1 - slot)


def _bucket_ffn(hx, idx, tile_g, tile_a, tile_b, tile_cb, n_real, p):
    n_tiles = tile_a.shape[0]
    grp = lambda j, idx, tg, ta, tb, cb, nr: (tg[j], 0, 0, 0)
    out = lambda j, idx, tg, ta, tb, cb, nr: (jnp.minimum(j, nr[0] - 1), 0)
    epg = EXPERTS_PER_GROUP
    return pl.pallas_call(
        _bucket_kernel,
        grid_spec=pltpu.PrefetchScalarGridSpec(
            num_scalar_prefetch=6,
            grid=(n_tiles,),
            in_specs=[
                pl.BlockSpec(memory_space=pl.ANY),
                pl.BlockSpec((1, epg, D_MODEL, 2 * D_FF), grp), pl.BlockSpec((1, epg, D_FF, D_MODEL), grp),
            ],
            out_specs=pl.BlockSpec((TM_EXP * ROW_TILES, LANES), out),
            scratch_shapes=[pltpu.VMEM((2, TM_EXP * HX_ROWS, LANES), F32), pltpu.SemaphoreType.DMA((2,))],
        ),
        out_shape=jax.ShapeDtypeStruct((n_tiles * TM_EXP * ROW_TILES, LANES), F32),
        compiler_params=_cparams(("arbitrary",)),
        name="bucket_ffn",
    )(idx, tile_g, tile_a, tile_b, tile_cb, n_real, hx,
      p["w_exp_in"].reshape(N_GROUPS_E, epg, D_MODEL, 2 * D_FF),
      p["w_exp_down"].reshape(N_GROUPS_E, epg, D_FF, D_MODEL))


def _back_kernel(pos_ref, x1_ref, ys_hbm, o_ref, buf, sem, *, t0):
    i = pl.program_id(0)
    tm = o_ref.shape[0]

    def issue(tile, slot):
        base = t0 + tile * tm

        def body(r8, _):
            for k in range(SUBLANES):
                r = r8 * SUBLANES + k
                src = pl.multiple_of(pos_ref[base + r] * ROW_TILES, ROW_TILES)
                pltpu.make_async_copy(ys_hbm.at[pl.ds(src, ROW_TILES), :],
                                      buf.at[slot, pl.ds(pl.multiple_of(r * ROW_TILES, ROW_TILES), ROW_TILES), :],
                                      sem.at[slot]).start(priority=k % 2)
            return 0

        lax.fori_loop(0, tm // SUBLANES, body, 0)

    @pl.when(i == 0)
    def _():
        issue(0, 0)

    @pl.when(i + 1 < pl.num_programs(0))
    def _():
        issue(i + 1, (i + 1) % 2)

    slot = i % 2
    pltpu.make_async_copy(ys_hbm.at[pl.ds(0, tm * ROW_TILES), :], buf.at[slot], sem.at[slot]).wait()
    y = jnp.concatenate([buf[slot, pl.ds(c, tm, stride=ROW_TILES), :] for c in range(ROW_TILES)], axis=1)
    o_ref[...] = x1_ref[...] + y


def _unsort_add(x1, ys, pos, t0, t):
    tm = TM_BACK
    row = lambda i, pos: (i, 0)
    return pl.pallas_call(
        functools.partial(_back_kernel, t0=t0),
        grid_spec=pltpu.PrefetchScalarGridSpec(
            num_scalar_prefetch=1,
            grid=(t // tm,),
            in_specs=[pl.BlockSpec((tm, D_MODEL), lambda i, pos: (i + t0 // tm, 0)),
                      pl.BlockSpec(memory_space=pl.ANY)],
            out_specs=pl.BlockSpec((tm, D_MODEL), row),
            scratch_shapes=[pltpu.VMEM((2, tm * ROW_TILES, LANES), F32), pltpu.SemaphoreType.DMA((2,))],
        ),
        out_shape=jax.ShapeDtypeStruct((t, D_MODEL), F32),
        compiler_params=_cparams(("arbitrary",)),
        name="unsort_add",
    )(pos, x1, ys)


def _bucket_experts():
    lo, hi = [], []
    for g in range(N_GROUPS_E):
        for a in range(EXPERTS_PER_GROUP):
            for b in range(a + 1, EXPERTS_PER_GROUP):
                lo.append(g * EXPERTS_PER_GROUP + a)
                hi.append(g * EXPERTS_PER_GROUP + b)
    return jnp.asarray(lo, jnp.int32), jnp.asarray(hi, jnp.int32)


def _tile_tables(counts, n_tiles):
    cnt = counts[0, :N_BUCKETS].astype(jnp.int32)
    nt = (cnt + TM_EXP - 1) // TM_EXP
    tend = jnp.cumsum(nt)
    tstart = tend - nt
    cstart = jnp.cumsum(cnt) - cnt
    pad = lambda v: jnp.zeros((1, LANES), F32).at[0, :N_BUCKETS].set(v.astype(F32))
    j = jnp.arange(n_tiles, dtype=jnp.int32)
    b = jnp.minimum(jnp.sum((tend[None, :] <= j[:, None]).astype(jnp.int32), axis=1), N_BUCKETS - 1)
    in_bucket = (j - tstart[b]) * TM_EXP
    e_lo, e_hi = _bucket_experts()
    epg = EXPERTS_PER_GROUP
    return (pad(tstart * TM_EXP), pad(cstart), b // PAIRS_PER_GROUP, e_lo[b] % epg, e_hi[b] % epg,
            cstart[b] + in_bucket, tend[-1:])


def _rope_tables(pos):
    half = SWA_HD // 2
    inv = ROPE_THETA ** (-jnp.arange(half, dtype=F32) / half)
    ang = pos.astype(F32)[:, None] * inv[None, :]
    cos = jnp.cos(ang)
    sin = jnp.sin(ang)
    cos = jnp.concatenate([cos, cos, cos, cos], axis=1)
    sin = jnp.concatenate([-sin, sin, -sin, sin], axis=1)
    return cos, sin


def kernel(x_prompt, x_sample, mem_prompt, state_ssm_re, state_ssm_im, cache_swa_k, cache_swa_v, cache_mem_k, cache_mem_v, norm_mix, w_in, ssm_a_re, ssm_a_im, ssm_log_dt, ssm_b_re, ssm_b_im, ssm_c_re, ssm_c_im, ssm_d, w_glu, swa_q_norm, swa_k_norm, swa_sinks, norm_mem, w_mem_kv, mem_q_norm, mem_k_norm, w_br_ssm, w_br_swa, w_br_mem, w_o, norm_ffn, w_router_group, b_router_group, w_router_expert, b_router_expert, w_exp_in, w_exp_down):
    depth = w_in.shape[0]
    assert depth == 1
    nb, seq, _ = x_prompt.shape
    db, dseq, _ = x_sample.shape
    assert dseq == 2 * LB
    l = 0

    w_r = jnp.concatenate([w_router_group[l], w_router_expert[l]], axis=1)
    w_r = jnp.pad(w_r, ((0, 0), (0, LANES - w_r.shape[1])))
    w_r_hi = w_r.astype(BF16)
    b_r = jnp.pad(jnp.concatenate([b_router_group[l], b_router_expert[l]]), (0, LANES - N_GROUPS_E - N_EXPERTS))
    p = dict(
        g_mix=norm_mix[l][None], w_a=w_in[l][:, :PROJ_A].astype(BF16), w_gates=w_in[l][:, PROJ_A:].astype(BF16),
        g_q=jnp.tile(swa_q_norm[l], SWA_HEADS)[None], g_k=jnp.tile(swa_k_norm[l], SWA_KV_HEADS)[None],
        g_qm=jnp.tile(mem_q_norm[l], MEM_HEADS)[None], g_km=jnp.tile(mem_k_norm[l], MEM_HEADS)[None],
        ones64=_block_ones(SWA_WIDTH, SWA_HD), ones128=_block_ones(MEM_WIDTH, MEM_HD),
        ssm_d=ssm_d[l][None], w_glu=w_glu[l].astype(BF16),
        g_mem=norm_mem[l][None], w_mem_kv=w_mem_kv[l].astype(BF16),
        w_br_ssm=w_br_ssm[l].astype(BF16), w_br_swa=w_br_swa[l].astype(BF16), w_br_mem=w_br_mem[l].astype(BF16),
        w_o=w_o[l].astype(BF16), g_ffn=norm_ffn[l][None],
        w_r_hi=w_r_hi, w_r_lo=(w_r - w_r_hi.astype(F32)).astype(BF16), b_r=b_r[None],
        w_exp_in=w_exp_in[l].astype(BF16), w_exp_down=w_exp_down[l].astype(BF16),
    )
    p.update(_ssm_params(ssm_a_re[l], ssm_a_im[l], ssm_log_dt[l], ssm_b_re[l], ssm_b_im[l],
                         ssm_c_re[l], ssm_c_im[l]))
    sinks = swa_sinks[l]

    xp = x_prompt.reshape(nb * seq, D_MODEL)
    cos_p, sin_p = _rope_tables(jnp.arange(seq, dtype=jnp.int32))
    u, q, k, v, qm = _inproj(xp, cos_p, sin_p, seq // TM_IN, p)
    zeros_state = jnp.zeros((nb, 1, SSM_CH), F32)
    y_ssm, pr, pi = _ssm(u, zeros_state, zeros_state, p, n_seq=nb, chained=True)
    mk, mv = _memkv(mem_prompt, p)
    y_swa, y_mem = _attn_prompt(q, k, v, qm, mk, mv, sinks, nb)
    win = min(WINDOW, seq)
    last_win = lambda a: a.reshape(nb, seq, SWA_KV_WIDTH)[:, seq - win:].reshape(nb, win, SWA_KV_HEADS, SWA_HD)
    p_k, p_v = last_win(k), last_win(v)

    xs = x_sample.reshape(db * dseq, D_MODEL)
    cos_s, sin_s = _rope_tables(PAST_LEN + jnp.arange(dseq, dtype=jnp.int32))
    reps = TM_IN // dseq
    us, qs, ks, vs, qms = _inproj(xs, jnp.tile(cos_s, (reps, 1)), jnp.tile(sin_s, (reps, 1)), 1, p)
    two_rows = lambda st: jnp.repeat(st.reshape(db, SSM_CH), 2, axis=0)
    ys_ssm, sr, si = _ssm(us, two_rows(state_ssm_re), two_rows(state_ssm_im), p, n_seq=db, chained=False)
    sr, si = sr[1::2], si[1::2]
    wb = cache_swa_k.shape[2]
    ys_swa, ys_mem, s_k, s_v = _attn_sample(
        qs, ks, vs, cache_swa_k[l].reshape(db, wb, SWA_KV_WIDTH), cache_swa_v[l].reshape(db, wb, SWA_KV_WIDTH),
        qms, cache_mem_k.reshape(db, N_MEM * MEM_HEADS, MEM_HD), cache_mem_v.reshape(db, N_MEM * MEM_HEADS, MEM_HD),
        sinks, dseq)
    t_p, t_s = nb * seq, db * dseq
    t_all = t_p + t_s
    x1, hx, route, counts = _merge((xp, y_ssm, y_swa, y_mem), (xs, ys_ssm, ys_swa, ys_mem), p)

    n_tiles = pl.cdiv(t_all, TM_EXP) + N_BUCKETS
    off, coff, tile_g, tile_a, tile_b, tile_cb, n_real = _tile_tables(counts, n_tiles)
    pos, cpos = _sorted_pos(route, off, coff)
    idx = _invert(cpos)
    y_sorted = _bucket_ffn(hx, idx, tile_g, tile_a, tile_b, tile_cb, n_real, p)
    yp = _unsort_add(x1, y_sorted, pos, 0, t_p).reshape(nb, seq, D_MODEL)
    ys = _unsort_add(x1, y_sorted, pos, t_p, t_s).reshape(db, dseq, D_MODEL)

    g, s = SSM_GROUPS, SSM_STATE
    return (yp, ys,
            pr.reshape(1, nb, g, s), pi.reshape(1, nb, g, s),
            p_k[None], p_v[None],
            mk.reshape(1, nb, N_MEM, MEM_HEADS, MEM_HD), mv.reshape(1, nb, N_MEM, MEM_HEADS, MEM_HD),
            sr.reshape(1, db, g, s), si.reshape(1, db, g, s),
            s_k.reshape(1, db, wb, SWA_KV_HEADS, SWA_HD), s_v.reshape(1, db, wb, SWA_KV_HEADS, SWA_HD))
```

```python
import functools
import math

import jax
import jax.numpy as jnp
from jax import lax
from jax.experimental import pallas as pl
from jax.experimental.pallas import tpu as pltpu

F32 = jnp.float32
BF16 = jnp.bfloat16

D_MODEL = 1024
SSM_WIDTH = 512
SSM_GROUP = 16
SSM_GROUPS = 32
SSM_STATE = 64
SSM_CH = SSM_GROUPS * SSM_STATE
LB = 4
OCT = 8
N_OCT = SSM_GROUPS // OCT
OCT_CH = OCT * SSM_STATE
SWA_HEADS = 8
SWA_KV_HEADS = 2
SWA_Q_PER_KV = SWA_HEADS // SWA_KV_HEADS
SWA_HD = 64
SWA_WIDTH = SWA_HEADS * SWA_HD
SWA_KV_WIDTH = SWA_KV_HEADS * SWA_HD
WINDOW = 128
PAST_LEN = 16384
ROPE_THETA = 10000.0
N_MEM = 256
MEM_HEADS = 4
MEM_HD = 128
MEM_WIDTH = MEM_HEADS * MEM_HD
N_BRANCH = 3
PROJ_A = SSM_WIDTH + SWA_WIDTH + 2 * SWA_KV_WIDTH + MEM_WIDTH
N_GROUPS_E = 4
EXPERTS_PER_GROUP = 8
N_EXPERTS = 32
D_FF = 256
EPS = 1e-6
NEG_INF = -1e30
SWA_SCALE = SWA_HD ** -0.5
MEM_SCALE = MEM_HD ** -0.5

LANES = 128
SUBLANES = 8
VMEM_LIMIT = 56 * 1024 * 1024

TM_IN = 512
TB_SSM = 512
SCAN_W = 512
TQ_ATT = 512
SEQ_BLK = 8
TM_MRG = 512
TM_BACK = 512
TM_EXP = 192

ROUTE_W = LANES
ROW_TILES = D_MODEL // LANES
HX_ROWS = 2 * ROW_TILES
PAIRS_PER_GROUP = EXPERTS_PER_GROUP * (EXPERTS_PER_GROUP - 1) // 2
N_BUCKETS = N_GROUPS_E * PAIRS_PER_GROUP


def _cparams(sem):
    return pltpu.CompilerParams(dimension_semantics=sem, vmem_limit_bytes=VMEM_LIMIT)


def _full(shape):
    nd = len(shape)
    return pl.BlockSpec(shape, lambda *_: (0,) * nd)


def _resident(shape):
    nd = len(shape)
    return pl.BlockSpec(shape, lambda *_: (0,) * nd, pipeline_mode=pl.Buffered(1))


def _split_bf16(x):
    hi = x.astype(BF16)
    lo = (x - hi.astype(F32)).astype(BF16)
    return hi, lo


def _seg_mean_sq(x, ones_blk, width):
    hi, lo = _split_bf16(x * x)
    s = jnp.dot(hi, ones_blk, preferred_element_type=F32) + jnp.dot(lo, ones_blk, preferred_element_type=F32)
    return s * (1.0 / width)


def _rms_rows(x, gain):
    return x * lax.rsqrt(jnp.mean(x * x, axis=-1, keepdims=True) + EPS) * gain


def _block_ones(n, width):
    i = jnp.arange(n) // width
    return (i[:, None] == i[None, :]).astype(BF16)


def _rope_cols(x, cos, sin_signed, lane_in_head):
    n = x.shape[1]
    reps = n // LANES
    if reps > 1:
        cos = jnp.concatenate([cos] * reps, axis=1)
        sin_signed = jnp.concatenate([sin_signed] * reps, axis=1)
    half = SWA_HD // 2
    partner = jnp.where(lane_in_head < half, pltpu.roll(x, n - half, axis=1), pltpu.roll(x, half, axis=1))
    return x * cos + partner * sin_signed


def _inproj_kernel(x_ref, gmix_ref, w_ref, gq_ref, gk_ref, gm_ref, cos_ref, sin_ref, o64_ref, o128_ref,
                   u_ref, q_ref, k_ref, v_ref, qm_ref, u_scr):
    x = x_ref[...]
    h = _rms_rows(x, gmix_ref[...]).astype(BF16)
    proj = jnp.dot(h, w_ref[...], preferred_element_type=F32)
    c0 = SSM_WIDTH
    c1 = c0 + SWA_WIDTH
    c2 = c1 + SWA_KV_WIDTH
    c3 = c2 + SWA_KV_WIDTH
    n_blk = u_ref.shape[0]
    for c in range(c0 // LANES):
        u_scr[c] = proj[:, c * LANES:(c + 1) * LANES]
        for t in range(LB):
            u_ref[:, t * c0 + c * LANES:t * c0 + (c + 1) * LANES] = u_scr[c, pl.ds(t, n_blk, stride=LB), :]
    q = proj[:, c0:c1]
    k = proj[:, c1:c2]
    v_ref[...] = proj[:, c2:c3]
    qm = proj[:, c3:]
    cos = cos_ref[...]
    sin = sin_ref[...]
    o64 = o64_ref[...]
    lane_q = lax.broadcasted_iota(jnp.int32, q.shape, 1) % SWA_HD
    qn = q * lax.rsqrt(_seg_mean_sq(q, o64, SWA_HD) + EPS) * gq_ref[...]
    q_ref[...] = (_rope_cols(qn, cos, sin, lane_q) * SWA_SCALE).astype(BF16)
    lane_k = lax.broadcasted_iota(jnp.int32, k.shape, 1) % SWA_HD
    kn = k * lax.rsqrt(_seg_mean_sq(k, o64[:SWA_KV_WIDTH, :SWA_KV_WIDTH], SWA_HD) + EPS) * gk_ref[...]
    k_ref[...] = _rope_cols(kn, cos, sin, lane_k)
    qmn = qm * lax.rsqrt(_seg_mean_sq(qm, o128_ref[...], MEM_HD) + EPS) * gm_ref[...]
    qm_ref[...] = qmn.astype(BF16)


def _inproj(x2d, cos, sin, pos_blocks, p):
    t = x2d.shape[0]
    tm = TM_IN
    grid = (t // tm,)
    row = lambda i: (i, 0)
    tab = lambda i: (i % pos_blocks, 0)
    out_shape = (
        jax.ShapeDtypeStruct((t // LB, LB * SSM_WIDTH), F32),
        jax.ShapeDtypeStruct((t, SWA_WIDTH), BF16),
        jax.ShapeDtypeStruct((t, SWA_KV_WIDTH), F32),
        jax.ShapeDtypeStruct((t, SWA_KV_WIDTH), F32),
        jax.ShapeDtypeStruct((t, MEM_WIDTH), BF16),
    )
    return pl.pallas_call(
        _inproj_kernel,
        grid=grid,
        in_specs=[
            pl.BlockSpec((tm, D_MODEL), row),
            _full((1, D_MODEL)),
            _resident((D_MODEL, PROJ_A)),
            _full((1, SWA_WIDTH)),
            _full((1, SWA_KV_WIDTH)),
            _full((1, MEM_WIDTH)),
            pl.BlockSpec((tm, LANES), tab),
            pl.BlockSpec((tm, LANES), tab),
            _full((SWA_WIDTH, SWA_WIDTH)),
            _full((MEM_WIDTH, MEM_WIDTH)),
        ],
        out_specs=(
            pl.BlockSpec((tm // LB, LB * SSM_WIDTH), row),
            pl.BlockSpec((tm, SWA_WIDTH), row),
            pl.BlockSpec((tm, SWA_KV_WIDTH), row),
            pl.BlockSpec((tm, SWA_KV_WIDTH), row),
            pl.BlockSpec((tm, MEM_WIDTH), row),
        ),
        out_shape=out_shape,
        scratch_shapes=[pltpu.VMEM((SSM_WIDTH // LANES, tm, LANES), F32)],
        compiler_params=_cparams(("parallel",)),
        name="inproj",
    )(x2d, p["g_mix"], p["w_a"], p["g_q"], p["g_k"], p["g_qm"], cos, sin, p["ones64"], p["ones128"])


def _cmul(x, y):
    return x[0] * y[0] - x[1] * y[1], x[0] * y[1] + x[1] * y[0]


def _ssm_kernel(u_ref, s0r_ref, s0i_ref, wx_ref, wt_ref, wc_ref, d_ref, wglu_ref, abr_ref, abi_ref,
                lvr_ref, lvi_ref, cpr_ref, cpi_ref,
                y_ref, fr_ref, fi_ref, sr_ref, si_ref, car_ref, cai_ref, o_scr, *, chained):
    tb = u_ref.shape[0]
    u = u_ref[...]
    ub = u.astype(BF16)
    lhs = [jnp.concatenate([ub[:, t * SSM_WIDTH + c * LANES:t * SSM_WIDTH + (c + 1) * LANES] for t in range(LB)],
                           axis=1) for c in range(N_OCT)]
    for c in range(N_OCT):
        x = jnp.dot(lhs[c], wx_ref[c], preferred_element_type=F32)
        cs = slice(c * OCT_CH, (c + 1) * OCT_CH)
        sr_ref[:, cs] = x[:, :OCT_CH]
        si_ref[:, cs] = x[:, OCT_CH:]

    if chained:
        @pl.when(pl.program_id(1) == 0)
        def _():
            car_ref[...] = s0r_ref[0]
            cai_ref[...] = s0i_ref[0]

        first_row = lax.broadcasted_iota(jnp.int32, (SUBLANES, SCAN_W), 0) == 0
        for sl in range(SSM_CH // SCAN_W):
            cols = slice(sl * SCAN_W, (sl + 1) * SCAN_W)
            lv = [(lvr_ref[j, :, cols], lvi_ref[j, :, cols]) for j in range(3)]
            cpr = cpr_ref[:, cols]
            cpi = cpi_ref[:, cols]

            def tile(i, carry, cols=cols, lv=lv, cpr=cpr, cpi=cpi):
                r0 = pl.multiple_of(i * SUBLANES, SUBLANES)
                xr = sr_ref[pl.ds(r0, SUBLANES), cols]
                xi = si_ref[pl.ds(r0, SUBLANES), cols]
                for j, d in enumerate((1, 2, 4)):
                    pr, pi = lv[j]
                    shr = pltpu.roll(xr, d, axis=0)
                    shi = pltpu.roll(xi, d, axis=0)
                    xr, xi = xr + pr * shr - pi * shi, xi + pr * shi + pi * shr
                cb_r = jnp.broadcast_to(carry[0], xr.shape)
                cb_i = jnp.broadcast_to(carry[1], xr.shape)
                xr, xi = xr + cpr * cb_r - cpi * cb_i, xi + cpr * cb_i + cpi * cb_r
                sr_ref[pl.ds(r0, SUBLANES), cols] = jnp.where(first_row, cb_r, pltpu.roll(xr, 1, axis=0))
                si_ref[pl.ds(r0, SUBLANES), cols] = jnp.where(first_row, cb_i, pltpu.roll(xi, 1, axis=0))
                return xr[SUBLANES - 1:SUBLANES, :], xi[SUBLANES - 1:SUBLANES, :]

            c_r, c_i = lax.fori_loop(0, tb // SUBLANES, tile, (car_ref[:, cols], cai_ref[:, cols]))
            car_ref[:, cols] = c_r
            cai_ref[:, cols] = c_i
        fr_ref[0] = car_ref[...]
        fi_ref[0] = cai_ref[...]
    else:
        odd = lax.broadcasted_iota(jnp.int32, (tb, SSM_CH), 0) % 2 == 1
        ab = (abr_ref[...], abi_ref[...])
        s0 = (s0r_ref[...], s0i_ref[...])
        x = (sr_ref[...], si_ref[...])
        e_first = _cmul(ab, s0)
        e_first = (e_first[0] + x[0], e_first[1] + x[1])
        prev = (pltpu.roll(e_first[0], 1, axis=0), pltpu.roll(e_first[1], 1, axis=0))
        e_second = _cmul(ab, prev)
        fr_ref[...] = jnp.where(odd, e_second[0] + x[0], e_first[0])
        fi_ref[...] = jnp.where(odd, e_second[1] + x[1], e_first[1])
        sr_ref[...] = jnp.where(odd, prev[0], s0[0])
        si_ref[...] = jnp.where(odd, prev[1], s0[1])

    ys = [[None] * N_OCT for _ in range(LB)]
    for c in range(N_OCT):
        cs = slice(c * OCT_CH, (c + 1) * OCT_CH)
        s_in = jnp.concatenate([sr_ref[:, cs], si_ref[:, cs]], axis=1).astype(BF16)
        yc = (jnp.dot(lhs[c], wt_ref[c], preferred_element_type=F32)
              + jnp.dot(s_in, wc_ref[c], preferred_element_type=F32))
        for t in range(LB):
            ys[t][c] = yc[:, t * LANES:(t + 1) * LANES]
    y = jnp.concatenate([jnp.concatenate(ys[t], axis=1) for t in range(LB)], axis=0)
    us = jnp.concatenate([u[:, t * SSM_WIDTH:(t + 1) * SSM_WIDTH] for t in range(LB)], axis=0)
    y = jax.nn.gelu(y + d_ref[...] * us)
    gate = jax.nn.sigmoid(jnp.dot(y.astype(BF16), wglu_ref[...], preferred_element_type=F32))
    out = y * gate
    for c in range(SSM_WIDTH // LANES):
        for t in range(LB):
            o_scr[c, pl.ds(t, tb, stride=LB), :] = out[t * tb:(t + 1) * tb, c * LANES:(c + 1) * LANES]
        y_ref[:, c * LANES:(c + 1) * LANES] = o_scr[c]


def _ssm(u_blk, s0r, s0i, p, *, n_seq, chained):
    rows = u_blk.shape[0]
    if chained:
        tb = TB_SSM
        per = rows // n_seq // tb
        grid = (n_seq, per)
        row = lambda n, c: (n * per + c, 0)
        st = lambda n, c: (n, 0, 0)
        s0_spec = pl.BlockSpec((1, 1, SSM_CH), st)
        f_spec = pl.BlockSpec((1, 1, SSM_CH), st)
        f_shape = jax.ShapeDtypeStruct((n_seq, 1, SSM_CH), F32)
        sem = ("parallel", "arbitrary")
    else:
        tb = min(TB_SSM, rows)
        grid = (rows // tb,)
        row = lambda c: (c, 0)
        s0_spec = pl.BlockSpec((tb, SSM_CH), row)
        f_spec = pl.BlockSpec((tb, SSM_CH), row)
        f_shape = jax.ShapeDtypeStruct((rows, SSM_CH), F32)
        sem = ("parallel",)
    blk_w = LB * SSM_WIDTH
    return pl.pallas_call(
        functools.partial(_ssm_kernel, chained=chained),
        grid=grid,
        in_specs=[
            pl.BlockSpec((tb, blk_w), row),
            s0_spec, s0_spec,
            _resident((N_OCT, LB * LANES, 2 * OCT_CH)), _resident((N_OCT, LB * LANES, LB * LANES)),
            _resident((N_OCT, 2 * OCT_CH, LB * LANES)),
            _full((1, SSM_WIDTH)),
            _resident((SSM_WIDTH, SSM_WIDTH)),
            _full((1, SSM_CH)), _full((1, SSM_CH)),
            _full((3, SUBLANES, SSM_CH)), _full((3, SUBLANES, SSM_CH)),
            _full((SUBLANES, SSM_CH)), _full((SUBLANES, SSM_CH)),
        ],
        out_specs=(pl.BlockSpec((tb * LB, SSM_WIDTH), row), f_spec, f_spec),
        out_shape=(jax.ShapeDtypeStruct((rows * LB, SSM_WIDTH), F32), f_shape, f_shape),
        scratch_shapes=[
            pltpu.VMEM((tb, SSM_CH), F32), pltpu.VMEM((tb, SSM_CH), F32),
            pltpu.VMEM((1, SSM_CH), F32), pltpu.VMEM((1, SSM_CH), F32),
            pltpu.VMEM((SSM_WIDTH // LANES, tb * LB, LANES), F32),
        ],
        compiler_params=_cparams(sem),
        name="ssm_chained" if chained else "ssm_pairs",
    )(u_blk, s0r, s0i, p["w_x"], p["w_t"], p["w_c"], p["ssm_d"], p["w_glu"], p["ab_re"], p["ab_im"],
      p["lv_re"], p["lv_im"], p["cp_re"], p["cp_im"])


def _ssm_params(a_re, a_im, log_dt, b_re, b_im, c_re, c_im):
    dt = jnp.exp(log_dt)[:, None]
    mag = jnp.exp(a_re * dt)
    abr = mag * jnp.cos(a_im * dt)
    abi = mag * jnp.sin(a_im * dt)
    den = a_re * a_re + a_im * a_im
    nr = abr - 1.0
    ni = abi
    coef_re = (nr * a_re + ni * a_im) / den
    coef_im = (ni * a_re - nr * a_im) / den
    bb = (coef_re[..., None] * b_re - coef_im[..., None] * b_im,
          coef_re[..., None] * b_im + coef_im[..., None] * b_re)
    cc = (c_re, c_im)

    apow = [(jnp.ones_like(abr), jnp.zeros_like(abi))]
    for _ in range(LB):
        apow.append(_cmul(apow[-1], (abr, abi)))
    hp = lax.Precision.HIGHEST

    def group_diagonal(vals, col_group_width):
        n_col = vals.shape[-1]
        col_group = (jnp.arange(n_col) // col_group_width) % OCT
        keep = (col_group[None, :] == jnp.arange(OCT)[:, None]).astype(F32)
        out = vals[:, :, None, :, :] * keep[None, None, :, None, :]
        return out.reshape(N_OCT, LB * OCT * SSM_GROUP, n_col).astype(BF16)

    def octets_last(v):
        lead = v.shape[1:-3]
        v = v.reshape((LB,) + lead + (N_OCT, OCT, SSM_GROUP, v.shape[-1]))
        nl = len(lead)
        perm = (1 + nl, 0, 3 + nl) + tuple(range(1, 1 + nl)) + (2 + nl, 4 + nl)
        return v.transpose(perm).reshape(N_OCT, LB, SSM_GROUP, -1)

    bb_t = (bb[0].transpose(0, 2, 1), bb[1].transpose(0, 2, 1))
    mx = [jnp.stack(_cmul((apow[LB - 1 - t][0][:, None, :], apow[LB - 1 - t][1][:, None, :]), bb_t))
          for t in range(LB)]
    w_x = group_diagonal(octets_last(jnp.stack(mx)), SSM_STATE)

    mc = []
    for t in range(LB):
        m = _cmul(cc, (apow[t + 1][0][:, None, :], apow[t + 1][1][:, None, :]))
        mc.append(jnp.stack([m[0], -m[1]]))
    w_c = jnp.swapaxes(group_diagonal(octets_last(jnp.stack(mc)), SSM_STATE), 1, 2)

    kd = []
    for d in range(LB):
        m = _cmul(cc, (apow[d][0][:, None, :], apow[d][1][:, None, :]))
        kd.append(jnp.einsum("ghp,gpk->gkh", m[0], bb[0], precision=hp)
                  - jnp.einsum("ghp,gpk->gkh", m[1], bb[1], precision=hp))
    zero = jnp.zeros_like(kd[0])
    lagged = jnp.stack([jnp.stack([kd[t - t0] if t >= t0 else zero for t in range(LB)]) for t0 in range(LB)])
    w_t = group_diagonal(octets_last(lagged), SSM_GROUP)

    ab = (apow[LB][0].reshape(1, SSM_CH), apow[LB][1].reshape(1, SSM_CH))
    pows = [ab]
    for _ in range(SUBLANES - 1):
        pows.append(_cmul(pows[-1], ab))
    rows = jnp.arange(SUBLANES)[:, None]
    lv_re = jnp.stack([jnp.where(rows >= d, pows[d - 1][0], 0.0) for d in (1, 2, 4)])
    lv_im = jnp.stack([jnp.where(rows >= d, pows[d - 1][1], 0.0) for d in (1, 2, 4)])
    cp_re = jnp.concatenate([pw[0] for pw in pows], axis=0)
    cp_im = jnp.concatenate([pw[1] for pw in pows], axis=0)
    return dict(w_t=w_t.astype(BF16), w_x=w_x.astype(BF16), w_c=w_c.astype(BF16), ab_re=ab[0], ab_im=ab[1],
                lv_re=lv_re, lv_im=lv_im, cp_re=cp_re, cp_im=cp_im)


def _memkv_kernel(m_ref, g_ref, w_ref, gk_ref, o128_ref, k_ref, v_ref):
    hm = _rms_rows(m_ref[0], g_ref[...]).astype(BF16)
    kv = jnp.dot(hm, w_ref[...], preferred_element_type=F32)
    k = kv[:, :MEM_WIDTH]
    k_ref[0] = k * lax.rsqrt(_seg_mean_sq(k, o128_ref[...], MEM_HD) + EPS) * gk_ref[...]
    v_ref[0] = kv[:, MEM_WIDTH:]


def _memkv(mem, p):
    n = mem.shape[0]
    blk = lambda i: (i, 0, 0)
    shp = jax.ShapeDtypeStruct((n, N_MEM, MEM_WIDTH), F32)
    return pl.pallas_call(
        _memkv_kernel,
        grid=(n,),
        in_specs=[pl.BlockSpec((1, N_MEM, D_MODEL), blk), _full((1, D_MODEL)),
                  _full((D_MODEL, 2 * MEM_WIDTH)), _full((1, MEM_WIDTH)), _full((MEM_WIDTH, MEM_WIDTH))],
        out_specs=(pl.BlockSpec((1, N_MEM, MEM_WIDTH), blk), pl.BlockSpec((1, N_MEM, MEM_WIDTH), blk)),
        out_shape=(shp, shp),
        compiler_params=_cparams(("parallel",)),
        name="memkv",
    )(mem, p["g_mem"], p["w_mem_kv"], p["g_km"], p["ones128"])


def _dup_heads(x, lane):
    sw = pltpu.roll(x, SWA_HD, axis=x.ndim - 1)
    lo = lane < SWA_HD
    return jnp.where(lo, x, sw), jnp.where(lo, sw, x)


def _swa_group(q_blk, kk, vv, g, mask, sink_ref):
    tq = q_blk.shape[-2]
    shp = q_blk.shape[:-1]
    lane = lax.broadcasted_iota(jnp.int32, shp + (LANES,), len(shp))
    rows = []
    sinks = []
    for hl in range(SWA_Q_PER_KV):
        h = g * SWA_Q_PER_KV + hl
        pair = q_blk[..., (h // 2) * LANES:(h // 2 + 1) * LANES]
        keep = (lane < SWA_HD) if h % 2 == 0 else (lane >= SWA_HD)
        rows.append(jnp.where(keep, pair, 0.0))
        sinks.append(jnp.full(shp + (1,), sink_ref[h], F32))
    qq = jnp.concatenate(rows, axis=-2).astype(BF16)
    sk = jnp.concatenate(sinks, axis=-2)
    s = jnp.einsum("...qd,...kd->...qk", qq, kk, preferred_element_type=F32)
    s = jnp.where(mask, s, NEG_INF)
    m = jnp.maximum(jnp.max(s, axis=-1, keepdims=True), sk)
    e = jnp.exp(s - m)
    pr = e / (jnp.sum(e, axis=-1, keepdims=True) + jnp.exp(sk - m))
    o = jnp.einsum("...qk,...kd->...qd", pr.astype(BF16), vv, preferred_element_type=F32)
    lo = lane < SWA_HD
    return [jnp.where(lo, o[..., (2 * j) * tq:(2 * j + 1) * tq, :], o[..., (2 * j + 1) * tq:(2 * j + 2) * tq, :])
            for j in range(2)]


def _mem_heads(qm, k_head, v_head):
    outs = []
    for h in range(MEM_HEADS):
        cs = slice(h * MEM_HD, (h + 1) * MEM_HD)
        s = jnp.einsum("...qd,...kd->...qk", qm[..., cs], k_head(h).astype(BF16),
                       preferred_element_type=F32) * MEM_SCALE
        m = jnp.max(s, axis=-1, keepdims=True)
        e = jnp.exp(s - m)
        pr = e / jnp.sum(e, axis=-1, keepdims=True)
        outs.append(jnp.einsum("...qk,...kd->...qd", pr.astype(BF16), v_head(h).astype(BF16),
                               preferred_element_type=F32))
    return jnp.concatenate(outs, axis=-1)


def _attn_prompt_kernel(sink_ref, q_ref, k_ref, v_ref, kp_ref, vp_ref, qm_ref, mk_ref, mv_ref, ys_ref, ym_ref):
    tq = q_ref.shape[0]
    blk = WINDOW
    rows = SWA_Q_PER_KV * blk
    i = lax.broadcasted_iota(jnp.int32, (rows, 2 * blk), 0) % blk
    j = lax.broadcasted_iota(jnp.int32, (rows, 2 * blk), 1)
    lo = jnp.where(j < blk, i + 1, blk)
    hi = jnp.where(j < blk, blk, blk + i + 1)
    first_lo = jnp.where(pl.program_id(1) == 0, blk, 0)
    lane_k = lax.broadcasted_iota(jnp.int32, (2 * blk, LANES), 1)
    for b in range(tq // blk):
        rs = slice(b * blk, (b + 1) * blk)
        if b == 0:
            k2 = jnp.concatenate([kp_ref[...], k_ref[rs, :]], axis=0)
            v2 = jnp.concatenate([vp_ref[...], v_ref[rs, :]], axis=0)
            mask = (j >= jnp.maximum(lo, first_lo)) & (j < hi)
        else:
            k2 = k_ref[(b - 1) * blk:(b + 1) * blk, :]
            v2 = v_ref[(b - 1) * blk:(b + 1) * blk, :]
            mask = (j >= lo) & (j < hi)
        kks = _dup_heads(k2, lane_k)
        vvs = _dup_heads(v2, lane_k)
        q_blk = q_ref[rs, :].astype(F32)
        pairs = []
        for g in range(SWA_KV_HEADS):
            pairs += _swa_group(q_blk, kks[g].astype(BF16), vvs[g].astype(BF16), g, mask, sink_ref)
        ys_ref[rs, :] = jnp.concatenate(pairs, axis=1).astype(BF16)
    ym_ref[...] = _mem_heads(qm_ref[...], lambda h: mk_ref[0, :, h * MEM_HD:(h + 1) * MEM_HD],
                             lambda h: mv_ref[0, :, h * MEM_HD:(h + 1) * MEM_HD]).astype(BF16)


def _attn_prompt(q, k, v, qm, mk, mv, sinks, n_seq):
    t = q.shape[0]
    tq = TQ_ATT
    per = t // n_seq // tq
    sub = tq // WINDOW
    row = lambda n, c: (n * per + c, 0)
    prev = lambda n, c: (jnp.maximum((n * per + c) * sub - 1, 0), 0)
    memb = lambda n, c: (n, 0, 0)
    return pl.pallas_call(
        _attn_prompt_kernel,
        grid=(n_seq, per),
        in_specs=[
            pl.BlockSpec(memory_space=pltpu.SMEM),
            pl.BlockSpec((tq, SWA_WIDTH), row),
            pl.BlockSpec((tq, SWA_KV_WIDTH), row),
            pl.BlockSpec((tq, SWA_KV_WIDTH), row),
            pl.BlockSpec((WINDOW, SWA_KV_WIDTH), prev),
            pl.BlockSpec((WINDOW, SWA_KV_WIDTH), prev),
            pl.BlockSpec((tq, MEM_WIDTH), row),
            pl.BlockSpec((1, N_MEM, MEM_WIDTH), memb),
            pl.BlockSpec((1, N_MEM, MEM_WIDTH), memb),
        ],
        out_specs=(pl.BlockSpec((tq, SWA_WIDTH), row), pl.BlockSpec((tq, MEM_WIDTH), row)),
        out_shape=(jax.ShapeDtypeStruct((t, SWA_WIDTH), BF16), jax.ShapeDtypeStruct((t, MEM_WIDTH), BF16)),
        compiler_params=_cparams(("parallel", "parallel")),
        name="attn_prompt",
    )(sinks, q, k, v, k, v, qm, mk, mv)


def _attn_sample_kernel(sink_ref, q_ref, k_ref, v_ref, pk_ref, pv_ref, qm_ref, mk_ref, mv_ref,
                        ys_ref, ym_ref, nk_ref, nv_ref, *, s_len):
    sb, wb = pk_ref.shape[0], pk_ref.shape[1]
    n_keys = wb + s_len
    rows = SWA_Q_PER_KV * s_len
    i = lax.broadcasted_iota(jnp.int32, (sb, rows, n_keys), 1) % s_len
    j = lax.broadcasted_iota(jnp.int32, (sb, rows, n_keys), 2)
    rel = i + wb - j
    mask = (rel >= 0) & (rel < WINDOW)
    k_all = jnp.concatenate([pk_ref[...], k_ref[...].reshape(sb, s_len, SWA_KV_WIDTH)], axis=1)
    v_all = jnp.concatenate([pv_ref[...], v_ref[...].reshape(sb, s_len, SWA_KV_WIDTH)], axis=1)
    nk_ref[...] = k_all[:, n_keys - wb:, :]
    nv_ref[...] = v_all[:, n_keys - wb:, :]
    lane_k = lax.broadcasted_iota(jnp.int32, k_all.shape, 2)
    kks = _dup_heads(k_all, lane_k)
    vvs = _dup_heads(v_all, lane_k)
    q3 = q_ref[...].astype(F32).reshape(sb, s_len, SWA_WIDTH)
    pairs = []
    for g in range(SWA_KV_HEADS):
        pairs += _swa_group(q3, kks[g].astype(BF16), vvs[g].astype(BF16), g, mask, sink_ref)
    ys_ref[...] = jnp.concatenate(pairs, axis=-1).reshape(sb * s_len, SWA_WIDTH).astype(BF16)
    qm3 = qm_ref[...].astype(F32).reshape(sb, s_len, MEM_WIDTH).astype(BF16)
    head_rows = lambda h: pl.ds(h, N_MEM, stride=MEM_HEADS)
    ym = _mem_heads(qm3, lambda h: mk_ref[:, head_rows(h), :], lambda h: mv_ref[:, head_rows(h), :])
    ym_ref[...] = ym.reshape(sb * s_len, MEM_WIDTH).astype(BF16)


def _attn_sample(q, k, v, past_k, past_v, qm, mk, mv, sinks, s_len):
    t = q.shape[0]
    n_seq, wb = past_k.shape[0], past_k.shape[1]
    sb = SEQ_BLK
    rows = sb * s_len
    row = lambda c: (c, 0)
    seq = lambda c: (c, 0, 0)
    cache_shape = jax.ShapeDtypeStruct((n_seq, wb, SWA_KV_WIDTH), F32)
    return pl.pallas_call(
        functools.partial(_attn_sample_kernel, s_len=s_len),
        grid=(n_seq // sb,),
        in_specs=[
            pl.BlockSpec(memory_space=pltpu.SMEM),
            pl.BlockSpec((rows, SWA_WIDTH), row),
            pl.BlockSpec((rows, SWA_KV_WIDTH), row),
            pl.BlockSpec((rows, SWA_KV_WIDTH), row),
            pl.BlockSpec((sb, wb, SWA_KV_WIDTH), seq),
            pl.BlockSpec((sb, wb, SWA_KV_WIDTH), seq),
            pl.BlockSpec((rows, MEM_WIDTH), row),
            pl.BlockSpec((sb, N_MEM * MEM_HEADS, MEM_HD), seq),
            pl.BlockSpec((sb, N_MEM * MEM_HEADS, MEM_HD), seq),
        ],
        out_specs=(pl.BlockSpec((rows, SWA_WIDTH), row), pl.BlockSpec((rows, MEM_WIDTH), row),
                   pl.BlockSpec((sb, wb, SWA_KV_WIDTH), seq), pl.BlockSpec((sb, wb, SWA_KV_WIDTH), seq)),
        out_shape=(jax.ShapeDtypeStruct((t, SWA_WIDTH), BF16), jax.ShapeDtypeStruct((t, MEM_WIDTH), BF16),
                   cache_shape, cache_shape),
        compiler_params=_cparams(("parallel",)),
        name="attn_sample",
    )(sinks, q, k, v, past_k, past_v, qm, mk, mv)


def _first_argmax(x, valid, lane):
    xm = jnp.where(valid, x, -jnp.inf)
    mx = jnp.max(xm, axis=-1, keepdims=True)
    idx = jnp.min(jnp.where(xm == mx, lane, LANES), axis=-1, keepdims=True)
    return mx, lane == idx, idx


def _merge_kernel(xp_ref, yap_ref, ybp_ref, ycp_ref, xs_ref, yas_ref, ybs_ref, ycs_ref,
                  gmix_ref, wg_ref, wa_ref, wb_ref, wc_ref, wo_ref, gffn_ref, wr_hi_ref, wr_lo_ref, br_ref,
                  x1_ref, hx_ref, route_ref, cnt_ref, carry_ref, x1_prev, *, n_blk_p, n_blk):
    i = pl.program_id(0)

    @pl.when(i == 0)
    def _():
        x1_prev[...] = jnp.zeros_like(x1_prev)
        carry_ref[...] = jnp.zeros_like(carry_ref)

    x1_routed = x1_prev[...]

    is_prompt = jnp.minimum(i, n_blk - 1) < n_blk_p
    pick = lambda a_ref, b_ref: jnp.where(is_prompt, a_ref[...], b_ref[...])
    x = pick(xp_ref, xs_ref)
    h = _rms_rows(x, gmix_ref[...]).astype(BF16)

    def branch(k, y, w_ref):
        gate = jax.nn.sigmoid(jnp.dot(h, wg_ref[:, k * D_MODEL:(k + 1) * D_MODEL], preferred_element_type=F32))
        return gate * jnp.dot(y.astype(BF16), w_ref[...], preferred_element_type=F32)

    routing = _route_rows(x1_routed, i >= 1, gffn_ref, wr_hi_ref, wr_lo_ref, br_ref, hx_ref, route_ref, cnt_ref,
                          carry_ref)
    next(routing)
    merged = branch(0, pick(yap_ref, yas_ref), wa_ref)
    next(routing)
    merged = merged + branch(1, pick(ybp_ref, ybs_ref), wb_ref)
    next(routing)
    merged = merged + branch(2, pick(ycp_ref, ycs_ref), wc_ref)
    next(routing, None)
    x1 = x + jnp.dot(merged.astype(BF16), wo_ref[...], preferred_element_type=F32)
    x1_ref[...] = x1
    x1_prev[...] = x1


def _route_rows(x1, valid, gffn_ref, wr_hi_ref, wr_lo_ref, br_ref, hx_ref, route_ref, cnt_ref, carry_ref):
    hn = _rms_rows(x1, gffn_ref[...])
    tm = x1.shape[0]
    slab_row = lambda c: pl.ds(c, tm, stride=HX_ROWS)
    for c in range(ROW_TILES):
        hx_ref[slab_row(c), :] = hn[:, c * LANES:(c + 1) * LANES]
    yield

    hi, lo = _split_bf16(hn)
    w_hi = wr_hi_ref[...]
    logits = (jnp.dot(hi, w_hi, preferred_element_type=F32) + jnp.dot(lo, w_hi, preferred_element_type=F32)
              + jnp.dot(hi, wr_lo_ref[...], preferred_element_type=F32)) + br_ref[...]
    yield
    lane = lax.broadcasted_iota(jnp.int32, logits.shape, 1)
    is_grp = lane < N_GROUPS_E
    g_max, _, g_idx = _first_argmax(logits, is_grp, lane)
    pg_top = 1.0 / jnp.sum(jnp.where(is_grp, jnp.exp(logits - g_max), 0.0), axis=-1, keepdims=True)
    e_lo = N_GROUPS_E + g_idx * EXPERTS_PER_GROUP
    in_grp = (lane >= e_lo) & (lane < e_lo + EXPERTS_PER_GROUP)
    e_max, first, i1 = _first_argmax(logits, in_grp, lane)
    ex = jnp.where(in_grp, jnp.exp(logits - e_max), 0.0)
    pe = ex / jnp.sum(ex, axis=-1, keepdims=True)
    _, second, i2 = _first_argmax(logits, in_grp & jnp.logical_not(first), lane)
    p1 = jnp.sum(jnp.where(first, pe, 0.0), axis=-1, keepdims=True)
    p2 = jnp.sum(jnp.where(second, pe, 0.0), axis=-1, keepdims=True)
    w1 = pg_top * p1 / (p1 + p2)
    w2 = pg_top * p2 / (p1 + p2)

    a1 = i1 - e_lo
    a2 = i2 - e_lo
    e_a = jnp.minimum(a1, a2)
    e_b = jnp.maximum(a1, a2)
    pair = jnp.right_shift(e_a * (2 * EXPERTS_PER_GROUP - 1 - e_a), 1) + (e_b - e_a - 1)
    bucket = g_idx * PAIRS_PER_GROUP + pair
    w_a = jnp.where(a1 < a2, w1, w2)
    w_b = jnp.where(a1 < a2, w2, w1)
    yield

    onehot = lane == jnp.where(valid, bucket, -1)
    tri = (lax.broadcasted_iota(jnp.int32, (tm, tm), 1) <= lax.broadcasted_iota(jnp.int32, (tm, tm), 0))
    csum = jnp.dot(jnp.where(tri, 1.0, 0.0).astype(BF16), jnp.where(onehot, 1.0, 0.0).astype(BF16),
                   preferred_element_type=F32)
    carry = carry_ref[...]
    rank = jnp.sum(jnp.where(onehot, csum + carry, 0.0), axis=-1, keepdims=True) - 1.0
    carry = carry + csum[tm - 1:tm, :]
    carry_ref[...] = carry
    cnt_ref[...] = carry
    route = jnp.where(lane == 0, bucket.astype(F32),
                      jnp.where(lane == 1, w_a, jnp.where(lane == 2, w_b, jnp.where(lane == 3, rank, 0.0))))
    route_ref[...] = route
    hx_ref[slab_row(ROW_TILES), :] = route
    for c in range(ROW_TILES + 1, HX_ROWS):
        hx_ref[slab_row(c), :] = jnp.zeros_like(route)


def _merge(prompt_rows, sample_rows, p):
    tm = TM_MRG
    t_p, t_s = prompt_rows[0].shape[0], sample_rows[0].shape[0]
    nbp = t_p // tm
    t_all = t_p + t_s
    n_blk = t_all // tm
    first = lambda i: (jnp.minimum(i, nbp - 1), 0)
    second = lambda i: (jnp.clip(i - nbp, 0, n_blk - nbp - 1), 0)
    merged_blk = lambda i: (jnp.minimum(i, n_blk - 1), 0)
    routed_blk = lambda i: (jnp.maximum(i - 1, 0), 0)
    widths = (D_MODEL, SSM_WIDTH, SWA_WIDTH, MEM_WIDTH)
    in_specs = ([pl.BlockSpec((tm, w), first) for w in widths] + [pl.BlockSpec((tm, w), second) for w in widths] + [
        _resident((1, D_MODEL)),
        _resident((D_MODEL, N_BRANCH * D_MODEL)),
        _resident((SSM_WIDTH, D_MODEL)), _resident((SWA_WIDTH, D_MODEL)), _resident((MEM_WIDTH, D_MODEL)),
        _resident((D_MODEL, D_MODEL)),
        _resident((1, D_MODEL)),
        _resident((D_MODEL, LANES)), _resident((D_MODEL, LANES)), _resident((1, LANES)),
    ])
    return pl.pallas_call(
        functools.partial(_merge_kernel, n_blk_p=nbp, n_blk=n_blk),
        grid=(n_blk + 1,),
        in_specs=in_specs,
        out_specs=(pl.BlockSpec((tm, D_MODEL), merged_blk), pl.BlockSpec((tm * HX_ROWS, LANES), routed_blk),
                   pl.BlockSpec((tm, ROUTE_W), routed_blk), _full((1, LANES))),
        out_shape=(jax.ShapeDtypeStruct((t_all, D_MODEL), F32),
                   jax.ShapeDtypeStruct((t_all * HX_ROWS, LANES), F32),
                   jax.ShapeDtypeStruct((t_all, ROUTE_W), F32), jax.ShapeDtypeStruct((1, LANES), F32)),
        scratch_shapes=[pltpu.VMEM((1, LANES), F32), pltpu.VMEM((tm, D_MODEL), F32)],
        compiler_params=_cparams(("arbitrary",)),
        name="merge",
    )(*prompt_rows, *sample_rows, p["g_mix"], p["w_gates"], p["w_br_ssm"], p["w_br_swa"], p["w_br_mem"], p["w_o"],
      p["g_ffn"], p["w_r_hi"], p["w_r_lo"], p["b_r"])


def _rows_to_lanes(col):
    out = []
    for k in range(col.shape[0] // LANES):
        blk = jnp.broadcast_to(col[k * LANES:(k + 1) * LANES], (LANES, LANES))
        out.append(blk.T[0:1, :])
    return jnp.concatenate(out, axis=0)


def _pos_kernel(route_ref, off_ref, coff_ref, pos_ref, cpos_ref):
    r = route_ref[...]
    lane = lax.broadcasted_iota(jnp.int32, r.shape, 1)
    mine = lane == r[:, 0:1].astype(jnp.int32)
    off = jnp.sum(jnp.where(mine, off_ref[...], 0.0), axis=-1, keepdims=True)
    coff = jnp.sum(jnp.where(mine, coff_ref[...], 0.0), axis=-1, keepdims=True)
    rank = r[:, 3:4]
    pos_ref[...] = _rows_to_lanes(off + rank).astype(jnp.int32)
    cpos_ref[...] = _rows_to_lanes(coff + rank).astype(jnp.int32)


def _sorted_pos(route, off, coff):
    t = route.shape[0]
    tm = SUBLANES * LANES
    shp = jax.ShapeDtypeStruct((t // LANES, LANES), jnp.int32)
    pos, cpos = pl.pallas_call(
        _pos_kernel,
        grid=(t // tm,),
        in_specs=[pl.BlockSpec((tm, ROUTE_W), lambda i: (i, 0)), _full((1, LANES)), _full((1, LANES))],
        out_specs=(pl.BlockSpec((SUBLANES, LANES), lambda i: (i, 0)), pl.BlockSpec((SUBLANES, LANES), lambda i: (i, 0))),
        out_shape=(shp, shp),
        compiler_params=_cparams(("parallel",)),
        name="sorted_pos",
    )(route, off, coff)
    return pos.reshape(t), cpos.reshape(t)


def _inv_kernel(pos_ref, idx_ref):
    def body(t, _):
        idx_ref[pos_ref[t]] = t
        return 0

    lax.fori_loop(0, pos_ref.shape[0], body, 0, unroll=8)


def _invert(pos):
    return pl.pallas_call(
        _inv_kernel,
        in_specs=[pl.BlockSpec(memory_space=pltpu.SMEM)],
        out_specs=pl.BlockSpec(memory_space=pltpu.SMEM),
        out_shape=jax.ShapeDtypeStruct(pos.shape, jnp.int32),
        name="invert_perm",
    )(pos)


def _bucket_kernel(idx_ref, tg_ref, ta_ref, tb_ref, cb_ref, nr_ref, hx_hbm, wi_ref, wd_ref,
                   ys_ref, buf, sem):
    j = pl.program_id(0)
    n_real = nr_ref[0]

    last = idx_ref.shape[0] - 1

    def issue_row(tile_base, slot, r, dst_row, prio):
        src = idx_ref[jnp.minimum(tile_base + r, last)]
        pltpu.make_async_copy(hx_hbm.at[pl.ds(pl.multiple_of(src * HX_ROWS, HX_ROWS), HX_ROWS), :],
                              buf.at[slot, pl.ds(dst_row, HX_ROWS), :], sem.at[slot]).start(priority=prio)

    def wait_tile(slot):
        pltpu.make_async_copy(hx_hbm.at[pl.ds(0, TM_EXP * HX_ROWS), :], buf.at[slot], sem.at[slot]).wait()

    @pl.when(j == 0)
    def _():
        base = cb_ref[0]

        def body(r8, _):
            for k in range(SUBLANES):
                r = r8 * SUBLANES + k
                issue_row(base, 0, r, pl.multiple_of(r * HX_ROWS, HX_ROWS), k % 2)
            return 0

        lax.fori_loop(0, TM_EXP // SUBLANES, body, 0)

    @pl.when(j < n_real)
    def _():
        slot = j % 2
        wait_tile(slot)
        nxt = jnp.minimum(j + 1, n_real - 1)
        base = cb_ref[nxt]
        for r in range(TM_EXP):
            issue_row(base, 1 - slot, r, r * HX_ROWS, r % 2)

        slab_row = lambda c: buf[slot, pl.ds(c, TM_EXP, stride=HX_ROWS), :]
        x = jnp.concatenate([slab_row(c) for c in range(ROW_TILES)], axis=1).astype(BF16)
        route = slab_row(ROW_TILES)

        def ffn(e):
            gu = jnp.dot(x, wi_ref[0, e], preferred_element_type=F32)
            a = jax.nn.silu(gu[:, :D_FF]) * gu[:, D_FF:]
            return jnp.dot(a.astype(BF16), wd_ref[0, e], preferred_element_type=F32)

        y = route[:, 1:2] * ffn(ta_ref[j]) + route[:, 2:3] * ffn(tb_ref[j])
        for c in range(ROW_TILES):
            ys_ref[pl.ds(c, TM_EXP, stride=ROW_TILES), :] = y[:, c * LANES:(c + 1) * LANES]

        @pl.when(j == n_real - 1)
        def _():
            wait_tile(1 - slot)


def _bucket_ffn(hx, idx, tile_g, tile_a, tile_b, tile_cb, n_real, p):
    n_tiles = tile_a.shape[0]
    grp = lambda j, idx, tg, ta, tb, cb, nr: (tg[j], 0, 0, 0)
    out = lambda j, idx, tg, ta, tb, cb, nr: (jnp.minimum(j, nr[0] - 1), 0)
    epg = EXPERTS_PER_GROUP
    return pl.pallas_call(
        _bucket_kernel,
        grid_spec=pltpu.PrefetchScalarGridSpec(
            num_scalar_prefetch=6,
            grid=(n_tiles,),
            in_specs=[
                pl.BlockSpec(memory_space=pl.ANY),
                pl.BlockSpec((1, epg, D_MODEL, 2 * D_FF), grp), pl.BlockSpec((1, epg, D_FF, D_MODEL), grp),
            ],
            out_specs=pl.BlockSpec((TM_EXP * ROW_TILES, LANES), out),
            scratch_shapes=[pltpu.VMEM((2, TM_EXP * HX_ROWS, LANES), F32), pltpu.SemaphoreType.DMA((2,))],
        ),
        out_shape=jax.ShapeDtypeStruct((n_tiles * TM_EXP * ROW_TILES, LANES), F32),
        compiler_params=_cparams(("arbitrary",)),
        name="bucket_ffn",
    )(idx, tile_g, tile_a, tile_b, tile_cb, n_real, hx,
      p["w_exp_in"].reshape(N_GROUPS_E, epg, D_MODEL, 2 * D_FF),
      p["w_exp_down"].reshape(N_GROUPS_E, epg, D_FF, D_MODEL))


def _back_kernel(pos_ref, x1_ref, ys_hbm, o_ref, buf, sem, *, t0):
    i = pl.program_id(0)
    tm = o_ref.shape[0]

    def issue(tile, slot):
        base = t0 + tile * tm

        def body(r8, _):
            for k in range(SUBLANES):
                r = r8 * SUBLANES + k
                src = pl.multiple_of(pos_ref[base + r] * ROW_TILES, ROW_TILES)
                pltpu.make_async_copy(ys_hbm.at[pl.ds(src, ROW_TILES), :],
                                      buf.at[slot, pl.ds(pl.multiple_of(r * ROW_TILES, ROW_TILES), ROW_TILES), :],
                                      sem.at[slot]).start(priority=k % 2)
            return 0

        lax.fori_loop(0, tm // SUBLANES, body, 0)

    @pl.when(i == 0)
    def _():
        issue(0, 0)

    @pl.when(i + 1 < pl.num_programs(0))
    def _():
        issue(i + 1, (i + 1) % 2)

    slot = i % 2
    pltpu.make_async_copy(ys_hbm.at[pl.ds(0, tm * ROW_TILES), :], buf.at[slot], sem.at[slot]).wait()
    y = jnp.concatenate([buf[slot, pl.ds(c, tm, stride=ROW_TILES), :] for c in range(ROW_TILES)], axis=1)
    o_ref[...] = x1_ref[...] + y


def _unsort_add(x1, ys, pos, t0, t):
    tm = TM_BACK
    row = lambda i, pos: (i, 0)
    return pl.pallas_call(
        functools.partial(_back_kernel, t0=t0),
        grid_spec=pltpu.PrefetchScalarGridSpec(
            num_scalar_prefetch=1,
            grid=(t // tm,),
            in_specs=[pl.BlockSpec((tm, D_MODEL), lambda i, pos: (i + t0 // tm, 0)),
                      pl.BlockSpec(memory_space=pl.ANY)],
            out_specs=pl.BlockSpec((tm, D_MODEL), row),
            scratch_shapes=[pltpu.VMEM((2, tm * ROW_TILES, LANES), F32), pltpu.SemaphoreType.DMA((2,))],
        ),
        out_shape=jax.ShapeDtypeStruct((t, D_MODEL), F32),
        compiler_params=_cparams(("arbitrary",)),
        name="unsort_add",
    )(pos, x1, ys)


def _bucket_experts():
    lo, hi = [], []
    for g in range(N_GROUPS_E):
        for a in range(EXPERTS_PER_GROUP):
            for b in range(a + 1, EXPERTS_PER_GROUP):
                lo.append(g * EXPERTS_PER_GROUP + a)
                hi.append(g * EXPERTS_PER_GROUP + b)
    return jnp.asarray(lo, jnp.int32), jnp.asarray(hi, jnp.int32)


def _tile_tables(counts, n_tiles):
    cnt = counts[0, :N_BUCKETS].astype(jnp.int32)
    nt = (cnt + TM_EXP - 1) // TM_EXP
    tend = jnp.cumsum(nt)
    tstart = tend - nt
    cstart = jnp.cumsum(cnt) - cnt
    pad = lambda v: jnp.zeros((1, LANES), F32).at[0, :N_BUCKETS].set(v.astype(F32))
    j = jnp.arange(n_tiles, dtype=jnp.int32)
    b = jnp.minimum(jnp.sum((tend[None, :] <= j[:, None]).astype(jnp.int32), axis=1), N_BUCKETS - 1)
    in_bucket = (j - tstart[b]) * TM_EXP
    e_lo, e_hi = _bucket_experts()
    epg = EXPERTS_PER_GROUP
    return (pad(tstart * TM_EXP), pad(cstart), b // PAIRS_PER_GROUP, e_lo[b] % epg, e_hi[b] % epg,
            cstart[b] + in_bucket, tend[-1:])


def _rope_tables(pos):
    half = SWA_HD // 2
    inv = ROPE_THETA ** (-jnp.arange(half, dtype=F32) / half)
    ang = pos.astype(F32)[:, None] * inv[None, :]
    cos = jnp.cos(ang)
    sin = jnp.sin(ang)
    cos = jnp.concatenate([cos, cos, cos, cos], axis=1)
    sin = jnp.concatenate([-sin, sin, -sin, sin], axis=1)
    return cos, sin


def kernel(x_prompt, x_sample, mem_prompt, state_ssm_re, state_ssm_im, cache_swa_k, cache_swa_v, cache_mem_k, cache_mem_v, norm_mix, w_in, ssm_a_re, ssm_a_im, ssm_log_dt, ssm_b_re, ssm_b_im, ssm_c_re, ssm_c_im, ssm_d, w_glu, swa_q_norm, swa_k_norm, swa_sinks, norm_mem, w_mem_kv, mem_q_norm, mem_k_norm, w_br_ssm, w_br_swa, w_br_mem, w_o, norm_ffn, w_router_group, b_router_group, w_router_expert, b_router_expert, w_exp_in, w_exp_down):
    depth = w_in.shape[0]
    assert depth == 1
    nb, seq, _ = x_prompt.shape
    db, dseq, _ = x_sample.shape
    assert dseq == 2 * LB
    l = 0

    w_r = jnp.concatenate([w_router_group[l], w_router_expert[l]], axis=1)
    w_r = jnp.pad(w_r, ((0, 0), (0, LANES - w_r.shape[1])))
    w_r_hi = w_r.astype(BF16)
    b_r = jnp.pad(jnp.concatenate([b_router_group[l], b_router_expert[l]]), (0, LANES - N_GROUPS_E - N_EXPERTS))
    p = dict(
        g_mix=norm_mix[l][None], w_a=w_in[l][:, :PROJ_A].astype(BF16), w_gates=w_in[l][:, PROJ_A:].astype(BF16),
        g_q=jnp.tile(swa_q_norm[l], SWA_HEADS)[None], g_k=jnp.tile(swa_k_norm[l], SWA_KV_HEADS)[None],
        g_qm=jnp.tile(mem_q_norm[l], MEM_HEADS)[None], g_km=jnp.tile(mem_k_norm[l], MEM_HEADS)[None],
        ones64=_block_ones(SWA_WIDTH, SWA_HD), ones128=_block_ones(MEM_WIDTH, MEM_HD),
        ssm_d=ssm_d[l][None], w_glu=w_glu[l].astype(BF16),
        g_mem=norm_mem[l][None], w_mem_kv=w_mem_kv[l].astype(BF16),
        w_br_ssm=w_br_ssm[l].astype(BF16), w_br_swa=w_br_swa[l].astype(BF16), w_br_mem=w_br_mem[l].astype(BF16),
        w_o=w_o[l].astype(BF16), g_ffn=norm_ffn[l][None],
        w_r_hi=w_r_hi, w_r_lo=(w_r - w_r_hi.astype(F32)).astype(BF16), b_r=b_r[None],
        w_exp_in=w_exp_in[l].astype(BF16), w_exp_down=w_exp_down[l].astype(BF16),
    )
    p.update(_ssm_params(ssm_a_re[l], ssm_a_im[l], ssm_log_dt[l], ssm_b_re[l], ssm_b_im[l],
                         ssm_c_re[l], ssm_c_im[l]))
    sinks = swa_sinks[l]

    xp = x_prompt.reshape(nb * seq, D_MODEL)
    cos_p, sin_p = _rope_tables(jnp.arange(seq, dtype=jnp.int32))
    u, q, k, v, qm = _inproj(xp, cos_p, sin_p, seq // TM_IN, p)
    zeros_state = jnp.zeros((nb, 1, SSM_CH), F32)
    y_ssm, pr, pi = _ssm(u, zeros_state, zeros_state, p, n_seq=nb, chained=True)
    mk, mv = _memkv(mem_prompt, p)
    y_swa, y_mem = _attn_prompt(q, k, v, qm, mk, mv, sinks, nb)
    win = min(WINDOW, seq)
    last_win = lambda a: a.reshape(nb, seq, SWA_KV_WIDTH)[:, seq - win:].reshape(nb, win, SWA_KV_HEADS, SWA_HD)
    p_k, p_v = last_win(k), last_win(v)

    xs = x_sample.reshape(db * dseq, D_MODEL)
    cos_s, sin_s = _rope_tables(PAST_LEN + jnp.arange(dseq, dtype=jnp.int32))
    reps = TM_IN // dseq
    us, qs, ks, vs, qms = _inproj(xs, jnp.tile(cos_s, (reps, 1)), jnp.tile(sin_s, (reps, 1)), 1, p)
    two_rows = lambda st: jnp.repeat(st.reshape(db, SSM_CH), 2, axis=0)
    ys_ssm, sr, si = _ssm(us, two_rows(state_ssm_re), two_rows(state_ssm_im), p, n_seq=db, chained=False)
    sr, si = sr[1::2], si[1::2]
    wb = cache_swa_k.shape[2]
    ys_swa, ys_mem, s_k, s_v = _attn_sample(
        qs, ks, vs, cache_swa_k[l].reshape(db, wb, SWA_KV_WIDTH), cache_swa_v[l].reshape(db, wb, SWA_KV_WIDTH),
        qms, cache_mem_k.reshape(db, N_MEM * MEM_HEADS, MEM_HD), cache_mem_v.reshape(db, N_MEM * MEM_HEADS, MEM_HD),
        sinks, dseq)
    t_p, t_s = nb * seq, db * dseq
    t_all = t_p + t_s
    x1, hx, route, counts = _merge((xp, y_ssm, y_swa, y_mem), (xs, ys_ssm, ys_swa, ys_mem), p)

    n_tiles = pl.cdiv(t_all, TM_EXP) + N_BUCKETS
    off, coff, tile_g, tile_a, tile_b, tile_cb, n_real = _tile_tables(counts, n_tiles)
    pos, cpos = _sorted_pos(route, off, coff)
    idx = _invert(cpos)
    y_sorted = _bucket_ffn(hx, idx, tile_g, tile_a, tile_b, tile_cb, n_real, p)
    yp = _unsort_add(x1, y_sorted, pos, 0, t_p).reshape(nb, seq, D_MODEL)
    ys = _unsort_add(x1, y_sorted, pos, t_p, t_s).reshape(db, dseq, D_MODEL)

    g, s = SSM_GROUPS, SSM_STATE
    return (yp, ys,
            pr.reshape(1, nb, g, s), pi.reshape(1, nb, g, s),
            p_k[None], p_v[None],
            mk.reshape(1, nb, N_MEM, MEM_HEADS, MEM_HD), mv.reshape(1, nb, N_MEM, MEM_HEADS, MEM_HD),
            sr.reshape(1, db, g, s), si.reshape(1, db, g, s),
            s_k.reshape(1, db, wb, SWA_KV_HEADS, SWA_HD), s_v.reshape(1, db, wb, SWA_KV_HEADS, SWA_HD))
```

```python
import functools

import jax
import jax.numpy as jnp
import numpy as np
from jax import lax
from jax.experimental import pallas as pl
from jax.experimental.pallas import tpu as pltpu

F32 = jnp.float32
BF16 = jnp.bfloat16

D_MODEL = 1024
SSM_WIDTH = 512
SSM_GROUP = 16
SSM_GROUPS = 32
SSM_STATE = 64
SSM_CH = SSM_GROUPS * SSM_STATE
LB = 4
OCT = 8
N_OCT = SSM_GROUPS // OCT
OCT_CH = OCT * SSM_STATE
SWA_HEADS = 8
SWA_KV_HEADS = 2
SWA_Q_PER_KV = SWA_HEADS // SWA_KV_HEADS
SWA_HD = 64
SWA_WIDTH = SWA_HEADS * SWA_HD
SWA_KV_WIDTH = SWA_KV_HEADS * SWA_HD
WINDOW = 128
PAST_LEN = 16384
ROPE_THETA = 10000.0
N_MEM = 256
MEM_HEADS = 4
MEM_HD = 128
MEM_WIDTH = MEM_HEADS * MEM_HD
N_BRANCH = 3
PROJ_A = SSM_WIDTH + SWA_WIDTH + 2 * SWA_KV_WIDTH + MEM_WIDTH
N_GROUPS_E = 4
EXPERTS_PER_GROUP = 8
N_EXPERTS = 32
D_FF = 256
EPS = 1e-6
NEG_INF = -1e30
SWA_SCALE = SWA_HD ** -0.5
MEM_SCALE = MEM_HD ** -0.5

LANES = 128
SUBLANES = 8
VMEM_LIMIT = 56 * 1024 * 1024

TM_IN = 512
TB_SSM = 512
SCAN_W = 512
TQ_ATT = 512
SEQ_BLK = 8
TM_MRG = 512
TM_BACK = 512
TM_EXP = 192

ROUTE_W = LANES
ROW_TILES = D_MODEL // LANES
HX_ROWS = 2 * ROW_TILES
PAIRS_PER_GROUP = EXPERTS_PER_GROUP * (EXPERTS_PER_GROUP - 1) // 2
N_BUCKETS = N_GROUPS_E * PAIRS_PER_GROUP


def _cparams(sem):
    return pltpu.CompilerParams(dimension_semantics=sem, vmem_limit_bytes=VMEM_LIMIT)


def _full(shape):
    nd = len(shape)
    return pl.BlockSpec(shape, lambda *_: (0,) * nd)


def _resident(shape):
    nd = len(shape)
    return pl.BlockSpec(shape, lambda *_: (0,) * nd, pipeline_mode=pl.Buffered(1))


def _split_bf16(x):
    hi = x.astype(BF16)
    lo = (x - hi.astype(F32)).astype(BF16)
    return hi, lo


def _seg_mean_sq(x, ones_blk, width):
    hi, lo = _split_bf16(x * x)
    s = jnp.dot(hi, ones_blk, preferred_element_type=F32) + jnp.dot(lo, ones_blk, preferred_element_type=F32)
    return s * (1.0 / width)


def _rms_rows(x, gain):
    return x * lax.rsqrt(jnp.mean(x * x, axis=-1, keepdims=True) + EPS) * gain


def _block_ones(n, width):
    i = jnp.arange(n) // width
    return (i[:, None] == i[None, :]).astype(BF16)


def _rope_cols(x, cos, sin_signed, lane_in_head):
    n = x.shape[1]
    reps = n // LANES
    if reps > 1:
        cos = jnp.concatenate([cos] * reps, axis=1)
        sin_signed = jnp.concatenate([sin_signed] * reps, axis=1)
    half = SWA_HD // 2
    partner = jnp.where(lane_in_head < half, pltpu.roll(x, n - half, axis=1), pltpu.roll(x, half, axis=1))
    return x * cos + partner * sin_signed


def _inproj_kernel(x_ref, gmix_ref, w_ref, gq_ref, gk_ref, gm_ref, cos_ref, sin_ref, o64_ref, o128_ref,
                   u_ref, q_ref, k_ref, v_ref, qm_ref, u_scr):
    x = x_ref[...]
    h = _rms_rows(x, gmix_ref[...]).astype(BF16)
    proj = jnp.dot(h, w_ref[...], preferred_element_type=F32)
    c0 = SSM_WIDTH
    c1 = c0 + SWA_WIDTH
    c2 = c1 + SWA_KV_WIDTH
    c3 = c2 + SWA_KV_WIDTH
    n_blk = u_ref.shape[0]
    for c in range(c0 // LANES):
        u_scr[c] = proj[:, c * LANES:(c + 1) * LANES]
        for t in range(LB):
            u_ref[:, t * c0 + c * LANES:t * c0 + (c + 1) * LANES] = u_scr[c, pl.ds(t, n_blk, stride=LB), :]
    q = proj[:, c0:c1]
    k = proj[:, c1:c2]
    v_ref[...] = proj[:, c2:c3]
    qm = proj[:, c3:]
    cos = cos_ref[...]
    sin = sin_ref[...]
    o64 = o64_ref[...]
    lane_q = lax.broadcasted_iota(jnp.int32, q.shape, 1) % SWA_HD
    qn = q * lax.rsqrt(_seg_mean_sq(q, o64, SWA_HD) + EPS) * gq_ref[...]
    q_ref[...] = (_rope_cols(qn, cos, sin, lane_q) * SWA_SCALE).astype(BF16)
    lane_k = lax.broadcasted_iota(jnp.int32, k.shape, 1) % SWA_HD
    kn = k * lax.rsqrt(_seg_mean_sq(k, o64[:SWA_KV_WIDTH, :SWA_KV_WIDTH], SWA_HD) + EPS) * gk_ref[...]
    k_ref[...] = _rope_cols(kn, cos, sin, lane_k)
    qmn = qm * lax.rsqrt(_seg_mean_sq(qm, o128_ref[...], MEM_HD) + EPS) * gm_ref[...]
    qm_ref[...] = qmn.astype(BF16)


def _inproj(x2d, cos, sin, pos_blocks, p):
    t = x2d.shape[0]
    tm = TM_IN
    grid = (t // tm,)
    row = lambda i: (i, 0)
    tab = lambda i: (i % pos_blocks, 0)
    out_shape = (
        jax.ShapeDtypeStruct((t // LB, LB * SSM_WIDTH), F32),
        jax.ShapeDtypeStruct((t, SWA_WIDTH), BF16),
        jax.ShapeDtypeStruct((t, SWA_KV_WIDTH), F32),
        jax.ShapeDtypeStruct((t, SWA_KV_WIDTH), F32),
        jax.ShapeDtypeStruct((t, MEM_WIDTH), BF16),
    )
    return pl.pallas_call(
        _inproj_kernel,
        grid=grid,
        in_specs=[
            pl.BlockSpec((tm, D_MODEL), row),
            _full((1, D_MODEL)),
            _resident((D_MODEL, PROJ_A)),
            _full((1, SWA_WIDTH)),
            _full((1, SWA_KV_WIDTH)),
            _full((1, MEM_WIDTH)),
            pl.BlockSpec((tm, LANES), tab),
            pl.BlockSpec((tm, LANES), tab),
            _full((SWA_WIDTH, SWA_WIDTH)),
            _full((MEM_WIDTH, MEM_WIDTH)),
        ],
        out_specs=(
            pl.BlockSpec((tm // LB, LB * SSM_WIDTH), row),
            pl.BlockSpec((tm, SWA_WIDTH), row),
            pl.BlockSpec((tm, SWA_KV_WIDTH), row),
            pl.BlockSpec((tm, SWA_KV_WIDTH), row),
            pl.BlockSpec((tm, MEM_WIDTH), row),
        ),
        out_shape=out_shape,
        scratch_shapes=[pltpu.VMEM((SSM_WIDTH // LANES, tm, LANES), F32)],
        compiler_params=_cparams(("parallel",)),
        name="inproj",
    )(x2d, p["g_mix"], p["w_in"], p["g_q"], p["g_k"], p["g_qm"], cos, sin, p["ones64"], p["ones128"])


def _cmul(x, y):
    return x[0] * y[0] - x[1] * y[1], x[0] * y[1] + x[1] * y[0]


def _ssm_kernel(u_ref, s0r_ref, s0i_ref, wx_ref, wt_ref, wc_ref, d_ref, wglu_ref, abr_ref, abi_ref,
                lvr_ref, lvi_ref, cpr_ref, cpi_ref,
                y_ref, fr_ref, fi_ref, sr_ref, si_ref, car_ref, cai_ref, o_scr, *, chained):
    tb = u_ref.shape[0]
    u = u_ref[...]
    ub = u.astype(BF16)
    lhs = [jnp.concatenate([ub[:, t * SSM_WIDTH + c * LANES:t * SSM_WIDTH + (c + 1) * LANES] for t in range(LB)],
                           axis=1) for c in range(N_OCT)]
    for c in range(N_OCT):
        x = jnp.dot(lhs[c], wx_ref[c], preferred_element_type=F32)
        cs = slice(c * OCT_CH, (c + 1) * OCT_CH)
        sr_ref[:, cs] = x[:, :OCT_CH]
        si_ref[:, cs] = x[:, OCT_CH:]

    if chained:
        @pl.when(pl.program_id(1) == 0)
        def _():
            car_ref[...] = s0r_ref[0]
            cai_ref[...] = s0i_ref[0]

        first_row = lax.broadcasted_iota(jnp.int32, (SUBLANES, SCAN_W), 0) == 0
        for sl in range(SSM_CH // SCAN_W):
            cols = slice(sl * SCAN_W, (sl + 1) * SCAN_W)
            lv = [(lvr_ref[j, :, cols], lvi_ref[j, :, cols]) for j in range(3)]
            cpr = cpr_ref[:, cols]
            cpi = cpi_ref[:, cols]

            def tile(i, carry, cols=cols, lv=lv, cpr=cpr, cpi=cpi):
                r0 = pl.multiple_of(i * SUBLANES, SUBLANES)
                xr = sr_ref[pl.ds(r0, SUBLANES), cols]
                xi = si_ref[pl.ds(r0, SUBLANES), cols]
                for j, d in enumerate((1, 2, 4)):
                    pr, pi = lv[j]
                    shr = pltpu.roll(xr, d, axis=0)
                    shi = pltpu.roll(xi, d, axis=0)
                    xr, xi = xr + pr * shr - pi * shi, xi + pr * shi + pi * shr
                cb_r = jnp.broadcast_to(carry[0], xr.shape)
                cb_i = jnp.broadcast_to(carry[1], xr.shape)
                xr, xi = xr + cpr * cb_r - cpi * cb_i, xi + cpr * cb_i + cpi * cb_r
                sr_ref[pl.ds(r0, SUBLANES), cols] = jnp.where(first_row, cb_r, pltpu.roll(xr, 1, axis=0))
                si_ref[pl.ds(r0, SUBLANES), cols] = jnp.where(first_row, cb_i, pltpu.roll(xi, 1, axis=0))
                return xr[SUBLANES - 1:SUBLANES, :], xi[SUBLANES - 1:SUBLANES, :]

            c_r, c_i = lax.fori_loop(0, tb // SUBLANES, tile, (car_ref[:, cols], cai_ref[:, cols]))
            car_ref[:, cols] = c_r
            cai_ref[:, cols] = c_i
        fr_ref[0] = car_ref[...]
        fi_ref[0] = cai_ref[...]
    else:
        odd = lax.broadcasted_iota(jnp.int32, (tb, SSM_CH), 0) % 2 == 1
        ab = (abr_ref[...], abi_ref[...])
        s0 = (s0r_ref[...], s0i_ref[...])
        x = (sr_ref[...], si_ref[...])
        e_first = _cmul(ab, s0)
        e_first = (e_first[0] + x[0], e_first[1] + x[1])
        prev = (pltpu.roll(e_first[0], 1, axis=0), pltpu.roll(e_first[1], 1, axis=0))
        e_second = _cmul(ab, prev)
        fr_ref[...] = jnp.where(odd, e_second[0] + x[0], e_first[0])
        fi_ref[...] = jnp.where(odd, e_second[1] + x[1], e_first[1])
        sr_ref[...] = jnp.where(odd, prev[0], s0[0])
        si_ref[...] = jnp.where(odd, prev[1], s0[1])

    ys = [[None] * N_OCT for _ in range(LB)]
    for c in range(N_OCT):
        cs = slice(c * OCT_CH, (c + 1) * OCT_CH)
        s_in = jnp.concatenate([sr_ref[:, cs], si_ref[:, cs]], axis=1).astype(BF16)
        yc = (jnp.dot(lhs[c], wt_ref[c], preferred_element_type=F32)
              + lax.dot_general(s_in, wc_ref[c], (((1,), (1,)), ((), ())), preferred_element_type=F32))
        for t in range(LB):
            ys[t][c] = yc[:, t * LANES:(t + 1) * LANES]
    y = jnp.concatenate([jnp.concatenate(ys[t], axis=1) for t in range(LB)], axis=0)
    us = jnp.concatenate([u[:, t * SSM_WIDTH:(t + 1) * SSM_WIDTH] for t in range(LB)], axis=0)
    y = jax.nn.gelu(y + d_ref[...] * us)
    gate = jax.nn.sigmoid(jnp.dot(y.astype(BF16), wglu_ref[...], preferred_element_type=F32))
    out = y * gate
    for c in range(SSM_WIDTH // LANES):
        for t in range(LB):
            o_scr[c, pl.ds(t, tb, stride=LB), :] = out[t * tb:(t + 1) * tb, c * LANES:(c + 1) * LANES]
        y_ref[:, c * LANES:(c + 1) * LANES] = o_scr[c]


def _ssm(u_blk, s0r, s0i, p, *, n_seq, chained):
    rows = u_blk.shape[0]
    if chained:
        tb = TB_SSM
        per = rows // n_seq // tb
        grid = (n_seq, per)
        row = lambda n, c: (n * per + c, 0)
        st = lambda n, c: (n, 0, 0)
        s0_spec = pl.BlockSpec((1, 1, SSM_CH), st)
        f_spec = pl.BlockSpec((1, 1, SSM_CH), st)
        f_shape = jax.ShapeDtypeStruct((n_seq, 1, SSM_CH), F32)
        sem = ("parallel", "arbitrary")
    else:
        tb = min(TB_SSM, rows)
        grid = (rows // tb,)
        row = lambda c: (c, 0)
        s0_spec = pl.BlockSpec((tb, SSM_CH), row)
        f_spec = pl.BlockSpec((tb, SSM_CH), row)
        f_shape = jax.ShapeDtypeStruct((rows, SSM_CH), F32)
        sem = ("parallel",)
    blk_w = LB * SSM_WIDTH
    return pl.pallas_call(
        functools.partial(_ssm_kernel, chained=chained),
        grid=grid,
        in_specs=[
            pl.BlockSpec((tb, blk_w), row),
            s0_spec, s0_spec,
            _resident((N_OCT, LB * LANES, 2 * OCT_CH)), _resident((N_OCT, LB * LANES, LB * LANES)),
            _resident((N_OCT, LB * LANES, 2 * OCT_CH)),
            _full((1, SSM_WIDTH)),
            _resident((SSM_WIDTH, SSM_WIDTH)),
            _full((1, SSM_CH)), _full((1, SSM_CH)),
            _full((3, SUBLANES, SSM_CH)), _full((3, SUBLANES, SSM_CH)),
            _full((SUBLANES, SSM_CH)), _full((SUBLANES, SSM_CH)),
        ],
        out_specs=(pl.BlockSpec((tb * LB, SSM_WIDTH), row), f_spec, f_spec),
        out_shape=(jax.ShapeDtypeStruct((rows * LB, SSM_WIDTH), F32), f_shape, f_shape),
        scratch_shapes=[
            pltpu.VMEM((tb, SSM_CH), F32), pltpu.VMEM((tb, SSM_CH), F32),
            pltpu.VMEM((1, SSM_CH), F32), pltpu.VMEM((1, SSM_CH), F32),
            pltpu.VMEM((SSM_WIDTH // LANES, tb * LB, LANES), F32),
        ],
        compiler_params=_cparams(sem),
        name="ssm_chained" if chained else "ssm_pairs",
    )(u_blk, s0r, s0i, p["w_x"], p["w_t"], p["w_c"], p["ssm_d"], p["w_glu"], p["ab_re"], p["ab_im"],
      p["lv_re"], p["lv_im"], p["cp_re"], p["cp_im"])


def _ssm_params(a_re, a_im, log_dt, b_re, b_im, c_re, c_im):
    dt = jnp.exp(log_dt)[:, None]
    mag = jnp.exp(a_re * dt)
    abr = mag * jnp.cos(a_im * dt)
    abi = mag * jnp.sin(a_im * dt)
    den = a_re * a_re + a_im * a_im
    nr = abr - 1.0
    ni = abi
    coef_re = (nr * a_re + ni * a_im) / den
    coef_im = (ni * a_re - nr * a_im) / den
    bb = (coef_re[..., None] * b_re - coef_im[..., None] * b_im,
          coef_re[..., None] * b_im + coef_im[..., None] * b_re)
    cc = (c_re, c_im)

    apow = [(jnp.ones_like(abr), jnp.zeros_like(abi))]
    for _ in range(LB):
        apow.append(_cmul(apow[-1], (abr, abi)))
    hp = lax.Precision.HIGHEST

    def group_diagonal(vals, col_group_width):
        n_col = vals.shape[-1]
        col_group = (jnp.arange(n_col) // col_group_width) % OCT
        keep = (col_group[None, :] == jnp.arange(OCT)[:, None]).astype(BF16)
        out = vals.astype(BF16)[:, :, None, :, :] * keep[None, None, :, None, :]
        return out.reshape(N_OCT, LB * OCT * SSM_GROUP, n_col)

    def octets_last(v):
        lead = v.shape[1:-3]
        v = v.reshape((LB,) + lead + (N_OCT, OCT, SSM_GROUP, v.shape[-1]))
        nl = len(lead)
        perm = (1 + nl, 0, 3 + nl) + tuple(range(1, 1 + nl)) + (2 + nl, 4 + nl)
        return v.transpose(perm).reshape(N_OCT, LB, SSM_GROUP, -1)

    bb_t = (bb[0].transpose(0, 2, 1), bb[1].transpose(0, 2, 1))
    mx = [jnp.stack(_cmul((apow[LB - 1 - t][0][:, None, :], apow[LB - 1 - t][1][:, None, :]), bb_t))
          for t in range(LB)]
    w_x = group_diagonal(octets_last(jnp.stack(mx)), SSM_STATE)

    mc = []
    for t in range(LB):
        m = _cmul(cc, (apow[t + 1][0][:, None, :], apow[t + 1][1][:, None, :]))
        mc.append(jnp.stack([m[0], -m[1]]))
    w_c = group_diagonal(octets_last(jnp.stack(mc)), SSM_STATE)

    kd = []
    for d in range(LB):
        m = _cmul(cc, (apow[d][0][:, None, :], apow[d][1][:, None, :]))
        kd.append(jnp.einsum("ghp,gpk->gkh", m[0], bb[0], precision=hp)
                  - jnp.einsum("ghp,gpk->gkh", m[1], bb[1], precision=hp))
    zero = jnp.zeros_like(kd[0])
    lagged = jnp.stack([jnp.stack([kd[t - t0] if t >= t0 else zero for t in range(LB)]) for t0 in range(LB)])
    w_t = group_diagonal(octets_last(lagged), SSM_GROUP)

    ab = (apow[LB][0].reshape(1, SSM_CH), apow[LB][1].reshape(1, SSM_CH))
    pows = [ab]
    for _ in range(SUBLANES - 1):
        pows.append(_cmul(pows[-1], ab))
    rows = jnp.arange(SUBLANES)[:, None]
    lv_re = jnp.stack([jnp.where(rows >= d, pows[d - 1][0], 0.0) for d in (1, 2, 4)])
    lv_im = jnp.stack([jnp.where(rows >= d, pows[d - 1][1], 0.0) for d in (1, 2, 4)])
    cp_re = jnp.concatenate([pw[0] for pw in pows], axis=0)
    cp_im = jnp.concatenate([pw[1] for pw in pows], axis=0)
    return dict(w_t=w_t.astype(BF16), w_x=w_x.astype(BF16), w_c=w_c.astype(BF16), ab_re=ab[0], ab_im=ab[1],
                lv_re=lv_re, lv_im=lv_im, cp_re=cp_re, cp_im=cp_im)


def _memkv_kernel(m_ref, g_ref, w_ref, gk_ref, o128_ref, k_ref, v_ref):
    hm = _rms_rows(m_ref[0], g_ref[...]).astype(BF16)
    kv = jnp.dot(hm, w_ref[...], preferred_element_type=F32)
    k = kv[:, :MEM_WIDTH]
    k_ref[0] = k * lax.rsqrt(_seg_mean_sq(k, o128_ref[...], MEM_HD) + EPS) * gk_ref[...]
    v_ref[0] = kv[:, MEM_WIDTH:]


def _memkv(mem, p):
    n = mem.shape[0]
    blk = lambda i: (i, 0, 0)
    shp = jax.ShapeDtypeStruct((n, N_MEM, MEM_WIDTH), F32)
    return pl.pallas_call(
        _memkv_kernel,
        grid=(n,),
        in_specs=[pl.BlockSpec((1, N_MEM, D_MODEL), blk), _full((1, D_MODEL)),
                  _full((D_MODEL, 2 * MEM_WIDTH)), _full((1, MEM_WIDTH)), _full((MEM_WIDTH, MEM_WIDTH))],
        out_specs=(pl.BlockSpec((1, N_MEM, MEM_WIDTH), blk), pl.BlockSpec((1, N_MEM, MEM_WIDTH), blk)),
        out_shape=(shp, shp),
        compiler_params=_cparams(("parallel",)),
        name="memkv",
    )(mem, p["g_mem"], p["w_mem_kv"], p["g_km"], p["ones128"])


def _dup_heads(x, lane):
    sw = pltpu.roll(x, SWA_HD, axis=x.ndim - 1)
    lo = lane < SWA_HD
    return jnp.where(lo, x, sw), jnp.where(lo, sw, x)


def _swa_group(q_blk, kk, vv, g, mask, sink_ref):
    tq = q_blk.shape[-2]
    shp = q_blk.shape[:-1]
    lane = lax.broadcasted_iota(jnp.int32, shp + (LANES,), len(shp))
    rows = []
    sinks = []
    for hl in range(SWA_Q_PER_KV):
        h = g * SWA_Q_PER_KV + hl
        pair = q_blk[..., (h // 2) * LANES:(h // 2 + 1) * LANES]
        keep = (lane < SWA_HD) if h % 2 == 0 else (lane >= SWA_HD)
        rows.append(jnp.where(keep, pair, 0.0))
        sinks.append(jnp.full(shp + (1,), sink_ref[h], F32))
    qq = jnp.concatenate(rows, axis=-2).astype(BF16)
    sk = jnp.concatenate(sinks, axis=-2)
    s = jnp.einsum("...qd,...kd->...qk", qq, kk, preferred_element_type=F32)
    s = jnp.where(mask, s, NEG_INF)
    m = jnp.maximum(jnp.max(s, axis=-1, keepdims=True), sk)
    e = jnp.exp(s - m)
    pr = e / (jnp.sum(e, axis=-1, keepdims=True) + jnp.exp(sk - m))
    o = jnp.einsum("...qk,...kd->...qd", pr.astype(BF16), vv, preferred_element_type=F32)
    lo = lane < SWA_HD
    return [jnp.where(lo, o[..., (2 * j) * tq:(2 * j + 1) * tq, :], o[..., (2 * j + 1) * tq:(2 * j + 2) * tq, :])
            for j in range(2)]


def _mem_heads(qm, k_head, v_head):
    outs = []
    for h in range(MEM_HEADS):
        cs = slice(h * MEM_HD, (h + 1) * MEM_HD)
        s = jnp.einsum("...qd,...kd->...qk", qm[..., cs], k_head(h).astype(BF16),
                       preferred_element_type=F32) * MEM_SCALE
        m = jnp.max(s, axis=-1, keepdims=True)
        e = jnp.exp(s - m)
        pr = e / jnp.sum(e, axis=-1, keepdims=True)
        outs.append(jnp.einsum("...qk,...kd->...qd", pr.astype(BF16), v_head(h).astype(BF16),
                               preferred_element_type=F32))
    return jnp.concatenate(outs, axis=-1)


def _attn_prompt_kernel(sink_ref, q_ref, k_ref, v_ref, kp_ref, vp_ref, qm_ref, mk_ref, mv_ref, ys_ref, ym_ref):
    tq = q_ref.shape[0]
    blk = WINDOW
    rows = SWA_Q_PER_KV * blk
    i = lax.broadcasted_iota(jnp.int32, (rows, 2 * blk), 0) % blk
    j = lax.broadcasted_iota(jnp.int32, (rows, 2 * blk), 1)
    lo = jnp.where(j < blk, i + 1, blk)
    hi = jnp.where(j < blk, blk, blk + i + 1)
    first_lo = jnp.where(pl.program_id(1) == 0, blk, 0)
    lane_k = lax.broadcasted_iota(jnp.int32, (2 * blk, LANES), 1)
    for b in range(tq // blk):
        rs = slice(b * blk, (b + 1) * blk)
        if b == 0:
            k2 = jnp.concatenate([kp_ref[...], k_ref[rs, :]], axis=0)
            v2 = jnp.concatenate([vp_ref[...], v_ref[rs, :]], axis=0)
            mask = (j >= jnp.maximum(lo, first_lo)) & (j < hi)
        else:
            k2 = k_ref[(b - 1) * blk:(b + 1) * blk, :]
            v2 = v_ref[(b - 1) * blk:(b + 1) * blk, :]
            mask = (j >= lo) & (j < hi)
        kks = _dup_heads(k2, lane_k)
        vvs = _dup_heads(v2, lane_k)
        q_blk = q_ref[rs, :].astype(F32)
        pairs = []
        for g in range(SWA_KV_HEADS):
            pairs += _swa_group(q_blk, kks[g].astype(BF16), vvs[g].astype(BF16), g, mask, sink_ref)
        ys_ref[rs, :] = jnp.concatenate(pairs, axis=1).astype(BF16)
    ym_ref[...] = _mem_heads(qm_ref[...], lambda h: mk_ref[0, :, h * MEM_HD:(h + 1) * MEM_HD],
                             lambda h: mv_ref[0, :, h * MEM_HD:(h + 1) * MEM_HD]).astype(BF16)


def _attn_prompt(q, k, v, qm, mk, mv, sinks, n_seq):
    t = q.shape[0]
    tq = TQ_ATT
    per = t // n_seq // tq
    sub = tq // WINDOW
    row = lambda n, c: (n * per + c, 0)
    prev = lambda n, c: (jnp.maximum((n * per + c) * sub - 1, 0), 0)
    memb = lambda n, c: (n, 0, 0)
    return pl.pallas_call(
        _attn_prompt_kernel,
        grid=(n_seq, per),
        in_specs=[
            pl.BlockSpec(memory_space=pltpu.SMEM),
            pl.BlockSpec((tq, SWA_WIDTH), row),
            pl.BlockSpec((tq, SWA_KV_WIDTH), row),
            pl.BlockSpec((tq, SWA_KV_WIDTH), row),
            pl.BlockSpec((WINDOW, SWA_KV_WIDTH), prev),
            pl.BlockSpec((WINDOW, SWA_KV_WIDTH), prev),
            pl.BlockSpec((tq, MEM_WIDTH), row),
            pl.BlockSpec((1, N_MEM, MEM_WIDTH), memb),
            pl.BlockSpec((1, N_MEM, MEM_WIDTH), memb),
        ],
        out_specs=(pl.BlockSpec((tq, SWA_WIDTH), row), pl.BlockSpec((tq, MEM_WIDTH), row)),
        out_shape=(jax.ShapeDtypeStruct((t, SWA_WIDTH), BF16), jax.ShapeDtypeStruct((t, MEM_WIDTH), BF16)),
        compiler_params=_cparams(("parallel", "parallel")),
        name="attn_prompt",
    )(sinks, q, k, v, k, v, qm, mk, mv)


def _attn_sample_kernel(sink_ref, q_ref, k_ref, v_ref, pk_ref, pv_ref, qm_ref, mk_ref, mv_ref,
                        ys_ref, ym_ref, nk_ref, nv_ref, *, s_len):
    sb, wb = pk_ref.shape[0], pk_ref.shape[1]
    n_keys = wb + s_len
    rows = SWA_Q_PER_KV * s_len
    i = lax.broadcasted_iota(jnp.int32, (sb, rows, n_keys), 1) % s_len
    j = lax.broadcasted_iota(jnp.int32, (sb, rows, n_keys), 2)
    rel = i + wb - j
    mask = (rel >= 0) & (rel < WINDOW)
    k_all = jnp.concatenate([pk_ref[...], k_ref[...].reshape(sb, s_len, SWA_KV_WIDTH)], axis=1)
    v_all = jnp.concatenate([pv_ref[...], v_ref[...].reshape(sb, s_len, SWA_KV_WIDTH)], axis=1)
    nk_ref[...] = k_all[:, n_keys - wb:, :]
    nv_ref[...] = v_all[:, n_keys - wb:, :]
    lane_k = lax.broadcasted_iota(jnp.int32, k_all.shape, 2)
    kks = _dup_heads(k_all, lane_k)
    vvs = _dup_heads(v_all, lane_k)
    q3 = q_ref[...].astype(F32).reshape(sb, s_len, SWA_WIDTH)
    pairs = []
    for g in range(SWA_KV_HEADS):
        pairs += _swa_group(q3, kks[g].astype(BF16), vvs[g].astype(BF16), g, mask, sink_ref)
    ys_ref[...] = jnp.concatenate(pairs, axis=-1).reshape(sb * s_len, SWA_WIDTH).astype(BF16)
    qm3 = qm_ref[...].astype(F32).reshape(sb, s_len, MEM_WIDTH).astype(BF16)
    head_rows = lambda h: pl.ds(h, N_MEM, stride=MEM_HEADS)
    ym = _mem_heads(qm3, lambda h: mk_ref[:, head_rows(h), :], lambda h: mv_ref[:, head_rows(h), :])
    ym_ref[...] = ym.reshape(sb * s_len, MEM_WIDTH).astype(BF16)


def _attn_sample(q, k, v, past_k, past_v, qm, mk, mv, sinks, s_len):
    t = q.shape[0]
    n_seq, wb = past_k.shape[0], past_k.shape[1]
    sb = SEQ_BLK
    rows = sb * s_len
    row = lambda c: (c, 0)
    seq = lambda c: (c, 0, 0)
    cache_shape = jax.ShapeDtypeStruct((n_seq, wb, SWA_KV_WIDTH), F32)
    return pl.pallas_call(
        functools.partial(_attn_sample_kernel, s_len=s_len),
        grid=(n_seq // sb,),
        in_specs=[
            pl.BlockSpec(memory_space=pltpu.SMEM),
            pl.BlockSpec((rows, SWA_WIDTH), row),
            pl.BlockSpec((rows, SWA_KV_WIDTH), row),
            pl.BlockSpec((rows, SWA_KV_WIDTH), row),
            pl.BlockSpec((sb, wb, SWA_KV_WIDTH), seq),
            pl.BlockSpec((sb, wb, SWA_KV_WIDTH), seq),
            pl.BlockSpec((rows, MEM_WIDTH), row),
            pl.BlockSpec((sb, N_MEM * MEM_HEADS, MEM_HD), seq),
            pl.BlockSpec((sb, N_MEM * MEM_HEADS, MEM_HD), seq),
        ],
        out_specs=(pl.BlockSpec((rows, SWA_WIDTH), row), pl.BlockSpec((rows, MEM_WIDTH), row),
                   pl.BlockSpec((sb, wb, SWA_KV_WIDTH), seq), pl.BlockSpec((sb, wb, SWA_KV_WIDTH), seq)),
        out_shape=(jax.ShapeDtypeStruct((t, SWA_WIDTH), BF16), jax.ShapeDtypeStruct((t, MEM_WIDTH), BF16),
                   cache_shape, cache_shape),
        compiler_params=_cparams(("parallel",)),
        name="attn_sample",
    )(sinks, q, k, v, past_k, past_v, qm, mk, mv)


def _first_argmax(x, valid, lane):
    xm = jnp.where(valid, x, -jnp.inf)
    mx = jnp.max(xm, axis=-1, keepdims=True)
    idx = jnp.min(jnp.where(xm == mx, lane, LANES), axis=-1, keepdims=True)
    return mx, lane == idx, idx


def _merge_kernel(xp_ref, yap_ref, ybp_ref, ycp_ref, xs_ref, yas_ref, ybs_ref, ycs_ref,
                  gmix_ref, wg_ref, wa_ref, wb_ref, wc_ref, wo_ref, gffn_ref, wr_hi_ref, wr_lo_ref, br_ref,
                  x1_ref, hx_ref, route_ref, cnt_ref, carry_ref, x1_prev, *, n_blk_p, n_blk):
    i = pl.program_id(0)

    @pl.when(i == 0)
    def _():
        x1_prev[...] = jnp.zeros_like(x1_prev)
        carry_ref[...] = jnp.zeros_like(carry_ref)

    x1_routed = x1_prev[...]

    is_prompt = jnp.minimum(i, n_blk - 1) < n_blk_p
    pick = lambda a_ref, b_ref: jnp.where(is_prompt, a_ref[...], b_ref[...])
    x = pick(xp_ref, xs_ref)
    h = _rms_rows(x, gmix_ref[...]).astype(BF16)

    def branch(k, y, w_ref):
        cols = slice(PROJ_A + k * D_MODEL, PROJ_A + (k + 1) * D_MODEL)
        gate = jax.nn.sigmoid(jnp.dot(h, wg_ref[:, cols], preferred_element_type=F32))
        return gate * jnp.dot(y.astype(BF16), w_ref[...], preferred_element_type=F32)

    routing = _route_rows(x1_routed, i >= 1, gffn_ref, wr_hi_ref, wr_lo_ref, br_ref, hx_ref, route_ref, cnt_ref,
                          carry_ref)
    next(routing)
    merged = branch(0, pick(yap_ref, yas_ref), wa_ref)
    next(routing)
    merged = merged + branch(1, pick(ybp_ref, ybs_ref), wb_ref)
    next(routing)
    merged = merged + branch(2, pick(ycp_ref, ycs_ref), wc_ref)
    next(routing, None)
    x1 = x + jnp.dot(merged.astype(BF16), wo_ref[...], preferred_element_type=F32)
    x1_ref[...] = x1
    x1_prev[...] = x1


def _route_rows(x1, valid, gffn_ref, wr_hi_ref, wr_lo_ref, br_ref, hx_ref, route_ref, cnt_ref, carry_ref):
    hn = _rms_rows(x1, gffn_ref[...])
    tm = x1.shape[0]
    slab_row = lambda c: pl.ds(c, tm, stride=HX_ROWS)
    for c in range(ROW_TILES):
        hx_ref[slab_row(c), :] = hn[:, c * LANES:(c + 1) * LANES]
    yield

    hi, lo = _split_bf16(hn)
    w_hi = wr_hi_ref[...]
    logits = (jnp.dot(hi, w_hi, preferred_element_type=F32) + jnp.dot(lo, w_hi, preferred_element_type=F32)
              + jnp.dot(hi, wr_lo_ref[...], preferred_element_type=F32)) + br_ref[...]
    yield
    lane = lax.broadcasted_iota(jnp.int32, logits.shape, 1)
    is_grp = lane < N_GROUPS_E
    g_max, _, g_idx = _first_argmax(logits, is_grp, lane)
    pg_top = 1.0 / jnp.sum(jnp.where(is_grp, jnp.exp(logits - g_max), 0.0), axis=-1, keepdims=True)
    e_lo = N_GROUPS_E + g_idx * EXPERTS_PER_GROUP
    in_grp = (lane >= e_lo) & (lane < e_lo + EXPERTS_PER_GROUP)
    e_max, first, i1 = _first_argmax(logits, in_grp, lane)
    ex = jnp.where(in_grp, jnp.exp(logits - e_max), 0.0)
    pe = ex / jnp.sum(ex, axis=-1, keepdims=True)
    _, second, i2 = _first_argmax(logits, in_grp & jnp.logical_not(first), lane)
    p1 = jnp.sum(jnp.where(first, pe, 0.0), axis=-1, keepdims=True)
    p2 = jnp.sum(jnp.where(second, pe, 0.0), axis=-1, keepdims=True)
    w1 = pg_top * p1 / (p1 + p2)
    w2 = pg_top * p2 / (p1 + p2)

    a1 = i1 - e_lo
    a2 = i2 - e_lo
    e_a = jnp.minimum(a1, a2)
    e_b = jnp.maximum(a1, a2)
    pair = jnp.right_shift(e_a * (2 * EXPERTS_PER_GROUP - 1 - e_a), 1) + (e_b - e_a - 1)
    bucket = g_idx * PAIRS_PER_GROUP + pair
    w_a = jnp.where(a1 < a2, w1, w2)
    w_b = jnp.where(a1 < a2, w2, w1)
    yield

    onehot = lane == jnp.where(valid, bucket, -1)
    tri = (lax.broadcasted_iota(jnp.int32, (tm, tm), 1) <= lax.broadcasted_iota(jnp.int32, (tm, tm), 0))
    csum = jnp.dot(jnp.where(tri, 1.0, 0.0).astype(BF16), jnp.where(onehot, 1.0, 0.0).astype(BF16),
                   preferred_element_type=F32)
    carry = carry_ref[...]
    rank = jnp.sum(jnp.where(onehot, csum + carry, 0.0), axis=-1, keepdims=True) - 1.0
    carry = carry + csum[tm - 1:tm, :]
    carry_ref[...] = carry
    cnt_ref[...] = carry
    route = jnp.where(lane == 0, bucket.astype(F32),
                      jnp.where(lane == 1, w_a, jnp.where(lane == 2, w_b, jnp.where(lane == 3, rank, 0.0))))
    route_ref[...] = route
    hx_ref[slab_row(ROW_TILES), :] = route
    for c in range(ROW_TILES + 1, HX_ROWS):
        hx_ref[slab_row(c), :] = jnp.zeros_like(route)


def _merge(prompt_rows, sample_rows, p):
    tm = TM_MRG
    t_p, t_s = prompt_rows[0].shape[0], sample_rows[0].shape[0]
    nbp = t_p // tm
    t_all = t_p + t_s
    n_blk = t_all // tm
    first = lambda i: (jnp.minimum(i, nbp - 1), 0)
    second = lambda i: (jnp.clip(i - nbp, 0, n_blk - nbp - 1), 0)
    merged_blk = lambda i: (jnp.minimum(i, n_blk - 1), 0)
    routed_blk = lambda i: (jnp.maximum(i - 1, 0), 0)
    widths = (D_MODEL, SSM_WIDTH, SWA_WIDTH, MEM_WIDTH)
    in_specs = ([pl.BlockSpec((tm, w), first) for w in widths] + [pl.BlockSpec((tm, w), second) for w in widths] + [
        _resident((1, D_MODEL)),
        _resident((D_MODEL, PROJ_A + N_BRANCH * D_MODEL)),
        _resident((SSM_WIDTH, D_MODEL)), _resident((SWA_WIDTH, D_MODEL)), _resident((MEM_WIDTH, D_MODEL)),
        _resident((D_MODEL, D_MODEL)),
        _resident((1, D_MODEL)),
        _resident((D_MODEL, LANES)), _resident((D_MODEL, LANES)), _resident((1, LANES)),
    ])
    return pl.pallas_call(
        functools.partial(_merge_kernel, n_blk_p=nbp, n_blk=n_blk),
        grid=(n_blk + 1,),
        in_specs=in_specs,
        out_specs=(pl.BlockSpec((tm, D_MODEL), merged_blk), pl.BlockSpec((tm * HX_ROWS, LANES), routed_blk),
                   pl.BlockSpec((tm, ROUTE_W), routed_blk), _full((1, LANES))),
        out_shape=(jax.ShapeDtypeStruct((t_all, D_MODEL), F32),
                   jax.ShapeDtypeStruct((t_all * HX_ROWS, LANES), F32),
                   jax.ShapeDtypeStruct((t_all, ROUTE_W), F32), jax.ShapeDtypeStruct((1, LANES), F32)),
        scratch_shapes=[pltpu.VMEM((1, LANES), F32), pltpu.VMEM((tm, D_MODEL), F32)],
        compiler_params=_cparams(("arbitrary",)),
        name="merge",
    )(*prompt_rows, *sample_rows, p["g_mix"], p["w_in"], p["w_br_ssm"], p["w_br_swa"], p["w_br_mem"], p["w_o"],
      p["g_ffn"], p["w_r_hi"], p["w_r_lo"], p["b_r"])


def _rows_to_lanes(col):
    out = []
    for k in range(col.shape[0] // LANES):
        blk = jnp.broadcast_to(col[k * LANES:(k + 1) * LANES], (LANES, LANES))
        out.append(blk.T[0:1, :])
    return jnp.concatenate(out, axis=0)


def _pos_kernel(route_ref, off_ref, coff_ref, pos_ref, cpos_ref):
    r = route_ref[...]
    lane = lax.broadcasted_iota(jnp.int32, r.shape, 1)
    mine = lane == r[:, 0:1].astype(jnp.int32)
    off = jnp.sum(jnp.where(mine, off_ref[...], 0.0), axis=-1, keepdims=True)
    coff = jnp.sum(jnp.where(mine, coff_ref[...], 0.0), axis=-1, keepdims=True)
    rank = r[:, 3:4]
    pos_ref[...] = _rows_to_lanes(off + rank).astype(jnp.int32)
    cpos_ref[...] = _rows_to_lanes(coff + rank).astype(jnp.int32)


def _sorted_pos(route, off, coff):
    t = route.shape[0]
    tm = SUBLANES * LANES
    shp = jax.ShapeDtypeStruct((t // LANES, LANES), jnp.int32)
    pos, cpos = pl.pallas_call(
        _pos_kernel,
        grid=(t // tm,),
        in_specs=[pl.BlockSpec((tm, ROUTE_W), lambda i: (i, 0)), _full((1, LANES)), _full((1, LANES))],
        out_specs=(pl.BlockSpec((SUBLANES, LANES), lambda i: (i, 0)), pl.BlockSpec((SUBLANES, LANES), lambda i: (i, 0))),
        out_shape=(shp, shp),
        compiler_params=_cparams(("parallel",)),
        name="sorted_pos",
    )(route, off, coff)
    return pos.reshape(t), cpos.reshape(t)


def _inv_kernel(pos_ref, idx_ref):
    def body(t, _):
        idx_ref[pos_ref[t]] = t
        return 0

    lax.fori_loop(0, pos_ref.shape[0], body, 0, unroll=8)


def _invert(pos):
    return pl.pallas_call(
        _inv_kernel,
        in_specs=[pl.BlockSpec(memory_space=pltpu.SMEM)],
        out_specs=pl.BlockSpec(memory_space=pltpu.SMEM),
        out_shape=jax.ShapeDtypeStruct(pos.shape, jnp.int32),
        name="invert_perm",
    )(pos)


def _bucket_kernel(idx_ref, tg_ref, ta_ref, tb_ref, cb_ref, nr_ref, hx_hbm, wi_ref, wd_ref,
                   ys_ref, buf, sem):
    j = pl.program_id(0)
    n_real = nr_ref[0]

    last = idx_ref.shape[0] - 1

    def issue_row(tile_base, slot, r, dst_row, prio):
        src = idx_ref[jnp.minimum(tile_base + r, last)]
        pltpu.make_async_copy(hx_hbm.at[pl.ds(pl.multiple_of(src * HX_ROWS, HX_ROWS), HX_ROWS), :],
                              buf.at[slot, pl.ds(dst_row, HX_ROWS), :], sem.at[slot]).start(priority=prio)

    def wait_tile(slot):
        pltpu.make_async_copy(hx_hbm.at[pl.ds(0, TM_EXP * HX_ROWS), :], buf.at[slot], sem.at[slot]).wait()

    @pl.when(j == 0)
    def _():
        base = cb_ref[0]

        def body(r8, _):
            for k in range(SUBLANES):
                r = r8 * SUBLANES + k
                issue_row(base, 0, r, pl.multiple_of(r * HX_ROWS, HX_ROWS), k % 2)
            return 0

        lax.fori_loop(0, TM_EXP // SUBLANES, body, 0)

    @pl.when(j < n_real)
    def _():
        slot = j % 2
        wait_tile(slot)
        nxt = jnp.minimum(j + 1, n_real - 1)
        base = cb_ref[nxt]
        for r in range(TM_EXP):
            issue_row(base, 1 - slot, r, r * HX_ROWS, r % 2)

        slab_row = lambda c: buf[slot, pl.ds(c, TM_EXP, stride=HX_ROWS), :]
        x = jnp.concatenate([slab_row(c) for c in range(ROW_TILES)], axis=1).astype(BF16)
        route = slab_row(ROW_TILES)

        def ffn(e):
            gu = jnp.dot(x, wi_ref[0, e], preferred_element_type=F32)
            a = jax.nn.silu(gu[:, :D_FF]) * gu[:, D_FF:]
            return jnp.dot(a.astype(BF16), wd_ref[0, e], preferred_element_type=F32)

        y = route[:, 1:2] * ffn(ta_ref[j]) + route[:, 2:3] * ffn(tb_ref[j])
        for c in range(ROW_TILES):
            ys_ref[pl.ds(c, TM_EXP, stride=ROW_TILES), :] = y[:, c * LANES:(c + 1) * LANES]

        @pl.when(j == n_real - 1)
        def _():
            wait_tile(1 - slot)


def _bucket_ffn(hx, idx, tile_g, tile_a, tile_b, tile_cb, n_real, p):
    n_tiles = tile_a.shape[0]
    grp = lambda j, idx, tg, ta, tb, cb, nr: (tg[j], 0, 0, 0)
    out = lambda j, idx, tg, ta, tb, cb, nr: (jnp.minimum(j, nr[0] - 1), 0)
    epg = EXPERTS_PER_GROUP
    return pl.pallas_call(
        _bucket_kernel,
        grid_spec=pltpu.PrefetchScalarGridSpec(
            num_scalar_prefetch=6,
            grid=(n_tiles,),
            in_specs=[
                pl.BlockSpec(memory_space=pl.ANY),
                pl.BlockSpec((1, epg, D_MODEL, 2 * D_FF), grp), pl.BlockSpec((1, epg, D_FF, D_MODEL), grp),
            ],
            out_specs=pl.BlockSpec((TM_EXP * ROW_TILES, LANES), out),
            scratch_shapes=[pltpu.VMEM((2, TM_EXP * HX_ROWS, LANES), F32), pltpu.SemaphoreType.DMA((2,))],
        ),
        out_shape=jax.ShapeDtypeStruct((n_tiles * TM_EXP * ROW_TILES, LANES), F32),
        compiler_params=_cparams(("arbitrary",)),
        name="bucket_ffn",
    )(idx, tile_g, tile_a, tile_b, tile_cb, n_real, hx,
      p["w_exp_in"].reshape(N_GROUPS_E, epg, D_MODEL, 2 * D_FF),
      p["w_exp_down"].reshape(N_GROUPS_E, epg, D_FF, D_MODEL))


def _back_kernel(pos_ref, x1_ref, ys_hbm, o_ref, buf, sem, *, t0):
    i = pl.program_id(0)
    tm = o_ref.shape[0]

    def issue(tile, slot):
        base = t0 + tile * tm

        def body(r8, _):
            for k in range(SUBLANES):
                r = r8 * SUBLANES + k
                src = pl.multiple_of(pos_ref[base + r] * ROW_TILES, ROW_TILES)
                pltpu.make_async_copy(ys_hbm.at[pl.ds(src, ROW_TILES), :],
                                      buf.at[slot, pl.ds(pl.multiple_of(r * ROW_TILES, ROW_TILES), ROW_TILES), :],
                                      sem.at[slot]).start(priority=k % 2)
            return 0

        lax.fori_loop(0, tm // SUBLANES, body, 0)

    @pl.when(i == 0)
    def _():
        issue(0, 0)

    @pl.when(i + 1 < pl.num_programs(0))
    def _():
        issue(i + 1, (i + 1) % 2)

    slot = i % 2
    pltpu.make_async_copy(ys_hbm.at[pl.ds(0, tm * ROW_TILES), :], buf.at[slot], sem.at[slot]).wait()
    y = jnp.concatenate([buf[slot, pl.ds(c, tm, stride=ROW_TILES), :] for c in range(ROW_TILES)], axis=1)
    o_ref[...] = x1_ref[...] + y


def _unsort_add(x1, ys, pos, t0, t):
    tm = TM_BACK
    row = lambda i, pos: (i, 0)
    return pl.pallas_call(
        functools.partial(_back_kernel, t0=t0),
        grid_spec=pltpu.PrefetchScalarGridSpec(
            num_scalar_prefetch=1,
            grid=(t // tm,),
            in_specs=[pl.BlockSpec((tm, D_MODEL), lambda i, pos: (i + t0 // tm, 0)),
                      pl.BlockSpec(memory_space=pl.ANY)],
            out_specs=pl.BlockSpec((tm, D_MODEL), row),
            scratch_shapes=[pltpu.VMEM((2, tm * ROW_TILES, LANES), F32), pltpu.SemaphoreType.DMA((2,))],
        ),
        out_shape=jax.ShapeDtypeStruct((t, D_MODEL), F32),
        compiler_params=_cparams(("arbitrary",)),
        name="unsort_add",
    )(pos, x1, ys)


def _bucket_experts():
    lo, hi = [], []
    for g in range(N_GROUPS_E):
        for a in range(EXPERTS_PER_GROUP):
            for b in range(a + 1, EXPERTS_PER_GROUP):
                lo.append(g * EXPERTS_PER_GROUP + a)
                hi.append(g * EXPERTS_PER_GROUP + b)
    return jnp.asarray(lo, jnp.int32), jnp.asarray(hi, jnp.int32)


def _tile_tables(counts, n_tiles):
    cnt = counts[0, :N_BUCKETS].astype(jnp.int32)
    nt = (cnt + TM_EXP - 1) // TM_EXP
    tend = jnp.cumsum(nt)
    tstart = tend - nt
    cstart = jnp.cumsum(cnt) - cnt
    pad = lambda v: jnp.zeros((1, LANES), F32).at[0, :N_BUCKETS].set(v.astype(F32))
    j = jnp.arange(n_tiles, dtype=jnp.int32)
    b = jnp.minimum(jnp.sum((tend[None, :] <= j[:, None]).astype(jnp.int32), axis=1), N_BUCKETS - 1)
    in_bucket = (j - tstart[b]) * TM_EXP
    e_lo, e_hi = _bucket_experts()
    epg = EXPERTS_PER_GROUP
    return (pad(tstart * TM_EXP), pad(cstart), b // PAIRS_PER_GROUP, e_lo[b] % epg, e_hi[b] % epg,
            cstart[b] + in_bucket, tend[-1:])


def _rope_tables(first_pos, n_pos):
    half = SWA_HD // 2
    inv = ROPE_THETA ** (-np.arange(half, dtype=np.float64) / half)
    ang = (first_pos + np.arange(n_pos, dtype=np.float64))[:, None] * inv[None, :]
    cos = np.cos(ang)
    sin = np.sin(ang)
    cos = np.concatenate([cos, cos, cos, cos], axis=1)
    sin = np.concatenate([-sin, sin, -sin, sin], axis=1)
    return jnp.asarray(cos, F32), jnp.asarray(sin, F32)


def kernel(x_prompt, x_sample, mem_prompt, state_ssm_re, state_ssm_im, cache_swa_k, cache_swa_v, cache_mem_k, cache_mem_v, norm_mix, w_in, ssm_a_re, ssm_a_im, ssm_log_dt, ssm_b_re, ssm_b_im, ssm_c_re, ssm_c_im, ssm_d, w_glu, swa_q_norm, swa_k_norm, swa_sinks, norm_mem, w_mem_kv, mem_q_norm, mem_k_norm, w_br_ssm, w_br_swa, w_br_mem, w_o, norm_ffn, w_router_group, b_router_group, w_router_expert, b_router_expert, w_exp_in, w_exp_down):
    depth = w_in.shape[0]
    assert depth == 1
    nb, seq, _ = x_prompt.shape
    db, dseq, _ = x_sample.shape
    assert dseq == 2 * LB
    l = 0

    w_r = jnp.concatenate([w_router_group[l], w_router_expert[l]], axis=1)
    w_r = jnp.pad(w_r, ((0, 0), (0, LANES - w_r.shape[1])))
    w_r_hi = w_r.astype(BF16)
    b_r = jnp.pad(jnp.concatenate([b_router_group[l], b_router_expert[l]]), (0, LANES - N_GROUPS_E - N_EXPERTS))
    p = dict(
        g_mix=norm_mix[l][None], w_in=w_in[l].astype(BF16),
        g_q=jnp.tile(swa_q_norm[l], SWA_HEADS)[None], g_k=jnp.tile(swa_k_norm[l], SWA_KV_HEADS)[None],
        g_qm=jnp.tile(mem_q_norm[l], MEM_HEADS)[None], g_km=jnp.tile(mem_k_norm[l], MEM_HEADS)[None],
        ones64=_block_ones(SWA_WIDTH, SWA_HD), ones128=_block_ones(MEM_WIDTH, MEM_HD),
        ssm_d=ssm_d[l][None], w_glu=w_glu[l].astype(BF16),
        g_mem=norm_mem[l][None], w_mem_kv=w_mem_kv[l].astype(BF16),
        w_br_ssm=w_br_ssm[l].astype(BF16), w_br_swa=w_br_swa[l].astype(BF16), w_br_mem=w_br_mem[l].astype(BF16),
        w_o=w_o[l].astype(BF16), g_ffn=norm_ffn[l][None],
        w_r_hi=w_r_hi, w_r_lo=(w_r - w_r_hi.astype(F32)).astype(BF16), b_r=b_r[None],
        w_exp_in=w_exp_in[l].astype(BF16), w_exp_down=w_exp_down[l].astype(BF16),
    )
    p.update(_ssm_params(ssm_a_re[l], ssm_a_im[l], ssm_log_dt[l], ssm_b_re[l], ssm_b_im[l],
                         ssm_c_re[l], ssm_c_im[l]))
    sinks = swa_sinks[l]

    xp = x_prompt.reshape(nb * seq, D_MODEL)
    cos_p, sin_p = _rope_tables(0, seq)
    u, q, k, v, qm = _inproj(xp, cos_p, sin_p, seq // TM_IN, p)
    zeros_state = jnp.zeros((nb, 1, SSM_CH), F32)
    y_ssm, pr, pi = _ssm(u, zeros_state, zeros_state, p, n_seq=nb, chained=True)
    mk, mv = _memkv(mem_prompt, p)
    y_swa, y_mem = _attn_prompt(q, k, v, qm, mk, mv, sinks, nb)
    win = min(WINDOW, seq)
    last_win = lambda a: a.reshape(nb, seq, SWA_KV_WIDTH)[:, seq - win:].reshape(nb, win, SWA_KV_HEADS, SWA_HD)
    p_k, p_v = last_win(k), last_win(v)

    xs = x_sample.reshape(db * dseq, D_MODEL)
    cos_s, sin_s = _rope_tables(PAST_LEN, dseq)
    reps = TM_IN // dseq
    us, qs, ks, vs, qms = _inproj(xs, jnp.tile(cos_s, (reps, 1)), jnp.tile(sin_s, (reps, 1)), 1, p)
    two_rows = lambda st: jnp.repeat(st.reshape(db, SSM_CH), 2, axis=0)
    ys_ssm, sr, si = _ssm(us, two_rows(state_ssm_re), two_rows(state_ssm_im), p, n_seq=db, chained=False)
    sr, si = sr[1::2], si[1::2]
    wb = cache_swa_k.shape[2]
    ys_swa, ys_mem, s_k, s_v = _attn_sample(
        qs, ks, vs, cache_swa_k[l].reshape(db, wb, SWA_KV_WIDTH), cache_swa_v[l].reshape(db, wb, SWA_KV_WIDTH),
        qms, cache_mem_k.reshape(db, N_MEM * MEM_HEADS, MEM_HD), cache_mem_v.reshape(db, N_MEM * MEM_HEADS, MEM_HD),
        sinks, dseq)
    t_p, t_s = nb * seq, db * dseq
    t_all = t_p + t_s
    x1, hx, route, counts = _merge((xp, y_ssm, y_swa, y_mem), (xs, ys_ssm, ys_swa, ys_mem), p)

    n_tiles = pl.cdiv(t_all, TM_EXP) + N_BUCKETS
    off, coff, tile_g, tile_a, tile_b, tile_cb, n_real = _tile_tables(counts, n_tiles)
    pos, cpos = _sorted_pos(route, off, coff)
    idx = _invert(cpos)
    y_sorted = _bucket_ffn(hx, idx, tile_g, tile_a, tile_b, tile_cb, n_real, p)
    yp = _unsort_add(x1, y_sorted, pos, 0, t_p).reshape(nb, seq, D_MODEL)
    ys = _unsort_add(x1, y_sorted, pos, t_p, t_s).reshape(db, dseq, D_MODEL)

    g, s = SSM_GROUPS, SSM_STATE
    return (yp, ys,
            pr.reshape(1, nb, g, s), pi.reshape(1, nb, g, s),
            p_k[None], p_v[None],
            mk.reshape(1, nb, N_MEM, MEM_HEADS, MEM_HD), mv.reshape(1, nb, N_MEM, MEM_HEADS, MEM_HD),
            sr.reshape(1, db, g, s), si.reshape(1, db, g, s),
            s_k.reshape(1, db, wb, SWA_KV_HEADS, SWA_HD), s_v.reshape(1, db, wb, SWA_KV_HEADS, SWA_HD))
```

```python
import functools

import jax
import jax.numpy as jnp
import numpy as np
from jax import lax
from jax.experimental import pallas as pl
from jax.experimental.pallas import tpu as pltpu

F32 = jnp.float32
BF16 = jnp.bfloat16

D_MODEL = 1024
SSM_WIDTH = 512
SSM_GROUP = 16
SSM_GROUPS = 32
SSM_STATE = 64
SSM_CH = SSM_GROUPS * SSM_STATE
LB = 4
OCT = 8
N_OCT = SSM_GROUPS // OCT
OCT_CH = OCT * SSM_STATE
SWA_HEADS = 8
SWA_KV_HEADS = 2
SWA_Q_PER_KV = SWA_HEADS // SWA_KV_HEADS
SWA_HD = 64
SWA_WIDTH = SWA_HEADS * SWA_HD
SWA_KV_WIDTH = SWA_KV_HEADS * SWA_HD
WINDOW = 128
PAST_LEN = 16384
ROPE_THETA = 10000.0
N_MEM = 256
MEM_HEADS = 4
MEM_HD = 128
MEM_WIDTH = MEM_HEADS * MEM_HD
N_BRANCH = 3
PROJ_A = SSM_WIDTH + SWA_WIDTH + 2 * SWA_KV_WIDTH + MEM_WIDTH
N_GROUPS_E = 4
EXPERTS_PER_GROUP = 8
N_EXPERTS = 32
D_FF = 256
EPS = 1e-6
NEG_INF = -1e30
SWA_SCALE = SWA_HD ** -0.5
MEM_SCALE = MEM_HD ** -0.5

LANES = 128
SUBLANES = 8
VMEM_LIMIT = 56 * 1024 * 1024

TM_IN = 512
TB_SSM = 512
SCAN_W = 512
TQ_ATT = 512
SEQ_BLK = 8
TM_MRG = 512
TM_BACK = 512
TM_EXP = 192

ROUTE_W = LANES
ROW_TILES = D_MODEL // LANES
HX_ROWS = 2 * ROW_TILES
PAIRS_PER_GROUP = EXPERTS_PER_GROUP * (EXPERTS_PER_GROUP - 1) // 2
N_BUCKETS = N_GROUPS_E * PAIRS_PER_GROUP


def _cparams(sem):
    return pltpu.CompilerParams(dimension_semantics=sem, vmem_limit_bytes=VMEM_LIMIT)


def _full(shape):
    nd = len(shape)
    return pl.BlockSpec(shape, lambda *_: (0,) * nd)


def _resident(shape):
    nd = len(shape)
    return pl.BlockSpec(shape, lambda *_: (0,) * nd, pipeline_mode=pl.Buffered(1))


def _split_bf16(x):
    hi = x.astype(BF16)
    lo = (x - hi.astype(F32)).astype(BF16)
    return hi, lo


def _seg_mean_sq(x, ones_blk, width):
    hi, lo = _split_bf16(x * x)
    s = jnp.dot(hi, ones_blk, preferred_element_type=F32) + jnp.dot(lo, ones_blk, preferred_element_type=F32)
    return s * (1.0 / width)


def _rms_rows(x, gain):
    return x * lax.rsqrt(jnp.mean(x * x, axis=-1, keepdims=True) + EPS) * gain


def _block_ones(n, width):
    i = jnp.arange(n) // width
    return (i[:, None] == i[None, :]).astype(BF16)


def _rope_cols(x, cos, sin_signed, lane_in_head):
    n = x.shape[1]
    reps = n // LANES
    if reps > 1:
        cos = jnp.concatenate([cos] * reps, axis=1)
        sin_signed = jnp.concatenate([sin_signed] * reps, axis=1)
    half = SWA_HD // 2
    partner = jnp.where(lane_in_head < half, pltpu.roll(x, n - half, axis=1), pltpu.roll(x, half, axis=1))
    return x * cos + partner * sin_signed


def _inproj_kernel(x_ref, gmix_ref, w_ref, gq_ref, gk_ref, gm_ref, cos_ref, sin_ref, o64_ref, o128_ref,
                   u_ref, q_ref, k_ref, v_ref, qm_ref, u_scr):
    x = x_ref[...]
    h = _rms_rows(x, gmix_ref[...]).astype(BF16)
    proj = jnp.dot(h, w_ref[...], preferred_element_type=F32)
    c0 = SSM_WIDTH
    c1 = c0 + SWA_WIDTH
    c2 = c1 + SWA_KV_WIDTH
    c3 = c2 + SWA_KV_WIDTH
    n_blk = u_ref.shape[0]
    for c in range(c0 // LANES):
        u_scr[c] = proj[:, c * LANES:(c + 1) * LANES]
        for t in range(LB):
            u_ref[:, t * c0 + c * LANES:t * c0 + (c + 1) * LANES] = u_scr[c, pl.ds(t, n_blk, stride=LB), :]
    q = proj[:, c0:c1]
    k = proj[:, c1:c2]
    v_ref[...] = proj[:, c2:c3]
    qm = proj[:, c3:]
    cos = cos_ref[...]
    sin = sin_ref[...]
    o64 = o64_ref[...]
    lane_q = lax.broadcasted_iota(jnp.int32, q.shape, 1) % SWA_HD
    qn = q * lax.rsqrt(_seg_mean_sq(q, o64, SWA_HD) + EPS) * gq_ref[...]
    q_ref[...] = (_rope_cols(qn, cos, sin, lane_q) * SWA_SCALE).astype(BF16)
    lane_k = lax.broadcasted_iota(jnp.int32, k.shape, 1) % SWA_HD
    kn = k * lax.rsqrt(_seg_mean_sq(k, o64[:SWA_KV_WIDTH, :SWA_KV_WIDTH], SWA_HD) + EPS) * gk_ref[...]
    k_ref[...] = _rope_cols(kn, cos, sin, lane_k)
    qmn = qm * lax.rsqrt(_seg_mean_sq(qm, o128_ref[...], MEM_HD) + EPS) * gm_ref[...]
    qm_ref[...] = qmn.astype(BF16)


def _inproj(x2d, cos, sin, pos_blocks, p):
    t = x2d.shape[0]
    tm = TM_IN
    grid = (t // tm,)
    row = lambda i: (i, 0)
    tab = lambda i: (i % pos_blocks, 0)
    out_shape = (
        jax.ShapeDtypeStruct((t // LB, LB * SSM_WIDTH), F32),
        jax.ShapeDtypeStruct((t, SWA_WIDTH), BF16),
        jax.ShapeDtypeStruct((t, SWA_KV_WIDTH), F32),
        jax.ShapeDtypeStruct((t, SWA_KV_WIDTH), F32),
        jax.ShapeDtypeStruct((t, MEM_WIDTH), BF16),
    )
    return pl.pallas_call(
        _inproj_kernel,
        grid=grid,
        in_specs=[
            pl.BlockSpec((tm, D_MODEL), row),
            _full((1, D_MODEL)),
            _resident((D_MODEL, PROJ_A)),
            _full((1, SWA_WIDTH)),
            _full((1, SWA_KV_WIDTH)),
            _full((1, MEM_WIDTH)),
            pl.BlockSpec((tm, LANES), tab),
            pl.BlockSpec((tm, LANES), tab),
            _full((SWA_WIDTH, SWA_WIDTH)),
            _full((MEM_WIDTH, MEM_WIDTH)),
        ],
        out_specs=(
            pl.BlockSpec((tm // LB, LB * SSM_WIDTH), row),
            pl.BlockSpec((tm, SWA_WIDTH), row),
            pl.BlockSpec((tm, SWA_KV_WIDTH), row),
            pl.BlockSpec((tm, SWA_KV_WIDTH), row),
            pl.BlockSpec((tm, MEM_WIDTH), row),
        ),
        out_shape=out_shape,
        scratch_shapes=[pltpu.VMEM((SSM_WIDTH // LANES, tm, LANES), F32)],
        compiler_params=_cparams(("parallel",)),
        name="inproj",
    )(x2d, p["g_mix"], p["w_in"], p["g_q"], p["g_k"], p["g_qm"], cos, sin, p["ones64"], p["ones128"])


def _cmul(x, y):
    return x[0] * y[0] - x[1] * y[1], x[0] * y[1] + x[1] * y[0]


def _ssm_kernel(u_ref, s0r_ref, s0i_ref, wx_ref, wt_ref, wc_ref, d_ref, wglu_ref, abr_ref, abi_ref,
                lvr_ref, lvi_ref, cpr_ref, cpi_ref,
                y_ref, fr_ref, fi_ref, sr_ref, si_ref, car_ref, cai_ref, o_scr, *, chained):
    tb = u_ref.shape[0]
    if chained:
        @pl.when(pl.program_id(1) == 0)
        def _():
            car_ref[...] = s0r_ref[0]
            cai_ref[...] = s0i_ref[0]

    u = u_ref[...]
    ub = u.astype(BF16)
    lhs = [jnp.concatenate([ub[:, t * SSM_WIDTH + c * LANES:t * SSM_WIDTH + (c + 1) * LANES] for t in range(LB)],
                           axis=1) for c in range(N_OCT)]
    for c in range(N_OCT):
        x = jnp.dot(lhs[c], wx_ref[c], preferred_element_type=F32)
        cs = slice(c * OCT_CH, (c + 1) * OCT_CH)
        sr_ref[:, cs] = x[:, :OCT_CH]
        si_ref[:, cs] = x[:, OCT_CH:]

    if chained:
        first_row = lax.broadcasted_iota(jnp.int32, (SUBLANES, SCAN_W), 0) == 0
        for sl in range(SSM_CH // SCAN_W):
            cols = slice(sl * SCAN_W, (sl + 1) * SCAN_W)
            lv = [(lvr_ref[j, :, cols], lvi_ref[j, :, cols]) for j in range(3)]
            cpr = cpr_ref[:, cols]
            cpi = cpi_ref[:, cols]

            def tile(i, carry, cols=cols, lv=lv, cpr=cpr, cpi=cpi):
                r0 = pl.multiple_of(i * SUBLANES, SUBLANES)
                xr = sr_ref[pl.ds(r0, SUBLANES), cols]
                xi = si_ref[pl.ds(r0, SUBLANES), cols]
                for j, d in enumerate((1, 2, 4)):
                    pr, pi = lv[j]
                    shr = pltpu.roll(xr, d, axis=0)
                    shi = pltpu.roll(xi, d, axis=0)
                    xr, xi = xr + pr * shr - pi * shi, xi + pr * shi + pi * shr
                cb_r = jnp.broadcast_to(carry[0], xr.shape)
                cb_i = jnp.broadcast_to(carry[1], xr.shape)
                xr, xi = xr + cpr * cb_r - cpi * cb_i, xi + cpr * cb_i + cpi * cb_r
                sr_ref[pl.ds(r0, SUBLANES), cols] = jnp.where(first_row, cb_r, pltpu.roll(xr, 1, axis=0))
                si_ref[pl.ds(r0, SUBLANES), cols] = jnp.where(first_row, cb_i, pltpu.roll(xi, 1, axis=0))
                return xr[SUBLANES - 1:SUBLANES, :], xi[SUBLANES - 1:SUBLANES, :]

            c_r, c_i = lax.fori_loop(0, tb // SUBLANES, tile, (car_ref[:, cols], cai_ref[:, cols]), unroll=True)
            car_ref[:, cols] = c_r
            cai_ref[:, cols] = c_i
        fr_ref[0] = car_ref[...]
        fi_ref[0] = cai_ref[...]
    else:
        odd = lax.broadcasted_iota(jnp.int32, (tb, SSM_CH), 0) % 2 == 1
        ab = (abr_ref[...], abi_ref[...])
        s0 = (s0r_ref[...], s0i_ref[...])
        x = (sr_ref[...], si_ref[...])
        e_first = _cmul(ab, s0)
        e_first = (e_first[0] + x[0], e_first[1] + x[1])
        prev = (pltpu.roll(e_first[0], 1, axis=0), pltpu.roll(e_first[1], 1, axis=0))
        e_second = _cmul(ab, prev)
        fr_ref[...] = jnp.where(odd, e_second[0] + x[0], e_first[0])
        fi_ref[...] = jnp.where(odd, e_second[1] + x[1], e_first[1])
        sr_ref[...] = jnp.where(odd, prev[0], s0[0])
        si_ref[...] = jnp.where(odd, prev[1], s0[1])

    ys = [[None] * N_OCT for _ in range(LB)]
    for c in range(N_OCT):
        cs = slice(c * OCT_CH, (c + 1) * OCT_CH)
        s_in = jnp.concatenate([sr_ref[:, cs], si_ref[:, cs]], axis=1).astype(BF16)
        yc = (jnp.dot(lhs[c], wt_ref[c], preferred_element_type=F32)
              + lax.dot_general(s_in, wc_ref[c], (((1,), (1,)), ((), ())), preferred_element_type=F32))
        for t in range(LB):
            ys[t][c] = yc[:, t * LANES:(t + 1) * LANES]
    y = jnp.concatenate([jnp.concatenate(ys[t], axis=1) for t in range(LB)], axis=0)
    us = jnp.concatenate([u[:, t * SSM_WIDTH:(t + 1) * SSM_WIDTH] for t in range(LB)], axis=0)
    y = jax.nn.gelu(y + d_ref[...] * us)
    gate = jax.nn.sigmoid(jnp.dot(y.astype(BF16), wglu_ref[...], preferred_element_type=F32))
    out = y * gate
    for c in range(SSM_WIDTH // LANES):
        for t in range(LB):
            o_scr[c, pl.ds(t, tb, stride=LB), :] = out[t * tb:(t + 1) * tb, c * LANES:(c + 1) * LANES]
        y_ref[:, c * LANES:(c + 1) * LANES] = o_scr[c]


def _ssm(u_blk, s0r, s0i, p, *, n_seq, chained):
    rows = u_blk.shape[0]
    if chained:
        tb = TB_SSM
        per = rows // n_seq // tb
        grid = (n_seq, per)
        row = lambda n, c: (n * per + c, 0)
        st = lambda n, c: (n, 0, 0)
        s0_spec = pl.BlockSpec((1, 1, SSM_CH), st)
        f_spec = pl.BlockSpec((1, 1, SSM_CH), st)
        f_shape = jax.ShapeDtypeStruct((n_seq, 1, SSM_CH), F32)
        sem = ("parallel", "arbitrary")
    else:
        tb = min(TB_SSM, rows)
        grid = (rows // tb,)
        row = lambda c: (c, 0)
        s0_spec = pl.BlockSpec((tb, SSM_CH), row)
        f_spec = pl.BlockSpec((tb, SSM_CH), row)
        f_shape = jax.ShapeDtypeStruct((rows, SSM_CH), F32)
        sem = ("parallel",)
    blk_w = LB * SSM_WIDTH
    return pl.pallas_call(
        functools.partial(_ssm_kernel, chained=chained),
        grid=grid,
        in_specs=[
            pl.BlockSpec((tb, blk_w), row),
            s0_spec, s0_spec,
            _resident((N_OCT, LB * LANES, 2 * OCT_CH)), _resident((N_OCT, LB * LANES, LB * LANES)),
            _resident((N_OCT, LB * LANES, 2 * OCT_CH)),
            _full((1, SSM_WIDTH)),
            _resident((SSM_WIDTH, SSM_WIDTH)),
            _full((1, SSM_CH)), _full((1, SSM_CH)),
            _full((3, SUBLANES, SSM_CH)), _full((3, SUBLANES, SSM_CH)),
            _full((SUBLANES, SSM_CH)), _full((SUBLANES, SSM_CH)),
        ],
        out_specs=(pl.BlockSpec((tb * LB, SSM_WIDTH), row), f_spec, f_spec),
        out_shape=(jax.ShapeDtypeStruct((rows * LB, SSM_WIDTH), F32), f_shape, f_shape),
        scratch_shapes=[
            pltpu.VMEM((tb, SSM_CH), F32), pltpu.VMEM((tb, SSM_CH), F32),
            pltpu.VMEM((1, SSM_CH), F32), pltpu.VMEM((1, SSM_CH), F32),
            pltpu.VMEM((SSM_WIDTH // LANES, tb * LB, LANES), F32),
        ],
        compiler_params=_cparams(sem),
        name="ssm_chained" if chained else "ssm_pairs",
    )(u_blk, s0r, s0i, p["w_x"], p["w_t"], p["w_c"], p["ssm_d"], p["w_glu"], p["ab_re"], p["ab_im"],
      p["lv_re"], p["lv_im"], p["cp_re"], p["cp_im"])


def _ssm_params(a_re, a_im, log_dt, b_re, b_im, c_re, c_im):
    dt = jnp.exp(log_dt)[:, None]
    mag = jnp.exp(a_re * dt)
    abr = mag * jnp.cos(a_im * dt)
    abi = mag * jnp.sin(a_im * dt)
    den = a_re * a_re + a_im * a_im
    nr = abr - 1.0
    ni = abi
    coef_re = (nr * a_re + ni * a_im) / den
    coef_im = (ni * a_re - nr * a_im) / den
    bb = (coef_re[..., None] * b_re - coef_im[..., None] * b_im,
          coef_re[..., None] * b_im + coef_im[..., None] * b_re)
    cc = (c_re, c_im)

    apow = [(jnp.ones_like(abr), jnp.zeros_like(abi))]
    for _ in range(LB):
        apow.append(_cmul(apow[-1], (abr, abi)))
    hp = lax.Precision.HIGHEST

    def group_diagonal(vals, col_group_width):
        n_col = vals.shape[-1]
        col_group = (jnp.arange(n_col) // col_group_width) % OCT
        keep = (col_group[None, :] == jnp.arange(OCT)[:, None]).astype(BF16)
        out = vals.astype(BF16)[:, :, None, :, :] * keep[None, None, :, None, :]
        return out.reshape(N_OCT, LB * OCT * SSM_GROUP, n_col)

    def octets_last(v):
        lead = v.shape[1:-3]
        v = v.reshape((LB,) + lead + (N_OCT, OCT, SSM_GROUP, v.shape[-1]))
        nl = len(lead)
        perm = (1 + nl, 0, 3 + nl) + tuple(range(1, 1 + nl)) + (2 + nl, 4 + nl)
        return v.transpose(perm).reshape(N_OCT, LB, SSM_GROUP, -1)

    bb_t = (bb[0].transpose(0, 2, 1), bb[1].transpose(0, 2, 1))
    mx = [jnp.stack(_cmul((apow[LB - 1 - t][0][:, None, :], apow[LB - 1 - t][1][:, None, :]), bb_t))
          for t in range(LB)]
    w_x = group_diagonal(octets_last(jnp.stack(mx)), SSM_STATE)

    mc = []
    for t in range(LB):
        m = _cmul(cc, (apow[t + 1][0][:, None, :], apow[t + 1][1][:, None, :]))
        mc.append(jnp.stack([m[0], -m[1]]))
    w_c = group_diagonal(octets_last(jnp.stack(mc)), SSM_STATE)

    kd = []
    for d in range(LB):
        m = _cmul(cc, (apow[d][0][:, None, :], apow[d][1][:, None, :]))
        kd.append(jnp.einsum("ghp,gpk->gkh", m[0], bb[0], precision=hp)
                  - jnp.einsum("ghp,gpk->gkh", m[1], bb[1], precision=hp))
    zero = jnp.zeros_like(kd[0])
    lagged = jnp.stack([jnp.stack([kd[t - t0] if t >= t0 else zero for t in range(LB)]) for t0 in range(LB)])
    w_t = group_diagonal(octets_last(lagged), SSM_GROUP)

    ab = (apow[LB][0].reshape(1, SSM_CH), apow[LB][1].reshape(1, SSM_CH))
    pows = [ab]
    for _ in range(SUBLANES - 1):
        pows.append(_cmul(pows[-1], ab))
    rows = jnp.arange(SUBLANES)[:, None]
    lv_re = jnp.stack([jnp.where(rows >= d, pows[d - 1][0], 0.0) for d in (1, 2, 4)])
    lv_im = jnp.stack([jnp.where(rows >= d, pows[d - 1][1], 0.0) for d in (1, 2, 4)])
    cp_re = jnp.concatenate([pw[0] for pw in pows], axis=0)
    cp_im = jnp.concatenate([pw[1] for pw in pows], axis=0)
    return dict(w_t=w_t.astype(BF16), w_x=w_x.astype(BF16), w_c=w_c.astype(BF16), ab_re=ab[0], ab_im=ab[1],
                lv_re=lv_re, lv_im=lv_im, cp_re=cp_re, cp_im=cp_im)


def _memkv_kernel(m_ref, g_ref, w_ref, gk_ref, o128_ref, k_ref, v_ref):
    hm = _rms_rows(m_ref[0], g_ref[...]).astype(BF16)
    kv = jnp.dot(hm, w_ref[...], preferred_element_type=F32)
    k = kv[:, :MEM_WIDTH]
    k_ref[0] = k * lax.rsqrt(_seg_mean_sq(k, o128_ref[...], MEM_HD) + EPS) * gk_ref[...]
    v_ref[0] = kv[:, MEM_WIDTH:]


def _memkv(mem, p):
    n = mem.shape[0]
    blk = lambda i: (i, 0, 0)
    shp = jax.ShapeDtypeStruct((n, N_MEM, MEM_WIDTH), F32)
    return pl.pallas_call(
        _memkv_kernel,
        grid=(n,),
        in_specs=[pl.BlockSpec((1, N_MEM, D_MODEL), blk), _full((1, D_MODEL)),
                  _full((D_MODEL, 2 * MEM_WIDTH)), _full((1, MEM_WIDTH)), _full((MEM_WIDTH, MEM_WIDTH))],
        out_specs=(pl.BlockSpec((1, N_MEM, MEM_WIDTH), blk), pl.BlockSpec((1, N_MEM, MEM_WIDTH), blk)),
        out_shape=(shp, shp),
        compiler_params=_cparams(("parallel",)),
        name="memkv",
    )(mem, p["g_mem"], p["w_mem_kv"], p["g_km"], p["ones128"])


def _dup_heads(x, lane):
    sw = pltpu.roll(x, SWA_HD, axis=x.ndim - 1)
    lo = lane < SWA_HD
    return jnp.where(lo, x, sw), jnp.where(lo, sw, x)


def _swa_group(q_blk, kk, vv, g, mask, sink_ref):
    tq = q_blk.shape[-2]
    shp = q_blk.shape[:-1]
    lane = lax.broadcasted_iota(jnp.int32, shp + (LANES,), len(shp))
    rows = []
    sinks = []
    for hl in range(SWA_Q_PER_KV):
        h = g * SWA_Q_PER_KV + hl
        pair = q_blk[..., (h // 2) * LANES:(h // 2 + 1) * LANES]
        keep = (lane < SWA_HD) if h % 2 == 0 else (lane >= SWA_HD)
        rows.append(jnp.where(keep, pair, 0.0))
        sinks.append(jnp.full(shp + (1,), sink_ref[h], F32))
    qq = jnp.concatenate(rows, axis=-2).astype(BF16)
    sk = jnp.concatenate(sinks, axis=-2)
    s = jnp.einsum("...qd,...kd->...qk", qq, kk, preferred_element_type=F32)
    s = jnp.where(mask, s, NEG_INF)
    m = jnp.maximum(jnp.max(s, axis=-1, keepdims=True), sk)
    e = jnp.exp(s - m)
    pr = e / (jnp.sum(e, axis=-1, keepdims=True) + jnp.exp(sk - m))
    o = jnp.einsum("...qk,...kd->...qd", pr.astype(BF16), vv, preferred_element_type=F32)
    lo = lane < SWA_HD
    return [jnp.where(lo, o[..., (2 * j) * tq:(2 * j + 1) * tq, :], o[..., (2 * j + 1) * tq:(2 * j + 2) * tq, :])
            for j in range(2)]


def _mem_heads(qm, k_head, v_head):
    outs = []
    for h in range(MEM_HEADS):
        cs = slice(h * MEM_HD, (h + 1) * MEM_HD)
        s = jnp.einsum("...qd,...kd->...qk", qm[..., cs], k_head(h).astype(BF16),
                       preferred_element_type=F32) * MEM_SCALE
        m = jnp.max(s, axis=-1, keepdims=True)
        e = jnp.exp(s - m)
        pr = e / jnp.sum(e, axis=-1, keepdims=True)
        outs.append(jnp.einsum("...qk,...kd->...qd", pr.astype(BF16), v_head(h).astype(BF16),
                               preferred_element_type=F32))
    return jnp.concatenate(outs, axis=-1)


def _attn_prompt_kernel(sink_ref, q_ref, k_ref, v_ref, kp_ref, vp_ref, qm_ref, mk_ref, mv_ref, ys_ref, ym_ref):
    tq = q_ref.shape[0]
    blk = WINDOW
    rows = SWA_Q_PER_KV * blk
    i = lax.broadcasted_iota(jnp.int32, (rows, 2 * blk), 0) % blk
    j = lax.broadcasted_iota(jnp.int32, (rows, 2 * blk), 1)
    lo = jnp.where(j < blk, i + 1, blk)
    hi = jnp.where(j < blk, blk, blk + i + 1)
    first_lo = jnp.where(pl.program_id(1) == 0, blk, 0)
    lane_k = lax.broadcasted_iota(jnp.int32, (2 * blk, LANES), 1)
    for b in range(tq // blk):
        rs = slice(b * blk, (b + 1) * blk)
        if b == 0:
            k2 = jnp.concatenate([kp_ref[...], k_ref[rs, :]], axis=0)
            v2 = jnp.concatenate([vp_ref[...], v_ref[rs, :]], axis=0)
            mask = (j >= jnp.maximum(lo, first_lo)) & (j < hi)
        else:
            k2 = k_ref[(b - 1) * blk:(b + 1) * blk, :]
            v2 = v_ref[(b - 1) * blk:(b + 1) * blk, :]
            mask = (j >= lo) & (j < hi)
        kks = _dup_heads(k2, lane_k)
        vvs = _dup_heads(v2, lane_k)
        q_blk = q_ref[rs, :].astype(F32)
        pairs = []
        for g in range(SWA_KV_HEADS):
            pairs += _swa_group(q_blk, kks[g].astype(BF16), vvs[g].astype(BF16), g, mask, sink_ref)
        ys_ref[rs, :] = jnp.concatenate(pairs, axis=1).astype(BF16)
    ym_ref[...] = _mem_heads(qm_ref[...], lambda h: mk_ref[0, :, h * MEM_HD:(h + 1) * MEM_HD],
                             lambda h: mv_ref[0, :, h * MEM_HD:(h + 1) * MEM_HD]).astype(BF16)


def _attn_prompt(q, k, v, qm, mk, mv, sinks, n_seq):
    t = q.shape[0]
    tq = TQ_ATT
    per = t // n_seq // tq
    sub = tq // WINDOW
    row = lambda n, c: (n * per + c, 0)
    prev = lambda n, c: (jnp.maximum((n * per + c) * sub - 1, 0), 0)
    memb = lambda n, c: (n, 0, 0)
    return pl.pallas_call(
        _attn_prompt_kernel,
        grid=(n_seq, per),
        in_specs=[
            pl.BlockSpec(memory_space=pltpu.SMEM),
            pl.BlockSpec((tq, SWA_WIDTH), row),
            pl.BlockSpec((tq, SWA_KV_WIDTH), row),
            pl.BlockSpec((tq, SWA_KV_WIDTH), row),
            pl.BlockSpec((WINDOW, SWA_KV_WIDTH), prev),
            pl.BlockSpec((WINDOW, SWA_KV_WIDTH), prev),
            pl.BlockSpec((tq, MEM_WIDTH), row),
            pl.BlockSpec((1, N_MEM, MEM_WIDTH), memb),
            pl.BlockSpec((1, N_MEM, MEM_WIDTH), memb),
        ],
        out_specs=(pl.BlockSpec((tq, SWA_WIDTH), row), pl.BlockSpec((tq, MEM_WIDTH), row)),
        out_shape=(jax.ShapeDtypeStruct((t, SWA_WIDTH), BF16), jax.ShapeDtypeStruct((t, MEM_WIDTH), BF16)),
        compiler_params=_cparams(("parallel", "parallel")),
        name="attn_prompt",
    )(sinks, q, k, v, k, v, qm, mk, mv)


def _attn_sample_kernel(sink_ref, q_ref, k_ref, v_ref, pk_ref, pv_ref, qm_ref, mk_ref, mv_ref,
                        ys_ref, ym_ref, nk_ref, nv_ref, *, s_len):
    sb, wb = pk_ref.shape[0], pk_ref.shape[1]
    n_keys = wb + s_len
    rows = SWA_Q_PER_KV * s_len
    i = lax.broadcasted_iota(jnp.int32, (sb, rows, n_keys), 1) % s_len
    j = lax.broadcasted_iota(jnp.int32, (sb, rows, n_keys), 2)
    rel = i + wb - j
    mask = (rel >= 0) & (rel < WINDOW)
    k_all = jnp.concatenate([pk_ref[...], k_ref[...].reshape(sb, s_len, SWA_KV_WIDTH)], axis=1)
    v_all = jnp.concatenate([pv_ref[...], v_ref[...].reshape(sb, s_len, SWA_KV_WIDTH)], axis=1)
    nk_ref[...] = k_all[:, n_keys - wb:, :]
    nv_ref[...] = v_all[:, n_keys - wb:, :]
    lane_k = lax.broadcasted_iota(jnp.int32, k_all.shape, 2)
    kks = _dup_heads(k_all, lane_k)
    vvs = _dup_heads(v_all, lane_k)
    q3 = q_ref[...].astype(F32).reshape(sb, s_len, SWA_WIDTH)
    pairs = []
    for g in range(SWA_KV_HEADS):
        pairs += _swa_group(q3, kks[g].astype(BF16), vvs[g].astype(BF16), g, mask, sink_ref)
    ys_ref[...] = jnp.concatenate(pairs, axis=-1).reshape(sb * s_len, SWA_WIDTH).astype(BF16)
    qm3 = qm_ref[...].astype(F32).reshape(sb, s_len, MEM_WIDTH).astype(BF16)
    head_rows = lambda h: pl.ds(h, N_MEM, stride=MEM_HEADS)
    ym = _mem_heads(qm3, lambda h: mk_ref[:, head_rows(h), :], lambda h: mv_ref[:, head_rows(h), :])
    ym_ref[...] = ym.reshape(sb * s_len, MEM_WIDTH).astype(BF16)


def _attn_sample(q, k, v, past_k, past_v, qm, mk, mv, sinks, s_len):
    t = q.shape[0]
    n_seq, wb = past_k.shape[0], past_k.shape[1]
    sb = SEQ_BLK
    rows = sb * s_len
    row = lambda c: (c, 0)
    seq = lambda c: (c, 0, 0)
    cache_shape = jax.ShapeDtypeStruct((n_seq, wb, SWA_KV_WIDTH), F32)
    return pl.pallas_call(
        functools.partial(_attn_sample_kernel, s_len=s_len),
        grid=(n_seq // sb,),
        in_specs=[
            pl.BlockSpec(memory_space=pltpu.SMEM),
            pl.BlockSpec((rows, SWA_WIDTH), row),
            pl.BlockSpec((rows, SWA_KV_WIDTH), row),
            pl.BlockSpec((rows, SWA_KV_WIDTH), row),
            pl.BlockSpec((sb, wb, SWA_KV_WIDTH), seq),
            pl.BlockSpec((sb, wb, SWA_KV_WIDTH), seq),
            pl.BlockSpec((rows, MEM_WIDTH), row),
            pl.BlockSpec((sb, N_MEM * MEM_HEADS, MEM_HD), seq),
            pl.BlockSpec((sb, N_MEM * MEM_HEADS, MEM_HD), seq),
        ],
        out_specs=(pl.BlockSpec((rows, SWA_WIDTH), row), pl.BlockSpec((rows, MEM_WIDTH), row),
                   pl.BlockSpec((sb, wb, SWA_KV_WIDTH), seq), pl.BlockSpec((sb, wb, SWA_KV_WIDTH), seq)),
        out_shape=(jax.ShapeDtypeStruct((t, SWA_WIDTH), BF16), jax.ShapeDtypeStruct((t, MEM_WIDTH), BF16),
                   cache_shape, cache_shape),
        compiler_params=_cparams(("parallel",)),
        name="attn_sample",
    )(sinks, q, k, v, past_k, past_v, qm, mk, mv)


def _first_argmax(x, valid, lane):
    xm = jnp.where(valid, x, -jnp.inf)
    mx = jnp.max(xm, axis=-1, keepdims=True)
    idx = jnp.min(jnp.where(xm == mx, lane, LANES), axis=-1, keepdims=True)
    return mx, lane == idx, idx


def _merge_kernel(xp_ref, yap_ref, ybp_ref, ycp_ref, xs_ref, yas_ref, ybs_ref, ycs_ref,
                  gmix_ref, wg_ref, wa_ref, wb_ref, wc_ref, wo_ref, gffn_ref, wr_hi_ref, wr_lo_ref, br_ref,
                  x1_ref, hx_ref, route_ref, cnt_ref, carry_ref, x1_prev, *, n_blk_p, n_blk):
    i = pl.program_id(0)

    @pl.when(i == 0)
    def _():
        x1_prev[...] = jnp.zeros_like(x1_prev)
        carry_ref[...] = jnp.zeros_like(carry_ref)

    x1_routed = x1_prev[...]

    is_prompt = jnp.minimum(i, n_blk - 1) < n_blk_p
    pick = lambda a_ref, b_ref: jnp.where(is_prompt, a_ref[...], b_ref[...])
    x = pick(xp_ref, xs_ref)
    h = _rms_rows(x, gmix_ref[...]).astype(BF16)

    def branch(k, y, w_ref):
        cols = slice(PROJ_A + k * D_MODEL, PROJ_A + (k + 1) * D_MODEL)
        gate = jax.nn.sigmoid(jnp.dot(h, wg_ref[:, cols], preferred_element_type=F32))
        return gate * jnp.dot(y.astype(BF16), w_ref[...], preferred_element_type=F32)

    routing = _route_rows(x1_routed, i >= 1, gffn_ref, wr_hi_ref, wr_lo_ref, br_ref, hx_ref, route_ref, cnt_ref,
                          carry_ref)
    next(routing)
    merged = branch(0, pick(yap_ref, yas_ref), wa_ref)
    next(routing)
    merged = merged + branch(1, pick(ybp_ref, ybs_ref), wb_ref)
    next(routing)
    merged = merged + branch(2, pick(ycp_ref, ycs_ref), wc_ref)
    next(routing, None)
    x1 = x + jnp.dot(merged.astype(BF16), wo_ref[...], preferred_element_type=F32)
    x1_ref[...] = x1
    x1_prev[...] = x1


def _route_rows(x1, valid, gffn_ref, wr_hi_ref, wr_lo_ref, br_ref, hx_ref, route_ref, cnt_ref, carry_ref):
    hn = _rms_rows(x1, gffn_ref[...])
    tm = x1.shape[0]
    slab_row = lambda c: pl.ds(c, tm, stride=HX_ROWS)
    for c in range(ROW_TILES):
        hx_ref[slab_row(c), :] = hn[:, c * LANES:(c + 1) * LANES]
    yield

    hi, lo = _split_bf16(hn)
    w_hi = wr_hi_ref[...]
    logits = (jnp.dot(hi, w_hi, preferred_element_type=F32) + jnp.dot(lo, w_hi, preferred_element_type=F32)
              + jnp.dot(hi, wr_lo_ref[...], preferred_element_type=F32)) + br_ref[...]
    yield
    lane = lax.broadcasted_iota(jnp.int32, logits.shape, 1)
    is_grp = lane < N_GROUPS_E
    g_max, _, g_idx = _first_argmax(logits, is_grp, lane)
    pg_top = 1.0 / jnp.sum(jnp.where(is_grp, jnp.exp(logits - g_max), 0.0), axis=-1, keepdims=True)
    e_lo = N_GROUPS_E + g_idx * EXPERTS_PER_GROUP
    in_grp = (lane >= e_lo) & (lane < e_lo + EXPERTS_PER_GROUP)
    e_max, first, i1 = _first_argmax(logits, in_grp, lane)
    ex = jnp.where(in_grp, jnp.exp(logits - e_max), 0.0)
    pe = ex / jnp.sum(ex, axis=-1, keepdims=True)
    _, second, i2 = _first_argmax(logits, in_grp & jnp.logical_not(first), lane)
    p1 = jnp.sum(jnp.where(first, pe, 0.0), axis=-1, keepdims=True)
    p2 = jnp.sum(jnp.where(second, pe, 0.0), axis=-1, keepdims=True)
    w1 = pg_top * p1 / (p1 + p2)
    w2 = pg_top * p2 / (p1 + p2)

    a1 = i1 - e_lo
    a2 = i2 - e_lo
    e_a = jnp.minimum(a1, a2)
    e_b = jnp.maximum(a1, a2)
    pair = jnp.right_shift(e_a * (2 * EXPERTS_PER_GROUP - 1 - e_a), 1) + (e_b - e_a - 1)
    bucket = g_idx * PAIRS_PER_GROUP + pair
    w_a = jnp.where(a1 < a2, w1, w2)
    w_b = jnp.where(a1 < a2, w2, w1)
    yield

    onehot = lane == jnp.where(valid, bucket, -1)
    tri = (lax.broadcasted_iota(jnp.int32, (tm, tm), 1) <= lax.broadcasted_iota(jnp.int32, (tm, tm), 0))
    csum = jnp.dot(jnp.where(tri, 1.0, 0.0).astype(BF16), jnp.where(onehot, 1.0, 0.0).astype(BF16),
                   preferred_element_type=F32)
    carry = carry_ref[...]
    rank = jnp.sum(jnp.where(onehot, csum + carry, 0.0), axis=-1, keepdims=True) - 1.0
    carry = carry + csum[tm - 1:tm, :]
    carry_ref[...] = carry
    cnt_ref[...] = carry
    route = jnp.where(lane == 0, bucket.astype(F32),
                      jnp.where(lane == 1, w_a, jnp.where(lane == 2, w_b, jnp.where(lane == 3, rank, 0.0))))
    route_ref[...] = route
    hx_ref[slab_row(ROW_TILES), :] = route
    for c in range(ROW_TILES + 1, HX_ROWS):
        hx_ref[slab_row(c), :] = jnp.zeros_like(route)


def _merge(prompt_rows, sample_rows, p):
    tm = TM_MRG
    t_p, t_s = prompt_rows[0].shape[0], sample_rows[0].shape[0]
    nbp = t_p // tm
    t_all = t_p + t_s
    n_blk = t_all // tm
    first = lambda i: (jnp.minimum(i, nbp - 1), 0)
    second = lambda i: (jnp.clip(i - nbp, 0, n_blk - nbp - 1), 0)
    merged_blk = lambda i: (jnp.minimum(i, n_blk - 1), 0)
    routed_blk = lambda i: (jnp.maximum(i - 1, 0), 0)
    widths = (D_MODEL, SSM_WIDTH, SWA_WIDTH, MEM_WIDTH)
    in_specs = ([pl.BlockSpec((tm, w), first) for w in widths] + [pl.BlockSpec((tm, w), second) for w in widths] + [
        _resident((1, D_MODEL)),
        _resident((D_MODEL, PROJ_A + N_BRANCH * D_MODEL)),
        _resident((SSM_WIDTH, D_MODEL)), _resident((SWA_WIDTH, D_MODEL)), _resident((MEM_WIDTH, D_MODEL)),
        _resident((D_MODEL, D_MODEL)),
        _resident((1, D_MODEL)),
        _resident((D_MODEL, LANES)), _resident((D_MODEL, LANES)), _resident((1, LANES)),
    ])
    return pl.pallas_call(
        functools.partial(_merge_kernel, n_blk_p=nbp, n_blk=n_blk),
        grid=(n_blk + 1,),
        in_specs=in_specs,
        out_specs=(pl.BlockSpec((tm, D_MODEL), merged_blk), pl.BlockSpec((tm * HX_ROWS, LANES), routed_blk),
                   pl.BlockSpec((tm, ROUTE_W), routed_blk), _full((1, LANES))),
        out_shape=(jax.ShapeDtypeStruct((t_all, D_MODEL), F32),
                   jax.ShapeDtypeStruct((t_all * HX_ROWS, LANES), F32),
                   jax.ShapeDtypeStruct((t_all, ROUTE_W), F32), jax.ShapeDtypeStruct((1, LANES), F32)),
        scratch_shapes=[pltpu.VMEM((1, LANES), F32), pltpu.VMEM((tm, D_MODEL), F32)],
        compiler_params=_cparams(("arbitrary",)),
        name="merge",
    )(*prompt_rows, *sample_rows, p["g_mix"], p["w_in"], p["w_br_ssm"], p["w_br_swa"], p["w_br_mem"], p["w_o"],
      p["g_ffn"], p["w_r_hi"], p["w_r_lo"], p["b_r"])


def _rows_to_lanes(col):
    out = []
    for k in range(col.shape[0] // LANES):
        blk = jnp.broadcast_to(col[k * LANES:(k + 1) * LANES], (LANES, LANES))
        out.append(blk.T[0:1, :])
    return jnp.concatenate(out, axis=0)


def _pos_kernel(route_ref, off_ref, coff_ref, pos_ref, cpos_ref):
    r = route_ref[...]
    lane = lax.broadcasted_iota(jnp.int32, r.shape, 1)
    mine = lane == r[:, 0:1].astype(jnp.int32)
    off = jnp.sum(jnp.where(mine, off_ref[...], 0.0), axis=-1, keepdims=True)
    coff = jnp.sum(jnp.where(mine, coff_ref[...], 0.0), axis=-1, keepdims=True)
    rank = r[:, 3:4]
    pos_ref[...] = _rows_to_lanes(off + rank).astype(jnp.int32)
    cpos_ref[...] = _rows_to_lanes(coff + rank).astype(jnp.int32)


def _sorted_pos(route, off, coff):
    t = route.shape[0]
    tm = SUBLANES * LANES
    shp = jax.ShapeDtypeStruct((t // LANES, LANES), jnp.int32)
    pos, cpos = pl.pallas_call(
        _pos_kernel,
        grid=(t // tm,),
        in_specs=[pl.BlockSpec((tm, ROUTE_W), lambda i: (i, 0)), _full((1, LANES)), _full((1, LANES))],
        out_specs=(pl.BlockSpec((SUBLANES, LANES), lambda i: (i, 0)), pl.BlockSpec((SUBLANES, LANES), lambda i: (i, 0))),
        out_shape=(shp, shp),
        compiler_params=_cparams(("parallel",)),
        name="sorted_pos",
    )(route, off, coff)
    return pos.reshape(t), cpos.reshape(t)


def _inv_kernel(pos_ref, idx_ref):
    def body(t, _):
        idx_ref[pos_ref[t]] = t
        return 0

    lax.fori_loop(0, pos_ref.shape[0], body, 0, unroll=8)


def _invert(pos):
    return pl.pallas_call(
        _inv_kernel,
        in_specs=[pl.BlockSpec(memory_space=pltpu.SMEM)],
        out_specs=pl.BlockSpec(memory_space=pltpu.SMEM),
        out_shape=jax.ShapeDtypeStruct(pos.shape, jnp.int32),
        name="invert_perm",
    )(pos)


def _bucket_kernel(idx_ref, tg_ref, ta_ref, tb_ref, cb_ref, nr_ref, hx_hbm, wi_ref, wd_ref,
                   ys_ref, buf, sem):
    j = pl.program_id(0)
    n_real = nr_ref[0]

    last = idx_ref.shape[0] - 1

    def issue_row(tile_base, slot, r, dst_row, prio):
        src = idx_ref[jnp.minimum(tile_base + r, last)]
        pltpu.make_async_copy(hx_hbm.at[pl.ds(pl.multiple_of(src * HX_ROWS, HX_ROWS), HX_ROWS), :],
                              buf.at[slot, pl.ds(dst_row, HX_ROWS), :], sem.at[slot]).start(priority=prio)

    def wait_tile(slot):
        pltpu.make_async_copy(hx_hbm.at[pl.ds(0, TM_EXP * HX_ROWS), :], buf.at[slot], sem.at[slot]).wait()

    @pl.when(j == 0)
    def _():
        base = cb_ref[0]

        def body(r8, _):
            for k in range(SUBLANES):
                r = r8 * SUBLANES + k
                issue_row(base, 0, r, pl.multiple_of(r * HX_ROWS, HX_ROWS), k % 2)
            return 0

        lax.fori_loop(0, TM_EXP // SUBLANES, body, 0)

    @pl.when(j < n_real)
    def _():
        slot = j % 2
        wait_tile(slot)
        nxt = jnp.minimum(j + 1, n_real - 1)
        base = cb_ref[nxt]
        for r in range(TM_EXP):
            issue_row(base, 1 - slot, r, r * HX_ROWS, r % 2)

        slab_row = lambda c: buf[slot, pl.ds(c, TM_EXP, stride=HX_ROWS), :]
        x = jnp.concatenate([slab_row(c) for c in range(ROW_TILES)], axis=1).astype(BF16)
        route = slab_row(ROW_TILES)

        def ffn(e):
            gu = jnp.dot(x, wi_ref[0, e], preferred_element_type=F32)
            a = jax.nn.silu(gu[:, :D_FF]) * gu[:, D_FF:]
            return jnp.dot(a.astype(BF16), wd_ref[0, e], preferred_element_type=F32)

        y = route[:, 1:2] * ffn(ta_ref[j]) + route[:, 2:3] * ffn(tb_ref[j])
        for c in range(ROW_TILES):
            ys_ref[pl.ds(c, TM_EXP, stride=ROW_TILES), :] = y[:, c * LANES:(c + 1) * LANES]

        @pl.when(j == n_real - 1)
        def _():
            wait_tile(1 - slot)


def _bucket_ffn(hx, idx, tile_g, tile_a, tile_b, tile_cb, n_real, p):
    n_tiles = tile_a.shape[0]
    grp = lambda j, idx, tg, ta, tb, cb, nr: (tg[j], 0, 0, 0)
    out = lambda j, idx, tg, ta, tb, cb, nr: (jnp.minimum(j, nr[0] - 1), 0)
    epg = EXPERTS_PER_GROUP
    return pl.pallas_call(
        _bucket_kernel,
        grid_spec=pltpu.PrefetchScalarGridSpec(
            num_scalar_prefetch=6,
            grid=(n_tiles,),
            in_specs=[
                pl.BlockSpec(memory_space=pl.ANY),
                pl.BlockSpec((1, epg, D_MODEL, 2 * D_FF), grp), pl.BlockSpec((1, epg, D_FF, D_MODEL), grp),
            ],
            out_specs=pl.BlockSpec((TM_EXP * ROW_TILES, LANES), out),
            scratch_shapes=[pltpu.VMEM((2, TM_EXP * HX_ROWS, LANES), F32), pltpu.SemaphoreType.DMA((2,))],
        ),
        out_shape=jax.ShapeDtypeStruct((n_tiles * TM_EXP * ROW_TILES, LANES), F32),
        compiler_params=_cparams(("arbitrary",)),
        name="bucket_ffn",
    )(idx, tile_g, tile_a, tile_b, tile_cb, n_real, hx,
      p["w_exp_in"].reshape(N_GROUPS_E, epg, D_MODEL, 2 * D_FF),
      p["w_exp_down"].reshape(N_GROUPS_E, epg, D_FF, D_MODEL))


def _back_kernel(pos_ref, x1_ref, ys_hbm, o_ref, buf, sem, *, t0):
    i = pl.program_id(0)
    tm = o_ref.shape[0]

    def issue(tile, slot):
        base = t0 + tile * tm

        def body(r8, _):
            for k in range(SUBLANES):
                r = r8 * SUBLANES + k
                src = pl.multiple_of(pos_ref[base + r] * ROW_TILES, ROW_TILES)
                pltpu.make_async_copy(ys_hbm.at[pl.ds(src, ROW_TILES), :],
                                      buf.at[slot, pl.ds(pl.multiple_of(r * ROW_TILES, ROW_TILES), ROW_TILES), :],
                                      sem.at[slot]).start(priority=k % 2)
            return 0

        lax.fori_loop(0, tm // SUBLANES, body, 0)

    @pl.when(i == 0)
    def _():
        issue(0, 0)

    @pl.when(i + 1 < pl.num_programs(0))
    def _():
        issue(i + 1, (i + 1) % 2)

    slot = i % 2
    pltpu.make_async_copy(ys_hbm.at[pl.ds(0, tm * ROW_TILES), :], buf.at[slot], sem.at[slot]).wait()
    y = jnp.concatenate([buf[slot, pl.ds(c, tm, stride=ROW_TILES), :] for c in range(ROW_TILES)], axis=1)
    o_ref[...] = x1_ref[...] + y


def _unsort_add(x1, ys, pos, t0, t):
    tm = TM_BACK
    row = lambda i, pos: (i, 0)
    return pl.pallas_call(
        functools.partial(_back_kernel, t0=t0),
        grid_spec=pltpu.PrefetchScalarGridSpec(
            num_scalar_prefetch=1,
            grid=(t // tm,),
            in_specs=[pl.BlockSpec((tm, D_MODEL), lambda i, pos: (i + t0 // tm, 0)),
                      pl.BlockSpec(memory_space=pl.ANY)],
            out_specs=pl.BlockSpec((tm, D_MODEL), row),
            scratch_shapes=[pltpu.VMEM((2, tm * ROW_TILES, LANES), F32), pltpu.SemaphoreType.DMA((2,))],
        ),
        out_shape=jax.ShapeDtypeStruct((t, D_MODEL), F32),
        compiler_params=_cparams(("arbitrary",)),
        name="unsort_add",
    )(pos, x1, ys)


def _bucket_experts():
    lo, hi = [], []
    for g in range(N_GROUPS_E):
        for a in range(EXPERTS_PER_GROUP):
            for b in range(a + 1, EXPERTS_PER_GROUP):
                lo.append(g * EXPERTS_PER_GROUP + a)
                hi.append(g * EXPERTS_PER_GROUP + b)
    return jnp.asarray(lo, jnp.int32), jnp.asarray(hi, jnp.int32)


def _tile_tables(counts, n_tiles):
    cnt = counts[0, :N_BUCKETS].astype(jnp.int32)
    nt = (cnt + TM_EXP - 1) // TM_EXP
    tend = jnp.cumsum(nt)
    tstart = tend - nt
    cstart = jnp.cumsum(cnt) - cnt
    pad = lambda v: jnp.zeros((1, LANES), F32).at[0, :N_BUCKETS].set(v.astype(F32))
    j = jnp.arange(n_tiles, dtype=jnp.int32)
    b = jnp.minimum(jnp.sum((tend[None, :] <= j[:, None]).astype(jnp.int32), axis=1), N_BUCKETS - 1)
    in_bucket = (j - tstart[b]) * TM_EXP
    e_lo, e_hi = _bucket_experts()
    epg = EXPERTS_PER_GROUP
    return (pad(tstart * TM_EXP), pad(cstart), b // PAIRS_PER_GROUP, e_lo[b] % epg, e_hi[b] % epg,
            cstart[b] + in_bucket, tend[-1:])


def _rope_tables(first_pos, n_pos):
    half = SWA_HD // 2
    inv = ROPE_THETA ** (-np.arange(half, dtype=np.float64) / half)
    ang = (first_pos + np.arange(n_pos, dtype=np.float64))[:, None] * inv[None, :]
    cos = np.cos(ang)
    sin = np.sin(ang)
    cos = np.concatenate([cos, cos, cos, cos], axis=1)
    sin = np.concatenate([-sin, sin, -sin, sin], axis=1)
    return jnp.asarray(cos, F32), jnp.asarray(sin, F32)


def kernel(x_prompt, x_sample, mem_prompt, state_ssm_re, state_ssm_im, cache_swa_k, cache_swa_v, cache_mem_k, cache_mem_v, norm_mix, w_in, ssm_a_re, ssm_a_im, ssm_log_dt, ssm_b_re, ssm_b_im, ssm_c_re, ssm_c_im, ssm_d, w_glu, swa_q_norm, swa_k_norm, swa_sinks, norm_mem, w_mem_kv, mem_q_norm, mem_k_norm, w_br_ssm, w_br_swa, w_br_mem, w_o, norm_ffn, w_router_group, b_router_group, w_router_expert, b_router_expert, w_exp_in, w_exp_down):
    depth = w_in.shape[0]
    assert depth == 1
    nb, seq, _ = x_prompt.shape
    db, dseq, _ = x_sample.shape
    assert dseq == 2 * LB
    l = 0

    w_r = jnp.concatenate([w_router_group[l], w_router_expert[l]], axis=1)
    w_r = jnp.pad(w_r, ((0, 0), (0, LANES - w_r.shape[1])))
    w_r_hi = w_r.astype(BF16)
    b_r = jnp.pad(jnp.concatenate([b_router_group[l], b_router_expert[l]]), (0, LANES - N_GROUPS_E - N_EXPERTS))
    p = dict(
        g_mix=norm_mix[l][None], w_in=w_in[l].astype(BF16),
        g_q=jnp.tile(swa_q_norm[l], SWA_HEADS)[None], g_k=jnp.tile(swa_k_norm[l], SWA_KV_HEADS)[None],
        g_qm=jnp.tile(mem_q_norm[l], MEM_HEADS)[None], g_km=jnp.tile(mem_k_norm[l], MEM_HEADS)[None],
        ones64=_block_ones(SWA_WIDTH, SWA_HD), ones128=_block_ones(MEM_WIDTH, MEM_HD),
        ssm_d=ssm_d[l][None], w_glu=w_glu[l].astype(BF16),
        g_mem=norm_mem[l][None], w_mem_kv=w_mem_kv[l].astype(BF16),
        w_br_ssm=w_br_ssm[l].astype(BF16), w_br_swa=w_br_swa[l].astype(BF16), w_br_mem=w_br_mem[l].astype(BF16),
        w_o=w_o[l].astype(BF16), g_ffn=norm_ffn[l][None],
        w_r_hi=w_r_hi, w_r_lo=(w_r - w_r_hi.astype(F32)).astype(BF16), b_r=b_r[None],
        w_exp_in=w_exp_in[l].astype(BF16), w_exp_down=w_exp_down[l].astype(BF16),
    )
    p.update(_ssm_params(ssm_a_re[l], ssm_a_im[l], ssm_log_dt[l], ssm_b_re[l], ssm_b_im[l],
                         ssm_c_re[l], ssm_c_im[l]))
    sinks = swa_sinks[l]

    xp = x_prompt.reshape(nb * seq, D_MODEL)
    cos_p, sin_p = _rope_tables(0, seq)
    u, q, k, v, qm = _inproj(xp, cos_p, sin_p, seq // TM_IN, p)
    zeros_state = jnp.zeros((nb, 1, SSM_CH), F32)
    y_ssm, pr, pi = _ssm(u, zeros_state, zeros_state, p, n_seq=nb, chained=True)
    mk, mv = _memkv(mem_prompt, p)
    y_swa, y_mem = _attn_prompt(q, k, v, qm, mk, mv, sinks, nb)
    win = min(WINDOW, seq)
    last_win = lambda a: a.reshape(nb, seq, SWA_KV_WIDTH)[:, seq - win:].reshape(nb, win, SWA_KV_HEADS, SWA_HD)
    p_k, p_v = last_win(k), last_win(v)

    xs = x_sample.reshape(db * dseq, D_MODEL)
    cos_s, sin_s = _rope_tables(PAST_LEN, dseq)
    reps = TM_IN // dseq
    us, qs, ks, vs, qms = _inproj(xs, jnp.tile(cos_s, (reps, 1)), jnp.tile(sin_s, (reps, 1)), 1, p)
    two_rows = lambda st: jnp.repeat(st.reshape(db, SSM_CH), 2, axis=0)
    ys_ssm, sr, si = _ssm(us, two_rows(state_ssm_re), two_rows(state_ssm_im), p, n_seq=db, chained=False)
    sr, si = sr[1::2], si[1::2]
    wb = cache_swa_k.shape[2]
    ys_swa, ys_mem, s_k, s_v = _attn_sample(
        qs, ks, vs, cache_swa_k[l].reshape(db, wb, SWA_KV_WIDTH), cache_swa_v[l].reshape(db, wb, SWA_KV_WIDTH),
        qms, cache_mem_k.reshape(db, N_MEM * MEM_HEADS, MEM_HD), cache_mem_v.reshape(db, N_MEM * MEM_HEADS, MEM_HD),
        sinks, dseq)
    t_p, t_s = nb * seq, db * dseq
    t_all = t_p + t_s
    x1, hx, route, counts = _merge((xp, y_ssm, y_swa, y_mem), (xs, ys_ssm, ys_swa, ys_mem), p)

    n_tiles = pl.cdiv(t_all, TM_EXP) + N_BUCKETS
    off, coff, tile_g, tile_a, tile_b, tile_cb, n_real = _tile_tables(counts, n_tiles)
    pos, cpos = _sorted_pos(route, off, coff)
    idx = _invert(cpos)
    y_sorted = _bucket_ffn(hx, idx, tile_g, tile_a, tile_b, tile_cb, n_real, p)
    yp = _unsort_add(x1, y_sorted, pos, 0, t_p).reshape(nb, seq, D_MODEL)
    ys = _unsort_add(x1, y_sorted, pos, t_p, t_s).reshape(db, dseq, D_MODEL)

    g, s = SSM_GROUPS, SSM_STATE
    return (yp, ys,
            pr.reshape(1, nb, g, s), pi.reshape(1, nb, g, s),
            p_k[None], p_v[None],
            mk.reshape(1, nb, N_MEM, MEM_HEADS, MEM_HD), mv.reshape(1, nb, N_MEM, MEM_HEADS, MEM_HD),
            sr.reshape(1, db, g, s), si.reshape(1, db, g, s),
            s_k.reshape(1, db, wb, SWA_KV_HEADS, SWA_HD), s_v.reshape(1, db, wb, SWA_KV_HEADS, SWA_HD))
```

```python
import functools

import jax
import jax.numpy as jnp
import numpy as np
from jax import lax
from jax.experimental import pallas as pl
from jax.experimental.pallas import tpu as pltpu

F32 = jnp.float32
BF16 = jnp.bfloat16

D_MODEL = 1024
SSM_WIDTH = 512
SSM_GROUP = 16
SSM_GROUPS = 32
SSM_STATE = 64
SSM_CH = SSM_GROUPS * SSM_STATE
LB = 4
OCT = 8
N_OCT = SSM_GROUPS // OCT
OCT_CH = OCT * SSM_STATE
SWA_HEADS = 8
SWA_KV_HEADS = 2
SWA_Q_PER_KV = SWA_HEADS // SWA_KV_HEADS
SWA_HD = 64
SWA_WIDTH = SWA_HEADS * SWA_HD
SWA_KV_WIDTH = SWA_KV_HEADS * SWA_HD
WINDOW = 128
PAST_LEN = 16384
ROPE_THETA = 10000.0
N_MEM = 256
MEM_HEADS = 4
MEM_HD = 128
MEM_WIDTH = MEM_HEADS * MEM_HD
N_BRANCH = 3
PROJ_A = SSM_WIDTH + SWA_WIDTH + 2 * SWA_KV_WIDTH + MEM_WIDTH
N_GROUPS_E = 4
EXPERTS_PER_GROUP = 8
N_EXPERTS = 32
D_FF = 256
EPS = 1e-6
NEG_INF = -1e30
SWA_SCALE = SWA_HD ** -0.5
MEM_SCALE = MEM_HD ** -0.5

LANES = 128
SUBLANES = 8
VMEM_LIMIT = 56 * 1024 * 1024

TM_IN = 512
TB_SSM = 512
SCAN_W = 512
TQ_ATT = 512
SEQ_BLK = 16
TM_MRG = 512
TM_BACK = 512
TM_EXP = 192

ROUTE_W = LANES
ROW_TILES = D_MODEL // LANES
HX_ROWS = 2 * ROW_TILES
PAIRS_PER_GROUP = EXPERTS_PER_GROUP * (EXPERTS_PER_GROUP - 1) // 2
N_BUCKETS = N_GROUPS_E * PAIRS_PER_GROUP


def _cparams(sem):
    return pltpu.CompilerParams(dimension_semantics=sem, vmem_limit_bytes=VMEM_LIMIT)


def _full(shape):
    nd = len(shape)
    return pl.BlockSpec(shape, lambda *_: (0,) * nd)


def _resident(shape):
    nd = len(shape)
    return pl.BlockSpec(shape, lambda *_: (0,) * nd, pipeline_mode=pl.Buffered(1))


def _split_bf16(x):
    hi = x.astype(BF16)
    lo = (x - hi.astype(F32)).astype(BF16)
    return hi, lo


def _seg_mean_sq(x, ones_blk, width):
    hi, lo = _split_bf16(x * x)
    s = jnp.dot(hi, ones_blk, preferred_element_type=F32) + jnp.dot(lo, ones_blk, preferred_element_type=F32)
    return s * (1.0 / width)


def _rms_rows(x, gain):
    return x * lax.rsqrt(jnp.mean(x * x, axis=-1, keepdims=True) + EPS) * gain


def _block_ones(n, width):
    i = jnp.arange(n) // width
    return (i[:, None] == i[None, :]).astype(BF16)


def _rope_cols(x, cos, sin_signed, lane_in_head):
    n = x.shape[1]
    reps = n // LANES
    if reps > 1:
        cos = jnp.concatenate([cos] * reps, axis=1)
        sin_signed = jnp.concatenate([sin_signed] * reps, axis=1)
    half = SWA_HD // 2
    partner = jnp.where(lane_in_head < half, pltpu.roll(x, n - half, axis=1), pltpu.roll(x, half, axis=1))
    return x * cos + partner * sin_signed


def _inproj_kernel(x_ref, gmix_ref, w_ref, gq_ref, gk_ref, gm_ref, cos_ref, sin_ref, o64_ref, o128_ref,
                   u_ref, q_ref, k_ref, v_ref, qm_ref, u_scr):
    x = x_ref[...]
    h = _rms_rows(x, gmix_ref[...]).astype(BF16)
    proj = jnp.dot(h, w_ref[...], preferred_element_type=F32)
    c0 = SSM_WIDTH
    c1 = c0 + SWA_WIDTH
    c2 = c1 + SWA_KV_WIDTH
    c3 = c2 + SWA_KV_WIDTH
    n_blk = u_ref.shape[0]
    for c in range(c0 // LANES):
        u_scr[c] = proj[:, c * LANES:(c + 1) * LANES]
        for t in range(LB):
            u_ref[:, t * c0 + c * LANES:t * c0 + (c + 1) * LANES] = u_scr[c, pl.ds(t, n_blk, stride=LB), :]
    q = proj[:, c0:c1]
    k = proj[:, c1:c2]
    v_ref[...] = proj[:, c2:c3]
    qm = proj[:, c3:]
    cos = cos_ref[...]
    sin = sin_ref[...]
    o64 = o64_ref[...]
    lane_q = lax.broadcasted_iota(jnp.int32, q.shape, 1) % SWA_HD
    qn = q * lax.rsqrt(_seg_mean_sq(q, o64, SWA_HD) + EPS) * gq_ref[...]
    q_ref[...] = (_rope_cols(qn, cos, sin, lane_q) * SWA_SCALE).astype(BF16)
    lane_k = lax.broadcasted_iota(jnp.int32, k.shape, 1) % SWA_HD
    kn = k * lax.rsqrt(_seg_mean_sq(k, o64[:SWA_KV_WIDTH, :SWA_KV_WIDTH], SWA_HD) + EPS) * gk_ref[...]
    k_ref[...] = _rope_cols(kn, cos, sin, lane_k)
    qmn = qm * lax.rsqrt(_seg_mean_sq(qm, o128_ref[...], MEM_HD) + EPS) * gm_ref[...]
    qm_ref[...] = qmn.astype(BF16)


def _inproj(x2d, cos, sin, pos_blocks, p):
    t = x2d.shape[0]
    tm = TM_IN
    grid = (t // tm,)
    row = lambda i: (i, 0)
    tab = lambda i: (i % pos_blocks, 0)
    out_shape = (
        jax.ShapeDtypeStruct((t // LB, LB * SSM_WIDTH), F32),
        jax.ShapeDtypeStruct((t, SWA_WIDTH), BF16),
        jax.ShapeDtypeStruct((t, SWA_KV_WIDTH), F32),
        jax.ShapeDtypeStruct((t, SWA_KV_WIDTH), F32),
        jax.ShapeDtypeStruct((t, MEM_WIDTH), BF16),
    )
    return pl.pallas_call(
        _inproj_kernel,
        grid=grid,
        in_specs=[
            pl.BlockSpec((tm, D_MODEL), row),
            _full((1, D_MODEL)),
            _resident((D_MODEL, PROJ_A)),
            _full((1, SWA_WIDTH)),
            _full((1, SWA_KV_WIDTH)),
            _full((1, MEM_WIDTH)),
            pl.BlockSpec((tm, LANES), tab),
            pl.BlockSpec((tm, LANES), tab),
            _full((SWA_WIDTH, SWA_WIDTH)),
            _full((MEM_WIDTH, MEM_WIDTH)),
        ],
        out_specs=(
            pl.BlockSpec((tm // LB, LB * SSM_WIDTH), row),
            pl.BlockSpec((tm, SWA_WIDTH), row),
            pl.BlockSpec((tm, SWA_KV_WIDTH), row),
            pl.BlockSpec((tm, SWA_KV_WIDTH), row),
            pl.BlockSpec((tm, MEM_WIDTH), row),
        ),
        out_shape=out_shape,
        scratch_shapes=[pltpu.VMEM((SSM_WIDTH // LANES, tm, LANES), F32)],
        compiler_params=_cparams(("parallel",)),
        name="inproj",
    )(x2d, p["g_mix"], p["w_in"], p["g_q"], p["g_k"], p["g_qm"], cos, sin, p["ones64"], p["ones128"])


def _cmul(x, y):
    return x[0] * y[0] - x[1] * y[1], x[0] * y[1] + x[1] * y[0]


def _ssm_kernel(u_ref, s0r_ref, s0i_ref, wx_ref, wt_ref, wc_ref, d_ref, wglu_ref, abr_ref, abi_ref,
                lvr_ref, lvi_ref, cpr_ref, cpi_ref,
                y_ref, fr_ref, fi_ref, sr_ref, si_ref, car_ref, cai_ref, o_scr, *, chained):
    tb = u_ref.shape[0]
    if chained:
        @pl.when(pl.program_id(1) == 0)
        def _():
            car_ref[...] = s0r_ref[0]
            cai_ref[...] = s0i_ref[0]

    u = u_ref[...]
    ub = u.astype(BF16)
    lhs = [jnp.concatenate([ub[:, t * SSM_WIDTH + c * LANES:t * SSM_WIDTH + (c + 1) * LANES] for t in range(LB)],
                           axis=1) for c in range(N_OCT)]
    for c in range(N_OCT):
        x = jnp.dot(lhs[c], wx_ref[c], preferred_element_type=F32)
        cs = slice(c * OCT_CH, (c + 1) * OCT_CH)
        sr_ref[:, cs] = x[:, :OCT_CH]
        si_ref[:, cs] = x[:, OCT_CH:]

    if chained:
        first_row = lax.broadcasted_iota(jnp.int32, (SUBLANES, SCAN_W), 0) == 0
        for sl in range(SSM_CH // SCAN_W):
            cols = slice(sl * SCAN_W, (sl + 1) * SCAN_W)
            lv = [(lvr_ref[j, :, cols], lvi_ref[j, :, cols]) for j in range(3)]
            cpr = cpr_ref[:, cols]
            cpi = cpi_ref[:, cols]

            def tile(i, carry, cols=cols, lv=lv, cpr=cpr, cpi=cpi):
                r0 = pl.multiple_of(i * SUBLANES, SUBLANES)
                xr = sr_ref[pl.ds(r0, SUBLANES), cols]
                xi = si_ref[pl.ds(r0, SUBLANES), cols]
                for j, d in enumerate((1, 2, 4)):
                    pr, pi = lv[j]
                    shr = pltpu.roll(xr, d, axis=0)
                    shi = pltpu.roll(xi, d, axis=0)
                    xr, xi = xr + pr * shr - pi * shi, xi + pr * shi + pi * shr
                cb_r = jnp.broadcast_to(carry[0], xr.shape)
                cb_i = jnp.broadcast_to(carry[1], xr.shape)
                xr, xi = xr + cpr * cb_r - cpi * cb_i, xi + cpr * cb_i + cpi * cb_r
                sr_ref[pl.ds(r0, SUBLANES), cols] = jnp.where(first_row, cb_r, pltpu.roll(xr, 1, axis=0))
                si_ref[pl.ds(r0, SUBLANES), cols] = jnp.where(first_row, cb_i, pltpu.roll(xi, 1, axis=0))
                return xr[SUBLANES - 1:SUBLANES, :], xi[SUBLANES - 1:SUBLANES, :]

            c_r, c_i = lax.fori_loop(0, tb // SUBLANES, tile, (car_ref[:, cols], cai_ref[:, cols]), unroll=True)
            car_ref[:, cols] = c_r
            cai_ref[:, cols] = c_i
        fr_ref[0] = car_ref[...]
        fi_ref[0] = cai_ref[...]
    else:
        odd = lax.broadcasted_iota(jnp.int32, (tb, SSM_CH), 0) % 2 == 1
        ab = (abr_ref[...], abi_ref[...])
        s0 = (s0r_ref[...], s0i_ref[...])
        x = (sr_ref[...], si_ref[...])
        e_first = _cmul(ab, s0)
        e_first = (e_first[0] + x[0], e_first[1] + x[1])
        prev = (pltpu.roll(e_first[0], 1, axis=0), pltpu.roll(e_first[1], 1, axis=0))
        e_second = _cmul(ab, prev)
        fr_ref[...] = jnp.where(odd, e_second[0] + x[0], e_first[0])
        fi_ref[...] = jnp.where(odd, e_second[1] + x[1], e_first[1])
        sr_ref[...] = jnp.where(odd, prev[0], s0[0])
        si_ref[...] = jnp.where(odd, prev[1], s0[1])

    ys = [[None] * N_OCT for _ in range(LB)]
    for c in range(N_OCT):
        cs = slice(c * OCT_CH, (c + 1) * OCT_CH)
        s_in = jnp.concatenate([sr_ref[:, cs], si_ref[:, cs]], axis=1).astype(BF16)
        yc = (jnp.dot(lhs[c], wt_ref[c], preferred_element_type=F32)
              + lax.dot_general(s_in, wc_ref[c], (((1,), (1,)), ((), ())), preferred_element_type=F32))
        for t in range(LB):
            ys[t][c] = yc[:, t * LANES:(t + 1) * LANES]
    y = jnp.concatenate([jnp.concatenate(ys[t], axis=1) for t in range(LB)], axis=0)
    us = jnp.concatenate([u[:, t * SSM_WIDTH:(t + 1) * SSM_WIDTH] for t in range(LB)], axis=0)
    y = jax.nn.gelu(y + d_ref[...] * us)
    gate = jax.nn.sigmoid(jnp.dot(y.astype(BF16), wglu_ref[...], preferred_element_type=F32))
    out = y * gate
    for c in range(SSM_WIDTH // LANES):
        for t in range(LB):
            o_scr[c, pl.ds(t, tb, stride=LB), :] = out[t * tb:(t + 1) * tb, c * LANES:(c + 1) * LANES]
        y_ref[:, c * LANES:(c + 1) * LANES] = o_scr[c]


def _ssm(u_blk, s0r, s0i, p, *, n_seq, chained):
    rows = u_blk.shape[0]
    if chained:
        tb = TB_SSM
        per = rows // n_seq // tb
        grid = (n_seq, per)
        row = lambda n, c: (n * per + c, 0)
        st = lambda n, c: (n, 0, 0)
        s0_spec = pl.BlockSpec((1, 1, SSM_CH), st)
        f_spec = pl.BlockSpec((1, 1, SSM_CH), st)
        f_shape = jax.ShapeDtypeStruct((n_seq, 1, SSM_CH), F32)
        sem = ("parallel", "arbitrary")
    else:
        tb = min(TB_SSM, rows)
        grid = (rows // tb,)
        row = lambda c: (c, 0)
        s0_spec = pl.BlockSpec((tb, SSM_CH), row)
        f_spec = pl.BlockSpec((tb, SSM_CH), row)
        f_shape = jax.ShapeDtypeStruct((rows, SSM_CH), F32)
        sem = ("parallel",)
    blk_w = LB * SSM_WIDTH
    return pl.pallas_call(
        functools.partial(_ssm_kernel, chained=chained),
        grid=grid,
        in_specs=[
            pl.BlockSpec((tb, blk_w), row),
            s0_spec, s0_spec,
            _resident((N_OCT, LB * LANES, 2 * OCT_CH)), _resident((N_OCT, LB * LANES, LB * LANES)),
            _resident((N_OCT, LB * LANES, 2 * OCT_CH)),
            _full((1, SSM_WIDTH)),
            _resident((SSM_WIDTH, SSM_WIDTH)),
            _full((1, SSM_CH)), _full((1, SSM_CH)),
            _full((3, SUBLANES, SSM_CH)), _full((3, SUBLANES, SSM_CH)),
            _full((SUBLANES, SSM_CH)), _full((SUBLANES, SSM_CH)),
        ],
        out_specs=(pl.BlockSpec((tb * LB, SSM_WIDTH), row), f_spec, f_spec),
        out_shape=(jax.ShapeDtypeStruct((rows * LB, SSM_WIDTH), F32), f_shape, f_shape),
        scratch_shapes=[
            pltpu.VMEM((tb, SSM_CH), F32), pltpu.VMEM((tb, SSM_CH), F32),
            pltpu.VMEM((1, SSM_CH), F32), pltpu.VMEM((1, SSM_CH), F32),
            pltpu.VMEM((SSM_WIDTH // LANES, tb * LB, LANES), F32),
        ],
        compiler_params=_cparams(sem),
        name="ssm_chained" if chained else "ssm_pairs",
    )(u_blk, s0r, s0i, p["w_x"], p["w_t"], p["w_c"], p["ssm_d"], p["w_glu"], p["ab_re"], p["ab_im"],
      p["lv_re"], p["lv_im"], p["cp_re"], p["cp_im"])


def _ssm_params(a_re, a_im, log_dt, b_re, b_im, c_re, c_im):
    dt = jnp.exp(log_dt)[:, None]
    mag = jnp.exp(a_re * dt)
    abr = mag * jnp.cos(a_im * dt)
    abi = mag * jnp.sin(a_im * dt)
    den = a_re * a_re + a_im * a_im
    nr = abr - 1.0
    ni = abi
    coef_re = (nr * a_re + ni * a_im) / den
    coef_im = (ni * a_re - nr * a_im) / den
    bb = (coef_re[..., None] * b_re - coef_im[..., None] * b_im,
          coef_re[..., None] * b_im + coef_im[..., None] * b_re)
    cc = (c_re, c_im)

    apow = [(jnp.ones_like(abr), jnp.zeros_like(abi))]
    for _ in range(LB):
        apow.append(_cmul(apow[-1], (abr, abi)))
    hp = lax.Precision.HIGHEST

    def group_diagonal(vals, col_group_width):
        n_col = vals.shape[-1]
        col_group = (jnp.arange(n_col) // col_group_width) % OCT
        keep = (col_group[None, :] == jnp.arange(OCT)[:, None]).astype(BF16)
        out = vals.astype(BF16)[:, :, None, :, :] * keep[None, None, :, None, :]
        return out.reshape(N_OCT, LB * OCT * SSM_GROUP, n_col)

    def octets_last(v):
        lead = v.shape[1:-3]
        v = v.reshape((LB,) + lead + (N_OCT, OCT, SSM_GROUP, v.shape[-1]))
        nl = len(lead)
        perm = (1 + nl, 0, 3 + nl) + tuple(range(1, 1 + nl)) + (2 + nl, 4 + nl)
        return v.transpose(perm).reshape(N_OCT, LB, SSM_GROUP, -1)

    bb_t = (bb[0].transpose(0, 2, 1), bb[1].transpose(0, 2, 1))
    mx = [jnp.stack(_cmul((apow[LB - 1 - t][0][:, None, :], apow[LB - 1 - t][1][:, None, :]), bb_t))
          for t in range(LB)]
    w_x = group_diagonal(octets_last(jnp.stack(mx)), SSM_STATE)

    mc = []
    for t in range(LB):
        m = _cmul(cc, (apow[t + 1][0][:, None, :], apow[t + 1][1][:, None, :]))
        mc.append(jnp.stack([m[0], -m[1]]))
    w_c = group_diagonal(octets_last(jnp.stack(mc)), SSM_STATE)

    kd = []
    for d in range(LB):
        m = _cmul(cc, (apow[d][0][:, None, :], apow[d][1][:, None, :]))
        kd.append(jnp.einsum("ghp,gpk->gkh", m[0], bb[0], precision=hp)
                  - jnp.einsum("ghp,gpk->gkh", m[1], bb[1], precision=hp))
    zero = jnp.zeros_like(kd[0])
    lagged = jnp.stack([jnp.stack([kd[t - t0] if t >= t0 else zero for t in range(LB)]) for t0 in range(LB)])
    w_t = group_diagonal(octets_last(lagged), SSM_GROUP)

    ab = (apow[LB][0].reshape(1, SSM_CH), apow[LB][1].reshape(1, SSM_CH))
    pows = [ab]
    for _ in range(SUBLANES - 1):
        pows.append(_cmul(pows[-1], ab))
    rows = jnp.arange(SUBLANES)[:, None]
    lv_re = jnp.stack([jnp.where(rows >= d, pows[d - 1][0], 0.0) for d in (1, 2, 4)])
    lv_im = jnp.stack([jnp.where(rows >= d, pows[d - 1][1], 0.0) for d in (1, 2, 4)])
    cp_re = jnp.concatenate([pw[0] for pw in pows], axis=0)
    cp_im = jnp.concatenate([pw[1] for pw in pows], axis=0)
    return dict(w_t=w_t.astype(BF16), w_x=w_x.astype(BF16), w_c=w_c.astype(BF16), ab_re=ab[0], ab_im=ab[1],
                lv_re=lv_re, lv_im=lv_im, cp_re=cp_re, cp_im=cp_im)


def _memkv_kernel(m_ref, g_ref, w_ref, gk_ref, o128_ref, k_ref, v_ref):
    hm = _rms_rows(m_ref[0], g_ref[...]).astype(BF16)
    kv = jnp.dot(hm, w_ref[...], preferred_element_type=F32)
    k = kv[:, :MEM_WIDTH]
    k_ref[0] = k * lax.rsqrt(_seg_mean_sq(k, o128_ref[...], MEM_HD) + EPS) * gk_ref[...]
    v_ref[0] = kv[:, MEM_WIDTH:]


def _memkv(mem, p):
    n = mem.shape[0]
    blk = lambda i: (i, 0, 0)
    shp = jax.ShapeDtypeStruct((n, N_MEM, MEM_WIDTH), F32)
    return pl.pallas_call(
        _memkv_kernel,
        grid=(n,),
        in_specs=[pl.BlockSpec((1, N_MEM, D_MODEL), blk), _full((1, D_MODEL)),
                  _full((D_MODEL, 2 * MEM_WIDTH)), _full((1, MEM_WIDTH)), _full((MEM_WIDTH, MEM_WIDTH))],
        out_specs=(pl.BlockSpec((1, N_MEM, MEM_WIDTH), blk), pl.BlockSpec((1, N_MEM, MEM_WIDTH), blk)),
        out_shape=(shp, shp),
        compiler_params=_cparams(("parallel",)),
        name="memkv",
    )(mem, p["g_mem"], p["w_mem_kv"], p["g_km"], p["ones128"])


def _dup_heads(x, lane):
    sw = pltpu.roll(x, SWA_HD, axis=x.ndim - 1)
    lo = lane < SWA_HD
    return jnp.where(lo, x, sw), jnp.where(lo, sw, x)


def _swa_group(q_blk, kk, vv, g, mask, sink_ref, stack_heads):
    tq = q_blk.shape[-2]
    shp = q_blk.shape[:-1]
    lane = lax.broadcasted_iota(jnp.int32, shp + (LANES,), len(shp))
    rows = []
    sinks = []
    for hl in range(SWA_Q_PER_KV):
        h = g * SWA_Q_PER_KV + hl
        pair = q_blk[..., (h // 2) * LANES:(h // 2 + 1) * LANES]
        keep = (lane < SWA_HD) if h % 2 == 0 else (lane >= SWA_HD)
        rows.append(jnp.where(keep, pair, 0.0))
        sinks.append(jnp.full(shp + (1,), sink_ref[h], F32))

    def attend(qq, sk, msk):
        s = jnp.einsum("...qd,...kd->...qk", qq.astype(BF16), kk, preferred_element_type=F32)
        s = jnp.where(msk, s, NEG_INF)
        m = jnp.maximum(jnp.max(s, axis=-1, keepdims=True), sk)
        e = jnp.exp(s - m)
        pr = e / (jnp.sum(e, axis=-1, keepdims=True) + jnp.exp(sk - m))
        return jnp.einsum("...qk,...kd->...qd", pr.astype(BF16), vv, preferred_element_type=F32)

    if stack_heads:
        o = attend(jnp.concatenate(rows, axis=-2), jnp.concatenate(sinks, axis=-2), mask)
        outs = [o[..., hl * tq:(hl + 1) * tq, :] for hl in range(SWA_Q_PER_KV)]
    else:
        outs = [attend(rows[hl], sinks[hl], mask[..., :tq, :]) for hl in range(SWA_Q_PER_KV)]
    lo = lane < SWA_HD
    return [jnp.where(lo, outs[2 * j], outs[2 * j + 1]) for j in range(2)]


def _mem_heads(qm, k_head, v_head, row_chunk=WINDOW):
    n_rows = qm.shape[-2]
    step = min(row_chunk, n_rows)
    outs = []
    for h in range(MEM_HEADS):
        cs = slice(h * MEM_HD, (h + 1) * MEM_HD)
        kh = k_head(h).astype(BF16)
        vh = v_head(h).astype(BF16)
        chunks = []
        for r0 in range(0, n_rows, step):
            s = jnp.einsum("...qd,...kd->...qk", qm[..., r0:r0 + step, cs], kh, preferred_element_type=F32) * MEM_SCALE
            m = jnp.max(s, axis=-1, keepdims=True)
            e = jnp.exp(s - m)
            pr = e / jnp.sum(e, axis=-1, keepdims=True)
            chunks.append(jnp.einsum("...qk,...kd->...qd", pr.astype(BF16), vh, preferred_element_type=F32))
        outs.append(jnp.concatenate(chunks, axis=-2) if len(chunks) > 1 else chunks[0])
    return jnp.concatenate(outs, axis=-1)


def _attn_prompt_kernel(sink_ref, q_ref, k_ref, v_ref, kp_ref, vp_ref, qm_ref, mk_ref, mv_ref, ys_ref, ym_ref):
    tq = q_ref.shape[0]
    blk = WINDOW
    rows = SWA_Q_PER_KV * blk
    i = lax.broadcasted_iota(jnp.int32, (rows, 2 * blk), 0) % blk
    j = lax.broadcasted_iota(jnp.int32, (rows, 2 * blk), 1)
    lo = jnp.where(j < blk, i + 1, blk)
    hi = jnp.where(j < blk, blk, blk + i + 1)
    first_lo = jnp.where(pl.program_id(1) == 0, blk, 0)
    lane_k = lax.broadcasted_iota(jnp.int32, (2 * blk, LANES), 1)
    for b in range(tq // blk):
        rs = slice(b * blk, (b + 1) * blk)
        if b == 0:
            k2 = jnp.concatenate([kp_ref[...], k_ref[rs, :]], axis=0)
            v2 = jnp.concatenate([vp_ref[...], v_ref[rs, :]], axis=0)
            mask = (j >= jnp.maximum(lo, first_lo)) & (j < hi)
        else:
            k2 = k_ref[(b - 1) * blk:(b + 1) * blk, :]
            v2 = v_ref[(b - 1) * blk:(b + 1) * blk, :]
            mask = (j >= lo) & (j < hi)
        kks = _dup_heads(k2, lane_k)
        vvs = _dup_heads(v2, lane_k)
        q_blk = q_ref[rs, :].astype(F32)
        pairs = []
        for g in range(SWA_KV_HEADS):
            pairs += _swa_group(q_blk, kks[g].astype(BF16), vvs[g].astype(BF16), g, mask, sink_ref, False)
        ys_ref[rs, :] = jnp.concatenate(pairs, axis=1).astype(BF16)
    ym_ref[...] = _mem_heads(qm_ref[...], lambda h: mk_ref[0, :, h * MEM_HD:(h + 1) * MEM_HD],
                             lambda h: mv_ref[0, :, h * MEM_HD:(h + 1) * MEM_HD]).astype(BF16)


def _attn_prompt(q, k, v, qm, mk, mv, sinks, n_seq):
    t = q.shape[0]
    tq = TQ_ATT
    per = t // n_seq // tq
    sub = tq // WINDOW
    row = lambda n, c: (n * per + c, 0)
    prev = lambda n, c: (jnp.maximum((n * per + c) * sub - 1, 0), 0)
    memb = lambda n, c: (n, 0, 0)
    return pl.pallas_call(
        _attn_prompt_kernel,
        grid=(n_seq, per),
        in_specs=[
            pl.BlockSpec(memory_space=pltpu.SMEM),
            pl.BlockSpec((tq, SWA_WIDTH), row),
            pl.BlockSpec((tq, SWA_KV_WIDTH), row),
            pl.BlockSpec((tq, SWA_KV_WIDTH), row),
            pl.BlockSpec((WINDOW, SWA_KV_WIDTH), prev),
            pl.BlockSpec((WINDOW, SWA_KV_WIDTH), prev),
            pl.BlockSpec((tq, MEM_WIDTH), row),
            pl.BlockSpec((1, N_MEM, MEM_WIDTH), memb),
            pl.BlockSpec((1, N_MEM, MEM_WIDTH), memb),
        ],
        out_specs=(pl.BlockSpec((tq, SWA_WIDTH), row), pl.BlockSpec((tq, MEM_WIDTH), row)),
        out_shape=(jax.ShapeDtypeStruct((t, SWA_WIDTH), BF16), jax.ShapeDtypeStruct((t, MEM_WIDTH), BF16)),
        compiler_params=_cparams(("parallel", "parallel")),
        name="attn_prompt",
    )(sinks, q, k, v, k, v, qm, mk, mv)


def _attn_sample_kernel(sink_ref, q_ref, k_ref, v_ref, pk_ref, pv_ref, qm_ref, mk_ref, mv_ref,
                        ys_ref, ym_ref, nk_ref, nv_ref, *, s_len):
    sb, wb = pk_ref.shape[0], pk_ref.shape[1]
    n_keys = wb + s_len
    rows = SWA_Q_PER_KV * s_len
    i = lax.broadcasted_iota(jnp.int32, (sb, rows, n_keys), 1) % s_len
    j = lax.broadcasted_iota(jnp.int32, (sb, rows, n_keys), 2)
    rel = i + wb - j
    mask = (rel >= 0) & (rel < WINDOW)
    k_all = jnp.concatenate([pk_ref[...], k_ref[...].reshape(sb, s_len, SWA_KV_WIDTH)], axis=1)
    v_all = jnp.concatenate([pv_ref[...], v_ref[...].reshape(sb, s_len, SWA_KV_WIDTH)], axis=1)
    nk_ref[...] = k_all[:, n_keys - wb:, :]
    nv_ref[...] = v_all[:, n_keys - wb:, :]
    lane_k = lax.broadcasted_iota(jnp.int32, k_all.shape, 2)
    kks = _dup_heads(k_all, lane_k)
    vvs = _dup_heads(v_all, lane_k)
    q3 = q_ref[...].astype(F32).reshape(sb, s_len, SWA_WIDTH)
    pairs = []
    for g in range(SWA_KV_HEADS):
        pairs += _swa_group(q3, kks[g].astype(BF16), vvs[g].astype(BF16), g, mask, sink_ref, True)
    ys_ref[...] = jnp.concatenate(pairs, axis=-1).reshape(sb * s_len, SWA_WIDTH).astype(BF16)
    qm3 = qm_ref[...].astype(F32).reshape(sb, s_len, MEM_WIDTH).astype(BF16)
    head_rows = lambda h: pl.ds(h, N_MEM, stride=MEM_HEADS)
    ym = _mem_heads(qm3, lambda h: mk_ref[:, head_rows(h), :], lambda h: mv_ref[:, head_rows(h), :])
    ym_ref[...] = ym.reshape(sb * s_len, MEM_WIDTH).astype(BF16)


def _attn_sample(q, k, v, past_k, past_v, qm, mk, mv, sinks, s_len):
    t = q.shape[0]
    n_seq, wb = past_k.shape[0], past_k.shape[1]
    sb = SEQ_BLK
    rows = sb * s_len
    row = lambda c: (c, 0)
    seq = lambda c: (c, 0, 0)
    cache_shape = jax.ShapeDtypeStruct((n_seq, wb, SWA_KV_WIDTH), F32)
    return pl.pallas_call(
        functools.partial(_attn_sample_kernel, s_len=s_len),
        grid=(n_seq // sb,),
        in_specs=[
            pl.BlockSpec(memory_space=pltpu.SMEM),
            pl.BlockSpec((rows, SWA_WIDTH), row),
            pl.BlockSpec((rows, SWA_KV_WIDTH), row),
            pl.BlockSpec((rows, SWA_KV_WIDTH), row),
            pl.BlockSpec((sb, wb, SWA_KV_WIDTH), seq),
            pl.BlockSpec((sb, wb, SWA_KV_WIDTH), seq),
            pl.BlockSpec((rows, MEM_WIDTH), row),
            pl.BlockSpec((sb, N_MEM * MEM_HEADS, MEM_HD), seq),
            pl.BlockSpec((sb, N_MEM * MEM_HEADS, MEM_HD), seq),
        ],
        out_specs=(pl.BlockSpec((rows, SWA_WIDTH), row), pl.BlockSpec((rows, MEM_WIDTH), row),
                   pl.BlockSpec((sb, wb, SWA_KV_WIDTH), seq), pl.BlockSpec((sb, wb, SWA_KV_WIDTH), seq)),
        out_shape=(jax.ShapeDtypeStruct((t, SWA_WIDTH), BF16), jax.ShapeDtypeStruct((t, MEM_WIDTH), BF16),
                   cache_shape, cache_shape),
        compiler_params=_cparams(("parallel",)),
        name="attn_sample",
    )(sinks, q, k, v, past_k, past_v, qm, mk, mv)


def _first_argmax(x, valid, lane):
    xm = jnp.where(valid, x, -jnp.inf)
    mx = jnp.max(xm, axis=-1, keepdims=True)
    idx = jnp.min(jnp.where(xm == mx, lane, LANES), axis=-1, keepdims=True)
    return mx, lane == idx, idx


def _merge_kernel(xp_ref, yap_ref, ybp_ref, ycp_ref, xs_ref, yas_ref, ybs_ref, ycs_ref,
                  gmix_ref, wg_ref, wa_ref, wb_ref, wc_ref, wo_ref, gffn_ref, wr_hi_ref, wr_lo_ref, br_ref,
                  x1_ref, hx_ref, route_ref, cnt_ref, carry_ref, x1_prev, *, n_blk_p, n_blk):
    i = pl.program_id(0)

    @pl.when(i == 0)
    def _():
        x1_prev[...] = jnp.zeros_like(x1_prev)
        carry_ref[...] = jnp.zeros_like(carry_ref)

    x1_routed = x1_prev[...]

    is_prompt = jnp.minimum(i, n_blk - 1) < n_blk_p
    pick = lambda a_ref, b_ref: jnp.where(is_prompt, a_ref[...], b_ref[...])
    x = pick(xp_ref, xs_ref)
    h = _rms_rows(x, gmix_ref[...]).astype(BF16)

    def branch(k, y, w_ref):
        cols = slice(PROJ_A + k * D_MODEL, PROJ_A + (k + 1) * D_MODEL)
        gate = jax.nn.sigmoid(jnp.dot(h, wg_ref[:, cols], preferred_element_type=F32))
        return gate * jnp.dot(y.astype(BF16), w_ref[...], preferred_element_type=F32)

    routing = _route_rows(x1_routed, i >= 1, gffn_ref, wr_hi_ref, wr_lo_ref, br_ref, hx_ref, route_ref, cnt_ref,
                          carry_ref)
    next(routing)
    merged = branch(0, pick(yap_ref, yas_ref), wa_ref)
    next(routing)
    merged = merged + branch(1, pick(ybp_ref, ybs_ref), wb_ref)
    next(routing)
    merged = merged + branch(2, pick(ycp_ref, ycs_ref), wc_ref)
    next(routing, None)
    x1 = x + jnp.dot(merged.astype(BF16), wo_ref[...], preferred_element_type=F32)
    x1_ref[...] = x1
    x1_prev[...] = x1


def _route_rows(x1, valid, gffn_ref, wr_hi_ref, wr_lo_ref, br_ref, hx_ref, route_ref, cnt_ref, carry_ref):
    hn = _rms_rows(x1, gffn_ref[...])
    tm = x1.shape[0]
    slab_row = lambda c: pl.ds(c, tm, stride=HX_ROWS)
    for c in range(ROW_TILES):
        hx_ref[slab_row(c), :] = hn[:, c * LANES:(c + 1) * LANES]
    yield

    hi, lo = _split_bf16(hn)
    w_hi = wr_hi_ref[...]
    logits = (jnp.dot(hi, w_hi, preferred_element_type=F32) + jnp.dot(lo, w_hi, preferred_element_type=F32)
              + jnp.dot(hi, wr_lo_ref[...], preferred_element_type=F32)) + br_ref[...]
    yield
    lane = lax.broadcasted_iota(jnp.int32, logits.shape, 1)
    is_grp = lane < N_GROUPS_E
    g_max, _, g_idx = _first_argmax(logits, is_grp, lane)
    pg_top = 1.0 / jnp.sum(jnp.where(is_grp, jnp.exp(logits - g_max), 0.0), axis=-1, keepdims=True)
    e_lo = N_GROUPS_E + g_idx * EXPERTS_PER_GROUP
    in_grp = (lane >= e_lo) & (lane < e_lo + EXPERTS_PER_GROUP)
    e_max, first, i1 = _first_argmax(logits, in_grp, lane)
    ex = jnp.where(in_grp, jnp.exp(logits - e_max), 0.0)
    pe = ex / jnp.sum(ex, axis=-1, keepdims=True)
    _, second, i2 = _first_argmax(logits, in_grp & jnp.logical_not(first), lane)
    p1 = jnp.sum(jnp.where(first, pe, 0.0), axis=-1, keepdims=True)
    p2 = jnp.sum(jnp.where(second, pe, 0.0), axis=-1, keepdims=True)
    w1 = pg_top * p1 / (p1 + p2)
    w2 = pg_top * p2 / (p1 + p2)

    a1 = i1 - e_lo
    a2 = i2 - e_lo
    e_a = jnp.minimum(a1, a2)
    e_b = jnp.maximum(a1, a2)
    pair = jnp.right_shift(e_a * (2 * EXPERTS_PER_GROUP - 1 - e_a), 1) + (e_b - e_a - 1)
    bucket = g_idx * PAIRS_PER_GROUP + pair
    w_a = jnp.where(a1 < a2, w1, w2)
    w_b = jnp.where(a1 < a2, w2, w1)
    yield

    onehot = lane == jnp.where(valid, bucket, -1)
    tri = (lax.broadcasted_iota(jnp.int32, (tm, tm), 1) <= lax.broadcasted_iota(jnp.int32, (tm, tm), 0))
    csum = jnp.dot(jnp.where(tri, 1.0, 0.0).astype(BF16), jnp.where(onehot, 1.0, 0.0).astype(BF16),
                   preferred_element_type=F32)
    carry = carry_ref[...]
    rank = jnp.sum(jnp.where(onehot, csum + carry, 0.0), axis=-1, keepdims=True) - 1.0
    carry = carry + csum[tm - 1:tm, :]
    carry_ref[...] = carry
    cnt_ref[...] = carry
    route = jnp.where(lane == 0, bucket.astype(F32),
                      jnp.where(lane == 1, w_a, jnp.where(lane == 2, w_b, jnp.where(lane == 3, rank, 0.0))))
    route_ref[...] = route
    hx_ref[slab_row(ROW_TILES), :] = route
    for c in range(ROW_TILES + 1, HX_ROWS):
        hx_ref[slab_row(c), :] = jnp.zeros_like(route)


def _merge(prompt_rows, sample_rows, p):
    tm = TM_MRG
    t_p, t_s = prompt_rows[0].shape[0], sample_rows[0].shape[0]
    nbp = t_p // tm
    t_all = t_p + t_s
    n_blk = t_all // tm
    first = lambda i: (jnp.minimum(i, nbp - 1), 0)
    second = lambda i: (jnp.clip(i - nbp, 0, n_blk - nbp - 1), 0)
    merged_blk = lambda i: (jnp.minimum(i, n_blk - 1), 0)
    routed_blk = lambda i: (jnp.maximum(i - 1, 0), 0)
    widths = (D_MODEL, SSM_WIDTH, SWA_WIDTH, MEM_WIDTH)
    in_specs = ([pl.BlockSpec((tm, w), first) for w in widths] + [pl.BlockSpec((tm, w), second) for w in widths] + [
        _resident((1, D_MODEL)),
        _resident((D_MODEL, PROJ_A + N_BRANCH * D_MODEL)),
        _resident((SSM_WIDTH, D_MODEL)), _resident((SWA_WIDTH, D_MODEL)), _resident((MEM_WIDTH, D_MODEL)),
        _resident((D_MODEL, D_MODEL)),
        _resident((1, D_MODEL)),
        _resident((D_MODEL, LANES)), _resident((D_MODEL, LANES)), _resident((1, LANES)),
    ])
    return pl.pallas_call(
        functools.partial(_merge_kernel, n_blk_p=nbp, n_blk=n_blk),
        grid=(n_blk + 1,),
        in_specs=in_specs,
        out_specs=(pl.BlockSpec((tm, D_MODEL), merged_blk), pl.BlockSpec((tm * HX_ROWS, LANES), routed_blk),
                   pl.BlockSpec((tm, ROUTE_W), routed_blk), _full((1, LANES))),
        out_shape=(jax.ShapeDtypeStruct((t_all, D_MODEL), F32),
                   jax.ShapeDtypeStruct((t_all * HX_ROWS, LANES), F32),
                   jax.ShapeDtypeStruct((t_all, ROUTE_W), F32), jax.ShapeDtypeStruct((1, LANES), F32)),
        scratch_shapes=[pltpu.VMEM((1, LANES), F32), pltpu.VMEM((tm, D_MODEL), F32)],
        compiler_params=_cparams(("arbitrary",)),
        name="merge",
    )(*prompt_rows, *sample_rows, p["g_mix"], p["w_in"], p["w_br_ssm"], p["w_br_swa"], p["w_br_mem"], p["w_o"],
      p["g_ffn"], p["w_r_hi"], p["w_r_lo"], p["b_r"])


def _pos_kernel(route_ref, off_ref, coff_ref, pos_ref, cpos_ref):
    r = route_ref[...]
    lane = lax.broadcasted_iota(jnp.int32, r.shape, 1)
    mine = lane == r[:, 0:1].astype(jnp.int32)
    off = jnp.sum(jnp.where(mine, off_ref[...], 0.0), axis=-1, keepdims=True)
    coff = jnp.sum(jnp.where(mine, coff_ref[...], 0.0), axis=-1, keepdims=True)
    rank = r[:, 3:4]
    both = jnp.where(lane == 0, off + rank, coff + rank)
    pos_rows, cpos_rows = [], []
    for k in range(r.shape[0] // LANES):
        t = both[k * LANES:(k + 1) * LANES].T
        pos_rows.append(t[0:1, :])
        cpos_rows.append(t[1:2, :])
    pos_ref[...] = jnp.concatenate(pos_rows, axis=0).astype(jnp.int32)
    cpos_ref[...] = jnp.concatenate(cpos_rows, axis=0).astype(jnp.int32)


def _sorted_pos(route, off, coff):
    t = route.shape[0]
    tm = SUBLANES * LANES
    shp = jax.ShapeDtypeStruct((t // LANES, LANES), jnp.int32)
    pos, cpos = pl.pallas_call(
        _pos_kernel,
        grid=(t // tm,),
        in_specs=[pl.BlockSpec((tm, ROUTE_W), lambda i: (i, 0)), _full((1, LANES)), _full((1, LANES))],
        out_specs=(pl.BlockSpec((SUBLANES, LANES), lambda i: (i, 0)), pl.BlockSpec((SUBLANES, LANES), lambda i: (i, 0))),
        out_shape=(shp, shp),
        compiler_params=_cparams(("parallel",)),
        name="sorted_pos",
    )(route, off, coff)
    return pos.reshape(t), cpos.reshape(t)


def _inv_kernel(pos_ref, idx_ref):
    def body(t, _):
        idx_ref[pos_ref[t]] = t
        return 0

    lax.fori_loop(0, pos_ref.shape[0], body, 0, unroll=8)


def _invert(pos):
    return pl.pallas_call(
        _inv_kernel,
        in_specs=[pl.BlockSpec(memory_space=pltpu.SMEM)],
        out_specs=pl.BlockSpec(memory_space=pltpu.SMEM),
        out_shape=jax.ShapeDtypeStruct(pos.shape, jnp.int32),
        name="invert_perm",
    )(pos)


def _bucket_kernel(idx_ref, tg_ref, ta_ref, tb_ref, cb_ref, nr_ref, hx_hbm, wi_ref, wd_ref,
                   ys_ref, buf, sem):
    j = pl.program_id(0)
    n_real = nr_ref[0]

    last = idx_ref.shape[0] - 1

    def issue_row(tile_base, slot, r, dst_row, prio):
        src = idx_ref[jnp.minimum(tile_base + r, last)]
        pltpu.make_async_copy(hx_hbm.at[pl.ds(pl.multiple_of(src * HX_ROWS, HX_ROWS), HX_ROWS), :],
                              buf.at[slot, pl.ds(dst_row, HX_ROWS), :], sem.at[slot]).start(priority=prio)

    def wait_tile(slot):
        pltpu.make_async_copy(hx_hbm.at[pl.ds(0, TM_EXP * HX_ROWS), :], buf.at[slot], sem.at[slot]).wait()

    @pl.when(j == 0)
    def _():
        base = cb_ref[0]

        def body(r8, _):
            for k in range(SUBLANES):
                r = r8 * SUBLANES + k
                issue_row(base, 0, r, pl.multiple_of(r * HX_ROWS, HX_ROWS), k % 2)
            return 0

        lax.fori_loop(0, TM_EXP // SUBLANES, body, 0)

    @pl.when(j < n_real)
    def _():
        slot = j % 2
        wait_tile(slot)
        nxt = jnp.minimum(j + 1, n_real - 1)
        base = cb_ref[nxt]
        for r in range(TM_EXP):
            issue_row(base, 1 - slot, r, r * HX_ROWS, r % 2)

        slab_row = lambda c: buf[slot, pl.ds(c, TM_EXP, stride=HX_ROWS), :]
        x = jnp.concatenate([slab_row(c) for c in range(ROW_TILES)], axis=1).astype(BF16)
        route = slab_row(ROW_TILES)

        def ffn(e):
            gu = jnp.dot(x, wi_ref[0, e], preferred_element_type=F32)
            a = jax.nn.silu(gu[:, :D_FF]) * gu[:, D_FF:]
            return jnp.dot(a.astype(BF16), wd_ref[0, e], preferred_element_type=F32)

        y = route[:, 1:2] * ffn(ta_ref[j]) + route[:, 2:3] * ffn(tb_ref[j])
        for c in range(ROW_TILES):
            ys_ref[pl.ds(c, TM_EXP, stride=ROW_TILES), :] = y[:, c * LANES:(c + 1) * LANES]

        @pl.when(j == n_real - 1)
        def _():
            wait_tile(1 - slot)


def _bucket_ffn(hx, idx, tile_g, tile_a, tile_b, tile_cb, n_real, p):
    n_tiles = tile_a.shape[0]
    grp = lambda j, idx, tg, ta, tb, cb, nr: (tg[j], 0, 0, 0)
    out = lambda j, idx, tg, ta, tb, cb, nr: (jnp.minimum(j, nr[0] - 1), 0)
    epg = EXPERTS_PER_GROUP
    return pl.pallas_call(
        _bucket_kernel,
        grid_spec=pltpu.PrefetchScalarGridSpec(
            num_scalar_prefetch=6,
            grid=(n_tiles,),
            in_specs=[
                pl.BlockSpec(memory_space=pl.ANY),
                pl.BlockSpec((1, epg, D_MODEL, 2 * D_FF), grp), pl.BlockSpec((1, epg, D_FF, D_MODEL), grp),
            ],
            out_specs=pl.BlockSpec((TM_EXP * ROW_TILES, LANES), out),
            scratch_shapes=[pltpu.VMEM((2, TM_EXP * HX_ROWS, LANES), F32), pltpu.SemaphoreType.DMA((2,))],
        ),
        out_shape=jax.ShapeDtypeStruct((n_tiles * TM_EXP * ROW_TILES, LANES), F32),
        compiler_params=_cparams(("arbitrary",)),
        name="bucket_ffn",
    )(idx, tile_g, tile_a, tile_b, tile_cb, n_real, hx,
      p["w_exp_in"].reshape(N_GROUPS_E, epg, D_MODEL, 2 * D_FF),
      p["w_exp_down"].reshape(N_GROUPS_E, epg, D_FF, D_MODEL))


def _back_kernel(pos_ref, x1_ref, ys_hbm, o_ref, buf, sem, *, t0):
    i = pl.program_id(0)
    tm = o_ref.shape[0]

    def issue(tile, slot):
        base = t0 + tile * tm

        def body(r8, _):
            for k in range(SUBLANES):
                r = r8 * SUBLANES + k
                src = pl.multiple_of(pos_ref[base + r] * ROW_TILES, ROW_TILES)
                pltpu.make_async_copy(ys_hbm.at[pl.ds(src, ROW_TILES), :],
                                      buf.at[slot, pl.ds(pl.multiple_of(r * ROW_TILES, ROW_TILES), ROW_TILES), :],
                                      sem.at[slot]).start(priority=k % 2)
            return 0

        lax.fori_loop(0, tm // SUBLANES, body, 0)

    @pl.when(i == 0)
    def _():
        issue(0, 0)

    @pl.when(i + 1 < pl.num_programs(0))
    def _():
        issue(i + 1, (i + 1) % 2)

    slot = i % 2
    pltpu.make_async_copy(ys_hbm.at[pl.ds(0, tm * ROW_TILES), :], buf.at[slot], sem.at[slot]).wait()
    y = jnp.concatenate([buf[slot, pl.ds(c, tm, stride=ROW_TILES), :] for c in range(ROW_TILES)], axis=1)
    o_ref[...] = x1_ref[...] + y


def _unsort_add(x1, ys, pos, t0, t):
    tm = TM_BACK
    row = lambda i, pos: (i, 0)
    return pl.pallas_call(
        functools.partial(_back_kernel, t0=t0),
        grid_spec=pltpu.PrefetchScalarGridSpec(
            num_scalar_prefetch=1,
            grid=(t // tm,),
            in_specs=[pl.BlockSpec((tm, D_MODEL), lambda i, pos: (i + t0 // tm, 0)),
                      pl.BlockSpec(memory_space=pl.ANY)],
            out_specs=pl.BlockSpec((tm, D_MODEL), row),
            scratch_shapes=[pltpu.VMEM((2, tm * ROW_TILES, LANES), F32), pltpu.SemaphoreType.DMA((2,))],
        ),
        out_shape=jax.ShapeDtypeStruct((t, D_MODEL), F32),
        compiler_params=_cparams(("arbitrary",)),
        name="unsort_add",
    )(pos, x1, ys)


def _bucket_experts():
    lo, hi = [], []
    for g in range(N_GROUPS_E):
        for a in range(EXPERTS_PER_GROUP):
            for b in range(a + 1, EXPERTS_PER_GROUP):
                lo.append(g * EXPERTS_PER_GROUP + a)
                hi.append(g * EXPERTS_PER_GROUP + b)
    return jnp.asarray(lo, jnp.int32), jnp.asarray(hi, jnp.int32)


def _tile_tables(counts, n_tiles):
    cnt = counts[0, :N_BUCKETS].astype(jnp.int32)
    nt = (cnt + TM_EXP - 1) // TM_EXP
    tend = jnp.cumsum(nt)
    tstart = tend - nt
    cstart = jnp.cumsum(cnt) - cnt
    pad = lambda v: jnp.zeros((1, LANES), F32).at[0, :N_BUCKETS].set(v.astype(F32))
    j = jnp.arange(n_tiles, dtype=jnp.int32)
    b = jnp.minimum(jnp.sum((tend[None, :] <= j[:, None]).astype(jnp.int32), axis=1), N_BUCKETS - 1)
    in_bucket = (j - tstart[b]) * TM_EXP
    e_lo, e_hi = _bucket_experts()
    epg = EXPERTS_PER_GROUP
    return (pad(tstart * TM_EXP), pad(cstart), b // PAIRS_PER_GROUP, e_lo[b] % epg, e_hi[b] % epg,
            cstart[b] + in_bucket, tend[-1:])


def _rope_tables(first_pos, n_pos):
    half = SWA_HD // 2
    inv = ROPE_THETA ** (-np.arange(half, dtype=np.float64) / half)
    ang = (first_pos + np.arange(n_pos, dtype=np.float64))[:, None] * inv[None, :]
    cos = np.cos(ang)
    sin = np.sin(ang)
    cos = np.concatenate([cos, cos, cos, cos], axis=1)
    sin = np.concatenate([-sin, sin, -sin, sin], axis=1)
    return jnp.asarray(cos, F32), jnp.asarray(sin, F32)


def kernel(x_prompt, x_sample, mem_prompt, state_ssm_re, state_ssm_im, cache_swa_k, cache_swa_v, cache_mem_k, cache_mem_v, norm_mix, w_in, ssm_a_re, ssm_a_im, ssm_log_dt, ssm_b_re, ssm_b_im, ssm_c_re, ssm_c_im, ssm_d, w_glu, swa_q_norm, swa_k_norm, swa_sinks, norm_mem, w_mem_kv, mem_q_norm, mem_k_norm, w_br_ssm, w_br_swa, w_br_mem, w_o, norm_ffn, w_router_group, b_router_group, w_router_expert, b_router_expert, w_exp_in, w_exp_down):
    depth = w_in.shape[0]
    assert depth == 1
    nb, seq, _ = x_prompt.shape
    db, dseq, _ = x_sample.shape
    assert dseq == 2 * LB
    l = 0

    w_r = jnp.concatenate([w_router_group[l], w_router_expert[l]], axis=1)
    w_r = jnp.pad(w_r, ((0, 0), (0, LANES - w_r.shape[1])))
    w_r_hi = w_r.astype(BF16)
    b_r = jnp.pad(jnp.concatenate([b_router_group[l], b_router_expert[l]]), (0, LANES - N_GROUPS_E - N_EXPERTS))
    p = dict(
        g_mix=norm_mix[l][None], w_in=w_in[l].astype(BF16),
        g_q=jnp.tile(swa_q_norm[l], SWA_HEADS)[None], g_k=jnp.tile(swa_k_norm[l], SWA_KV_HEADS)[None],
        g_qm=jnp.tile(mem_q_norm[l], MEM_HEADS)[None], g_km=jnp.tile(mem_k_norm[l], MEM_HEADS)[None],
        ones64=_block_ones(SWA_WIDTH, SWA_HD), ones128=_block_ones(MEM_WIDTH, MEM_HD),
        ssm_d=ssm_d[l][None], w_glu=w_glu[l].astype(BF16),
        g_mem=norm_mem[l][None], w_mem_kv=w_mem_kv[l].astype(BF16),
        w_br_ssm=w_br_ssm[l].astype(BF16), w_br_swa=w_br_swa[l].astype(BF16), w_br_mem=w_br_mem[l].astype(BF16),
        w_o=w_o[l].astype(BF16), g_ffn=norm_ffn[l][None],
        w_r_hi=w_r_hi, w_r_lo=(w_r - w_r_hi.astype(F32)).astype(BF16), b_r=b_r[None],
        w_exp_in=w_exp_in[l].astype(BF16), w_exp_down=w_exp_down[l].astype(BF16),
    )
    p.update(_ssm_params(ssm_a_re[l], ssm_a_im[l], ssm_log_dt[l], ssm_b_re[l], ssm_b_im[l],
                         ssm_c_re[l], ssm_c_im[l]))
    sinks = swa_sinks[l]

    xp = x_prompt.reshape(nb * seq, D_MODEL)
    cos_p, sin_p = _rope_tables(0, seq)
    u, q, k, v, qm = _inproj(xp, cos_p, sin_p, seq // TM_IN, p)
    zeros_state = jnp.zeros((nb, 1, SSM_CH), F32)
    y_ssm, pr, pi = _ssm(u, zeros_state, zeros_state, p, n_seq=nb, chained=True)
    mk, mv = _memkv(mem_prompt, p)
    y_swa, y_mem = _attn_prompt(q, k, v, qm, mk, mv, sinks, nb)
    win = min(WINDOW, seq)
    last_win = lambda a: a.reshape(nb, seq, SWA_KV_WIDTH)[:, seq - win:].reshape(nb, win, SWA_KV_HEADS, SWA_HD)
    p_k, p_v = last_win(k), last_win(v)

    xs = x_sample.reshape(db * dseq, D_MODEL)
    cos_s, sin_s = _rope_tables(PAST_LEN, dseq)
    reps = TM_IN // dseq
    us, qs, ks, vs, qms = _inproj(xs, jnp.tile(cos_s, (reps, 1)), jnp.tile(sin_s, (reps, 1)), 1, p)
    two_rows = lambda st: jnp.repeat(st.reshape(db, SSM_CH), 2, axis=0)
    ys_ssm, sr, si = _ssm(us, two_rows(state_ssm_re), two_rows(state_ssm_im), p, n_seq=db, chained=False)
    sr, si = sr[1::2], si[1::2]
    wb = cache_swa_k.shape[2]
    ys_swa, ys_mem, s_k, s_v = _attn_sample(
        qs, ks, vs, cache_swa_k[l].reshape(db, wb, SWA_KV_WIDTH), cache_swa_v[l].reshape(db, wb, SWA_KV_WIDTH),
        qms, cache_mem_k.reshape(db, N_MEM * MEM_HEADS, MEM_HD), cache_mem_v.reshape(db, N_MEM * MEM_HEADS, MEM_HD),
        sinks, dseq)
    t_p, t_s = nb * seq, db * dseq
    t_all = t_p + t_s
    x1, hx, route, counts = _merge((xp, y_ssm, y_swa, y_mem), (xs, ys_ssm, ys_swa, ys_mem), p)

    n_tiles = pl.cdiv(t_all, TM_EXP) + N_BUCKETS
    off, coff, tile_g, tile_a, tile_b, tile_cb, n_real = _tile_tables(counts, n_tiles)
    pos, cpos = _sorted_pos(route, off, coff)
    idx = _invert(cpos)
    y_sorted = _bucket_ffn(hx, idx, tile_g, tile_a, tile_b, tile_cb, n_real, p)
    yp = _unsort_add(x1, y_sorted, pos, 0, t_p).reshape(nb, seq, D_MODEL)
    ys = _unsort_add(x1, y_sorted, pos, t_p, t_s).reshape(db, dseq, D_MODEL)

    g, s = SSM_GROUPS, SSM_STATE
    return (yp, ys,
            pr.reshape(1, nb, g, s), pi.reshape(1, nb, g, s),
            p_k[None], p_v[None],
            mk.reshape(1, nb, N_MEM, MEM_HEADS, MEM_HD), mv.reshape(1, nb, N_MEM, MEM_HEADS, MEM_HD),
            sr.reshape(1, db, g, s), si.reshape(1, db, g, s),
            s_k.reshape(1, db, wb, SWA_KV_HEADS, SWA_HD), s_v.reshape(1, db, wb, SWA_KV_HEADS, SWA_HD))
```

```python
import functools

import jax
import jax.numpy as jnp
import numpy as np
from jax import lax
from jax.experimental import pallas as pl
from jax.experimental.pallas import tpu as pltpu

F32 = jnp.float32
BF16 = jnp.bfloat16

D_MODEL = 1024
SSM_WIDTH = 512
SSM_GROUP = 16
SSM_GROUPS = 32
SSM_STATE = 64
SSM_CH = SSM_GROUPS * SSM_STATE
LB = 4
OCT = 8
N_OCT = SSM_GROUPS // OCT
OCT_CH = OCT * SSM_STATE
SWA_HEADS = 8
SWA_KV_HEADS = 2
SWA_Q_PER_KV = SWA_HEADS // SWA_KV_HEADS
SWA_HD = 64
SWA_WIDTH = SWA_HEADS * SWA_HD
SWA_KV_WIDTH = SWA_KV_HEADS * SWA_HD
WINDOW = 128
PAST_LEN = 16384
ROPE_THETA = 10000.0
N_MEM = 256
MEM_HEADS = 4
MEM_HD = 128
MEM_WIDTH = MEM_HEADS * MEM_HD
N_BRANCH = 3
PROJ_A = SSM_WIDTH + SWA_WIDTH + 2 * SWA_KV_WIDTH + MEM_WIDTH
N_GROUPS_E = 4
EXPERTS_PER_GROUP = 8
N_EXPERTS = 32
D_FF = 256
EPS = 1e-6
NEG_INF = -1e30
SWA_SCALE = SWA_HD ** -0.5
MEM_SCALE = MEM_HD ** -0.5

LANES = 128
SUBLANES = 8
VMEM_LIMIT = 56 * 1024 * 1024

TM_IN = 512
TB_SSM = 512
SCAN_W = 512
TQ_ATT = 512
SEQ_BLK = 16
TM_MRG = 512
TM_BACK = 512
TM_EXP = 192

ROUTE_W = LANES
ROW_TILES = D_MODEL // LANES
HX_ROWS = 2 * ROW_TILES
PAIRS_PER_GROUP = EXPERTS_PER_GROUP * (EXPERTS_PER_GROUP - 1) // 2
N_BUCKETS = N_GROUPS_E * PAIRS_PER_GROUP


def _cparams(sem):
    return pltpu.CompilerParams(dimension_semantics=sem, vmem_limit_bytes=VMEM_LIMIT)


def _full(shape):
    nd = len(shape)
    return pl.BlockSpec(shape, lambda *_: (0,) * nd)


def _resident(shape):
    nd = len(shape)
    return pl.BlockSpec(shape, lambda *_: (0,) * nd, pipeline_mode=pl.Buffered(1))


def _split_bf16(x):
    hi = x.astype(BF16)
    lo = (x - hi.astype(F32)).astype(BF16)
    return hi, lo


def _seg_mean_sq(x, ones_blk, width, two_pass=True):
    hi, lo = _split_bf16(x * x)
    s = jnp.dot(hi, ones_blk, preferred_element_type=F32)
    if two_pass:
        s = s + jnp.dot(lo, ones_blk, preferred_element_type=F32)
    return s * (1.0 / width)


def _rms_rows(x, gain):
    return x * lax.rsqrt(jnp.mean(x * x, axis=-1, keepdims=True) + EPS) * gain


def _block_ones(n, width):
    i = jnp.arange(n) // width
    return (i[:, None] == i[None, :]).astype(BF16)


def _rope_cols(x, cos, sin_signed, lane_in_head):
    n = x.shape[1]
    reps = n // LANES
    if reps > 1:
        cos = jnp.concatenate([cos] * reps, axis=1)
        sin_signed = jnp.concatenate([sin_signed] * reps, axis=1)
    half = SWA_HD // 2
    partner = jnp.where(lane_in_head < half, pltpu.roll(x, n - half, axis=1), pltpu.roll(x, half, axis=1))
    return x * cos + partner * sin_signed


def _inproj_kernel(x_ref, gmix_ref, w_ref, gq_ref, gk_ref, gm_ref, cos_ref, sin_ref, o64_ref, o128_ref,
                   u_ref, q_ref, k_ref, v_ref, qm_ref, u_scr):
    x = x_ref[...]
    h = _rms_rows(x, gmix_ref[...]).astype(BF16)
    proj = jnp.dot(h, w_ref[...], preferred_element_type=F32)
    c0 = SSM_WIDTH
    c1 = c0 + SWA_WIDTH
    c2 = c1 + SWA_KV_WIDTH
    c3 = c2 + SWA_KV_WIDTH
    n_blk = u_ref.shape[0]
    for c in range(c0 // LANES):
        u_scr[c] = proj[:, c * LANES:(c + 1) * LANES]
        for t in range(LB):
            u_ref[:, t * c0 + c * LANES:t * c0 + (c + 1) * LANES] = u_scr[c, pl.ds(t, n_blk, stride=LB), :]
    q = proj[:, c0:c1]
    k = proj[:, c1:c2]
    v_ref[...] = proj[:, c2:c3]
    qm = proj[:, c3:]
    cos = cos_ref[...]
    sin = sin_ref[...]
    o64 = o64_ref[...]
    lane_q = lax.broadcasted_iota(jnp.int32, q.shape, 1) % SWA_HD
    qn = q * lax.rsqrt(_seg_mean_sq(q, o64, SWA_HD, two_pass=False) + EPS) * gq_ref[...]
    q_ref[...] = (_rope_cols(qn, cos, sin, lane_q) * SWA_SCALE).astype(BF16)
    lane_k = lax.broadcasted_iota(jnp.int32, k.shape, 1) % SWA_HD
    kn = k * lax.rsqrt(_seg_mean_sq(k, o64[:SWA_KV_WIDTH, :SWA_KV_WIDTH], SWA_HD) + EPS) * gk_ref[...]
    k_ref[...] = _rope_cols(kn, cos, sin, lane_k)
    qmn = qm * lax.rsqrt(_seg_mean_sq(qm, o128_ref[...], MEM_HD, two_pass=False) + EPS) * gm_ref[...]
    qm_ref[...] = qmn.astype(BF16)


def _inproj(x2d, cos, sin, pos_blocks, p):
    t = x2d.shape[0]
    tm = TM_IN
    grid = (t // tm,)
    row = lambda i: (i, 0)
    tab = lambda i: (i % pos_blocks, 0)
    out_shape = (
        jax.ShapeDtypeStruct((t // LB, LB * SSM_WIDTH), F32),
        jax.ShapeDtypeStruct((t, SWA_WIDTH), BF16),
        jax.ShapeDtypeStruct((t, SWA_KV_WIDTH), F32),
        jax.ShapeDtypeStruct((t, SWA_KV_WIDTH), F32),
        jax.ShapeDtypeStruct((t, MEM_WIDTH), BF16),
    )
    return pl.pallas_call(
        _inproj_kernel,
        grid=grid,
        in_specs=[
            pl.BlockSpec((tm, D_MODEL), row),
            _full((1, D_MODEL)),
            _resident((D_MODEL, PROJ_A)),
            _full((1, SWA_WIDTH)),
            _full((1, SWA_KV_WIDTH)),
            _full((1, MEM_WIDTH)),
            pl.BlockSpec((tm, LANES), tab),
            pl.BlockSpec((tm, LANES), tab),
            _full((SWA_WIDTH, SWA_WIDTH)),
            _full((MEM_WIDTH, MEM_WIDTH)),
        ],
        out_specs=(
            pl.BlockSpec((tm // LB, LB * SSM_WIDTH), row),
            pl.BlockSpec((tm, SWA_WIDTH), row),
            pl.BlockSpec((tm, SWA_KV_WIDTH), row),
            pl.BlockSpec((tm, SWA_KV_WIDTH), row),
            pl.BlockSpec((tm, MEM_WIDTH), row),
        ),
        out_shape=out_shape,
        scratch_shapes=[pltpu.VMEM((SSM_WIDTH // LANES, tm, LANES), F32)],
        compiler_params=_cparams(("parallel",)),
        name="inproj",
    )(x2d, p["g_mix"], p["w_in"], p["g_q"], p["g_k"], p["g_qm"], cos, sin, p["ones64"], p["ones128"])


def _cmul(x, y):
    return x[0] * y[0] - x[1] * y[1], x[0] * y[1] + x[1] * y[0]


def _ssm_kernel(u_ref, s0r_ref, s0i_ref, wx_ref, wt_ref, wc_ref, d_ref, wglu_ref, abr_ref, abi_ref,
                lvr_ref, lvi_ref, cpr_ref, cpi_ref,
                y_ref, fr_ref, fi_ref, sr_ref, si_ref, car_ref, cai_ref, o_scr, *, chained):
    tb = u_ref.shape[0]
    if chained:
        @pl.when(pl.program_id(1) == 0)
        def _():
            car_ref[...] = s0r_ref[0]
            cai_ref[...] = s0i_ref[0]

    u = u_ref[...]
    ub = u.astype(BF16)
    lhs = [jnp.concatenate([ub[:, t * SSM_WIDTH + c * LANES:t * SSM_WIDTH + (c + 1) * LANES] for t in range(LB)],
                           axis=1) for c in range(N_OCT)]
    for c in range(N_OCT):
        x = jnp.dot(lhs[c], wx_ref[c], preferred_element_type=F32)
        cs = slice(c * OCT_CH, (c + 1) * OCT_CH)
        sr_ref[:, cs] = x[:, :OCT_CH]
        si_ref[:, cs] = x[:, OCT_CH:]

    if chained:
        first_row = lax.broadcasted_iota(jnp.int32, (SUBLANES, SCAN_W), 0) == 0
        for sl in range(SSM_CH // SCAN_W):
            cols = slice(sl * SCAN_W, (sl + 1) * SCAN_W)
            lv = [(lvr_ref[j, :, cols], lvi_ref[j, :, cols]) for j in range(3)]
            cpr = cpr_ref[:, cols]
            cpi = cpi_ref[:, cols]

            def tile(i, carry, cols=cols, lv=lv, cpr=cpr, cpi=cpi):
                r0 = pl.multiple_of(i * SUBLANES, SUBLANES)
                xr = sr_ref[pl.ds(r0, SUBLANES), cols]
                xi = si_ref[pl.ds(r0, SUBLANES), cols]
                for j, d in enumerate((1, 2, 4)):
                    pr, pi = lv[j]
                    shr = pltpu.roll(xr, d, axis=0)
                    shi = pltpu.roll(xi, d, axis=0)
                    xr, xi = xr + pr * shr - pi * shi, xi + pr * shi + pi * shr
                cb_r = jnp.broadcast_to(carry[0], xr.shape)
                cb_i = jnp.broadcast_to(carry[1], xr.shape)
                xr, xi = xr + cpr * cb_r - cpi * cb_i, xi + cpr * cb_i + cpi * cb_r
                sr_ref[pl.ds(r0, SUBLANES), cols] = jnp.where(first_row, cb_r, pltpu.roll(xr, 1, axis=0))
                si_ref[pl.ds(r0, SUBLANES), cols] = jnp.where(first_row, cb_i, pltpu.roll(xi, 1, axis=0))
                return xr[SUBLANES - 1:SUBLANES, :], xi[SUBLANES - 1:SUBLANES, :]

            c_r, c_i = lax.fori_loop(0, tb // SUBLANES, tile, (car_ref[:, cols], cai_ref[:, cols]), unroll=True)
            car_ref[:, cols] = c_r
            cai_ref[:, cols] = c_i
        fr_ref[0] = car_ref[...]
        fi_ref[0] = cai_ref[...]
    else:
        odd = lax.broadcasted_iota(jnp.int32, (tb, SSM_CH), 0) % 2 == 1
        ab = (abr_ref[...], abi_ref[...])
        s0 = (s0r_ref[...], s0i_ref[...])
        x = (sr_ref[...], si_ref[...])
        e_first = _cmul(ab, s0)
        e_first = (e_first[0] + x[0], e_first[1] + x[1])
        prev = (pltpu.roll(e_first[0], 1, axis=0), pltpu.roll(e_first[1], 1, axis=0))
        e_second = _cmul(ab, prev)
        fr_ref[...] = jnp.where(odd, e_second[0] + x[0], e_first[0])
        fi_ref[...] = jnp.where(odd, e_second[1] + x[1], e_first[1])
        sr_ref[...] = jnp.where(odd, prev[0], s0[0])
        si_ref[...] = jnp.where(odd, prev[1], s0[1])

    ys = [[None] * N_OCT for _ in range(LB)]
    for c in range(N_OCT):
        cs = slice(c * OCT_CH, (c + 1) * OCT_CH)
        s_in = jnp.concatenate([sr_ref[:, cs], si_ref[:, cs]], axis=1).astype(BF16)
        yc = (jnp.dot(lhs[c], wt_ref[c], preferred_element_type=F32)
              + lax.dot_general(s_in, wc_ref[c], (((1,), (1,)), ((), ())), preferred_element_type=F32))
        for t in range(LB):
            ys[t][c] = yc[:, t * LANES:(t + 1) * LANES]
    y = jnp.concatenate([jnp.concatenate(ys[t], axis=1) for t in range(LB)], axis=0)
    us = jnp.concatenate([u[:, t * SSM_WIDTH:(t + 1) * SSM_WIDTH] for t in range(LB)], axis=0)
    y = jax.nn.gelu(y + d_ref[...] * us)
    gate = jax.nn.sigmoid(jnp.dot(y.astype(BF16), wglu_ref[...], preferred_element_type=F32))
    out = y * gate
    for c in range(SSM_WIDTH // LANES):
        for t in range(LB):
            o_scr[c, pl.ds(t, tb, stride=LB), :] = out[t * tb:(t + 1) * tb, c * LANES:(c + 1) * LANES]
        y_ref[:, c * LANES:(c + 1) * LANES] = o_scr[c]


def _ssm(u_blk, s0r, s0i, p, *, n_seq, chained):
    rows = u_blk.shape[0]
    if chained:
        tb = TB_SSM
        per = rows // n_seq // tb
        grid = (n_seq, per)
        row = lambda n, c: (n * per + c, 0)
        st = lambda n, c: (n, 0, 0)
        s0_spec = pl.BlockSpec((1, 1, SSM_CH), st)
        f_spec = pl.BlockSpec((1, 1, SSM_CH), st)
        f_shape = jax.ShapeDtypeStruct((n_seq, 1, SSM_CH), F32)
        sem = ("parallel", "arbitrary")
    else:
        tb = min(TB_SSM, rows)
        grid = (rows // tb,)
        row = lambda c: (c, 0)
        s0_spec = pl.BlockSpec((tb, SSM_CH), row)
        f_spec = pl.BlockSpec((tb, SSM_CH), row)
        f_shape = jax.ShapeDtypeStruct((rows, SSM_CH), F32)
        sem = ("parallel",)
    blk_w = LB * SSM_WIDTH
    return pl.pallas_call(
        functools.partial(_ssm_kernel, chained=chained),
        grid=grid,
        in_specs=[
            pl.BlockSpec((tb, blk_w), row),
            s0_spec, s0_spec,
            _resident((N_OCT, LB * LANES, 2 * OCT_CH)), _resident((N_OCT, LB * LANES, LB * LANES)),
            _resident((N_OCT, LB * LANES, 2 * OCT_CH)),
            _full((1, SSM_WIDTH)),
            _resident((SSM_WIDTH, SSM_WIDTH)),
            _full((1, SSM_CH)), _full((1, SSM_CH)),
            _full((3, SUBLANES, SSM_CH)), _full((3, SUBLANES, SSM_CH)),
            _full((SUBLANES, SSM_CH)), _full((SUBLANES, SSM_CH)),
        ],
        out_specs=(pl.BlockSpec((tb * LB, SSM_WIDTH), row), f_spec, f_spec),
        out_shape=(jax.ShapeDtypeStruct((rows * LB, SSM_WIDTH), F32), f_shape, f_shape),
        scratch_shapes=[
            pltpu.VMEM((tb, SSM_CH), F32), pltpu.VMEM((tb, SSM_CH), F32),
            pltpu.VMEM((1, SSM_CH), F32), pltpu.VMEM((1, SSM_CH), F32),
            pltpu.VMEM((SSM_WIDTH // LANES, tb * LB, LANES), F32),
        ],
        compiler_params=_cparams(sem),
        name="ssm_chained" if chained else "ssm_pairs",
    )(u_blk, s0r, s0i, p["w_x"], p["w_t"], p["w_c"], p["ssm_d"], p["w_glu"], p["ab_re"], p["ab_im"],
      p["lv_re"], p["lv_im"], p["cp_re"], p["cp_im"])


def _ssm_params(a_re, a_im, log_dt, b_re, b_im, c_re, c_im):
    dt = jnp.exp(log_dt)[:, None]
    mag = jnp.exp(a_re * dt)
    abr = mag * jnp.cos(a_im * dt)
    abi = mag * jnp.sin(a_im * dt)
    den = a_re * a_re + a_im * a_im
    nr = abr - 1.0
    ni = abi
    coef_re = (nr * a_re + ni * a_im) / den
    coef_im = (ni * a_re - nr * a_im) / den
    bb = (coef_re[..., None] * b_re - coef_im[..., None] * b_im,
          coef_re[..., None] * b_im + coef_im[..., None] * b_re)
    cc = (c_re, c_im)

    apow = [(jnp.ones_like(abr), jnp.zeros_like(abi))]
    for _ in range(LB):
        apow.append(_cmul(apow[-1], (abr, abi)))
    hp = lax.Precision.HIGHEST

    def group_diagonal(vals, col_group_width):
        n_col = vals.shape[-1]
        col_group = (jnp.arange(n_col) // col_group_width) % OCT
        keep = (col_group[None, :] == jnp.arange(OCT)[:, None]).astype(BF16)
        out = vals.astype(BF16)[:, :, None, :, :] * keep[None, None, :, None, :]
        return out.reshape(N_OCT, LB * OCT * SSM_GROUP, n_col)

    def octets_last(v):
        lead = v.shape[1:-3]
        v = v.reshape((LB,) + lead + (N_OCT, OCT, SSM_GROUP, v.shape[-1]))
        nl = len(lead)
        perm = (1 + nl, 0, 3 + nl) + tuple(range(1, 1 + nl)) + (2 + nl, 4 + nl)
        return v.transpose(perm).reshape(N_OCT, LB, SSM_GROUP, -1)

    bb_t = (bb[0].transpose(0, 2, 1), bb[1].transpose(0, 2, 1))
    mx = [jnp.stack(_cmul((apow[LB - 1 - t][0][:, None, :], apow[LB - 1 - t][1][:, None, :]), bb_t))
          for t in range(LB)]
    w_x = group_diagonal(octets_last(jnp.stack(mx)), SSM_STATE)

    mc = []
    for t in range(LB):
        m = _cmul(cc, (apow[t + 1][0][:, None, :], apow[t + 1][1][:, None, :]))
        mc.append(jnp.stack([m[0], -m[1]]))
    w_c = group_diagonal(octets_last(jnp.stack(mc)), SSM_STATE)

    kd = []
    for d in range(LB):
        m = _cmul(cc, (apow[d][0][:, None, :], apow[d][1][:, None, :]))
        kd.append(jnp.einsum("ghp,gpk->gkh", m[0], bb[0], precision=hp)
                  - jnp.einsum("ghp,gpk->gkh", m[1], bb[1], precision=hp))
    zero = jnp.zeros_like(kd[0])
    lagged = jnp.stack([jnp.stack([kd[t - t0] if t >= t0 else zero for t in range(LB)]) for t0 in range(LB)])
    w_t = group_diagonal(octets_last(lagged), SSM_GROUP)

    ab = (apow[LB][0].reshape(1, SSM_CH), apow[LB][1].reshape(1, SSM_CH))
    pows = [ab]
    for _ in range(SUBLANES - 1):
        pows.append(_cmul(pows[-1], ab))
    rows = jnp.arange(SUBLANES)[:, None]
    lv_re = jnp.stack([jnp.where(rows >= d, pows[d - 1][0], 0.0) for d in (1, 2, 4)])
    lv_im = jnp.stack([jnp.where(rows >= d, pows[d - 1][1], 0.0) for d in (1, 2, 4)])
    cp_re = jnp.concatenate([pw[0] for pw in pows], axis=0)
    cp_im = jnp.concatenate([pw[1] for pw in pows], axis=0)
    return dict(w_t=w_t.astype(BF16), w_x=w_x.astype(BF16), w_c=w_c.astype(BF16), ab_re=ab[0], ab_im=ab[1],
                lv_re=lv_re, lv_im=lv_im, cp_re=cp_re, cp_im=cp_im)


def _memkv_kernel(m_ref, g_ref, w_ref, gk_ref, o128_ref, k_ref, v_ref):
    hm = _rms_rows(m_ref[0], g_ref[...]).astype(BF16)
    kv = jnp.dot(hm, w_ref[...], preferred_element_type=F32)
    k = kv[:, :MEM_WIDTH]
    k_ref[0] = k * lax.rsqrt(_seg_mean_sq(k, o128_ref[...], MEM_HD) + EPS) * gk_ref[...]
    v_ref[0] = kv[:, MEM_WIDTH:]


def _memkv(mem, p):
    n = mem.shape[0]
    blk = lambda i: (i, 0, 0)
    shp = jax.ShapeDtypeStruct((n, N_MEM, MEM_WIDTH), F32)
    return pl.pallas_call(
        _memkv_kernel,
        grid=(n,),
        in_specs=[pl.BlockSpec((1, N_MEM, D_MODEL), blk), _full((1, D_MODEL)),
                  _full((D_MODEL, 2 * MEM_WIDTH)), _full((1, MEM_WIDTH)), _full((MEM_WIDTH, MEM_WIDTH))],
        out_specs=(pl.BlockSpec((1, N_MEM, MEM_WIDTH), blk), pl.BlockSpec((1, N_MEM, MEM_WIDTH), blk)),
        out_shape=(shp, shp),
        compiler_params=_cparams(("parallel",)),
        name="memkv",
    )(mem, p["g_mem"], p["w_mem_kv"], p["g_km"], p["ones128"])


def _dup_heads(x, lane):
    sw = pltpu.roll(x, SWA_HD, axis=x.ndim - 1)
    lo = lane < SWA_HD
    return jnp.where(lo, x, sw), jnp.where(lo, sw, x)


def _swa_group(q_blk, kk, vv, g, mask, sink_ref, stack_heads):
    tq = q_blk.shape[-2]
    shp = q_blk.shape[:-1]
    lane = lax.broadcasted_iota(jnp.int32, shp + (LANES,), len(shp))
    rows = []
    sinks = []
    for hl in range(SWA_Q_PER_KV):
        h = g * SWA_Q_PER_KV + hl
        pair = q_blk[..., (h // 2) * LANES:(h // 2 + 1) * LANES]
        keep = (lane < SWA_HD) if h % 2 == 0 else (lane >= SWA_HD)
        rows.append(jnp.where(keep, pair, 0.0))
        sinks.append(jnp.full(shp + (1,), sink_ref[h], F32))

    def attend(qq, sk, msk):
        s = jnp.einsum("...qd,...kd->...qk", qq.astype(BF16), kk, preferred_element_type=F32)
        s = jnp.where(msk, s, NEG_INF)
        m = jnp.maximum(jnp.max(s, axis=-1, keepdims=True), sk)
        e = jnp.exp(s - m)
        pr = e / (jnp.sum(e, axis=-1, keepdims=True) + jnp.exp(sk - m))
        return jnp.einsum("...qk,...kd->...qd", pr.astype(BF16), vv, preferred_element_type=F32)

    if stack_heads:
        o = attend(jnp.concatenate(rows, axis=-2), jnp.concatenate(sinks, axis=-2), mask)
        outs = [o[..., hl * tq:(hl + 1) * tq, :] for hl in range(SWA_Q_PER_KV)]
    else:
        outs = [attend(rows[hl], sinks[hl], mask[..., :tq, :]) for hl in range(SWA_Q_PER_KV)]
    lo = lane < SWA_HD
    return [jnp.where(lo, outs[2 * j], outs[2 * j + 1]) for j in range(2)]


def _mem_heads(qm, k_head, v_head, row_chunk=WINDOW):
    n_rows = qm.shape[-2]
    step = min(row_chunk, n_rows)
    outs = []
    for h in range(MEM_HEADS):
        cs = slice(h * MEM_HD, (h + 1) * MEM_HD)
        kh = k_head(h).astype(BF16)
        vh = v_head(h).astype(BF16)
        chunks = []
        for r0 in range(0, n_rows, step):
            s = jnp.einsum("...qd,...kd->...qk", qm[..., r0:r0 + step, cs], kh, preferred_element_type=F32) * MEM_SCALE
            m = jnp.max(s, axis=-1, keepdims=True)
            e = jnp.exp(s - m)
            pr = e / jnp.sum(e, axis=-1, keepdims=True)
            chunks.append(jnp.einsum("...qk,...kd->...qd", pr.astype(BF16), vh, preferred_element_type=F32))
        outs.append(jnp.concatenate(chunks, axis=-2) if len(chunks) > 1 else chunks[0])
    return jnp.concatenate(outs, axis=-1)


def _attn_prompt_kernel(sink_ref, q_ref, k_ref, v_ref, kp_ref, vp_ref, qm_ref, mk_ref, mv_ref, ys_ref, ym_ref):
    tq = q_ref.shape[0]
    blk = WINDOW
    rows = SWA_Q_PER_KV * blk
    i = lax.broadcasted_iota(jnp.int32, (rows, 2 * blk), 0) % blk
    j = lax.broadcasted_iota(jnp.int32, (rows, 2 * blk), 1)
    lo = jnp.where(j < blk, i + 1, blk)
    hi = jnp.where(j < blk, blk, blk + i + 1)
    first_lo = jnp.where(pl.program_id(1) == 0, blk, 0)
    lane_k = lax.broadcasted_iota(jnp.int32, (2 * blk, LANES), 1)
    for b in range(tq // blk):
        rs = slice(b * blk, (b + 1) * blk)
        if b == 0:
            k2 = jnp.concatenate([kp_ref[...], k_ref[rs, :]], axis=0)
            v2 = jnp.concatenate([vp_ref[...], v_ref[rs, :]], axis=0)
            mask = (j >= jnp.maximum(lo, first_lo)) & (j < hi)
        else:
            k2 = k_ref[(b - 1) * blk:(b + 1) * blk, :]
            v2 = v_ref[(b - 1) * blk:(b + 1) * blk, :]
            mask = (j >= lo) & (j < hi)
        kks = _dup_heads(k2, lane_k)
        vvs = _dup_heads(v2, lane_k)
        q_blk = q_ref[rs, :].astype(F32)
        pairs = []
        for g in range(SWA_KV_HEADS):
            pairs += _swa_group(q_blk, kks[g].astype(BF16), vvs[g].astype(BF16), g, mask, sink_ref, False)
        ys_ref[rs, :] = jnp.concatenate(pairs, axis=1).astype(BF16)
    ym_ref[...] = _mem_heads(qm_ref[...], lambda h: mk_ref[0, :, h * MEM_HD:(h + 1) * MEM_HD],
                             lambda h: mv_ref[0, :, h * MEM_HD:(h + 1) * MEM_HD]).astype(BF16)


def _attn_prompt(q, k, v, qm, mk, mv, sinks, n_seq):
    t = q.shape[0]
    tq = TQ_ATT
    per = t // n_seq // tq
    sub = tq // WINDOW
    row = lambda n, c: (n * per + c, 0)
    prev = lambda n, c: (jnp.maximum((n * per + c) * sub - 1, 0), 0)
    memb = lambda n, c: (n, 0, 0)
    return pl.pallas_call(
        _attn_prompt_kernel,
        grid=(n_seq, per),
        in_specs=[
            pl.BlockSpec(memory_space=pltpu.SMEM),
            pl.BlockSpec((tq, SWA_WIDTH), row),
            pl.BlockSpec((tq, SWA_KV_WIDTH), row),
            pl.BlockSpec((tq, SWA_KV_WIDTH), row),
            pl.BlockSpec((WINDOW, SWA_KV_WIDTH), prev),
            pl.BlockSpec((WINDOW, SWA_KV_WIDTH), prev),
            pl.BlockSpec((tq, MEM_WIDTH), row),
            pl.BlockSpec((1, N_MEM, MEM_WIDTH), memb),
            pl.BlockSpec((1, N_MEM, MEM_WIDTH), memb),
        ],
        out_specs=(pl.BlockSpec((tq, SWA_WIDTH), row), pl.BlockSpec((tq, MEM_WIDTH), row)),
        out_shape=(jax.ShapeDtypeStruct((t, SWA_WIDTH), BF16), jax.ShapeDtypeStruct((t, MEM_WIDTH), BF16)),
        compiler_params=_cparams(("parallel", "parallel")),
        name="attn_prompt",
    )(sinks, q, k, v, k, v, qm, mk, mv)


def _attn_sample_kernel(sink_ref, q_ref, k_ref, v_ref, pk_ref, pv_ref, qm_ref, mk_ref, mv_ref,
                        ys_ref, ym_ref, nk_ref, nv_ref, *, s_len):
    sb, wb = pk_ref.shape[0], pk_ref.shape[1]
    n_keys = wb + s_len
    rows = SWA_Q_PER_KV * s_len
    i = lax.broadcasted_iota(jnp.int32, (sb, rows, n_keys), 1) % s_len
    j = lax.broadcasted_iota(jnp.int32, (sb, rows, n_keys), 2)
    rel = i + wb - j
    mask = (rel >= 0) & (rel < WINDOW)
    k_all = jnp.concatenate([pk_ref[...], k_ref[...].reshape(sb, s_len, SWA_KV_WIDTH)], axis=1)
    v_all = jnp.concatenate([pv_ref[...], v_ref[...].reshape(sb, s_len, SWA_KV_WIDTH)], axis=1)
    nk_ref[...] = k_all[:, n_keys - wb:, :]
    nv_ref[...] = v_all[:, n_keys - wb:, :]
    lane_k = lax.broadcasted_iota(jnp.int32, k_all.shape, 2)
    kks = _dup_heads(k_all, lane_k)
    vvs = _dup_heads(v_all, lane_k)
    q3 = q_ref[...].astype(F32).reshape(sb, s_len, SWA_WIDTH)
    pairs = []
    for g in range(SWA_KV_HEADS):
        pairs += _swa_group(q3, kks[g].astype(BF16), vvs[g].astype(BF16), g, mask, sink_ref, True)
    ys_ref[...] = jnp.concatenate(pairs, axis=-1).reshape(sb * s_len, SWA_WIDTH).astype(BF16)
    qm3 = qm_ref[...].astype(F32).reshape(sb, s_len, MEM_WIDTH).astype(BF16)
    head_rows = lambda h: pl.ds(h, N_MEM, stride=MEM_HEADS)
    ym = _mem_heads(qm3, lambda h: mk_ref[:, head_rows(h), :], lambda h: mv_ref[:, head_rows(h), :])
    ym_ref[...] = ym.reshape(sb * s_len, MEM_WIDTH).astype(BF16)


def _attn_sample(q, k, v, past_k, past_v, qm, mk, mv, sinks, s_len):
    t = q.shape[0]
    n_seq, wb = past_k.shape[0], past_k.shape[1]
    sb = SEQ_BLK
    rows = sb * s_len
    row = lambda c: (c, 0)
    seq = lambda c: (c, 0, 0)
    cache_shape = jax.ShapeDtypeStruct((n_seq, wb, SWA_KV_WIDTH), F32)
    return pl.pallas_call(
        functools.partial(_attn_sample_kernel, s_len=s_len),
        grid=(n_seq // sb,),
        in_specs=[
            pl.BlockSpec(memory_space=pltpu.SMEM),
            pl.BlockSpec((rows, SWA_WIDTH), row),
            pl.BlockSpec((rows, SWA_KV_WIDTH), row),
            pl.BlockSpec((rows, SWA_KV_WIDTH), row),
            pl.BlockSpec((sb, wb, SWA_KV_WIDTH), seq),
            pl.BlockSpec((sb, wb, SWA_KV_WIDTH), seq),
            pl.BlockSpec((rows, MEM_WIDTH), row),
            pl.BlockSpec((sb, N_MEM * MEM_HEADS, MEM_HD), seq),
            pl.BlockSpec((sb, N_MEM * MEM_HEADS, MEM_HD), seq),
        ],
        out_specs=(pl.BlockSpec((rows, SWA_WIDTH), row), pl.BlockSpec((rows, MEM_WIDTH), row),
                   pl.BlockSpec((sb, wb, SWA_KV_WIDTH), seq), pl.BlockSpec((sb, wb, SWA_KV_WIDTH), seq)),
        out_shape=(jax.ShapeDtypeStruct((t, SWA_WIDTH), BF16), jax.ShapeDtypeStruct((t, MEM_WIDTH), BF16),
                   cache_shape, cache_shape),
        compiler_params=_cparams(("parallel",)),
        name="attn_sample",
    )(sinks, q, k, v, past_k, past_v, qm, mk, mv)


def _first_argmax(x, valid, lane):
    xm = jnp.where(valid, x, -jnp.inf)
    mx = jnp.max(xm, axis=-1, keepdims=True)
    idx = jnp.min(jnp.where(xm == mx, lane, LANES), axis=-1, keepdims=True)
    return mx, lane == idx, idx


def _merge_kernel(xp_ref, yap_ref, ybp_ref, ycp_ref, xs_ref, yas_ref, ybs_ref, ycs_ref,
                  gmix_ref, wg_ref, wa_ref, wb_ref, wc_ref, wo_ref, gffn_ref, wr_hi_ref, wr_lo_ref, br_ref,
                  x1_ref, hx_ref, route_ref, cnt_ref, carry_ref, x1_prev, *, n_blk_p, n_blk):
    i = pl.program_id(0)

    @pl.when(i == 0)
    def _():
        x1_prev[...] = jnp.zeros_like(x1_prev)
        carry_ref[...] = jnp.zeros_like(carry_ref)

    x1_routed = x1_prev[...]

    is_prompt = jnp.minimum(i, n_blk - 1) < n_blk_p
    pick = lambda a_ref, b_ref: jnp.where(is_prompt, a_ref[...], b_ref[...])
    x = pick(xp_ref, xs_ref)
    h = _rms_rows(x, gmix_ref[...]).astype(BF16)

    def branch(k, y, w_ref):
        cols = slice(PROJ_A + k * D_MODEL, PROJ_A + (k + 1) * D_MODEL)
        gate = jax.nn.sigmoid(jnp.dot(h, wg_ref[:, cols], preferred_element_type=F32))
        return gate * jnp.dot(y.astype(BF16), w_ref[...], preferred_element_type=F32)

    routing = _route_rows(x1_routed, i >= 1, gffn_ref, wr_hi_ref, wr_lo_ref, br_ref, hx_ref, route_ref, cnt_ref,
                          carry_ref)
    next(routing)
    merged = branch(0, pick(yap_ref, yas_ref), wa_ref)
    next(routing)
    merged = merged + branch(1, pick(ybp_ref, ybs_ref), wb_ref)
    next(routing)
    merged = merged + branch(2, pick(ycp_ref, ycs_ref), wc_ref)
    next(routing, None)
    x1 = x + jnp.dot(merged.astype(BF16), wo_ref[...], preferred_element_type=F32)
    x1_ref[...] = x1
    x1_prev[...] = x1


def _route_rows(x1, valid, gffn_ref, wr_hi_ref, wr_lo_ref, br_ref, hx_ref, route_ref, cnt_ref, carry_ref):
    hn = _rms_rows(x1, gffn_ref[...])
    tm = x1.shape[0]
    slab_row = lambda c: pl.ds(c, tm, stride=HX_ROWS)
    for c in range(ROW_TILES):
        hx_ref[slab_row(c), :] = hn[:, c * LANES:(c + 1) * LANES]
    yield

    hi, lo = _split_bf16(hn)
    w_hi = wr_hi_ref[...]
    logits = (jnp.dot(hi, w_hi, preferred_element_type=F32) + jnp.dot(lo, w_hi, preferred_element_type=F32)
              + jnp.dot(hi, wr_lo_ref[...], preferred_element_type=F32)) + br_ref[...]
    yield
    lane = lax.broadcasted_iota(jnp.int32, logits.shape, 1)
    is_grp = lane < N_GROUPS_E
    g_max, _, g_idx = _first_argmax(logits, is_grp, lane)
    pg_top = 1.0 / jnp.sum(jnp.where(is_grp, jnp.exp(logits - g_max), 0.0), axis=-1, keepdims=True)
    e_lo = N_GROUPS_E + g_idx * EXPERTS_PER_GROUP
    in_grp = (lane >= e_lo) & (lane < e_lo + EXPERTS_PER_GROUP)
    e_max, first, i1 = _first_argmax(logits, in_grp, lane)
    ex = jnp.where(in_grp, jnp.exp(logits - e_max), 0.0)
    pe = ex / jnp.sum(ex, axis=-1, keepdims=True)
    _, second, i2 = _first_argmax(logits, in_grp & jnp.logical_not(first), lane)
    p1 = jnp.sum(jnp.where(first, pe, 0.0), axis=-1, keepdims=True)
    p2 = jnp.sum(jnp.where(second, pe, 0.0), axis=-1, keepdims=True)
    w1 = pg_top * p1 / (p1 + p2)
    w2 = pg_top * p2 / (p1 + p2)

    a1 = i1 - e_lo
    a2 = i2 - e_lo
    e_a = jnp.minimum(a1, a2)
    e_b = jnp.maximum(a1, a2)
    pair = jnp.right_shift(e_a * (2 * EXPERTS_PER_GROUP - 1 - e_a), 1) + (e_b - e_a - 1)
    bucket = g_idx * PAIRS_PER_GROUP + pair
    w_a = jnp.where(a1 < a2, w1, w2)
    w_b = jnp.where(a1 < a2, w2, w1)
    yield

    onehot = lane == jnp.where(valid, bucket, -1)
    tri = (lax.broadcasted_iota(jnp.int32, (tm, tm), 1) <= lax.broadcasted_iota(jnp.int32, (tm, tm), 0))
    csum = jnp.dot(jnp.where(tri, 1.0, 0.0).astype(BF16), jnp.where(onehot, 1.0, 0.0).astype(BF16),
                   preferred_element_type=F32)
    carry = carry_ref[...]
    rank = jnp.sum(jnp.where(onehot, csum + carry, 0.0), axis=-1, keepdims=True) - 1.0
    carry = carry + csum[tm - 1:tm, :]
    carry_ref[...] = carry
    cnt_ref[...] = carry
    route = jnp.where(lane == 0, bucket.astype(F32),
                      jnp.where(lane == 1, w_a, jnp.where(lane == 2, w_b, jnp.where(lane == 3, rank, 0.0))))
    route_ref[...] = route
    hx_ref[slab_row(ROW_TILES), :] = route
    for c in range(ROW_TILES + 1, HX_ROWS):
        hx_ref[slab_row(c), :] = jnp.zeros_like(route)


def _merge(prompt_rows, sample_rows, p):
    tm = TM_MRG
    t_p, t_s = prompt_rows[0].shape[0], sample_rows[0].shape[0]
    nbp = t_p // tm
    t_all = t_p + t_s
    n_blk = t_all // tm
    first = lambda i: (jnp.minimum(i, nbp - 1), 0)
    second = lambda i: (jnp.clip(i - nbp, 0, n_blk - nbp - 1), 0)
    merged_blk = lambda i: (jnp.minimum(i, n_blk - 1), 0)
    routed_blk = lambda i: (jnp.maximum(i - 1, 0), 0)
    widths = (D_MODEL, SSM_WIDTH, SWA_WIDTH, MEM_WIDTH)
    in_specs = ([pl.BlockSpec((tm, w), first) for w in widths] + [pl.BlockSpec((tm, w), second) for w in widths] + [
        _resident((1, D_MODEL)),
        _resident((D_MODEL, PROJ_A + N_BRANCH * D_MODEL)),
        _resident((SSM_WIDTH, D_MODEL)), _resident((SWA_WIDTH, D_MODEL)), _resident((MEM_WIDTH, D_MODEL)),
        _resident((D_MODEL, D_MODEL)),
        _resident((1, D_MODEL)),
        _resident((D_MODEL, LANES)), _resident((D_MODEL, LANES)), _resident((1, LANES)),
    ])
    return pl.pallas_call(
        functools.partial(_merge_kernel, n_blk_p=nbp, n_blk=n_blk),
        grid=(n_blk + 1,),
        in_specs=in_specs,
        out_specs=(pl.BlockSpec((tm, D_MODEL), merged_blk), pl.BlockSpec((tm * HX_ROWS, LANES), routed_blk),
                   pl.BlockSpec((tm, ROUTE_W), routed_blk), _full((1, LANES))),
        out_shape=(jax.ShapeDtypeStruct((t_all, D_MODEL), F32),
                   jax.ShapeDtypeStruct((t_all * HX_ROWS, LANES), F32),
                   jax.ShapeDtypeStruct((t_all, ROUTE_W), F32), jax.ShapeDtypeStruct((1, LANES), F32)),
        scratch_shapes=[pltpu.VMEM((1, LANES), F32), pltpu.VMEM((tm, D_MODEL), F32)],
        compiler_params=_cparams(("arbitrary",)),
        name="merge",
    )(*prompt_rows, *sample_rows, p["g_mix"], p["w_in"], p["w_br_ssm"], p["w_br_swa"], p["w_br_mem"], p["w_o"],
      p["g_ffn"], p["w_r_hi"], p["w_r_lo"], p["b_r"])


def _pos_kernel(route_ref, off_ref, coff_ref, pos_ref, cpos_ref):
    r = route_ref[...]
    lane = lax.broadcasted_iota(jnp.int32, r.shape, 1)
    mine = lane == r[:, 0:1].astype(jnp.int32)
    off = jnp.sum(jnp.where(mine, off_ref[...], 0.0), axis=-1, keepdims=True)
    coff = jnp.sum(jnp.where(mine, coff_ref[...], 0.0), axis=-1, keepdims=True)
    rank = r[:, 3:4]
    both = jnp.where(lane == 0, off + rank, coff + rank)
    pos_rows, cpos_rows = [], []
    for k in range(r.shape[0] // LANES):
        t = both[k * LANES:(k + 1) * LANES].T
        pos_rows.append(t[0:1, :])
        cpos_rows.append(t[1:2, :])
    pos_ref[...] = jnp.concatenate(pos_rows, axis=0).astype(jnp.int32)
    cpos_ref[...] = jnp.concatenate(cpos_rows, axis=0).astype(jnp.int32)


def _sorted_pos(route, off, coff):
    t = route.shape[0]
    tm = SUBLANES * LANES
    shp = jax.ShapeDtypeStruct((t // LANES, LANES), jnp.int32)
    pos, cpos = pl.pallas_call(
        _pos_kernel,
        grid=(t // tm,),
        in_specs=[pl.BlockSpec((tm, ROUTE_W), lambda i: (i, 0)), _full((1, LANES)), _full((1, LANES))],
        out_specs=(pl.BlockSpec((SUBLANES, LANES), lambda i: (i, 0)), pl.BlockSpec((SUBLANES, LANES), lambda i: (i, 0))),
        out_shape=(shp, shp),
        compiler_params=_cparams(("parallel",)),
        name="sorted_pos",
    )(route, off, coff)
    return pos.reshape(t), cpos.reshape(t)


def _inv_kernel(pos_ref, idx_ref):
    def body(t, _):
        idx_ref[pos_ref[t]] = t
        return 0

    lax.fori_loop(0, pos_ref.shape[0], body, 0, unroll=8)


def _invert(pos):
    return pl.pallas_call(
        _inv_kernel,
        in_specs=[pl.BlockSpec(memory_space=pltpu.SMEM)],
        out_specs=pl.BlockSpec(memory_space=pltpu.SMEM),
        out_shape=jax.ShapeDtypeStruct(pos.shape, jnp.int32),
        name="invert_perm",
    )(pos)


def _bucket_kernel(idx_ref, tg_ref, ta_ref, tb_ref, cb_ref, nr_ref, hx_hbm, wi_ref, wd_ref,
                   ys_ref, buf, sem):
    j = pl.program_id(0)
    n_real = nr_ref[0]

    last = idx_ref.shape[0] - 1

    def issue_row(tile_base, slot, r, dst_row, prio):
        src = idx_ref[jnp.minimum(tile_base + r, last)]
        pltpu.make_async_copy(hx_hbm.at[pl.ds(pl.multiple_of(src * HX_ROWS, HX_ROWS), HX_ROWS), :],
                              buf.at[slot, pl.ds(dst_row, HX_ROWS), :], sem.at[slot]).start(priority=prio)

    def wait_tile(slot):
        pltpu.make_async_copy(hx_hbm.at[pl.ds(0, TM_EXP * HX_ROWS), :], buf.at[slot], sem.at[slot]).wait()

    @pl.when(j == 0)
    def _():
        base = cb_ref[0]

        def body(r8, _):
            for k in range(SUBLANES):
                r = r8 * SUBLANES + k
                issue_row(base, 0, r, pl.multiple_of(r * HX_ROWS, HX_ROWS), k % 2)
            return 0

        lax.fori_loop(0, TM_EXP // SUBLANES, body, 0)

    @pl.when(j < n_real)
    def _():
        slot = j % 2
        wait_tile(slot)
        nxt = jnp.minimum(j + 1, n_real - 1)
        base = cb_ref[nxt]
        for r in range(TM_EXP):
            issue_row(base, 1 - slot, r, r * HX_ROWS, r % 2)

        slab_row = lambda c: buf[slot, pl.ds(c, TM_EXP, stride=HX_ROWS), :]
        x = jnp.concatenate([slab_row(c) for c in range(ROW_TILES)], axis=1).astype(BF16)
        route = slab_row(ROW_TILES)

        def ffn(e):
            gu = jnp.dot(x, wi_ref[0, e], preferred_element_type=F32)
            a = jax.nn.silu(gu[:, :D_FF]) * gu[:, D_FF:]
            return jnp.dot(a.astype(BF16), wd_ref[0, e], preferred_element_type=F32)

        y = route[:, 1:2] * ffn(ta_ref[j]) + route[:, 2:3] * ffn(tb_ref[j])
        for c in range(ROW_TILES):
            ys_ref[pl.ds(c, TM_EXP, stride=ROW_TILES), :] = y[:, c * LANES:(c + 1) * LANES]

        @pl.when(j == n_real - 1)
        def _():
            wait_tile(1 - slot)


def _bucket_ffn(hx, idx, tile_g, tile_a, tile_b, tile_cb, n_real, p):
    n_tiles = tile_a.shape[0]
    grp = lambda j, idx, tg, ta, tb, cb, nr: (tg[j], 0, 0, 0)
    out = lambda j, idx, tg, ta, tb, cb, nr: (jnp.minimum(j, nr[0] - 1), 0)
    epg = EXPERTS_PER_GROUP
    return pl.pallas_call(
        _bucket_kernel,
        grid_spec=pltpu.PrefetchScalarGridSpec(
            num_scalar_prefetch=6,
            grid=(n_tiles,),
            in_specs=[
                pl.BlockSpec(memory_space=pl.ANY),
                pl.BlockSpec((1, epg, D_MODEL, 2 * D_FF), grp), pl.BlockSpec((1, epg, D_FF, D_MODEL), grp),
            ],
            out_specs=pl.BlockSpec((TM_EXP * ROW_TILES, LANES), out),
            scratch_shapes=[pltpu.VMEM((2, TM_EXP * HX_ROWS, LANES), F32), pltpu.SemaphoreType.DMA((2,))],
        ),
        out_shape=jax.ShapeDtypeStruct((n_tiles * TM_EXP * ROW_TILES, LANES), F32),
        compiler_params=_cparams(("arbitrary",)),
        name="bucket_ffn",
    )(idx, tile_g, tile_a, tile_b, tile_cb, n_real, hx,
      p["w_exp_in"].reshape(N_GROUPS_E, epg, D_MODEL, 2 * D_FF),
      p["w_exp_down"].reshape(N_GROUPS_E, epg, D_FF, D_MODEL))


def _back_kernel(pos_ref, x1_ref, ys_hbm, o_ref, buf, sem, *, t0):
    i = pl.program_id(0)
    tm = o_ref.shape[0]

    def issue(tile, slot):
        base = t0 + tile * tm

        def body(r8, _):
            for k in range(SUBLANES):
                r = r8 * SUBLANES + k
                src = pl.multiple_of(pos_ref[base + r] * ROW_TILES, ROW_TILES)
                pltpu.make_async_copy(ys_hbm.at[pl.ds(src, ROW_TILES), :],
                                      buf.at[slot, pl.ds(pl.multiple_of(r * ROW_TILES, ROW_TILES), ROW_TILES), :],
                                      sem.at[slot]).start(priority=k % 2)
            return 0

        lax.fori_loop(0, tm // SUBLANES, body, 0)

    @pl.when(i == 0)
    def _():
        issue(0, 0)

    @pl.when(i + 1 < pl.num_programs(0))
    def _():
        issue(i + 1, (i + 1) % 2)

    slot = i % 2
    pltpu.make_async_copy(ys_hbm.at[pl.ds(0, tm * ROW_TILES), :], buf.at[slot], sem.at[slot]).wait()
    y = jnp.concatenate([buf[slot, pl.ds(c, tm, stride=ROW_TILES), :] for c in range(ROW_TILES)], axis=1)
    o_ref[...] = x1_ref[...] + y


def _unsort_add(x1, ys, pos, t0, t):
    tm = TM_BACK
    row = lambda i, pos: (i, 0)
    return pl.pallas_call(
        functools.partial(_back_kernel, t0=t0),
        grid_spec=pltpu.PrefetchScalarGridSpec(
            num_scalar_prefetch=1,
            grid=(t // tm,),
            in_specs=[pl.BlockSpec((tm, D_MODEL), lambda i, pos: (i + t0 // tm, 0)),
                      pl.BlockSpec(memory_space=pl.ANY)],
            out_specs=pl.BlockSpec((tm, D_MODEL), row),
            scratch_shapes=[pltpu.VMEM((2, tm * ROW_TILES, LANES), F32), pltpu.SemaphoreType.DMA((2,))],
        ),
        out_shape=jax.ShapeDtypeStruct((t, D_MODEL), F32),
        compiler_params=_cparams(("arbitrary",)),
        name="unsort_add",
    )(pos, x1, ys)


def _bucket_experts():
    lo, hi = [], []
    for g in range(N_GROUPS_E):
        for a in range(EXPERTS_PER_GROUP):
            for b in range(a + 1, EXPERTS_PER_GROUP):
                lo.append(g * EXPERTS_PER_GROUP + a)
                hi.append(g * EXPERTS_PER_GROUP + b)
    return jnp.asarray(lo, jnp.int32), jnp.asarray(hi, jnp.int32)


def _tile_tables(counts, n_tiles):
    cnt = counts[0, :N_BUCKETS].astype(jnp.int32)
    nt = (cnt + TM_EXP - 1) // TM_EXP
    tend = jnp.cumsum(nt)
    tstart = tend - nt
    cstart = jnp.cumsum(cnt) - cnt
    pad = lambda v: jnp.zeros((1, LANES), F32).at[0, :N_BUCKETS].set(v.astype(F32))
    j = jnp.arange(n_tiles, dtype=jnp.int32)
    b = jnp.minimum(jnp.sum((tend[None, :] <= j[:, None]).astype(jnp.int32), axis=1), N_BUCKETS - 1)
    in_bucket = (j - tstart[b]) * TM_EXP
    e_lo, e_hi = _bucket_experts()
    epg = EXPERTS_PER_GROUP
    return (pad(tstart * TM_EXP), pad(cstart), b // PAIRS_PER_GROUP, e_lo[b] % epg, e_hi[b] % epg,
            cstart[b] + in_bucket, tend[-1:])


def _rope_tables(first_pos, n_pos):
    half = SWA_HD // 2
    inv = ROPE_THETA ** (-np.arange(half, dtype=np.float64) / half)
    ang = (first_pos + np.arange(n_pos, dtype=np.float64))[:, None] * inv[None, :]
    cos = np.cos(ang)
    sin = np.sin(ang)
    cos = np.concatenate([cos, cos, cos, cos], axis=1)
    sin = np.concatenate([-sin, sin, -sin, sin], axis=1)
    return jnp.asarray(cos, F32), jnp.asarray(sin, F32)


def kernel(x_prompt, x_sample, mem_prompt, state_ssm_re, state_ssm_im, cache_swa_k, cache_swa_v, cache_mem_k, cache_mem_v, norm_mix, w_in, ssm_a_re, ssm_a_im, ssm_log_dt, ssm_b_re, ssm_b_im, ssm_c_re, ssm_c_im, ssm_d, w_glu, swa_q_norm, swa_k_norm, swa_sinks, norm_mem, w_mem_kv, mem_q_norm, mem_k_norm, w_br_ssm, w_br_swa, w_br_mem, w_o, norm_ffn, w_router_group, b_router_group, w_router_expert, b_router_expert, w_exp_in, w_exp_down):
    depth = w_in.shape[0]
    assert depth == 1
    nb, seq, _ = x_prompt.shape
    db, dseq, _ = x_sample.shape
    assert dseq == 2 * LB
    l = 0

    w_r = jnp.concatenate([w_router_group[l], w_router_expert[l]], axis=1)
    w_r = jnp.pad(w_r, ((0, 0), (0, LANES - w_r.shape[1])))
    w_r_hi = w_r.astype(BF16)
    b_r = jnp.pad(jnp.concatenate([b_router_group[l], b_router_expert[l]]), (0, LANES - N_GROUPS_E - N_EXPERTS))
    p = dict(
        g_mix=norm_mix[l][None], w_in=w_in[l].astype(BF16),
        g_q=jnp.tile(swa_q_norm[l], SWA_HEADS)[None], g_k=jnp.tile(swa_k_norm[l], SWA_KV_HEADS)[None],
        g_qm=jnp.tile(mem_q_norm[l], MEM_HEADS)[None], g_km=jnp.tile(mem_k_norm[l], MEM_HEADS)[None],
        ones64=_block_ones(SWA_WIDTH, SWA_HD), ones128=_block_ones(MEM_WIDTH, MEM_HD),
        ssm_d=ssm_d[l][None], w_glu=w_glu[l].astype(BF16),
        g_mem=norm_mem[l][None], w_mem_kv=w_mem_kv[l].astype(BF16),
        w_br_ssm=w_br_ssm[l].astype(BF16), w_br_swa=w_br_swa[l].astype(BF16), w_br_mem=w_br_mem[l].astype(BF16),
        w_o=w_o[l].astype(BF16), g_ffn=norm_ffn[l][None],
        w_r_hi=w_r_hi, w_r_lo=(w_r - w_r_hi.astype(F32)).astype(BF16), b_r=b_r[None],
        w_exp_in=w_exp_in[l].astype(BF16), w_exp_down=w_exp_down[l].astype(BF16),
    )
    p.update(_ssm_params(ssm_a_re[l], ssm_a_im[l], ssm_log_dt[l], ssm_b_re[l], ssm_b_im[l],
                         ssm_c_re[l], ssm_c_im[l]))
    sinks = swa_sinks[l]

    xp = x_prompt.reshape(nb * seq, D_MODEL)
    cos_p, sin_p = _rope_tables(0, seq)
    u, q, k, v, qm = _inproj(xp, cos_p, sin_p, seq // TM_IN, p)
    zeros_state = jnp.zeros((nb, 1, SSM_CH), F32)
    y_ssm, pr, pi = _ssm(u, zeros_state, zeros_state, p, n_seq=nb, chained=True)
    mk, mv = _memkv(mem_prompt, p)
    y_swa, y_mem = _attn_prompt(q, k, v, qm, mk, mv, sinks, nb)
    win = min(WINDOW, seq)
    last_win = lambda a: a.reshape(nb, seq, SWA_KV_WIDTH)[:, seq - win:].reshape(nb, win, SWA_KV_HEADS, SWA_HD)
    p_k, p_v = last_win(k), last_win(v)

    xs = x_sample.reshape(db * dseq, D_MODEL)
    cos_s, sin_s = _rope_tables(PAST_LEN, dseq)
    reps = TM_IN // dseq
    us, qs, ks, vs, qms = _inproj(xs, jnp.tile(cos_s, (reps, 1)), jnp.tile(sin_s, (reps, 1)), 1, p)
    two_rows = lambda st: jnp.repeat(st.reshape(db, SSM_CH), 2, axis=0)
    ys_ssm, sr, si = _ssm(us, two_rows(state_ssm_re), two_rows(state_ssm_im), p, n_seq=db, chained=False)
    sr, si = sr[1::2], si[1::2]
    wb = cache_swa_k.shape[2]
    ys_swa, ys_mem, s_k, s_v = _attn_sample(
        qs, ks, vs, cache_swa_k[l].reshape(db, wb, SWA_KV_WIDTH), cache_swa_v[l].reshape(db, wb, SWA_KV_WIDTH),
        qms, cache_mem_k.reshape(db, N_MEM * MEM_HEADS, MEM_HD), cache_mem_v.reshape(db, N_MEM * MEM_HEADS, MEM_HD),
        sinks, dseq)
    t_p, t_s = nb * seq, db * dseq
    t_all = t_p + t_s
    x1, hx, route, counts = _merge((xp, y_ssm, y_swa, y_mem), (xs, ys_ssm, ys_swa, ys_mem), p)

    n_tiles = pl.cdiv(t_all, TM_EXP) + N_BUCKETS
    off, coff, tile_g, tile_a, tile_b, tile_cb, n_real = _tile_tables(counts, n_tiles)
    pos, cpos = _sorted_pos(route, off, coff)
    idx = _invert(cpos)
    y_sorted = _bucket_ffn(hx, idx, tile_g, tile_a, tile_b, tile_cb, n_real, p)
    yp = _unsort_add(x1, y_sorted, pos, 0, t_p).reshape(nb, seq, D_MODEL)
    ys = _unsort_add(x1, y_sorted, pos, t_p, t_s).reshape(db, dseq, D_MODEL)

    g, s = SSM_GROUPS, SSM_STATE
    return (yp, ys,
            pr.reshape(1, nb, g, s), pi.reshape(1, nb, g, s),
            p_k[None], p_v[None],
            mk.reshape(1, nb, N_MEM, MEM_HEADS, MEM_HD), mv.reshape(1, nb, N_MEM, MEM_HEADS, MEM_HD),
            sr.reshape(1, db, g, s), si.reshape(1, db, g, s),
            s_k.reshape(1, db, wb, SWA_KV_HEADS, SWA_HD), s_v.reshape(1, db, wb, SWA_KV_HEADS, SWA_HD))
```

```python
import functools

import jax
import jax.numpy as jnp
import numpy as np
from jax import lax
from jax.experimental import pallas as pl
from jax.experimental.pallas import tpu as pltpu

F32 = jnp.float32
BF16 = jnp.bfloat16

D_MODEL = 1024
SSM_WIDTH = 512
SSM_GROUP = 16
SSM_GROUPS = 32
SSM_STATE = 64
SSM_CH = SSM_GROUPS * SSM_STATE
LB = 4
OCT = 8
N_OCT = SSM_GROUPS // OCT
OCT_CH = OCT * SSM_STATE
SWA_HEADS = 8
SWA_KV_HEADS = 2
SWA_Q_PER_KV = SWA_HEADS // SWA_KV_HEADS
SWA_HD = 64
SWA_WIDTH = SWA_HEADS * SWA_HD
SWA_KV_WIDTH = SWA_KV_HEADS * SWA_HD
WINDOW = 128
PAST_LEN = 16384
ROPE_THETA = 10000.0
N_MEM = 256
MEM_HEADS = 4
MEM_HD = 128
MEM_WIDTH = MEM_HEADS * MEM_HD
N_BRANCH = 3
PROJ_A = SSM_WIDTH + SWA_WIDTH + 2 * SWA_KV_WIDTH + MEM_WIDTH
N_GROUPS_E = 4
EXPERTS_PER_GROUP = 8
N_EXPERTS = 32
D_FF = 256
EPS = 1e-6
NEG_INF = -1e30
SWA_SCALE = SWA_HD ** -0.5
MEM_SCALE = MEM_HD ** -0.5

LANES = 128
SUBLANES = 8
VMEM_LIMIT = 56 * 1024 * 1024

TM_IN = 512
TB_SSM = 512
SCAN_W = 512
TQ_ATT = 512
SEQ_BLK = 16
TM_MRG = 512
TM_BACK = 512
TM_EXP = 192
GATHER_SLOTS = 3

ROUTE_W = LANES
ROW_TILES = D_MODEL // LANES
HX_ROWS = 2 * ROW_TILES
PAIRS_PER_GROUP = EXPERTS_PER_GROUP * (EXPERTS_PER_GROUP - 1) // 2
N_BUCKETS = N_GROUPS_E * PAIRS_PER_GROUP


def _cparams(sem):
    return pltpu.CompilerParams(dimension_semantics=sem, vmem_limit_bytes=VMEM_LIMIT)


def _full(shape):
    nd = len(shape)
    return pl.BlockSpec(shape, lambda *_: (0,) * nd)


def _resident(shape):
    nd = len(shape)
    return pl.BlockSpec(shape, lambda *_: (0,) * nd, pipeline_mode=pl.Buffered(1))


def _split_bf16(x):
    hi = x.astype(BF16)
    lo = (x - hi.astype(F32)).astype(BF16)
    return hi, lo


def _seg_mean_sq(x, ones_blk, width, two_pass=True):
    hi, lo = _split_bf16(x * x)
    s = jnp.dot(hi, ones_blk, preferred_element_type=F32)
    if two_pass:
        s = s + jnp.dot(lo, ones_blk, preferred_element_type=F32)
    return s * (1.0 / width)


def _rms_rows(x, gain):
    return x * lax.rsqrt(jnp.mean(x * x, axis=-1, keepdims=True) + EPS) * gain


def _block_ones(n, width):
    i = jnp.arange(n) // width
    return (i[:, None] == i[None, :]).astype(BF16)


def _rope_cols(x, cos, sin_signed, lane_in_head):
    n = x.shape[1]
    reps = n // LANES
    if reps > 1:
        cos = jnp.concatenate([cos] * reps, axis=1)
        sin_signed = jnp.concatenate([sin_signed] * reps, axis=1)
    half = SWA_HD // 2
    partner = jnp.where(lane_in_head < half, pltpu.roll(x, n - half, axis=1), pltpu.roll(x, half, axis=1))
    return x * cos + partner * sin_signed


def _inproj_kernel(x_ref, gmix_ref, w_ref, gq_ref, gk_ref, gm_ref, cos_ref, sin_ref, o64_ref, o128_ref,
                   u_ref, q_ref, k_ref, v_ref, qm_ref, u_scr):
    x = x_ref[...]
    h = _rms_rows(x, gmix_ref[...]).astype(BF16)
    proj = jnp.dot(h, w_ref[...], preferred_element_type=F32)
    c0 = SSM_WIDTH
    c1 = c0 + SWA_WIDTH
    c2 = c1 + SWA_KV_WIDTH
    c3 = c2 + SWA_KV_WIDTH
    n_blk = u_ref.shape[0]
    for c in range(c0 // LANES):
        u_scr[c] = proj[:, c * LANES:(c + 1) * LANES]
        for t in range(LB):
            u_ref[:, t * c0 + c * LANES:t * c0 + (c + 1) * LANES] = u_scr[c, pl.ds(t, n_blk, stride=LB), :]
    q = proj[:, c0:c1]
    k = proj[:, c1:c2]
    v_ref[...] = proj[:, c2:c3]
    qm = proj[:, c3:]
    cos = cos_ref[...]
    sin = sin_ref[...]
    o64 = o64_ref[...]
    lane_q = lax.broadcasted_iota(jnp.int32, q.shape, 1) % SWA_HD
    qn = q * lax.rsqrt(_seg_mean_sq(q, o64, SWA_HD, two_pass=False) + EPS) * gq_ref[...]
    q_ref[...] = (_rope_cols(qn, cos, sin, lane_q) * SWA_SCALE).astype(BF16)
    lane_k = lax.broadcasted_iota(jnp.int32, k.shape, 1) % SWA_HD
    kn = k * lax.rsqrt(_seg_mean_sq(k, o64[:SWA_KV_WIDTH, :SWA_KV_WIDTH], SWA_HD) + EPS) * gk_ref[...]
    k_ref[...] = _rope_cols(kn, cos, sin, lane_k)
    qmn = qm * lax.rsqrt(_seg_mean_sq(qm, o128_ref[...], MEM_HD, two_pass=False) + EPS) * gm_ref[...]
    qm_ref[...] = qmn.astype(BF16)


def _inproj(x2d, cos, sin, pos_blocks, p):
    t = x2d.shape[0]
    tm = TM_IN
    grid = (t // tm,)
    row = lambda i: (i, 0)
    tab = lambda i: (i % pos_blocks, 0)
    out_shape = (
        jax.ShapeDtypeStruct((t // LB, LB * SSM_WIDTH), F32),
        jax.ShapeDtypeStruct((t, SWA_WIDTH), BF16),
        jax.ShapeDtypeStruct((t, SWA_KV_WIDTH), F32),
        jax.ShapeDtypeStruct((t, SWA_KV_WIDTH), F32),
        jax.ShapeDtypeStruct((t, MEM_WIDTH), BF16),
    )
    return pl.pallas_call(
        _inproj_kernel,
        grid=grid,
        in_specs=[
            pl.BlockSpec((tm, D_MODEL), row),
            _full((1, D_MODEL)),
            _resident((D_MODEL, PROJ_A)),
            _full((1, SWA_WIDTH)),
            _full((1, SWA_KV_WIDTH)),
            _full((1, MEM_WIDTH)),
            pl.BlockSpec((tm, LANES), tab),
            pl.BlockSpec((tm, LANES), tab),
            _full((SWA_WIDTH, SWA_WIDTH)),
            _full((MEM_WIDTH, MEM_WIDTH)),
        ],
        out_specs=(
            pl.BlockSpec((tm // LB, LB * SSM_WIDTH), row),
            pl.BlockSpec((tm, SWA_WIDTH), row),
            pl.BlockSpec((tm, SWA_KV_WIDTH), row),
            pl.BlockSpec((tm, SWA_KV_WIDTH), row),
            pl.BlockSpec((tm, MEM_WIDTH), row),
        ),
        out_shape=out_shape,
        scratch_shapes=[pltpu.VMEM((SSM_WIDTH // LANES, tm, LANES), F32)],
        compiler_params=_cparams(("parallel",)),
        name="inproj",
    )(x2d, p["g_mix"], p["w_in"], p["g_q"], p["g_k"], p["g_qm"], cos, sin, p["ones64"], p["ones128"])


def _cmul(x, y):
    return x[0] * y[0] - x[1] * y[1], x[0] * y[1] + x[1] * y[0]


def _ssm_kernel(u_ref, s0r_ref, s0i_ref, wx_ref, wt_ref, wc_ref, d_ref, wglu_ref, abr_ref, abi_ref,
                lvr_ref, lvi_ref, cpr_ref, cpi_ref,
                y_ref, fr_ref, fi_ref, sr_ref, si_ref, car_ref, cai_ref, o_scr, *, chained):
    tb = u_ref.shape[0]
    if chained:
        @pl.when(pl.program_id(1) == 0)
        def _():
            car_ref[...] = s0r_ref[0]
            cai_ref[...] = s0i_ref[0]

    u = u_ref[...]
    ub = u.astype(BF16)
    lhs = [jnp.concatenate([ub[:, t * SSM_WIDTH + c * LANES:t * SSM_WIDTH + (c + 1) * LANES] for t in range(LB)],
                           axis=1) for c in range(N_OCT)]
    for c in range(N_OCT):
        x = jnp.dot(lhs[c], wx_ref[c], preferred_element_type=F32)
        cs = slice(c * OCT_CH, (c + 1) * OCT_CH)
        sr_ref[:, cs] = x[:, :OCT_CH]
        si_ref[:, cs] = x[:, OCT_CH:]

    if chained:
        first_row = lax.broadcasted_iota(jnp.int32, (SUBLANES, SCAN_W), 0) == 0
        for sl in range(SSM_CH // SCAN_W):
            cols = slice(sl * SCAN_W, (sl + 1) * SCAN_W)
            lv = [(lvr_ref[j, :, cols], lvi_ref[j, :, cols]) for j in range(3)]
            cpr = cpr_ref[:, cols]
            cpi = cpi_ref[:, cols]

            def tile(i, carry, cols=cols, lv=lv, cpr=cpr, cpi=cpi):
                r0 = pl.multiple_of(i * SUBLANES, SUBLANES)
                xr = sr_ref[pl.ds(r0, SUBLANES), cols]
                xi = si_ref[pl.ds(r0, SUBLANES), cols]
                for j, d in enumerate((1, 2, 4)):
                    pr, pi = lv[j]
                    shr = pltpu.roll(xr, d, axis=0)
                    shi = pltpu.roll(xi, d, axis=0)
                    xr, xi = xr + pr * shr - pi * shi, xi + pr * shi + pi * shr
                cb_r = jnp.broadcast_to(carry[0], xr.shape)
                cb_i = jnp.broadcast_to(carry[1], xr.shape)
                xr, xi = xr + cpr * cb_r - cpi * cb_i, xi + cpr * cb_i + cpi * cb_r
                sr_ref[pl.ds(r0, SUBLANES), cols] = jnp.where(first_row, cb_r, pltpu.roll(xr, 1, axis=0))
                si_ref[pl.ds(r0, SUBLANES), cols] = jnp.where(first_row, cb_i, pltpu.roll(xi, 1, axis=0))
                return xr[SUBLANES - 1:SUBLANES, :], xi[SUBLANES - 1:SUBLANES, :]

            c_r, c_i = lax.fori_loop(0, tb // SUBLANES, tile, (car_ref[:, cols], cai_ref[:, cols]), unroll=True)
            car_ref[:, cols] = c_r
            cai_ref[:, cols] = c_i
        fr_ref[0] = car_ref[...]
        fi_ref[0] = cai_ref[...]
    else:
        odd = lax.broadcasted_iota(jnp.int32, (tb, SSM_CH), 0) % 2 == 1
        ab = (abr_ref[...], abi_ref[...])
        s0 = (s0r_ref[...], s0i_ref[...])
        x = (sr_ref[...], si_ref[...])
        e_first = _cmul(ab, s0)
        e_first = (e_first[0] + x[0], e_first[1] + x[1])
        prev = (pltpu.roll(e_first[0], 1, axis=0), pltpu.roll(e_first[1], 1, axis=0))
        e_second = _cmul(ab, prev)
        fr_ref[...] = jnp.where(odd, e_second[0] + x[0], e_first[0])
        fi_ref[...] = jnp.where(odd, e_second[1] + x[1], e_first[1])
        sr_ref[...] = jnp.where(odd, prev[0], s0[0])
        si_ref[...] = jnp.where(odd, prev[1], s0[1])

    ys = [[None] * N_OCT for _ in range(LB)]
    for c in range(N_OCT):
        cs = slice(c * OCT_CH, (c + 1) * OCT_CH)
        s_in = jnp.concatenate([sr_ref[:, cs], si_ref[:, cs]], axis=1).astype(BF16)
        yc = (jnp.dot(lhs[c], wt_ref[c], preferred_element_type=F32)
              + lax.dot_general(s_in, wc_ref[c], (((1,), (1,)), ((), ())), preferred_element_type=F32))
        for t in range(LB):
            ys[t][c] = yc[:, t * LANES:(t + 1) * LANES]
    y = jnp.concatenate([jnp.concatenate(ys[t], axis=1) for t in range(LB)], axis=0)
    us = jnp.concatenate([u[:, t * SSM_WIDTH:(t + 1) * SSM_WIDTH] for t in range(LB)], axis=0)
    y = jax.nn.gelu(y + d_ref[...] * us)
    gate = jax.nn.sigmoid(jnp.dot(y.astype(BF16), wglu_ref[...], preferred_element_type=F32))
    out = y * gate
    for c in range(SSM_WIDTH // LANES):
        for t in range(LB):
            o_scr[c, pl.ds(t, tb, stride=LB), :] = out[t * tb:(t + 1) * tb, c * LANES:(c + 1) * LANES]
        y_ref[:, c * LANES:(c + 1) * LANES] = o_scr[c]


def _ssm(u_blk, s0r, s0i, p, *, n_seq, chained):
    rows = u_blk.shape[0]
    if chained:
        tb = TB_SSM
        per = rows // n_seq // tb
        grid = (n_seq, per)
        row = lambda n, c: (n * per + c, 0)
        st = lambda n, c: (n, 0, 0)
        s0_spec = pl.BlockSpec((1, 1, SSM_CH), st)
        f_spec = pl.BlockSpec((1, 1, SSM_CH), st)
        f_shape = jax.ShapeDtypeStruct((n_seq, 1, SSM_CH), F32)
        sem = ("parallel", "arbitrary")
    else:
        tb = min(TB_SSM, rows)
        grid = (rows // tb,)
        row = lambda c: (c, 0)
        s0_spec = pl.BlockSpec((tb, SSM_CH), row)
        f_spec = pl.BlockSpec((tb, SSM_CH), row)
        f_shape = jax.ShapeDtypeStruct((rows, SSM_CH), F32)
        sem = ("parallel",)
    blk_w = LB * SSM_WIDTH
    return pl.pallas_call(
        functools.partial(_ssm_kernel, chained=chained),
        grid=grid,
        in_specs=[
            pl.BlockSpec((tb, blk_w), row),
            s0_spec, s0_spec,
            _resident((N_OCT, LB * LANES, 2 * OCT_CH)), _resident((N_OCT, LB * LANES, LB * LANES)),
            _resident((N_OCT, LB * LANES, 2 * OCT_CH)),
            _full((1, SSM_WIDTH)),
            _resident((SSM_WIDTH, SSM_WIDTH)),
            _full((1, SSM_CH)), _full((1, SSM_CH)),
            _full((3, SUBLANES, SSM_CH)), _full((3, SUBLANES, SSM_CH)),
            _full((SUBLANES, SSM_CH)), _full((SUBLANES, SSM_CH)),
        ],
        out_specs=(pl.BlockSpec((tb * LB, SSM_WIDTH), row), f_spec, f_spec),
        out_shape=(jax.ShapeDtypeStruct((rows * LB, SSM_WIDTH), F32), f_shape, f_shape),
        scratch_shapes=[
            pltpu.VMEM((tb, SSM_CH), F32), pltpu.VMEM((tb, SSM_CH), F32),
            pltpu.VMEM((1, SSM_CH), F32), pltpu.VMEM((1, SSM_CH), F32),
            pltpu.VMEM((SSM_WIDTH // LANES, tb * LB, LANES), F32),
        ],
        compiler_params=_cparams(sem),
        name="ssm_chained" if chained else "ssm_pairs",
    )(u_blk, s0r, s0i, p["w_x"], p["w_t"], p["w_c"], p["ssm_d"], p["w_glu"], p["ab_re"], p["ab_im"],
      p["lv_re"], p["lv_im"], p["cp_re"], p["cp_im"])


def _ssm_params(a_re, a_im, log_dt, b_re, b_im, c_re, c_im):
    dt = jnp.exp(log_dt)[:, None]
    mag = jnp.exp(a_re * dt)
    abr = mag * jnp.cos(a_im * dt)
    abi = mag * jnp.sin(a_im * dt)
    den = a_re * a_re + a_im * a_im
    nr = abr - 1.0
    ni = abi
    coef_re = (nr * a_re + ni * a_im) / den
    coef_im = (ni * a_re - nr * a_im) / den
    bb = (coef_re[..., None] * b_re - coef_im[..., None] * b_im,
          coef_re[..., None] * b_im + coef_im[..., None] * b_re)
    cc = (c_re, c_im)

    apow = [(jnp.ones_like(abr), jnp.zeros_like(abi))]
    for _ in range(LB):
        apow.append(_cmul(apow[-1], (abr, abi)))
    hp = lax.Precision.HIGHEST

    def group_diagonal(vals, col_group_width):
        n_col = vals.shape[-1]
        col_group = (jnp.arange(n_col) // col_group_width) % OCT
        keep = (col_group[None, :] == jnp.arange(OCT)[:, None]).astype(BF16)
        out = vals.astype(BF16)[:, :, None, :, :] * keep[None, None, :, None, :]
        return out.reshape(N_OCT, LB * OCT * SSM_GROUP, n_col)

    def octets_last(v):
        lead = v.shape[1:-3]
        v = v.reshape((LB,) + lead + (N_OCT, OCT, SSM_GROUP, v.shape[-1]))
        nl = len(lead)
        perm = (1 + nl, 0, 3 + nl) + tuple(range(1, 1 + nl)) + (2 + nl, 4 + nl)
        return v.transpose(perm).reshape(N_OCT, LB, SSM_GROUP, -1)

    bb_t = (bb[0].transpose(0, 2, 1), bb[1].transpose(0, 2, 1))
    mx = [jnp.stack(_cmul((apow[LB - 1 - t][0][:, None, :], apow[LB - 1 - t][1][:, None, :]), bb_t))
          for t in range(LB)]
    w_x = group_diagonal(octets_last(jnp.stack(mx)), SSM_STATE)

    mc = []
    for t in range(LB):
        m = _cmul(cc, (apow[t + 1][0][:, None, :], apow[t + 1][1][:, None, :]))
        mc.append(jnp.stack([m[0], -m[1]]))
    w_c = group_diagonal(octets_last(jnp.stack(mc)), SSM_STATE)

    kd = []
    for d in range(LB):
        m = _cmul(cc, (apow[d][0][:, None, :], apow[d][1][:, None, :]))
        kd.append(jnp.einsum("ghp,gpk->gkh", m[0], bb[0], precision=hp)
                  - jnp.einsum("ghp,gpk->gkh", m[1], bb[1], precision=hp))
    zero = jnp.zeros_like(kd[0])
    lagged = jnp.stack([jnp.stack([kd[t - t0] if t >= t0 else zero for t in range(LB)]) for t0 in range(LB)])
    w_t = group_diagonal(octets_last(lagged), SSM_GROUP)

    ab = (apow[LB][0].reshape(1, SSM_CH), apow[LB][1].reshape(1, SSM_CH))
    pows = [ab]
    for _ in range(SUBLANES - 1):
        pows.append(_cmul(pows[-1], ab))
    rows = jnp.arange(SUBLANES)[:, None]
    lv_re = jnp.stack([jnp.where(rows >= d, pows[d - 1][0], 0.0) for d in (1, 2, 4)])
    lv_im = jnp.stack([jnp.where(rows >= d, pows[d - 1][1], 0.0) for d in (1, 2, 4)])
    cp_re = jnp.concatenate([pw[0] for pw in pows], axis=0)
    cp_im = jnp.concatenate([pw[1] for pw in pows], axis=0)
    return dict(w_t=w_t.astype(BF16), w_x=w_x.astype(BF16), w_c=w_c.astype(BF16), ab_re=ab[0], ab_im=ab[1],
                lv_re=lv_re, lv_im=lv_im, cp_re=cp_re, cp_im=cp_im)


def _memkv_kernel(m_ref, g_ref, w_ref, gk_ref, o128_ref, k_ref, v_ref):
    hm = _rms_rows(m_ref[0], g_ref[...]).astype(BF16)
    kv = jnp.dot(hm, w_ref[...], preferred_element_type=F32)
    k = kv[:, :MEM_WIDTH]
    k_ref[0] = k * lax.rsqrt(_seg_mean_sq(k, o128_ref[...], MEM_HD) + EPS) * gk_ref[...]
    v_ref[0] = kv[:, MEM_WIDTH:]


def _memkv(mem, p):
    n = mem.shape[0]
    blk = lambda i: (i, 0, 0)
    shp = jax.ShapeDtypeStruct((n, N_MEM, MEM_WIDTH), F32)
    return pl.pallas_call(
        _memkv_kernel,
        grid=(n,),
        in_specs=[pl.BlockSpec((1, N_MEM, D_MODEL), blk), _full((1, D_MODEL)),
                  _full((D_MODEL, 2 * MEM_WIDTH)), _full((1, MEM_WIDTH)), _full((MEM_WIDTH, MEM_WIDTH))],
        out_specs=(pl.BlockSpec((1, N_MEM, MEM_WIDTH), blk), pl.BlockSpec((1, N_MEM, MEM_WIDTH), blk)),
        out_shape=(shp, shp),
        compiler_params=_cparams(("parallel",)),
        name="memkv",
    )(mem, p["g_mem"], p["w_mem_kv"], p["g_km"], p["ones128"])


def _dup_heads(x, lane):
    sw = pltpu.roll(x, SWA_HD, axis=x.ndim - 1)
    lo = lane < SWA_HD
    return jnp.where(lo, x, sw), jnp.where(lo, sw, x)


def _swa_group(q_blk, kk, vv, g, mask, sink_ref, stack_heads):
    tq = q_blk.shape[-2]
    shp = q_blk.shape[:-1]
    lane = lax.broadcasted_iota(jnp.int32, shp + (LANES,), len(shp))
    rows = []
    sinks = []
    for hl in range(SWA_Q_PER_KV):
        h = g * SWA_Q_PER_KV + hl
        pair = q_blk[..., (h // 2) * LANES:(h // 2 + 1) * LANES]
        keep = (lane < SWA_HD) if h % 2 == 0 else (lane >= SWA_HD)
        rows.append(jnp.where(keep, pair, 0.0))
        sinks.append(jnp.full(shp + (1,), sink_ref[h], F32))

    def attend(qq, sk, msk):
        s = jnp.einsum("...qd,...kd->...qk", qq.astype(BF16), kk, preferred_element_type=F32)
        s = jnp.where(msk, s, NEG_INF)
        m = jnp.maximum(jnp.max(s, axis=-1, keepdims=True), sk)
        e = jnp.exp(s - m)
        pr = e / (jnp.sum(e, axis=-1, keepdims=True) + jnp.exp(sk - m))
        return jnp.einsum("...qk,...kd->...qd", pr.astype(BF16), vv, preferred_element_type=F32)

    if stack_heads:
        o = attend(jnp.concatenate(rows, axis=-2), jnp.concatenate(sinks, axis=-2), mask)
        outs = [o[..., hl * tq:(hl + 1) * tq, :] for hl in range(SWA_Q_PER_KV)]
    else:
        outs = [attend(rows[hl], sinks[hl], mask[..., :tq, :]) for hl in range(SWA_Q_PER_KV)]
    lo = lane < SWA_HD
    return [jnp.where(lo, outs[2 * j], outs[2 * j + 1]) for j in range(2)]


def _mem_heads(qm, k_head, v_head, row_chunk=WINDOW):
    n_rows = qm.shape[-2]
    step = min(row_chunk, n_rows)
    outs = []
    for h in range(MEM_HEADS):
        cs = slice(h * MEM_HD, (h + 1) * MEM_HD)
        kh = k_head(h).astype(BF16)
        vh = v_head(h).astype(BF16)
        chunks = []
        for r0 in range(0, n_rows, step):
            s = jnp.einsum("...qd,...kd->...qk", qm[..., r0:r0 + step, cs], kh, preferred_element_type=F32) * MEM_SCALE
            m = jnp.max(s, axis=-1, keepdims=True)
            e = jnp.exp(s - m)
            pr = e / jnp.sum(e, axis=-1, keepdims=True)
            chunks.append(jnp.einsum("...qk,...kd->...qd", pr.astype(BF16), vh, preferred_element_type=F32))
        outs.append(jnp.concatenate(chunks, axis=-2) if len(chunks) > 1 else chunks[0])
    return jnp.concatenate(outs, axis=-1)


def _attn_prompt_kernel(sink_ref, q_ref, k_ref, v_ref, kp_ref, vp_ref, qm_ref, mk_ref, mv_ref, ys_ref, ym_ref):
    tq = q_ref.shape[0]
    blk = WINDOW
    rows = SWA_Q_PER_KV * blk
    i = lax.broadcasted_iota(jnp.int32, (rows, 2 * blk), 0) % blk
    j = lax.broadcasted_iota(jnp.int32, (rows, 2 * blk), 1)
    lo = jnp.where(j < blk, i + 1, blk)
    hi = jnp.where(j < blk, blk, blk + i + 1)
    first_lo = jnp.where(pl.program_id(1) == 0, blk, 0)
    lane_k = lax.broadcasted_iota(jnp.int32, (2 * blk, LANES), 1)
    for b in range(tq // blk):
        rs = slice(b * blk, (b + 1) * blk)
        if b == 0:
            k2 = jnp.concatenate([kp_ref[...], k_ref[rs, :]], axis=0)
            v2 = jnp.concatenate([vp_ref[...], v_ref[rs, :]], axis=0)
            mask = (j >= jnp.maximum(lo, first_lo)) & (j < hi)
        else:
            k2 = k_ref[(b - 1) * blk:(b + 1) * blk, :]
            v2 = v_ref[(b - 1) * blk:(b + 1) * blk, :]
            mask = (j >= lo) & (j < hi)
        kks = _dup_heads(k2, lane_k)
        vvs = _dup_heads(v2, lane_k)
        q_blk = q_ref[rs, :].astype(F32)
        pairs = []
        for g in range(SWA_KV_HEADS):
            pairs += _swa_group(q_blk, kks[g].astype(BF16), vvs[g].astype(BF16), g, mask, sink_ref, False)
        ys_ref[rs, :] = jnp.concatenate(pairs, axis=1).astype(BF16)
    ym_ref[...] = _mem_heads(qm_ref[...], lambda h: mk_ref[0, :, h * MEM_HD:(h + 1) * MEM_HD],
                             lambda h: mv_ref[0, :, h * MEM_HD:(h + 1) * MEM_HD]).astype(BF16)


def _attn_prompt(q, k, v, qm, mk, mv, sinks, n_seq):
    t = q.shape[0]
    tq = TQ_ATT
    per = t // n_seq // tq
    sub = tq // WINDOW
    row = lambda n, c: (n * per + c, 0)
    prev = lambda n, c: (jnp.maximum((n * per + c) * sub - 1, 0), 0)
    memb = lambda n, c: (n, 0, 0)
    return pl.pallas_call(
        _attn_prompt_kernel,
        grid=(n_seq, per),
        in_specs=[
            pl.BlockSpec(memory_space=pltpu.SMEM),
            pl.BlockSpec((tq, SWA_WIDTH), row),
            pl.BlockSpec((tq, SWA_KV_WIDTH), row),
            pl.BlockSpec((tq, SWA_KV_WIDTH), row),
            pl.BlockSpec((WINDOW, SWA_KV_WIDTH), prev),
            pl.BlockSpec((WINDOW, SWA_KV_WIDTH), prev),
            pl.BlockSpec((tq, MEM_WIDTH), row),
            pl.BlockSpec((1, N_MEM, MEM_WIDTH), memb),
            pl.BlockSpec((1, N_MEM, MEM_WIDTH), memb),
        ],
        out_specs=(pl.BlockSpec((tq, SWA_WIDTH), row), pl.BlockSpec((tq, MEM_WIDTH), row)),
        out_shape=(jax.ShapeDtypeStruct((t, SWA_WIDTH), BF16), jax.ShapeDtypeStruct((t, MEM_WIDTH), BF16)),
        compiler_params=_cparams(("parallel", "parallel")),
        name="attn_prompt",
    )(sinks, q, k, v, k, v, qm, mk, mv)


def _attn_sample_kernel(sink_ref, q_ref, k_ref, v_ref, pk_ref, pv_ref, qm_ref, mk_ref, mv_ref,
                        ys_ref, ym_ref, nk_ref, nv_ref, *, s_len):
    sb, wb = pk_ref.shape[0], pk_ref.shape[1]
    n_keys = wb + s_len
    rows = SWA_Q_PER_KV * s_len
    i = lax.broadcasted_iota(jnp.int32, (sb, rows, n_keys), 1) % s_len
    j = lax.broadcasted_iota(jnp.int32, (sb, rows, n_keys), 2)
    rel = i + wb - j
    mask = (rel >= 0) & (rel < WINDOW)
    k_all = jnp.concatenate([pk_ref[...], k_ref[...].reshape(sb, s_len, SWA_KV_WIDTH)], axis=1)
    v_all = jnp.concatenate([pv_ref[...], v_ref[...].reshape(sb, s_len, SWA_KV_WIDTH)], axis=1)
    nk_ref[...] = k_all[:, n_keys - wb:, :]
    nv_ref[...] = v_all[:, n_keys - wb:, :]
    lane_k = lax.broadcasted_iota(jnp.int32, k_all.shape, 2)
    kks = _dup_heads(k_all, lane_k)
    vvs = _dup_heads(v_all, lane_k)
    q3 = q_ref[...].astype(F32).reshape(sb, s_len, SWA_WIDTH)
    pairs = []
    for g in range(SWA_KV_HEADS):
        pairs += _swa_group(q3, kks[g].astype(BF16), vvs[g].astype(BF16), g, mask, sink_ref, True)
    ys_ref[...] = jnp.concatenate(pairs, axis=-1).reshape(sb * s_len, SWA_WIDTH).astype(BF16)
    qm3 = qm_ref[...].astype(F32).reshape(sb, s_len, MEM_WIDTH).astype(BF16)
    head_rows = lambda h: pl.ds(h, N_MEM, stride=MEM_HEADS)
    ym = _mem_heads(qm3, lambda h: mk_ref[:, head_rows(h), :], lambda h: mv_ref[:, head_rows(h), :])
    ym_ref[...] = ym.reshape(sb * s_len, MEM_WIDTH).astype(BF16)


def _attn_sample(q, k, v, past_k, past_v, qm, mk, mv, sinks, s_len):
    t = q.shape[0]
    n_seq, wb = past_k.shape[0], past_k.shape[1]
    sb = SEQ_BLK
    rows = sb * s_len
    row = lambda c: (c, 0)
    seq = lambda c: (c, 0, 0)
    cache_shape = jax.ShapeDtypeStruct((n_seq, wb, SWA_KV_WIDTH), F32)
    return pl.pallas_call(
        functools.partial(_attn_sample_kernel, s_len=s_len),
        grid=(n_seq // sb,),
        in_specs=[
            pl.BlockSpec(memory_space=pltpu.SMEM),
            pl.BlockSpec((rows, SWA_WIDTH), row),
            pl.BlockSpec((rows, SWA_KV_WIDTH), row),
            pl.BlockSpec((rows, SWA_KV_WIDTH), row),
            pl.BlockSpec((sb, wb, SWA_KV_WIDTH), seq),
            pl.BlockSpec((sb, wb, SWA_KV_WIDTH), seq),
            pl.BlockSpec((rows, MEM_WIDTH), row),
            pl.BlockSpec((sb, N_MEM * MEM_HEADS, MEM_HD), seq),
            pl.BlockSpec((sb, N_MEM * MEM_HEADS, MEM_HD), seq),
        ],
        out_specs=(pl.BlockSpec((rows, SWA_WIDTH), row), pl.BlockSpec((rows, MEM_WIDTH), row),
                   pl.BlockSpec((sb, wb, SWA_KV_WIDTH), seq), pl.BlockSpec((sb, wb, SWA_KV_WIDTH), seq)),
        out_shape=(jax.ShapeDtypeStruct((t, SWA_WIDTH), BF16), jax.ShapeDtypeStruct((t, MEM_WIDTH), BF16),
                   cache_shape, cache_shape),
        compiler_params=_cparams(("parallel",)),
        name="attn_sample",
    )(sinks, q, k, v, past_k, past_v, qm, mk, mv)


def _first_argmax(x, valid, lane):
    xm = jnp.where(valid, x, -jnp.inf)
    mx = jnp.max(xm, axis=-1, keepdims=True)
    idx = jnp.min(jnp.where(xm == mx, lane, LANES), axis=-1, keepdims=True)
    return mx, lane == idx, idx


def _merge_kernel(xp_ref, yap_ref, ybp_ref, ycp_ref, xs_ref, yas_ref, ybs_ref, ycs_ref,
                  gmix_ref, wg_ref, wa_ref, wb_ref, wc_ref, wo_ref, gffn_ref, wr_hi_ref, wr_lo_ref, br_ref,
                  x1_ref, hx_ref, route_ref, cnt_ref, carry_ref, x1_prev, *, n_blk_p, n_blk):
    i = pl.program_id(0)

    @pl.when(i == 0)
    def _():
        x1_prev[...] = jnp.zeros_like(x1_prev)
        carry_ref[...] = jnp.zeros_like(carry_ref)

    x1_routed = x1_prev[...]

    is_prompt = jnp.minimum(i, n_blk - 1) < n_blk_p
    pick = lambda a_ref, b_ref: jnp.where(is_prompt, a_ref[...], b_ref[...])
    x = pick(xp_ref, xs_ref)
    h = _rms_rows(x, gmix_ref[...]).astype(BF16)

    def branch(k, y, w_ref):
        cols = slice(PROJ_A + k * D_MODEL, PROJ_A + (k + 1) * D_MODEL)
        gate = jax.nn.sigmoid(jnp.dot(h, wg_ref[:, cols], preferred_element_type=F32))
        return gate * jnp.dot(y.astype(BF16), w_ref[...], preferred_element_type=F32)

    routing = _route_rows(x1_routed, i >= 1, gffn_ref, wr_hi_ref, wr_lo_ref, br_ref, hx_ref, route_ref, cnt_ref,
                          carry_ref)
    next(routing)
    merged = branch(0, pick(yap_ref, yas_ref), wa_ref)
    next(routing)
    merged = merged + branch(1, pick(ybp_ref, ybs_ref), wb_ref)
    next(routing)
    merged = merged + branch(2, pick(ycp_ref, ycs_ref), wc_ref)
    next(routing, None)
    x1 = x + jnp.dot(merged.astype(BF16), wo_ref[...], preferred_element_type=F32)
    x1_ref[...] = x1
    x1_prev[...] = x1


def _route_rows(x1, valid, gffn_ref, wr_hi_ref, wr_lo_ref, br_ref, hx_ref, route_ref, cnt_ref, carry_ref):
    hn = _rms_rows(x1, gffn_ref[...])
    tm = x1.shape[0]
    slab_row = lambda c: pl.ds(c, tm, stride=HX_ROWS)
    for c in range(ROW_TILES):
        hx_ref[slab_row(c), :] = hn[:, c * LANES:(c + 1) * LANES]
    yield

    hi, lo = _split_bf16(hn)
    w_hi = wr_hi_ref[...]
    logits = (jnp.dot(hi, w_hi, preferred_element_type=F32) + jnp.dot(lo, w_hi, preferred_element_type=F32)
              + jnp.dot(hi, wr_lo_ref[...], preferred_element_type=F32)) + br_ref[...]
    yield
    lane = lax.broadcasted_iota(jnp.int32, logits.shape, 1)
    is_grp = lane < N_GROUPS_E
    g_max, _, g_idx = _first_argmax(logits, is_grp, lane)
    pg_top = 1.0 / jnp.sum(jnp.where(is_grp, jnp.exp(logits - g_max), 0.0), axis=-1, keepdims=True)
    e_lo = N_GROUPS_E + g_idx * EXPERTS_PER_GROUP
    in_grp = (lane >= e_lo) & (lane < e_lo + EXPERTS_PER_GROUP)
    e_max, first, i1 = _first_argmax(logits, in_grp, lane)
    ex = jnp.where(in_grp, jnp.exp(logits - e_max), 0.0)
    pe = ex / jnp.sum(ex, axis=-1, keepdims=True)
    _, second, i2 = _first_argmax(logits, in_grp & jnp.logical_not(first), lane)
    p1 = jnp.sum(jnp.where(first, pe, 0.0), axis=-1, keepdims=True)
    p2 = jnp.sum(jnp.where(second, pe, 0.0), axis=-1, keepdims=True)
    w1 = pg_top * p1 / (p1 + p2)
    w2 = pg_top * p2 / (p1 + p2)

    a1 = i1 - e_lo
    a2 = i2 - e_lo
    e_a = jnp.minimum(a1, a2)
    e_b = jnp.maximum(a1, a2)
    pair = jnp.right_shift(e_a * (2 * EXPERTS_PER_GROUP - 1 - e_a), 1) + (e_b - e_a - 1)
    bucket = g_idx * PAIRS_PER_GROUP + pair
    w_a = jnp.where(a1 < a2, w1, w2)
    w_b = jnp.where(a1 < a2, w2, w1)
    yield

    onehot = lane == jnp.where(valid, bucket, -1)
    tri = (lax.broadcasted_iota(jnp.int32, (tm, tm), 1) <= lax.broadcasted_iota(jnp.int32, (tm, tm), 0))
    csum = jnp.dot(jnp.where(tri, 1.0, 0.0).astype(BF16), jnp.where(onehot, 1.0, 0.0).astype(BF16),
                   preferred_element_type=F32)
    carry = carry_ref[...]
    rank = jnp.sum(jnp.where(onehot, csum + carry, 0.0), axis=-1, keepdims=True) - 1.0
    carry = carry + csum[tm - 1:tm, :]
    carry_ref[...] = carry
    cnt_ref[...] = carry
    route = jnp.where(lane == 0, bucket.astype(F32),
                      jnp.where(lane == 1, w_a, jnp.where(lane == 2, w_b, jnp.where(lane == 3, rank, 0.0))))
    route_ref[...] = route
    hx_ref[slab_row(ROW_TILES), :] = route
    for c in range(ROW_TILES + 1, HX_ROWS):
        hx_ref[slab_row(c), :] = jnp.zeros_like(route)


def _merge(prompt_rows, sample_rows, p):
    tm = TM_MRG
    t_p, t_s = prompt_rows[0].shape[0], sample_rows[0].shape[0]
    nbp = t_p // tm
    t_all = t_p + t_s
    n_blk = t_all // tm
    first = lambda i: (jnp.minimum(i, nbp - 1), 0)
    second = lambda i: (jnp.clip(i - nbp, 0, n_blk - nbp - 1), 0)
    merged_blk = lambda i: (jnp.minimum(i, n_blk - 1), 0)
    routed_blk = lambda i: (jnp.maximum(i - 1, 0), 0)
    widths = (D_MODEL, SSM_WIDTH, SWA_WIDTH, MEM_WIDTH)
    in_specs = ([pl.BlockSpec((tm, w), first) for w in widths] + [pl.BlockSpec((tm, w), second) for w in widths] + [
        _resident((1, D_MODEL)),
        _resident((D_MODEL, PROJ_A + N_BRANCH * D_MODEL)),
        _resident((SSM_WIDTH, D_MODEL)), _resident((SWA_WIDTH, D_MODEL)), _resident((MEM_WIDTH, D_MODEL)),
        _resident((D_MODEL, D_MODEL)),
        _resident((1, D_MODEL)),
        _resident((D_MODEL, LANES)), _resident((D_MODEL, LANES)), _resident((1, LANES)),
    ])
    return pl.pallas_call(
        functools.partial(_merge_kernel, n_blk_p=nbp, n_blk=n_blk),
        grid=(n_blk + 1,),
        in_specs=in_specs,
        out_specs=(pl.BlockSpec((tm, D_MODEL), merged_blk), pl.BlockSpec((tm * HX_ROWS, LANES), routed_blk),
                   pl.BlockSpec((tm, ROUTE_W), routed_blk), _full((1, LANES))),
        out_shape=(jax.ShapeDtypeStruct((t_all, D_MODEL), F32),
                   jax.ShapeDtypeStruct((t_all * HX_ROWS, LANES), F32),
                   jax.ShapeDtypeStruct((t_all, ROUTE_W), F32), jax.ShapeDtypeStruct((1, LANES), F32)),
        scratch_shapes=[pltpu.VMEM((1, LANES), F32), pltpu.VMEM((tm, D_MODEL), F32)],
        compiler_params=_cparams(("arbitrary",)),
        name="merge",
    )(*prompt_rows, *sample_rows, p["g_mix"], p["w_in"], p["w_br_ssm"], p["w_br_swa"], p["w_br_mem"], p["w_o"],
      p["g_ffn"], p["w_r_hi"], p["w_r_lo"], p["b_r"])


def _pos_kernel(route_ref, off_ref, coff_ref, pos_ref, cpos_ref):
    r = route_ref[...]
    lane = lax.broadcasted_iota(jnp.int32, r.shape, 1)
    mine = lane == r[:, 0:1].astype(jnp.int32)
    off = jnp.sum(jnp.where(mine, off_ref[...], 0.0), axis=-1, keepdims=True)
    coff = jnp.sum(jnp.where(mine, coff_ref[...], 0.0), axis=-1, keepdims=True)
    rank = r[:, 3:4]
    both = jnp.where(lane == 0, off + rank, coff + rank)
    pos_rows, cpos_rows = [], []
    for k in range(r.shape[0] // LANES):
        t = both[k * LANES:(k + 1) * LANES].T
        pos_rows.append(t[0:1, :])
        cpos_rows.append(t[1:2, :])
    pos_ref[...] = jnp.concatenate(pos_rows, axis=0).astype(jnp.int32)
    cpos_ref[...] = jnp.concatenate(cpos_rows, axis=0).astype(jnp.int32)


def _sorted_pos(route, off, coff):
    t = route.shape[0]
    tm = SUBLANES * LANES
    shp = jax.ShapeDtypeStruct((t // LANES, LANES), jnp.int32)
    pos, cpos = pl.pallas_call(
        _pos_kernel,
        grid=(t // tm,),
        in_specs=[pl.BlockSpec((tm, ROUTE_W), lambda i: (i, 0)), _full((1, LANES)), _full((1, LANES))],
        out_specs=(pl.BlockSpec((SUBLANES, LANES), lambda i: (i, 0)), pl.BlockSpec((SUBLANES, LANES), lambda i: (i, 0))),
        out_shape=(shp, shp),
        compiler_params=_cparams(("parallel",)),
        name="sorted_pos",
    )(route, off, coff)
    return pos.reshape(t), cpos.reshape(t)


def _inv_kernel(pos_ref, idx_ref):
    def body(t, _):
        idx_ref[pos_ref[t]] = t
        return 0

    lax.fori_loop(0, pos_ref.shape[0], body, 0, unroll=8)


def _invert(pos):
    return pl.pallas_call(
        _inv_kernel,
        in_specs=[pl.BlockSpec(memory_space=pltpu.SMEM)],
        out_specs=pl.BlockSpec(memory_space=pltpu.SMEM),
        out_shape=jax.ShapeDtypeStruct(pos.shape, jnp.int32),
        name="invert_perm",
    )(pos)


def _bucket_kernel(idx_ref, tg_ref, ta_ref, tb_ref, cb_ref, nr_ref, hx_hbm, wi_ref, wd_ref,
                   ys_ref, buf, sem):
    j = pl.program_id(0)
    n_real = nr_ref[0]

    last = idx_ref.shape[0] - 1

    def issue_row(tile_base, slot, r, dst_row, prio):
        src = idx_ref[jnp.minimum(tile_base + r, last)]
        pltpu.make_async_copy(hx_hbm.at[pl.ds(pl.multiple_of(src * HX_ROWS, HX_ROWS), HX_ROWS), :],
                              buf.at[slot, pl.ds(dst_row, HX_ROWS), :], sem.at[slot]).start(priority=prio)

    def wait_tile(slot):
        pltpu.make_async_copy(hx_hbm.at[pl.ds(0, TM_EXP * HX_ROWS), :], buf.at[slot], sem.at[slot]).wait()

    @pl.when(j == 0)
    def _():
        for t in range(GATHER_SLOTS - 1):
            base = cb_ref[jnp.minimum(t, n_real - 1)]

            def body(r8, _, base=base, t=t):
                for k in range(SUBLANES):
                    r = r8 * SUBLANES + k
                    issue_row(base, t, r, pl.multiple_of(r * HX_ROWS, HX_ROWS), k % 2)
                return 0

            lax.fori_loop(0, TM_EXP // SUBLANES, body, 0)

    @pl.when(j < n_real)
    def _():
        slot = j % GATHER_SLOTS
        wait_tile(slot)
        slab_row = lambda c: buf[slot, pl.ds(c, TM_EXP, stride=HX_ROWS), :]
        x = jnp.concatenate([slab_row(c) for c in range(ROW_TILES)], axis=1).astype(BF16)
        route = slab_row(ROW_TILES)

        def ffn(e):
            gu = jnp.dot(x, wi_ref[0, e], preferred_element_type=F32)
            a = jax.nn.silu(gu[:, :D_FF]) * gu[:, D_FF:]
            return jnp.dot(a.astype(BF16), wd_ref[0, e], preferred_element_type=F32)

        y = route[:, 1:2] * ffn(ta_ref[j]) + route[:, 2:3] * ffn(tb_ref[j])
        for c in range(ROW_TILES):
            ys_ref[pl.ds(c, TM_EXP, stride=ROW_TILES), :] = y[:, c * LANES:(c + 1) * LANES]

        ahead = GATHER_SLOTS - 1
        base = cb_ref[jnp.minimum(j + ahead, n_real - 1)]
        nxt_slot = (j + ahead) % GATHER_SLOTS
        for r in range(TM_EXP):
            issue_row(base, nxt_slot, r, r * HX_ROWS, r % 2)

        @pl.when(j == n_real - 1)
        def _():
            for t in range(1, GATHER_SLOTS):
                wait_tile((j + t) % GATHER_SLOTS)


def _bucket_ffn(hx, idx, tile_g, tile_a, tile_b, tile_cb, n_real, p):
    n_tiles = tile_a.shape[0]
    grp = lambda j, idx, tg, ta, tb, cb, nr: (tg[j], 0, 0, 0)
    out = lambda j, idx, tg, ta, tb, cb, nr: (jnp.minimum(j, nr[0] - 1), 0)
    epg = EXPERTS_PER_GROUP
    return pl.pallas_call(
        _bucket_kernel,
        grid_spec=pltpu.PrefetchScalarGridSpec(
            num_scalar_prefetch=6,
            grid=(n_tiles,),
            in_specs=[
                pl.BlockSpec(memory_space=pl.ANY),
                pl.BlockSpec((1, epg, D_MODEL, 2 * D_FF), grp), pl.BlockSpec((1, epg, D_FF, D_MODEL), grp),
            ],
            out_specs=pl.BlockSpec((TM_EXP * ROW_TILES, LANES), out),
            scratch_shapes=[pltpu.VMEM((GATHER_SLOTS, TM_EXP * HX_ROWS, LANES), F32),
                            pltpu.SemaphoreType.DMA((GATHER_SLOTS,))],
        ),
        out_shape=jax.ShapeDtypeStruct((n_tiles * TM_EXP * ROW_TILES, LANES), F32),
        compiler_params=_cparams(("arbitrary",)),
        name="bucket_ffn",
    )(idx, tile_g, tile_a, tile_b, tile_cb, n_real, hx,
      p["w_exp_in"].reshape(N_GROUPS_E, epg, D_MODEL, 2 * D_FF),
      p["w_exp_down"].reshape(N_GROUPS_E, epg, D_FF, D_MODEL))


def _back_kernel(pos_ref, x1_ref, ys_hbm, o_ref, buf, sem, *, t0):
    i = pl.program_id(0)
    tm = o_ref.shape[0]

    n = pl.num_programs(0)
    ahead = GATHER_SLOTS - 1

    def issue_row(base, slot, r, dst_row, prio):
        src = pl.multiple_of(pos_ref[base + r] * ROW_TILES, ROW_TILES)
        pltpu.make_async_copy(ys_hbm.at[pl.ds(src, ROW_TILES), :], buf.at[slot, pl.ds(dst_row, ROW_TILES), :],
                              sem.at[slot]).start(priority=prio)

    def wait_tile(slot):
        pltpu.make_async_copy(ys_hbm.at[pl.ds(0, tm * ROW_TILES), :], buf.at[slot], sem.at[slot]).wait()

    @pl.when(i == 0)
    def _():
        for t in range(ahead):
            base = t0 + jnp.minimum(t, n - 1) * tm

            def body(r8, _, base=base, t=t):
                for k in range(SUBLANES):
                    r = r8 * SUBLANES + k
                    issue_row(base, t, r, pl.multiple_of(r * ROW_TILES, ROW_TILES), k % 2)
                return 0

            lax.fori_loop(0, tm // SUBLANES, body, 0)

    slot = i % GATHER_SLOTS
    wait_tile(slot)
    y = jnp.concatenate([buf[slot, pl.ds(c, tm, stride=ROW_TILES), :] for c in range(ROW_TILES)], axis=1)
    o_ref[...] = x1_ref[...] + y

    base = t0 + jnp.minimum(i + ahead, n - 1) * tm
    nxt_slot = (i + ahead) % GATHER_SLOTS
    for r in range(tm):
        issue_row(base, nxt_slot, r, r * ROW_TILES, r % 2)

    @pl.when(i == n - 1)
    def _():
        for t in range(1, GATHER_SLOTS):
            wait_tile((i + t) % GATHER_SLOTS)


def _unsort_add(x1, ys, pos, t0, t):
    tm = TM_BACK
    row = lambda i, pos: (i, 0)
    return pl.pallas_call(
        functools.partial(_back_kernel, t0=t0),
        grid_spec=pltpu.PrefetchScalarGridSpec(
            num_scalar_prefetch=1,
            grid=(t // tm,),
            in_specs=[pl.BlockSpec((tm, D_MODEL), lambda i, pos: (i + t0 // tm, 0)),
                      pl.BlockSpec(memory_space=pl.ANY)],
            out_specs=pl.BlockSpec((tm, D_MODEL), row),
            scratch_shapes=[pltpu.VMEM((GATHER_SLOTS, tm * ROW_TILES, LANES), F32),
                            pltpu.SemaphoreType.DMA((GATHER_SLOTS,))],
        ),
        out_shape=jax.ShapeDtypeStruct((t, D_MODEL), F32),
        compiler_params=_cparams(("arbitrary",)),
        name="unsort_add",
    )(pos, x1, ys)


def _bucket_experts():
    lo, hi = [], []
    for g in range(N_GROUPS_E):
        for a in range(EXPERTS_PER_GROUP):
            for b in range(a + 1, EXPERTS_PER_GROUP):
                lo.append(g * EXPERTS_PER_GROUP + a)
                hi.append(g * EXPERTS_PER_GROUP + b)
    return jnp.asarray(lo, jnp.int32), jnp.asarray(hi, jnp.int32)


def _tile_tables(counts, n_tiles):
    cnt = counts[0, :N_BUCKETS].astype(jnp.int32)
    nt = (cnt + TM_EXP - 1) // TM_EXP
    tend = jnp.cumsum(nt)
    tstart = tend - nt
    cstart = jnp.cumsum(cnt) - cnt
    pad = lambda v: jnp.zeros((1, LANES), F32).at[0, :N_BUCKETS].set(v.astype(F32))
    j = jnp.arange(n_tiles, dtype=jnp.int32)
    b = jnp.minimum(jnp.sum((tend[None, :] <= j[:, None]).astype(jnp.int32), axis=1), N_BUCKETS - 1)
    in_bucket = (j - tstart[b]) * TM_EXP
    e_lo, e_hi = _bucket_experts()
    epg = EXPERTS_PER_GROUP
    return (pad(tstart * TM_EXP), pad(cstart), b // PAIRS_PER_GROUP, e_lo[b] % epg, e_hi[b] % epg,
            cstart[b] + in_bucket, tend[-1:])


def _rope_tables(first_pos, n_pos):
    half = SWA_HD // 2
    inv = ROPE_THETA ** (-np.arange(half, dtype=np.float64) / half)
    ang = (first_pos + np.arange(n_pos, dtype=np.float64))[:, None] * inv[None, :]
    cos = np.cos(ang)
    sin = np.sin(ang)
    cos = np.concatenate([cos, cos, cos, cos], axis=1)
    sin = np.concatenate([-sin, sin, -sin, sin], axis=1)
    return jnp.asarray(cos, F32), jnp.asarray(sin, F32)


def kernel(x_prompt, x_sample, mem_prompt, state_ssm_re, state_ssm_im, cache_swa_k, cache_swa_v, cache_mem_k, cache_mem_v, norm_mix, w_in, ssm_a_re, ssm_a_im, ssm_log_dt, ssm_b_re, ssm_b_im, ssm_c_re, ssm_c_im, ssm_d, w_glu, swa_q_norm, swa_k_norm, swa_sinks, norm_mem, w_mem_kv, mem_q_norm, mem_k_norm, w_br_ssm, w_br_swa, w_br_mem, w_o, norm_ffn, w_router_group, b_router_group, w_router_expert, b_router_expert, w_exp_in, w_exp_down):
    depth = w_in.shape[0]
    assert depth == 1
    nb, seq, _ = x_prompt.shape
    db, dseq, _ = x_sample.shape
    assert dseq == 2 * LB
    l = 0

    w_r = jnp.concatenate([w_router_group[l], w_router_expert[l]], axis=1)
    w_r = jnp.pad(w_r, ((0, 0), (0, LANES - w_r.shape[1])))
    w_r_hi = w_r.astype(BF16)
    b_r = jnp.pad(jnp.concatenate([b_router_group[l], b_router_expert[l]]), (0, LANES - N_GROUPS_E - N_EXPERTS))
    p = dict(
        g_mix=norm_mix[l][None], w_in=w_in[l].astype(BF16),
        g_q=jnp.tile(swa_q_norm[l], SWA_HEADS)[None], g_k=jnp.tile(swa_k_norm[l], SWA_KV_HEADS)[None],
        g_qm=jnp.tile(mem_q_norm[l], MEM_HEADS)[None], g_km=jnp.tile(mem_k_norm[l], MEM_HEADS)[None],
        ones64=_block_ones(SWA_WIDTH, SWA_HD), ones128=_block_ones(MEM_WIDTH, MEM_HD),
        ssm_d=ssm_d[l][None], w_glu=w_glu[l].astype(BF16),
        g_mem=norm_mem[l][None], w_mem_kv=w_mem_kv[l].astype(BF16),
        w_br_ssm=w_br_ssm[l].astype(BF16), w_br_swa=w_br_swa[l].astype(BF16), w_br_mem=w_br_mem[l].astype(BF16),
        w_o=w_o[l].astype(BF16), g_ffn=norm_ffn[l][None],
        w_r_hi=w_r_hi, w_r_lo=(w_r - w_r_hi.astype(F32)).astype(BF16), b_r=b_r[None],
        w_exp_in=w_exp_in[l].astype(BF16), w_exp_down=w_exp_down[l].astype(BF16),
    )
    p.update(_ssm_params(ssm_a_re[l], ssm_a_im[l], ssm_log_dt[l], ssm_b_re[l], ssm_b_im[l],
                         ssm_c_re[l], ssm_c_im[l]))
    sinks = swa_sinks[l]

    xp = x_prompt.reshape(nb * seq, D_MODEL)
    cos_p, sin_p = _rope_tables(0, seq)
    u, q, k, v, qm = _inproj(xp, cos_p, sin_p, seq // TM_IN, p)
    zeros_state = jnp.zeros((nb, 1, SSM_CH), F32)
    y_ssm, pr, pi = _ssm(u, zeros_state, zeros_state, p, n_seq=nb, chained=True)
    mk, mv = _memkv(mem_prompt, p)
    y_swa, y_mem = _attn_prompt(q, k, v, qm, mk, mv, sinks, nb)
    win = min(WINDOW, seq)
    last_win = lambda a: a.reshape(nb, seq, SWA_KV_WIDTH)[:, seq - win:].reshape(nb, win, SWA_KV_HEADS, SWA_HD)
    p_k, p_v = last_win(k), last_win(v)

    xs = x_sample.reshape(db * dseq, D_MODEL)
    cos_s, sin_s = _rope_tables(PAST_LEN, dseq)
    reps = TM_IN // dseq
    us, qs, ks, vs, qms = _inproj(xs, jnp.tile(cos_s, (reps, 1)), jnp.tile(sin_s, (reps, 1)), 1, p)
    two_rows = lambda st: jnp.repeat(st.reshape(db, SSM_CH), 2, axis=0)
    ys_ssm, sr, si = _ssm(us, two_rows(state_ssm_re), two_rows(state_ssm_im), p, n_seq=db, chained=False)
    sr, si = sr[1::2], si[1::2]
    wb = cache_swa_k.shape[2]
    ys_swa, ys_mem, s_k, s_v = _attn_sample(
        qs, ks, vs, cache_swa_k[l].reshape(db, wb, SWA_KV_WIDTH), cache_swa_v[l].reshape(db, wb, SWA_KV_WIDTH),
        qms, cache_mem_k.reshape(db, N_MEM * MEM_HEADS, MEM_HD), cache_mem_v.reshape(db, N_MEM * MEM_HEADS, MEM_HD),
        sinks, dseq)
    t_p, t_s = nb * seq, db * dseq
    t_all = t_p + t_s
    x1, hx, route, counts = _merge((xp, y_ssm, y_swa, y_mem), (xs, ys_ssm, ys_swa, ys_mem), p)

    n_tiles = pl.cdiv(t_all, TM_EXP) + N_BUCKETS
    off, coff, tile_g, tile_a, tile_b, tile_cb, n_real = _tile_tables(counts, n_tiles)
    pos, cpos = _sorted_pos(route, off, coff)
    idx = _invert(cpos)
    y_sorted = _bucket_ffn(hx, idx, tile_g, tile_a, tile_b, tile_cb, n_real, p)
    yp = _unsort_add(x1, y_sorted, pos, 0, t_p).reshape(nb, seq, D_MODEL)
    ys = _unsort_add(x1, y_sorted, pos, t_p, t_s).reshape(db, dseq, D_MODEL)

    g, s = SSM_GROUPS, SSM_STATE
    return (yp, ys,
            pr.reshape(1, nb, g, s), pi.reshape(1, nb, g, s),
            p_k[None], p_v[None],
            mk.reshape(1, nb, N_MEM, MEM_HEADS, MEM_HD), mv.reshape(1, nb, N_MEM, MEM_HEADS, MEM_HD),
            sr.reshape(1, db, g, s), si.reshape(1, db, g, s),
            s_k.reshape(1, db, wb, SWA_KV_HEADS, SWA_HD), s_v.reshape(1, db, wb, SWA_KV_HEADS, SWA_HD))
```
